```python
import math
import jax, jax.numpy as jnp
from jax import lax
import numpy as np

D_MODEL = 1024
BATCH = 8
SEQ = 2048
DEPTH = 1
DEC_BATCH = 128
DEC_SEQ = 8
PAST_LEN = 16384
PAGE_SIZE = 128

D_MIX = D_MODEL
RG_WIDTH = D_MIX // 2
RG_HEADS = 8
RG_HEAD_DIM = RG_WIDTH // RG_HEADS
CONV_WIDTH = 4
RG_C = 8.0
GLA_WIDTH = D_MIX - RG_WIDTH
GLA_HEADS = 4
GLA_DV = GLA_WIDTH // GLA_HEADS
GLA_DK = GLA_DV // 2
GLA_KEY_WIDTH = GLA_HEADS * GLA_DK
GLA_GATE_RANK = 16
GLA_GATE_NORMALIZER = 16.0
GLA_CHUNK = 32
D_FF = 2816
N_MEM = 256
XA_HEADS = 4
XA_HEAD_DIM = D_MODEL // XA_HEADS
EPS = 1e-6

OFF_RG_X = 0
OFF_RG_Y = OFF_RG_X + RG_WIDTH
OFF_Q = OFF_RG_Y + RG_WIDTH
OFF_K = OFF_Q + GLA_KEY_WIDTH
OFF_V = OFF_K + GLA_KEY_WIDTH
OFF_G = OFF_V + GLA_WIDTH
OFF_A = OFF_G + GLA_WIDTH
D_IN = OFF_A + GLA_GATE_RANK

kernel_name = "hymba_rglru_gla_macaron_memxattn_step"


def rmsnorm(x, g):
    xf = x.astype(jnp.float32)
    xf = xf * lax.rsqrt(jnp.mean(xf * xf, axis=-1, keepdims=True) + EPS)
    return xf.astype(x.dtype) * g


def swiglu_ffn(x, g, w_gate, w_up, w_down):
    h = rmsnorm(x, g)
    return (jax.nn.silu(h @ w_gate) * (h @ w_up)) @ w_down


def causal_dwconv(xb, buf, w, b):
    T = xb.shape[1]
    xp = jnp.concatenate([buf, xb], axis=1)
    y = b + xp[:, 0:T] * w[0]
    for j in range(1, CONV_WIDTH):
        y = y + xp[:, j:j + T] * w[j]
    return y, xp[:, -(CONV_WIDTH - 1):]


def rglru(xc, h0, w_a, b_a, w_x, b_x, lam):
    B, T, _ = xc.shape
    xh = xc.reshape(B, T, RG_HEADS, RG_HEAD_DIM)
    r = jax.nn.sigmoid(jnp.einsum('bthi,hij->bthj', xh, w_a) + b_a).reshape(B, T, RG_WIDTH)
    i = jax.nn.sigmoid(jnp.einsum('bthi,hij->bthj', xh, w_x) + b_x).reshape(B, T, RG_WIDTH)
    log_a = -RG_C * jax.nn.softplus(-lam.astype(jnp.float32)) * r.astype(jnp.float32)
    a = jnp.exp(log_a)
    mult = jnp.sqrt(-jnp.expm1(2.0 * log_a))
    u = mult * (i * xc).astype(jnp.float32)
    u = u.at[:, 0].add(a[:, 0] * h0.astype(jnp.float32))

    def combine(lhs, rhs):
        return (lhs[0] * rhs[0], rhs[0] * lhs[1] + rhs[1])

    _, hs = lax.associative_scan(combine, (a, u), axis=1)
    return hs, hs[:, -1]


def gla_chunked(q, k, v, log_a, S0):
    B, T, H, DK = q.shape
    DV = v.shape[-1]
    C = math.gcd(T, GLA_CHUNK)
    n = T // C

    def to_chunks(t):
        return t.reshape(B, n, C, H, t.shape[-1]).swapaxes(0, 1)

    mask = jnp.tril(jnp.ones((C, C), dtype=bool))

    def step(S, inp):
        qc, kc, vc, lac = inp
        b = jnp.cumsum(lac, axis=1)
        b_last = b[:, -1]
        qt = qc * jnp.exp(b)
        kt = kc * jnp.exp(-b)
        attn = jnp.where(mask, jnp.einsum('bihk,bjhk->bhij', qt, kt), 0.0)
        o = jnp.einsum('bihk,bhkv->bihv', qt, S) + jnp.einsum('bhij,bjhv->bihv', attn, vc)
        kd = kc * jnp.exp(b_last[:, None] - b)
        S = jnp.exp(b_last)[..., None] * S + jnp.einsum('bjhk,bjhv->bhkv', kd, vc)
        return S, o

    S, o = lax.scan(step, S0, (to_chunks(q), to_chunks(k), to_chunks(v), to_chunks(log_a)))
    return o.swapaxes(0, 1).reshape(B, T, H, DV), S


def memory_kv(mem, mem_norm, w_ck, w_cv):
    B = mem.shape[0]
    m = rmsnorm(mem, mem_norm)
    k = (m @ w_ck).reshape(B, N_MEM, XA_HEADS, XA_HEAD_DIM)
    v = (m @ w_cv).reshape(B, N_MEM, XA_HEADS, XA_HEAD_DIM)
    return k, v


def cross_attend(x, mem_k, mem_v, g, w_cq, w_co):
    B, T, _ = x.shape
    q = (rmsnorm(x, g) @ w_cq).reshape(B, T, XA_HEADS, XA_HEAD_DIM)
    s = jnp.einsum('bthd,bmhd->bhtm', q, mem_k).astype(jnp.float32) * (XA_HEAD_DIM ** -0.5)
    pr = jax.nn.softmax(s, axis=-1).astype(mem_v.dtype)
    o = jnp.einsum('bhtm,bmhd->bthd', pr, mem_v).reshape(B, T, D_MODEL)
    return o @ w_co


def token_mixer(h, conv_buf, rg_h0, gla_S0, p):
    B, T, _ = h.shape
    pin = h @ p['w_in']
    xb = pin[..., OFF_RG_X:OFF_RG_Y]
    yb = pin[..., OFF_RG_Y:OFF_Q]
    q = pin[..., OFF_Q:OFF_K].reshape(B, T, GLA_HEADS, GLA_DK)
    k = pin[..., OFF_K:OFF_V].reshape(B, T, GLA_HEADS, GLA_DK)
    v = pin[..., OFF_V:OFF_G].reshape(B, T, GLA_HEADS, GLA_DV)
    g = pin[..., OFF_G:OFF_A]
    a_low = pin[..., OFF_A:D_IN]
    xc, new_conv = causal_dwconv(xb, conv_buf, p['conv_w'], p['conv_b'])
    hs, rg_h = rglru(xc, rg_h0, p['rg_w_a'], p['rg_b_a'], p['rg_w_x'], p['rg_b_x'], p['rg_lambda'])
    rg_out = rmsnorm(hs.astype(h.dtype) * jax.nn.gelu(yb), p['rg_out_norm'])
    z = (a_low @ p['gla_w_a2'] + p['gla_b_a2']).astype(jnp.float32)
    log_a = (jax.nn.log_sigmoid(z) / GLA_GATE_NORMALIZER).reshape(B, T, GLA_HEADS, GLA_DK)
    o, S = gla_chunked(q.astype(jnp.float32) * (GLA_DK ** -0.5), k.astype(jnp.float32),
                       v.astype(jnp.float32), log_a, gla_S0.astype(jnp.float32))
    gla_out = rmsnorm(o.astype(h.dtype), p['gla_out_norm']).reshape(B, T, GLA_WIDTH) * jax.nn.silu(g)
    out = jnp.concatenate([rg_out, gla_out], axis=-1) @ p['w_out']
    return out, new_conv, rg_h.astype(h.dtype), S.astype(h.dtype)


def decoder_layer(x, mem_k, mem_v, conv_buf, rg_h0, gla_S0, p):
    x = x + 0.5 * swiglu_ffn(x, p['ffn1_norm'], p['ffn1_w_gate'], p['ffn1_w_up'], p['ffn1_w_down'])
    mix, new_conv, new_h, new_S = token_mixer(rmsnorm(x, p['mix_norm']), conv_buf, rg_h0, gla_S0, p)
    x = x + mix
    x = x + cross_attend(x, mem_k, mem_v, p['xattn_norm'], p['w_cq'], p['w_co'])
    x = x + 0.5 * swiglu_ffn(x, p['ffn2_norm'], p['ffn2_w_gate'], p['ffn2_w_up'], p['ffn2_w_down'])
    return x, new_conv, new_h, new_S


def setup_inputs(seed: int = 0) -> dict:
    key = jax.random.key(seed)
    ks = iter(jax.random.split(key, 48))
    f32 = jnp.float32

    def nrm(shape, scale):
        return jax.random.normal(next(ks), shape, f32) * scale

    def gain(n):
        return 1.0 + nrm((n,), 0.02)

    u = jax.random.uniform(next(ks), (RG_WIDTH,), f32, 0.9, 0.999)
    rg_lambda = jnp.log(u) - jnp.log1p(-u)
    return {
        "x_prompt": nrm((BATCH, SEQ, D_MODEL), 1.0),
        "x_sample": nrm((DEC_BATCH, DEC_SEQ, D_MODEL), 1.0),
        "cache_mem_k": nrm((DEC_BATCH, N_MEM, XA_HEADS, XA_HEAD_DIM), 1.0),
        "cache_mem_v": nrm((DEC_BATCH, N_MEM, XA_HEADS, XA_HEAD_DIM), 1.0),
        "state_conv": nrm((DEC_BATCH, CONV_WIDTH - 1, RG_WIDTH), 1.0),
        "state_rglru": nrm((DEC_BATCH, RG_WIDTH), 0.5),
        "state_gla": nrm((DEC_BATCH, GLA_HEADS, GLA_DK, GLA_DV), 0.5),
        "mem_prompt": nrm((BATCH, N_MEM, D_MODEL), 1.0),
        "ffn1_norm": gain(D_MODEL),
        "ffn1_w_gate": nrm((D_MODEL, D_FF), D_MODEL ** -0.5),
        "ffn1_w_up": nrm((D_MODEL, D_FF), D_MODEL ** -0.5),
        "ffn1_w_down": nrm((D_FF, D_MODEL), D_FF ** -0.5),
        "mix_norm": gain(D_MODEL),
        "w_in": nrm((D_MODEL, D_IN), D_MODEL ** -0.5),
        "conv_w": nrm((CONV_WIDTH, RG_WIDTH), CONV_WIDTH ** -0.5),
        "conv_b": nrm((RG_WIDTH,), 0.01),
        "rg_w_a": nrm((RG_HEADS, RG_HEAD_DIM, RG_HEAD_DIM), RG_HEAD_DIM ** -0.5),
        "rg_b_a": nrm((RG_HEADS, RG_HEAD_DIM), 0.01),
        "rg_w_x": nrm((RG_HEADS, RG_HEAD_DIM, RG_HEAD_DIM), RG_HEAD_DIM ** -0.5),
        "rg_b_x": nrm((RG_HEADS, RG_HEAD_DIM), 0.01),
        "rg_lambda": rg_lambda,
        "rg_out_norm": gain(RG_WIDTH),
        "gla_w_a2": nrm((GLA_GATE_RANK, GLA_KEY_WIDTH), GLA_GATE_RANK ** -0.5),
        "gla_b_a2": nrm((GLA_KEY_WIDTH,), 0.01),
        "gla_out_norm": gain(GLA_DV),
        "w_out": nrm((D_MIX, D_MODEL), D_MIX ** -0.5),
        "xattn_norm": gain(D_MODEL),
        "mem_norm": gain(D_MODEL),
        "w_cq": nrm((D_MODEL, D_MODEL), D_MODEL ** -0.5),
        "w_ck": nrm((D_MODEL, D_MODEL), D_MODEL ** -0.5),
        "w_cv": nrm((D_MODEL, D_MODEL), D_MODEL ** -0.5),
        "w_co": nrm((D_MODEL, D_MODEL), D_MODEL ** -0.5),
        "ffn2_norm": gain(D_MODEL),
        "ffn2_w_gate": nrm((D_MODEL, D_FF), D_MODEL ** -0.5),
        "ffn2_w_up": nrm((D_MODEL, D_FF), D_MODEL ** -0.5),
        "ffn2_w_down": nrm((D_FF, D_MODEL), D_FF ** -0.5),
        "final_norm": gain(D_MODEL),
    }


def reference(x_prompt, x_sample, cache_mem_k, cache_mem_v, state_conv, state_rglru, state_gla,
              mem_prompt, ffn1_norm, ffn1_w_gate, ffn1_w_up, ffn1_w_down, mix_norm, w_in,
              conv_w, conv_b, rg_w_a, rg_b_a, rg_w_x, rg_b_x, rg_lambda, rg_out_norm,
              gla_w_a2, gla_b_a2, gla_out_norm, w_out, xattn_norm, mem_norm, w_cq, w_ck, w_cv,
              w_co, ffn2_norm, ffn2_w_gate, ffn2_w_up, ffn2_w_down, final_norm):
    p = dict(ffn1_norm=ffn1_norm, ffn1_w_gate=ffn1_w_gate, ffn1_w_up=ffn1_w_up,
             ffn1_w_down=ffn1_w_down, mix_norm=mix_norm, w_in=w_in, conv_w=conv_w,
             conv_b=conv_b, rg_w_a=rg_w_a, rg_b_a=rg_b_a, rg_w_x=rg_w_x, rg_b_x=rg_b_x,
             rg_lambda=rg_lambda, rg_out_norm=rg_out_norm, gla_w_a2=gla_w_a2,
             gla_b_a2=gla_b_a2, gla_out_norm=gla_out_norm, w_out=w_out,
             xattn_norm=xattn_norm, w_cq=w_cq, w_co=w_co, ffn2_norm=ffn2_norm,
             ffn2_w_gate=ffn2_w_gate, ffn2_w_up=ffn2_w_up, ffn2_w_down=ffn2_w_down)
    dt = x_prompt.dtype
    Bp = x_prompt.shape[0]

    mem_k_p, mem_v_p = memory_kv(mem_prompt, mem_norm, w_ck, w_cv)
    xp = x_prompt
    conv_p = jnp.zeros((Bp, CONV_WIDTH - 1, RG_WIDTH), dt)
    h_p = jnp.zeros((Bp, RG_WIDTH), dt)
    S_p = jnp.zeros((Bp, GLA_HEADS, GLA_DK, GLA_DV), dt)
    for _ in range(DEPTH):
        xp, conv_p, h_p, S_p = decoder_layer(xp, mem_k_p, mem_v_p, conv_p, h_p, S_p, p)
    y_prompt = rmsnorm(xp, final_norm)

    xs = x_sample
    conv_s, h_s, S_s = state_conv, state_rglru, state_gla
    for _ in range(DEPTH):
        xs, conv_s, h_s, S_s = decoder_layer(xs, cache_mem_k, cache_mem_v, conv_s, h_s, S_s, p)
    y_sample = rmsnorm(xs, final_norm)

    return (y_prompt, y_sample, mem_k_p, mem_v_p, conv_p, h_p, S_p, conv_s, h_s, S_s)
```

```python
import functools

import jax
import jax.numpy as jnp
import numpy as np
from jax import lax
from jax.experimental import pallas as pl
from jax.experimental.pallas import tpu as pltpu

F32 = jnp.float32
BF16 = jnp.bfloat16

D_MODEL = 1024
D_FF = 2816
RG_WIDTH = 512
RG_HEADS = 8
RG_HEAD_DIM = 64
CONV_WIDTH = 4
RG_C = 8.0
GLA_WIDTH = 512
GLA_HEADS = 4
GLA_DV = 128
GLA_DK = 64
GLA_KEY_WIDTH = 256
GLA_GATE_RANK = 16
GLA_GATE_NORMALIZER = 16.0
GLA_CHUNK = 32
N_MEM = 256
XA_HEADS = 4
XA_HEAD_DIM = 256
EPS = 1e-6

OFF_RG_X = 0
OFF_RG_Y = 512
OFF_Q = 1024
OFF_K = 1280
OFF_V = 1536
OFF_G = 2048
OFF_A = 2560
D_IN = 2576

LANES = 128
SUBLANES = 8
A_PAD = LANES
P_WIDTH = OFF_A + A_PAD
PAIR_K = 2 * GLA_DK
PAIR_V = 2 * GLA_DV
VMEM_LIMIT = 56 * 1024 * 1024

ROW_TILE = 256
SAMPLE_ATTN_BATCH = 4


def _rms(x, g):
    return x * lax.rsqrt(jnp.mean(x * x, axis=-1, keepdims=True) + EPS) * g


def _mm(a, w):
    return jnp.dot(a.astype(BF16), w, preferred_element_type=F32)


def _mm_nt(a, b):
    return lax.dot_general(a.astype(BF16), b.astype(BF16), (((1,), (1,)), ((), ())),
                           preferred_element_type=F32)


def _mm_tn(a, b):
    return lax.dot_general(a.astype(BF16), b.astype(BF16), (((0,), (0,)), ((), ())),
                           preferred_element_type=F32)


def _silu(x):
    return x * jax.nn.sigmoid(x)


def _gelu_tanh(x):
    c = np.float32(np.sqrt(2.0 / np.pi))
    return x * (0.5 * (1.0 + jnp.tanh(c * (x + 0.044715 * (x * x * x)))))


def _softplus(x):
    return jnp.maximum(x, 0.0) + jnp.log1p(jnp.exp(-jnp.abs(x)))


def _ffn(x, norm_ref, wg_ref, wu_ref, wd_ref):
    h = _rms(x, norm_ref[...]).astype(BF16)
    g = jnp.dot(h, wg_ref[...], preferred_element_type=F32)
    u = jnp.dot(h, wu_ref[...], preferred_element_type=F32)
    return _mm(_silu(g) * u, wd_ref[...])


def _last_rows(buf_ref, x, seg):
    n = x.shape[0] // seg
    outs = []
    for j in range(x.shape[1] // LANES):
        buf_ref[j] = x[:, j * LANES:(j + 1) * LANES]
        outs.append(buf_ref[j, pl.ds(seg - 1, n, stride=seg), :])
    return jnp.concatenate(outs, axis=1)


def _shift_rows(x, s):
    return pltpu.roll(x, s, axis=0)


def _row_in_segment(shape, seg):
    return lax.broadcasted_iota(jnp.int32, shape, 0) & (seg - 1)


def _segment_cumsum(x, seg):
    pos = _row_in_segment(x.shape, seg)
    s = 1
    while s < seg:
        x = jnp.where(pos >= s, x + _shift_rows(x, s), x)
        s *= 2
    return x


def _segment_affine_scan(a, u, seg):
    pos = _row_in_segment(a.shape, seg)
    s = 1
    while s < seg:
        m = pos >= s
        u = jnp.where(m, a * _shift_rows(u, s) + u, u)
        a = jnp.where(m, a * _shift_rows(a, s), a)
        s *= 2
    return a, u


def _conv(xb, sh1, sh2, sh3, cw_ref, cb_ref):
    y = cb_ref[...] + sh3 * cw_ref[0:1, :]
    y = y + sh2 * cw_ref[1:2, :]
    y = y + sh1 * cw_ref[2:3, :]
    return y + xb * cw_ref[3:4, :]


def _rg_gates(xc, wgate_ref, ba_ref, bx_ref):
    half = RG_WIDTH // 2
    r, i = [], []
    for c in range(2):
        z = _mm(xc[:, c * half:(c + 1) * half], wgate_ref[c])
        r.append(z[:, :half])
        i.append(z[:, half:])
    r = jax.nn.sigmoid(jnp.concatenate(r, axis=1) + ba_ref[...])
    i = jax.nn.sigmoid(jnp.concatenate(i, axis=1) + bx_ref[...])
    return r, i


def _rg_decay_input(xc, r, i, lam_ref):
    log_a = (-RG_C * _softplus(-lam_ref[...])) * r
    a = jnp.exp(log_a)
    mult = jnp.sqrt(-jnp.tanh(log_a) * (a * a + 1.0))
    return a, mult * (i * xc)


def _gla_log_decay(a_low, wa2_ref, ba2_ref):
    z = _mm(a_low, wa2_ref[...]) + ba2_ref[...]
    return (-_softplus(-z)) / GLA_GATE_NORMALIZER


def _gla_intra(qt, kt, v, chunk):
    t = qt.shape[0]
    shift = int(np.log2(chunk))
    ri = lax.broadcasted_iota(jnp.int32, (t, t), 0)
    ci = lax.broadcasted_iota(jnp.int32, (t, t), 1)
    causal = (ri >= ci) & ((ri >> shift) == (ci >> shift))
    lane = lax.broadcasted_iota(jnp.int32, (t, PAIR_K), 1)
    outs = []
    for h in range(GLA_HEADS):
        p = h // 2
        qp = qt[:, p * PAIR_K:(p + 1) * PAIR_K]
        kp = kt[:, p * PAIR_K:(p + 1) * PAIR_K]
        mine = (lane >= GLA_DK) if (h % 2) else (lane < GLA_DK)
        s = _mm_nt(jnp.where(mine, qp, 0.0), kp)
        attn = jnp.where(causal, s, 0.0)
        outs.append(_mm(attn, v[:, h * GLA_DV:(h + 1) * GLA_DV].astype(BF16)))
    return outs


def _pair_blockdiag_mask():
    r = lax.broadcasted_iota(jnp.int32, (PAIR_K, PAIR_V), 0)
    c = lax.broadcasted_iota(jnp.int32, (PAIR_K, PAIR_V), 1)
    return (r < GLA_DK) == (c < GLA_DV)


def _gla_out(o, g, norm_ref):
    outs = []
    for h in range(GLA_HEADS):
        sl = slice(h * GLA_DV, (h + 1) * GLA_DV)
        outs.append(_rms(o[:, sl], norm_ref[:, sl]))
    return jnp.concatenate(outs, axis=1) * _silu(g)


def _softmax_rows(s):
    m = jnp.max(s, axis=-1, keepdims=True)
    e = jnp.exp(s - m)
    return e * (1.0 / jnp.sum(e, axis=-1, keepdims=True))


def _split_p(p_ref):
    xb = p_ref[:, OFF_RG_X:OFF_RG_Y]
    yb = p_ref[:, OFF_RG_Y:OFF_Q]
    q = p_ref[:, OFF_Q:OFF_K]
    k = p_ref[:, OFF_K:OFF_V]
    v = p_ref[:, OFF_V:OFF_G]
    g = p_ref[:, OFF_G:OFF_A]
    a_low = p_ref[:, OFF_A:P_WIDTH]
    return xb, yb, q, k, v, g, a_low


def _gla_prepare(q, k, log_a, chunk):
    b = _segment_cumsum(log_a, chunk)
    qt = (q * (GLA_DK ** -0.5)) * jnp.exp(b)
    kt = k * jnp.exp(-b)
    return b, qt, kt


def _mix_project(rg_out, gla_out, wout_ref):
    return (_mm(rg_out, wout_ref[0:RG_WIDTH, :]) + _mm(gla_out, wout_ref[RG_WIDTH:, :]))


def _ffn_in_kernel(x_ref, n1_ref, wg_ref, wu_ref, wd_ref, n2_ref, win_ref, x1_ref, p_ref):
    x = x_ref[...]
    x1 = x + 0.5 * _ffn(x, n1_ref, wg_ref, wu_ref, wd_ref)
    x1_ref[...] = x1
    p_ref[...] = _mm(_rms(x1, n2_ref[...]), win_ref[...])


def _mem_kv_kernel(m_ref, n_ref, wk_ref, wv_ref, k_ref, v_ref, kb_ref, vb_ref):
    h = _rms(m_ref[...], n_ref[...]).astype(BF16)
    k = jnp.dot(h, wk_ref[...], preferred_element_type=F32)
    v = jnp.dot(h, wv_ref[...], preferred_element_type=F32)
    k_ref[...] = k
    v_ref[...] = v
    kb_ref[...] = k.astype(BF16)
    vb_ref[...] = v.astype(BF16)


def _prompt_tail_kernel(
        x1_ref, p_ref, kb_ref, vb_ref,
        cw_ref, cb_ref, wgate_ref, ba_ref, bx_ref, lam_ref, rgn_ref,
        wa2_ref, ba2_ref, glan_ref, wout_ref,
        xn_ref, wcq_ref, wco_ref, n2_ref, wg_ref, wu_ref, wd_ref, fn_ref,
        y_ref, rgh_ref, s_ref,
        xpad_ref, h_ref, sbd_ref, b_ref, o_ref):
    t_idx = pl.program_id(1)
    tt = x1_ref.shape[0]
    n_chunks = tt // GLA_CHUNK

    @pl.when(t_idx == 0)
    def _():
        xpad_ref[0:SUBLANES, :] = jnp.zeros((SUBLANES, RG_WIDTH), F32)
        h_ref[...] = jnp.zeros_like(h_ref)
        sbd_ref[...] = jnp.zeros_like(sbd_ref)

    xb, yb, q, k, v, g, a_low = _split_p(p_ref)

    xpad_ref[SUBLANES:SUBLANES + tt, :] = xb
    sh1 = xpad_ref[SUBLANES - 1:SUBLANES - 1 + tt, :]
    sh2 = xpad_ref[SUBLANES - 2:SUBLANES - 2 + tt, :]
    sh3 = xpad_ref[SUBLANES - 3:SUBLANES - 3 + tt, :]
    xc = _conv(xb, sh1, sh2, sh3, cw_ref, cb_ref)
    xpad_ref[0:SUBLANES, :] = xb[tt - SUBLANES:tt, :]
    r, i = _rg_gates(xc, wgate_ref, ba_ref, bx_ref)
    a, u = _rg_decay_input(xc, r, i, lam_ref)
    a_cum, h_zero = _segment_affine_scan(a, u, tt)
    hs = h_zero + a_cum * h_ref[0:1, :]
    h_last = hs[tt - 1:tt, :]
    h_ref[...] = jnp.broadcast_to(h_last, h_ref.shape)
    rg_out = _rms(hs * _gelu_tanh(yb), rgn_ref[...])

    log_a = _gla_log_decay(a_low, wa2_ref, ba2_ref)
    b, qt, kt = _gla_prepare(q, k, log_a, GLA_CHUNK)
    intra = _gla_intra(qt, kt, v, GLA_CHUNK)
    for h in range(GLA_HEADS):
        o_ref[:, h * GLA_DV:(h + 1) * GLA_DV] = intra[h]
    b_last = _last_rows(b_ref, b, GLA_CHUNK)
    dec_t = jnp.exp(b_last).T
    bd_mask = _pair_blockdiag_mask()
    for c in range(n_chunks):
        rows = slice(c * GLA_CHUNK, (c + 1) * GLA_CHUNK)
        for p in range(2):
            kl = slice(p * PAIR_K, (p + 1) * PAIR_K)
            vl = slice(p * PAIR_V, (p + 1) * PAIR_V)
            s_bd = sbd_ref[p]
            o_ref[rows, vl] += _mm(qt[rows, kl], s_bd.astype(BF16))
            kd = k[rows, kl] * jnp.exp(b_last[c:c + 1, kl] - b[rows, kl])
            kv = _mm_tn(kd, v[rows, vl])
            sbd_ref[p] = s_bd * dec_t[kl, c:c + 1] + jnp.where(bd_mask, kv, 0.0)
    gla_out = _gla_out(o_ref[...], g, glan_ref)

    x2 = x1_ref[...] + _mix_project(rg_out, gla_out, wout_ref)

    qx = _mm(_rms(x2, xn_ref[...]), wcq_ref[...])
    x3 = x2
    for h in range(XA_HEADS):
        sl = slice(h * XA_HEAD_DIM, (h + 1) * XA_HEAD_DIM)
        s = _mm_nt(qx[:, sl], kb_ref[:, sl]) * (XA_HEAD_DIM ** -0.5)
        oh = _mm(_softmax_rows(s), vb_ref[:, sl])
        x3 = x3 + _mm(oh, wco_ref[sl, :])

    x4 = x3 + 0.5 * _ffn(x3, n2_ref, wg_ref, wu_ref, wd_ref)
    y_ref[...] = _rms(x4, fn_ref[...])

    @pl.when(t_idx == pl.num_programs(1) - 1)
    def _():
        rgh_ref[...] = h_ref[...]
        row = lax.broadcasted_iota(jnp.int32, (PAIR_K, GLA_DV), 0)
        for p in range(2):
            s_bd = sbd_ref[p]
            s_ref[p * PAIR_K:(p + 1) * PAIR_K, :] = jnp.where(
                row < GLA_DK, s_bd[:, :GLA_DV], s_bd[:, GLA_DV:])


def _sample_mix_kernel(
        x1_ref, p_ref, econv_ref, eh_ref, s0_ref,
        cw_ref, cb_ref, wgate_ref, ba_ref, bx_ref, lam_ref, rgn_ref,
        wa2_ref, ba2_ref, glan_ref, wout_ref, xn_ref, wcq_ref,
        x2_ref, q_ref, rgh_ref, s_ref,
        hs_ref, b_ref, o_ref, *, seq):
    rows_n = x1_ref.shape[0]
    nb = rows_n // seq

    xb, yb, q, k, v, g, a_low = _split_p(p_ref)

    pos = _row_in_segment(xb.shape, seq)
    econv = econv_ref[...]
    sh = []
    for j in range(1, CONV_WIDTH):
        sh.append(jnp.where(pos >= j, _shift_rows(xb, j), _shift_rows(econv, rows_n - seq + j)))
    xc = _conv(xb, sh[0], sh[1], sh[2], cw_ref, cb_ref)
    r, i = _rg_gates(xc, wgate_ref, ba_ref, bx_ref)
    a, u = _rg_decay_input(xc, r, i, lam_ref)
    u = u + a * eh_ref[...]
    _, hs = _segment_affine_scan(a, u, seq)
    rgh_ref[...] = _last_rows(hs_ref, hs, seq)
    rg_out = _rms(hs * _gelu_tanh(yb), rgn_ref[...])

    log_a = _gla_log_decay(a_low, wa2_ref, ba2_ref)
    b, qt, kt = _gla_prepare(q, k, log_a, seq)
    intra = _gla_intra(qt, kt, v, seq)
    for h in range(GLA_HEADS):
        o_ref[:, h * GLA_DV:(h + 1) * GLA_DV] = intra[h]
    b_last = _last_rows(b_ref, b, seq)
    dec_t = jnp.exp(b_last).T
    bd_mask = _pair_blockdiag_mask()
    row = lax.broadcasted_iota(jnp.int32, (PAIR_K, GLA_DV), 0)
    top = row < GLA_DK
    for c in range(nb):
        rows = slice(c * seq, (c + 1) * seq)
        for p in range(2):
            kl = slice(p * PAIR_K, (p + 1) * PAIR_K)
            vl = slice(p * PAIR_V, (p + 1) * PAIR_V)
            s_pair = s0_ref[c, kl, :]
            s_bd = jnp.concatenate([jnp.where(top, s_pair, 0.0), jnp.where(top, 0.0, s_pair)], axis=1)
            o_ref[rows, vl] += _mm(qt[rows, kl], s_bd.astype(BF16))
            kd = k[rows, kl] * jnp.exp(b_last[c:c + 1, kl] - b[rows, kl])
            kv = _mm_tn(kd, v[rows, vl])
            s_new = s_bd * dec_t[kl, c:c + 1] + jnp.where(bd_mask, kv, 0.0)
            s_ref[c, kl, :] = jnp.where(top, s_new[:, :GLA_DV], s_new[:, GLA_DV:])
    gla_out = _gla_out(o_ref[...], g, glan_ref)

    x2 = x1_ref[...] + _mix_project(rg_out, gla_out, wout_ref)
    x2_ref[...] = x2
    q_ref[...] = _mm(_rms(x2, xn_ref[...]), wcq_ref[...])


def _sample_attn_kernel(q_ref, k_ref, v_ref, o_ref, *, seq):
    nb = k_ref.shape[0]
    for j in range(nb):
        rows = slice(j * seq, (j + 1) * seq)
        for h in range(XA_HEADS):
            sl = slice(h * XA_HEAD_DIM, (h + 1) * XA_HEAD_DIM)
            s = _mm_nt(q_ref[rows, sl], k_ref[j, :, sl]) * (XA_HEAD_DIM ** -0.5)
            o_ref[rows, sl] = _mm(_softmax_rows(s), v_ref[j, :, sl].astype(BF16))


def _out_tail_kernel(x2_ref, o_ref, wco_ref, n2_ref, wg_ref, wu_ref, wd_ref, fn_ref, y_ref):
    x3 = x2_ref[...] + _mm(o_ref[...], wco_ref[...])
    x4 = x3 + 0.5 * _ffn(x3, n2_ref, wg_ref, wu_ref, wd_ref)
    y_ref[...] = _rms(x4, fn_ref[...])


def _resident(arr):
    nd = arr.ndim
    return pl.BlockSpec(arr.shape, lambda *_: (0,) * nd, pipeline_mode=pl.Buffered(1))


def _params(sem):
    return pltpu.CompilerParams(dimension_semantics=sem, vmem_limit_bytes=VMEM_LIMIT)


def _ffn_in(x, weights):
    rows = x.shape[0]
    row_spec = lambda w: pl.BlockSpec((ROW_TILE, w), lambda i: (i, 0))
    return pl.pallas_call(
        _ffn_in_kernel,
        grid=(rows // ROW_TILE,),
        in_specs=[row_spec(D_MODEL)] + [_resident(w) for w in weights],
        out_specs=[row_spec(D_MODEL), row_spec(P_WIDTH)],
        out_shape=[jax.ShapeDtypeStruct((rows, D_MODEL), F32),
                   jax.ShapeDtypeStruct((rows, P_WIDTH), F32)],
        compiler_params=_params(("parallel",)),
        name="ffn_in",
    )(x, *weights)


def _mem_kv(mem, weights):
    rows = mem.shape[0]
    row_spec = pl.BlockSpec((ROW_TILE, D_MODEL), lambda i: (i, 0))
    return pl.pallas_call(
        _mem_kv_kernel,
        grid=(rows // ROW_TILE,),
        in_specs=[row_spec] + [_resident(w) for w in weights],
        out_specs=[row_spec] * 4,
        out_shape=[jax.ShapeDtypeStruct((rows, D_MODEL), F32)] * 2
                  + [jax.ShapeDtypeStruct((rows, D_MODEL), BF16)] * 2,
        compiler_params=_params(("parallel",)),
        name="mem_kv",
    )(mem, *weights)


def _prompt_tail(x1, p, kb, vb, weights):
    batch, seq, _ = x1.shape
    tt = ROW_TILE
    tok = lambda w: pl.BlockSpec((None, tt, w), lambda b, t: (b, t, 0))
    per_seq = lambda r, w: pl.BlockSpec((None, r, w), lambda b, t: (b, 0, 0))
    return pl.pallas_call(
        _prompt_tail_kernel,
        grid=(batch, seq // tt),
        in_specs=[tok(D_MODEL), tok(P_WIDTH), per_seq(N_MEM, D_MODEL), per_seq(N_MEM, D_MODEL)]
                 + [_resident(w) for w in weights],
        out_specs=[tok(D_MODEL), per_seq(SUBLANES, RG_WIDTH), per_seq(GLA_KEY_WIDTH, GLA_DV)],
        out_shape=[jax.ShapeDtypeStruct((batch, seq, D_MODEL), F32),
                   jax.ShapeDtypeStruct((batch, SUBLANES, RG_WIDTH), F32),
                   jax.ShapeDtypeStruct((batch, GLA_KEY_WIDTH, GLA_DV), F32)],
        scratch_shapes=[pltpu.VMEM((SUBLANES + tt, RG_WIDTH), F32),
                        pltpu.VMEM((SUBLANES, RG_WIDTH), F32),
                        pltpu.VMEM((2, PAIR_K, PAIR_V), F32),
                        pltpu.VMEM((GLA_KEY_WIDTH // LANES, tt, LANES), F32),
                        pltpu.VMEM((tt, GLA_WIDTH), F32)],
        compiler_params=_params(("parallel", "arbitrary")),
        name="prompt_tail",
    )(x1, p, kb, vb, *weights)


def _sample_mix(x1, p, econv, eh, s0, weights, seq):
    rows = x1.shape[0]
    nb = ROW_TILE // seq
    row_spec = lambda w: pl.BlockSpec((ROW_TILE, w), lambda i: (i, 0))
    state_spec = pl.BlockSpec((nb, GLA_KEY_WIDTH, GLA_DV), lambda i: (i, 0, 0))
    return pl.pallas_call(
        functools.partial(_sample_mix_kernel, seq=seq),
        grid=(rows // ROW_TILE,),
        in_specs=[row_spec(D_MODEL), row_spec(P_WIDTH), row_spec(RG_WIDTH), row_spec(RG_WIDTH),
                  state_spec] + [_resident(w) for w in weights],
        out_specs=[row_spec(D_MODEL), row_spec(D_MODEL),
                   pl.BlockSpec((nb, RG_WIDTH), lambda i: (i, 0)), state_spec],
        out_shape=[jax.ShapeDtypeStruct((rows, D_MODEL), F32),
                   jax.ShapeDtypeStruct((rows, D_MODEL), F32),
                   jax.ShapeDtypeStruct((rows // seq, RG_WIDTH), F32),
                   jax.ShapeDtypeStruct((rows // seq, GLA_KEY_WIDTH, GLA_DV), F32)],
        scratch_shapes=[pltpu.VMEM((RG_WIDTH // LANES, ROW_TILE, LANES), F32),
                        pltpu.VMEM((GLA_KEY_WIDTH // LANES, ROW_TILE, LANES), F32),
                        pltpu.VMEM((ROW_TILE, GLA_WIDTH), F32)],
        compiler_params=_params(("parallel",)),
        name="sample_mix",
    )(x1, p, econv, eh, s0, *weights)


def _sample_attn(q, k, v, seq):
    rows = q.shape[0]
    nb = SAMPLE_ATTN_BATCH
    row_spec = pl.BlockSpec((nb * seq, D_MODEL), lambda i: (i, 0))
    kv_spec = pl.BlockSpec((nb, N_MEM, D_MODEL), lambda i: (i, 0, 0))
    return pl.pallas_call(
        functools.partial(_sample_attn_kernel, seq=seq),
        grid=(rows // (nb * seq),),
        in_specs=[row_spec, kv_spec, kv_spec],
        out_specs=row_spec,
        out_shape=jax.ShapeDtypeStruct((rows, D_MODEL), F32),
        compiler_params=_params(("parallel",)),
        name="sample_attn",
    )(q, k, v)


def _out_tail(x2, o, weights):
    rows = x2.shape[0]
    row_spec = pl.BlockSpec((ROW_TILE, D_MODEL), lambda i: (i, 0))
    return pl.pallas_call(
        _out_tail_kernel,
        grid=(rows // ROW_TILE,),
        in_specs=[row_spec, row_spec] + [_resident(w) for w in weights],
        out_specs=row_spec,
        out_shape=jax.ShapeDtypeStruct((rows, D_MODEL), F32),
        compiler_params=_params(("parallel",)),
        name="out_tail",
    )(x2, o, *weights)


def _block_diag_heads(w):
    h, n, _ = w.shape
    eye = jnp.eye(h, dtype=w.dtype)
    return (eye[:, None, :, None] * w[:, :, None, :]).reshape(h * n, h * n)


def kernel(x_prompt, x_sample, cache_mem_k, cache_mem_v, state_conv, state_rglru, state_gla, mem_prompt, ffn1_norm, ffn1_w_gate, ffn1_w_up, ffn1_w_down, mix_norm, w_in, conv_w, conv_b, rg_w_a, rg_b_a, rg_w_x, rg_b_x, rg_lambda, rg_out_norm, gla_w_a2, gla_b_a2, gla_out_norm, w_out, xattn_norm, mem_norm, w_cq, w_ck, w_cv, w_co, ffn2_norm, ffn2_w_gate, ffn2_w_up, ffn2_w_down, final_norm):
    bp, tp, _ = x_prompt.shape
    bs, ts, _ = x_sample.shape
    row = lambda g: g.reshape(1, -1)
    bf = lambda w: w.astype(BF16)

    w_in_p = bf(jnp.pad(w_in, ((0, 0), (0, P_WIDTH - D_IN))))
    hp = RG_HEADS // 2
    w_gate_rg = bf(jnp.stack([
        jnp.concatenate([_block_diag_heads(rg_w_a[c * hp:(c + 1) * hp]),
                         _block_diag_heads(rg_w_x[c * hp:(c + 1) * hp])], axis=1)
        for c in range(2)]))
    w_a2_p = bf(jnp.pad(gla_w_a2, ((0, A_PAD - GLA_GATE_RANK), (0, 0))))
    ffn_in_w = (row(ffn1_norm), bf(ffn1_w_gate), bf(ffn1_w_up), bf(ffn1_w_down), row(mix_norm), w_in_p)
    mixer_w = (conv_w, row(conv_b), w_gate_rg, row(rg_b_a), row(rg_b_x), row(rg_lambda),
               row(rg_out_norm), w_a2_p, row(gla_b_a2), row(jnp.tile(gla_out_norm, GLA_HEADS)),
               bf(w_out))
    w_cq_b, w_co_b = bf(w_cq), bf(w_co)
    ffn2_w = (row(ffn2_norm), bf(ffn2_w_gate), bf(ffn2_w_up), bf(ffn2_w_down), row(final_norm))

    mem_k, mem_v, mem_kb, mem_vb = _mem_kv(mem_prompt.reshape(bp * N_MEM, D_MODEL),
                                           (row(mem_norm), bf(w_ck), bf(w_cv)))
    x1_p, p_p = _ffn_in(x_prompt.reshape(bp * tp, D_MODEL), ffn_in_w)
    p_p = p_p.reshape(bp, tp, P_WIDTH)
    y_p, rgh_p, s_p = _prompt_tail(
        x1_p.reshape(bp, tp, D_MODEL), p_p,
        mem_kb.reshape(bp, N_MEM, D_MODEL), mem_vb.reshape(bp, N_MEM, D_MODEL),
        mixer_w + (row(xattn_norm), w_cq_b, w_co_b) + ffn2_w)
    conv_p = p_p[:, tp - (CONV_WIDTH - 1):, OFF_RG_X:OFF_RG_Y]

    x1_s, p_s = _ffn_in(x_sample.reshape(bs * ts, D_MODEL), ffn_in_w)
    econv = jnp.pad(state_conv, ((0, 0), (ts - (CONV_WIDTH - 1), 0), (0, 0))).reshape(bs * ts, RG_WIDTH)
    eh = jnp.pad(state_rglru[:, None, :], ((0, 0), (0, ts - 1), (0, 0))).reshape(bs * ts, RG_WIDTH)
    x2_s, q_s, rgh_s, s_s = _sample_mix(
        x1_s, p_s, econv, eh, state_gla.reshape(bs, GLA_KEY_WIDTH, GLA_DV),
        mixer_w + (row(xattn_norm), w_cq_b), ts)
    o_s = _sample_attn(q_s, cache_mem_k.reshape(bs, N_MEM, D_MODEL),
                       cache_mem_v.reshape(bs, N_MEM, D_MODEL), ts)
    y_s = _out_tail(x2_s, o_s, (w_co_b,) + ffn2_w)
    conv_s = p_s.reshape(bs, ts, P_WIDTH)[:, ts - (CONV_WIDTH - 1):, OFF_RG_X:OFF_RG_Y]

    return (y_p, y_s.reshape(bs, ts, D_MODEL),
            mem_k.reshape(bp, N_MEM, XA_HEADS, XA_HEAD_DIM),
            mem_v.reshape(bp, N_MEM, XA_HEADS, XA_HEAD_DIM),
            conv_p, rgh_p[:, 0, :], s_p.reshape(bp, GLA_HEADS, GLA_DK, GLA_DV),
            conv_s, rgh_s, s_s.reshape(bs, GLA_HEADS, GLA_DK, GLA_DV))
```

```python
import functools

import jax
import jax.numpy as jnp
import numpy as np
from jax import lax
from jax.experimental import pallas as pl
from jax.experimental.pallas import tpu as pltpu

F32 = jnp.float32
BF16 = jnp.bfloat16

D_MODEL = 1024
D_FF = 2816
RG_WIDTH = 512
RG_HEADS = 8
RG_HEAD_DIM = 64
CONV_WIDTH = 4
RG_C = 8.0
GLA_WIDTH = 512
GLA_HEADS = 4
GLA_DV = 128
GLA_DK = 64
GLA_KEY_WIDTH = 256
GLA_GATE_RANK = 16
GLA_GATE_NORMALIZER = 16.0
GLA_CHUNK = 32
N_MEM = 256
XA_HEADS = 4
XA_HEAD_DIM = 256
EPS = 1e-6
MASKED_SCORE = -1e30

OFF_RG_X = 0
OFF_RG_Y = 512
OFF_Q = 1024
OFF_K = 1280
OFF_V = 1536
OFF_G = 2048
OFF_A = 2560
D_IN = 2576

LANES = 128
SUBLANES = 8
A_PAD = LANES
P_WIDTH = OFF_A + A_PAD
PAIR_K = 2 * GLA_DK
PAIR_V = 2 * GLA_DV
VMEM_LIMIT = 56 * 1024 * 1024

ROW_TILE = 256
SAMPLE_ATTN_BATCH = 4


def _rms(x, g):
    return x * lax.rsqrt(jnp.mean(x * x, axis=-1, keepdims=True) + EPS) * g


def _mm(a, w):
    return jnp.dot(a.astype(BF16), w, preferred_element_type=F32)


def _mm_nt(a, b):
    return lax.dot_general(a.astype(BF16), b.astype(BF16), (((1,), (1,)), ((), ())),
                           preferred_element_type=F32)


def _mm_tn(a, b):
    return lax.dot_general(a.astype(BF16), b.astype(BF16), (((0,), (0,)), ((), ())),
                           preferred_element_type=F32)


def _silu(x):
    return x * jax.nn.sigmoid(x)


def _gelu_tanh(x):
    c = np.float32(np.sqrt(2.0 / np.pi))
    return x * (0.5 * (1.0 + jnp.tanh(c * (x + 0.044715 * (x * x * x)))))


def _softplus(x):
    return jnp.maximum(x, 0.0) + jnp.log1p(jnp.exp(-jnp.abs(x)))


def _ffn(x, norm_ref, wg_ref, wu_ref, wd_ref):
    h = _rms(x, norm_ref[...]).astype(BF16)
    g = jnp.dot(h, wg_ref[...], preferred_element_type=F32)
    u = jnp.dot(h, wu_ref[...], preferred_element_type=F32)
    return _mm(_silu(g) * u, wd_ref[...])


def _last_rows(buf_ref, x, seg):
    n = x.shape[0] // seg
    outs = []
    for j in range(x.shape[1] // LANES):
        buf_ref[j] = x[:, j * LANES:(j + 1) * LANES]
        outs.append(buf_ref[j, pl.ds(seg - 1, n, stride=seg), :])
    return jnp.concatenate(outs, axis=1)


def _shift_rows(x, s):
    return pltpu.roll(x, s, axis=0)


def _row_in_segment(shape, seg):
    return lax.broadcasted_iota(jnp.int32, shape, 0) & (seg - 1)


def _segment_cumsum(x, seg):
    pos = _row_in_segment(x.shape, seg)
    s = 1
    while s < seg:
        x = jnp.where(pos >= s, x + _shift_rows(x, s), x)
        s *= 2
    return x


def _segment_affine_scan(a, u, seg):
    pos = _row_in_segment(a.shape, seg)
    s = 1
    while s < seg:
        m = pos >= s
        u = jnp.where(m, a * _shift_rows(u, s) + u, u)
        a = jnp.where(m, a * _shift_rows(a, s), a)
        s *= 2
    return a, u


def _conv(xb, sh1, sh2, sh3, cw_ref, cb_ref):
    y = cb_ref[...] + sh3 * cw_ref[0:1, :]
    y = y + sh2 * cw_ref[1:2, :]
    y = y + sh1 * cw_ref[2:3, :]
    return y + xb * cw_ref[3:4, :]


def _rg_gates(xc, wgate_ref, ba_ref, bx_ref):
    half = RG_WIDTH // 2
    r, i = [], []
    for c in range(2):
        z = _mm(xc[:, c * half:(c + 1) * half], wgate_ref[c])
        r.append(z[:, :half])
        i.append(z[:, half:])
    r = jax.nn.sigmoid(jnp.concatenate(r, axis=1) + ba_ref[...])
    i = jax.nn.sigmoid(jnp.concatenate(i, axis=1) + bx_ref[...])
    return r, i


def _rg_decay_input(xc, r, i, lam_ref):
    log_a = (-RG_C * _softplus(-lam_ref[...])) * r
    a = jnp.exp(log_a)
    mult = jnp.sqrt(-jnp.tanh(log_a) * (a * a + 1.0))
    return a, mult * (i * xc)


def _gla_log_decay(a_low, wa2_ref, ba2_ref):
    z = _mm(a_low, wa2_ref[...]) + ba2_ref[...]
    return (-_softplus(-z)) / GLA_GATE_NORMALIZER


def _gla_intra(qt, kt, v, chunk):
    t = qt.shape[0]
    shift = int(np.log2(chunk))
    ri = lax.broadcasted_iota(jnp.int32, (t, t), 0)
    ci = lax.broadcasted_iota(jnp.int32, (t, t), 1)
    causal = (ri >= ci) & ((ri >> shift) == (ci >> shift))
    lane = lax.broadcasted_iota(jnp.int32, (t, PAIR_K), 1)
    outs = []
    for h in range(GLA_HEADS):
        p = h // 2
        qp = qt[:, p * PAIR_K:(p + 1) * PAIR_K]
        kp = kt[:, p * PAIR_K:(p + 1) * PAIR_K]
        mine = (lane >= GLA_DK) if (h % 2) else (lane < GLA_DK)
        s = _mm_nt(jnp.where(mine, qp, 0.0), kp)
        attn = jnp.where(causal, s, 0.0)
        outs.append(_mm(attn, v[:, h * GLA_DV:(h + 1) * GLA_DV].astype(BF16)))
    return outs


def _pair_blockdiag_mask():
    r = lax.broadcasted_iota(jnp.int32, (PAIR_K, PAIR_V), 0)
    c = lax.broadcasted_iota(jnp.int32, (PAIR_K, PAIR_V), 1)
    return (r < GLA_DK) == (c < GLA_DV)


def _gla_out(o, g, norm_ref):
    outs = []
    for h in range(GLA_HEADS):
        sl = slice(h * GLA_DV, (h + 1) * GLA_DV)
        outs.append(_rms(o[:, sl], norm_ref[:, sl]))
    return jnp.concatenate(outs, axis=1) * _silu(g)


def _softmax_rows(s):
    m = jnp.max(s, axis=-1, keepdims=True)
    e = jnp.exp(s - m)
    return e * (1.0 / jnp.sum(e, axis=-1, keepdims=True))


def _split_p(p_ref):
    xb = p_ref[:, OFF_RG_X:OFF_RG_Y]
    yb = p_ref[:, OFF_RG_Y:OFF_Q]
    q = p_ref[:, OFF_Q:OFF_K]
    k = p_ref[:, OFF_K:OFF_V]
    v = p_ref[:, OFF_V:OFF_G]
    g = p_ref[:, OFF_G:OFF_A]
    a_low = p_ref[:, OFF_A:P_WIDTH]
    return xb, yb, q, k, v, g, a_low


def _gla_prepare(q, k, log_a, chunk):
    b = _segment_cumsum(log_a, chunk)
    qt = (q * (GLA_DK ** -0.5)) * jnp.exp(b)
    kt = k * jnp.exp(-b)
    return b, qt, kt


def _mix_project(rg_out, gla_out, wout_ref):
    return (_mm(rg_out, wout_ref[0:RG_WIDTH, :]) + _mm(gla_out, wout_ref[RG_WIDTH:, :]))


def _ffn_in_kernel(x_ref, n1_ref, wg_ref, wu_ref, wd_ref, n2_ref, win_ref, x1_ref, p_ref):
    x = x_ref[...]
    x1 = x + 0.5 * _ffn(x, n1_ref, wg_ref, wu_ref, wd_ref)
    x1_ref[...] = x1
    p_ref[...] = _mm(_rms(x1, n2_ref[...]), win_ref[...])


def _mem_kv_kernel(m_ref, n_ref, wk_ref, wv_ref, k_ref, v_ref, kb_ref, vb_ref):
    h = _rms(m_ref[...], n_ref[...]).astype(BF16)
    k = jnp.dot(h, wk_ref[...], preferred_element_type=F32)
    v = jnp.dot(h, wv_ref[...], preferred_element_type=F32)
    for h in range(XA_HEADS):
        sl = slice(h * XA_HEAD_DIM, (h + 1) * XA_HEAD_DIM)
        k_ref[:, h, :] = k[:, sl]
        v_ref[:, h, :] = v[:, sl]
    kb_ref[...] = k.astype(BF16)
    vb_ref[...] = v.astype(BF16)


def _prompt_tail_kernel(
        x1_ref, p_ref, kb_ref, vb_ref,
        cw_ref, cb_ref, wgate_ref, ba_ref, bx_ref, lam_ref, rgn_ref,
        wa2_ref, ba2_ref, glan_ref, wout_ref,
        xn_ref, wcq_ref, wco_ref, n2_ref, wg_ref, wu_ref, wd_ref, fn_ref,
        y_ref, rgh_ref, s_ref,
        xpad_ref, h_ref, sbd_ref, b_ref, o_ref):
    t_idx = pl.program_id(1)
    tt = x1_ref.shape[0]
    n_chunks = tt // GLA_CHUNK

    @pl.when(t_idx == 0)
    def _():
        xpad_ref[0:SUBLANES, :] = jnp.zeros((SUBLANES, RG_WIDTH), F32)
        h_ref[...] = jnp.zeros_like(h_ref)
        sbd_ref[...] = jnp.zeros_like(sbd_ref)

    xb, yb, q, k, v, g, a_low = _split_p(p_ref)

    xpad_ref[SUBLANES:SUBLANES + tt, :] = xb
    sh1 = xpad_ref[SUBLANES - 1:SUBLANES - 1 + tt, :]
    sh2 = xpad_ref[SUBLANES - 2:SUBLANES - 2 + tt, :]
    sh3 = xpad_ref[SUBLANES - 3:SUBLANES - 3 + tt, :]
    xc = _conv(xb, sh1, sh2, sh3, cw_ref, cb_ref)
    xpad_ref[0:SUBLANES, :] = xb[tt - SUBLANES:tt, :]
    r, i = _rg_gates(xc, wgate_ref, ba_ref, bx_ref)
    a, u = _rg_decay_input(xc, r, i, lam_ref)
    a_cum, h_zero = _segment_affine_scan(a, u, tt)
    hs = h_zero + a_cum * h_ref[0:1, :]
    h_last = hs[tt - 1:tt, :]
    h_ref[...] = jnp.broadcast_to(h_last, h_ref.shape)
    rg_out = _rms(hs * _gelu_tanh(yb), rgn_ref[...])

    log_a = _gla_log_decay(a_low, wa2_ref, ba2_ref)
    b, qt, kt = _gla_prepare(q, k, log_a, GLA_CHUNK)
    intra = _gla_intra(qt, kt, v, GLA_CHUNK)
    for h in range(GLA_HEADS):
        o_ref[:, h * GLA_DV:(h + 1) * GLA_DV] = intra[h]
    b_last = _last_rows(b_ref, b, GLA_CHUNK)
    dec_t = jnp.exp(b_last).T
    bd_mask = _pair_blockdiag_mask()
    for c in range(n_chunks):
        rows = slice(c * GLA_CHUNK, (c + 1) * GLA_CHUNK)
        for p in range(2):
            kl = slice(p * PAIR_K, (p + 1) * PAIR_K)
            vl = slice(p * PAIR_V, (p + 1) * PAIR_V)
            s_bd = sbd_ref[p]
            o_ref[rows, vl] += _mm(qt[rows, kl], s_bd.astype(BF16))
            kd = k[rows, kl] * jnp.exp(b_last[c:c + 1, kl] - b[rows, kl])
            kv = _mm_tn(kd, v[rows, vl])
            sbd_ref[p] = s_bd * dec_t[kl, c:c + 1] + jnp.where(bd_mask, kv, 0.0)
    gla_out = _gla_out(o_ref[...], g, glan_ref)

    x2 = x1_ref[...] + _mix_project(rg_out, gla_out, wout_ref)

    qx = _mm(_rms(x2, xn_ref[...]), wcq_ref[...])
    x3 = x2
    for h in range(XA_HEADS):
        sl = slice(h * XA_HEAD_DIM, (h + 1) * XA_HEAD_DIM)
        s = _mm_nt(qx[:, sl], kb_ref[:, sl]) * (XA_HEAD_DIM ** -0.5)
        oh = _mm(_softmax_rows(s), vb_ref[:, sl])
        x3 = x3 + _mm(oh, wco_ref[sl, :])

    x4 = x3 + 0.5 * _ffn(x3, n2_ref, wg_ref, wu_ref, wd_ref)
    y_ref[...] = _rms(x4, fn_ref[...])

    @pl.when(t_idx == pl.num_programs(1) - 1)
    def _():
        rgh_ref[...] = h_ref[...]
        row = lax.broadcasted_iota(jnp.int32, (PAIR_K, GLA_DV), 0)
        for p in range(2):
            s_bd = sbd_ref[p]
            s_ref[p * PAIR_K:(p + 1) * PAIR_K, :] = jnp.where(
                row < GLA_DK, s_bd[:, :GLA_DV], s_bd[:, GLA_DV:])


def _sample_mix_kernel(
        x1_ref, p_ref, econv_ref, eh_ref, s0_ref,
        cw_ref, cb_ref, wgate_ref, ba_ref, bx_ref, lam_ref, rgn_ref,
        wa2_ref, ba2_ref, glan_ref, wout_ref, xn_ref, wcq_ref,
        x2_ref, q_ref, rgh_ref, s_ref,
        hs_ref, b_ref, o_ref, *, seq):
    rows_n = x1_ref.shape[0]
    nb = rows_n // seq

    xb, yb, q, k, v, g, a_low = _split_p(p_ref)

    pos = _row_in_segment(xb.shape, seq)
    econv = econv_ref[...]
    sh = []
    for j in range(1, CONV_WIDTH):
        sh.append(jnp.where(pos >= j, _shift_rows(xb, j), _shift_rows(econv, rows_n - seq + j)))
    xc = _conv(xb, sh[0], sh[1], sh[2], cw_ref, cb_ref)
    r, i = _rg_gates(xc, wgate_ref, ba_ref, bx_ref)
    a, u = _rg_decay_input(xc, r, i, lam_ref)
    u = u + a * eh_ref[...]
    _, hs = _segment_affine_scan(a, u, seq)
    rgh_ref[...] = _last_rows(hs_ref, hs, seq)
    rg_out = _rms(hs * _gelu_tanh(yb), rgn_ref[...])

    log_a = _gla_log_decay(a_low, wa2_ref, ba2_ref)
    b, qt, kt = _gla_prepare(q, k, log_a, seq)
    intra = _gla_intra(qt, kt, v, seq)
    for h in range(GLA_HEADS):
        o_ref[:, h * GLA_DV:(h + 1) * GLA_DV] = intra[h]
    b_last = _last_rows(b_ref, b, seq)
    dec_t = jnp.exp(b_last).T
    bd_mask = _pair_blockdiag_mask()
    row = lax.broadcasted_iota(jnp.int32, (PAIR_K, GLA_DV), 0)
    top = row < GLA_DK
    for c in range(nb):
        rows = slice(c * seq, (c + 1) * seq)
        for p in range(2):
            kl = slice(p * PAIR_K, (p + 1) * PAIR_K)
            vl = slice(p * PAIR_V, (p + 1) * PAIR_V)
            s_pair = s0_ref[c, kl, :]
            s_bd = jnp.concatenate([jnp.where(top, s_pair, 0.0), jnp.where(top, 0.0, s_pair)], axis=1)
            o_ref[rows, vl] += _mm(qt[rows, kl], s_bd.astype(BF16))
            kd = k[rows, kl] * jnp.exp(b_last[c:c + 1, kl] - b[rows, kl])
            kv = _mm_tn(kd, v[rows, vl])
            s_new = s_bd * dec_t[kl, c:c + 1] + jnp.where(bd_mask, kv, 0.0)
            s_ref[c, kl, :] = jnp.where(top, s_new[:, :GLA_DV], s_new[:, GLA_DV:])
    gla_out = _gla_out(o_ref[...], g, glan_ref)

    x2 = x1_ref[...] + _mix_project(rg_out, gla_out, wout_ref)
    x2_ref[...] = x2
    q_ref[...] = _mm(_rms(x2, xn_ref[...]), wcq_ref[...])


def _sample_attn_kernel(q_ref, k_ref, v_ref, o_ref, *, seq):
    tiles = XA_HEAD_DIM // LANES
    group = tiles * XA_HEADS
    r = N_MEM * group
    nb = k_ref.shape[0] // r
    lane = lax.broadcasted_iota(jnp.int32, (XA_HEADS * seq, r), 1)
    head = lax.broadcasted_iota(jnp.int32, (XA_HEADS * seq, r), 0) // seq
    own = (lane & (group - 1)) == head
    for j in range(nb):
        q = q_ref[j * seq:(j + 1) * seq, :]
        q_parts = jnp.concatenate(
            [q[:, c * LANES:(c + 1) * LANES] for c in range(XA_HEADS * tiles)], axis=0)
        part = _mm_nt(q_parts, k_ref[j * r:(j + 1) * r, :])
        s = []
        for h in range(XA_HEADS):
            acc = part[h * tiles * seq:(h * tiles + 1) * seq, :]
            for c in range(1, tiles):
                blk = part[(h * tiles + c) * seq:(h * tiles + c + 1) * seq, :]
                acc = acc + pltpu.roll(blk, r - c * XA_HEADS, axis=1)
            s.append(acc)
        s = jnp.concatenate(s, axis=0) * (XA_HEAD_DIM ** -0.5)
        p = _softmax_rows(jnp.where(own, s, MASKED_SCORE))
        p_all = jnp.concatenate(
            [p] + [pltpu.roll(p, c * XA_HEADS, axis=1) for c in range(1, tiles)], axis=0)
        o = _mm(p_all, v_ref[j * r:(j + 1) * r, :].astype(BF16))
        for c in range(tiles):
            for h in range(XA_HEADS):
                col = h * XA_HEAD_DIM + c * LANES
                o_ref[j * seq:(j + 1) * seq, col:col + LANES] = (
                    o[(c * XA_HEADS + h) * seq:(c * XA_HEADS + h + 1) * seq, :])


def _out_tail_kernel(x2_ref, o_ref, wco_ref, n2_ref, wg_ref, wu_ref, wd_ref, fn_ref, y_ref):
    x3 = x2_ref[...] + _mm(o_ref[...], wco_ref[...])
    x4 = x3 + 0.5 * _ffn(x3, n2_ref, wg_ref, wu_ref, wd_ref)
    y_ref[...] = _rms(x4, fn_ref[...])


def _resident(arr):
    nd = arr.ndim
    return pl.BlockSpec(arr.shape, lambda *_: (0,) * nd, pipeline_mode=pl.Buffered(1))


def _params(sem):
    return pltpu.CompilerParams(dimension_semantics=sem, vmem_limit_bytes=VMEM_LIMIT)


def _ffn_in(x, weights):
    rows = x.shape[0]
    row_spec = lambda w: pl.BlockSpec((ROW_TILE, w), lambda i: (i, 0))
    return pl.pallas_call(
        _ffn_in_kernel,
        grid=(rows // ROW_TILE,),
        in_specs=[row_spec(D_MODEL)] + [_resident(w) for w in weights],
        out_specs=[row_spec(D_MODEL), row_spec(P_WIDTH)],
        out_shape=[jax.ShapeDtypeStruct((rows, D_MODEL), F32),
                   jax.ShapeDtypeStruct((rows, P_WIDTH), F32)],
        compiler_params=_params(("parallel",)),
        name="ffn_in",
    )(x, *weights)


def _mem_kv(mem, weights):
    rows = mem.shape[0]
    row_spec = pl.BlockSpec((ROW_TILE, D_MODEL), lambda i: (i, 0))
    head_spec = pl.BlockSpec((ROW_TILE, XA_HEADS, XA_HEAD_DIM), lambda i: (i, 0, 0))
    return pl.pallas_call(
        _mem_kv_kernel,
        grid=(rows // ROW_TILE,),
        in_specs=[row_spec] + [_resident(w) for w in weights],
        out_specs=[head_spec] * 2 + [row_spec] * 2,
        out_shape=[jax.ShapeDtypeStruct((rows, XA_HEADS, XA_HEAD_DIM), F32)] * 2
                  + [jax.ShapeDtypeStruct((rows, D_MODEL), BF16)] * 2,
        compiler_params=_params(("parallel",)),
        name="mem_kv",
    )(mem, *weights)


def _prompt_tail(x1, p, kb, vb, weights):
    batch, seq, _ = x1.shape
    tt = ROW_TILE
    tok = lambda w: pl.BlockSpec((None, tt, w), lambda b, t: (b, t, 0))
    per_seq = lambda r, w: pl.BlockSpec((None, r, w), lambda b, t: (b, 0, 0))
    return pl.pallas_call(
        _prompt_tail_kernel,
        grid=(batch, seq // tt),
        in_specs=[tok(D_MODEL), tok(P_WIDTH), per_seq(N_MEM, D_MODEL), per_seq(N_MEM, D_MODEL)]
                 + [_resident(w) for w in weights],
        out_specs=[tok(D_MODEL), per_seq(SUBLANES, RG_WIDTH), per_seq(GLA_KEY_WIDTH, GLA_DV)],
        out_shape=[jax.ShapeDtypeStruct((batch, seq, D_MODEL), F32),
                   jax.ShapeDtypeStruct((batch, SUBLANES, RG_WIDTH), F32),
                   jax.ShapeDtypeStruct((batch, GLA_KEY_WIDTH, GLA_DV), F32)],
        scratch_shapes=[pltpu.VMEM((SUBLANES + tt, RG_WIDTH), F32),
                        pltpu.VMEM((SUBLANES, RG_WIDTH), F32),
                        pltpu.VMEM((2, PAIR_K, PAIR_V), F32),
                        pltpu.VMEM((GLA_KEY_WIDTH // LANES, tt, LANES), F32),
                        pltpu.VMEM((tt, GLA_WIDTH), F32)],
        compiler_params=_params(("parallel", "arbitrary")),
        name="prompt_tail",
    )(x1, p, kb, vb, *weights)


def _sample_mix(x1, p, econv, eh, s0, weights, seq):
    rows = x1.shape[0]
    nb = ROW_TILE // seq
    row_spec = lambda w: pl.BlockSpec((ROW_TILE, w), lambda i: (i, 0))
    state_spec = pl.BlockSpec((nb, GLA_KEY_WIDTH, GLA_DV), lambda i: (i, 0, 0))
    return pl.pallas_call(
        functools.partial(_sample_mix_kernel, seq=seq),
        grid=(rows // ROW_TILE,),
        in_specs=[row_spec(D_MODEL), row_spec(P_WIDTH), row_spec(RG_WIDTH), row_spec(RG_WIDTH),
                  state_spec] + [_resident(w) for w in weights],
        out_specs=[row_spec(D_MODEL), row_spec(D_MODEL),
                   pl.BlockSpec((nb, RG_WIDTH), lambda i: (i, 0)), state_spec],
        out_shape=[jax.ShapeDtypeStruct((rows, D_MODEL), F32),
                   jax.ShapeDtypeStruct((rows, D_MODEL), F32),
                   jax.ShapeDtypeStruct((rows // seq, RG_WIDTH), F32),
                   jax.ShapeDtypeStruct((rows // seq, GLA_KEY_WIDTH, GLA_DV), F32)],
        scratch_shapes=[pltpu.VMEM((RG_WIDTH // LANES, ROW_TILE, LANES), F32),
                        pltpu.VMEM((GLA_KEY_WIDTH // LANES, ROW_TILE, LANES), F32),
                        pltpu.VMEM((ROW_TILE, GLA_WIDTH), F32)],
        compiler_params=_params(("parallel",)),
        name="sample_mix",
    )(x1, p, econv, eh, s0, *weights)


def _head_interleaved_rows(x):
    b, m, h, dh = x.shape
    tiles = dh // LANES
    return (x.reshape(b, m, h, tiles, LANES).transpose(0, 1, 3, 2, 4)
            .reshape(b * m * tiles * h, LANES))


def _sample_attn(q, k, v, seq):
    rows = q.shape[0]
    nb = SAMPLE_ATTN_BATCH
    row_spec = pl.BlockSpec((nb * seq, D_MODEL), lambda i: (i, 0))
    kv_rows = k.shape[0] // (rows // seq)
    kv_spec = pl.BlockSpec((nb * kv_rows, LANES), lambda i: (i, 0))
    return pl.pallas_call(
        functools.partial(_sample_attn_kernel, seq=seq),
        grid=(rows // (nb * seq),),
        in_specs=[row_spec, kv_spec, kv_spec],
        out_specs=row_spec,
        out_shape=jax.ShapeDtypeStruct((rows, D_MODEL), F32),
        compiler_params=_params(("parallel",)),
        name="sample_attn",
    )(q, k, v)


def _out_tail(x2, o, weights):
    rows = x2.shape[0]
    row_spec = pl.BlockSpec((ROW_TILE, D_MODEL), lambda i: (i, 0))
    return pl.pallas_call(
        _out_tail_kernel,
        grid=(rows // ROW_TILE,),
        in_specs=[row_spec, row_spec] + [_resident(w) for w in weights],
        out_specs=row_spec,
        out_shape=jax.ShapeDtypeStruct((rows, D_MODEL), F32),
        compiler_params=_params(("parallel",)),
        name="out_tail",
    )(x2, o, *weights)


def _block_diag_heads(w):
    h, n, _ = w.shape
    eye = jnp.eye(h, dtype=w.dtype)
    return (eye[:, None, :, None] * w[:, :, None, :]).reshape(h * n, h * n)


def kernel(x_prompt, x_sample, cache_mem_k, cache_mem_v, state_conv, state_rglru, state_gla, mem_prompt, ffn1_norm, ffn1_w_gate, ffn1_w_up, ffn1_w_down, mix_norm, w_in, conv_w, conv_b, rg_w_a, rg_b_a, rg_w_x, rg_b_x, rg_lambda, rg_out_norm, gla_w_a2, gla_b_a2, gla_out_norm, w_out, xattn_norm, mem_norm, w_cq, w_ck, w_cv, w_co, ffn2_norm, ffn2_w_gate, ffn2_w_up, ffn2_w_down, final_norm):
    bp, tp, _ = x_prompt.shape
    bs, ts, _ = x_sample.shape
    row = lambda g: g.reshape(1, -1)
    bf = lambda w: w.astype(BF16)

    w_in_p = bf(jnp.pad(w_in, ((0, 0), (0, P_WIDTH - D_IN))))
    hp = RG_HEADS // 2
    w_gate_rg = bf(jnp.stack([
        jnp.concatenate([_block_diag_heads(rg_w_a[c * hp:(c + 1) * hp]),
                         _block_diag_heads(rg_w_x[c * hp:(c + 1) * hp])], axis=1)
        for c in range(2)]))
    w_a2_p = bf(jnp.pad(gla_w_a2, ((0, A_PAD - GLA_GATE_RANK), (0, 0))))
    ffn_in_w = (row(ffn1_norm), bf(ffn1_w_gate), bf(ffn1_w_up), bf(ffn1_w_down), row(mix_norm), w_in_p)
    mixer_w = (conv_w, row(conv_b), w_gate_rg, row(rg_b_a), row(rg_b_x), row(rg_lambda),
               row(rg_out_norm), w_a2_p, row(gla_b_a2), row(jnp.tile(gla_out_norm, GLA_HEADS)),
               bf(w_out))
    w_cq_b, w_co_b = bf(w_cq), bf(w_co)
    ffn2_w = (row(ffn2_norm), bf(ffn2_w_gate), bf(ffn2_w_up), bf(ffn2_w_down), row(final_norm))

    mem_k, mem_v, mem_kb, mem_vb = _mem_kv(mem_prompt.reshape(bp * N_MEM, D_MODEL),
                                           (row(mem_norm), bf(w_ck), bf(w_cv)))
    x1_p, p_p = _ffn_in(x_prompt.reshape(bp * tp, D_MODEL), ffn_in_w)
    p_p = p_p.reshape(bp, tp, P_WIDTH)
    y_p, rgh_p, s_p = _prompt_tail(
        x1_p.reshape(bp, tp, D_MODEL), p_p,
        mem_kb.reshape(bp, N_MEM, D_MODEL), mem_vb.reshape(bp, N_MEM, D_MODEL),
        mixer_w + (row(xattn_norm), w_cq_b, w_co_b) + ffn2_w)
    conv_p = p_p[:, tp - (CONV_WIDTH - 1):, OFF_RG_X:OFF_RG_Y]

    x1_s, p_s = _ffn_in(x_sample.reshape(bs * ts, D_MODEL), ffn_in_w)
    econv = jnp.pad(state_conv, ((0, 0), (ts - (CONV_WIDTH - 1), 0), (0, 0))).reshape(bs * ts, RG_WIDTH)
    eh = jnp.pad(state_rglru[:, None, :], ((0, 0), (0, ts - 1), (0, 0))).reshape(bs * ts, RG_WIDTH)
    x2_s, q_s, rgh_s, s_s = _sample_mix(
        x1_s, p_s, econv, eh, state_gla.reshape(bs, GLA_KEY_WIDTH, GLA_DV),
        mixer_w + (row(xattn_norm), w_cq_b), ts)
    o_s = _sample_attn(q_s, _head_interleaved_rows(cache_mem_k),
                       _head_interleaved_rows(cache_mem_v), ts)
    y_s = _out_tail(x2_s, o_s, (w_co_b,) + ffn2_w)
    conv_s = p_s.reshape(bs, ts, P_WIDTH)[:, ts - (CONV_WIDTH - 1):, OFF_RG_X:OFF_RG_Y]

    return (y_p, y_s.reshape(bs, ts, D_MODEL),
            mem_k.reshape(bp, N_MEM, XA_HEADS, XA_HEAD_DIM),
            mem_v.reshape(bp, N_MEM, XA_HEADS, XA_HEAD_DIM),
            conv_p, rgh_p[:, 0, :], s_p.reshape(bp, GLA_HEADS, GLA_DK, GLA_DV),
            conv_s, rgh_s, s_s.reshape(bs, GLA_HEADS, GLA_DK, GLA_DV))
```

```python
import functools

import jax
import jax.numpy as jnp
import numpy as np
from jax import lax
from jax.experimental import pallas as pl
from jax.experimental.pallas import tpu as pltpu

F32 = jnp.float32
BF16 = jnp.bfloat16

D_MODEL = 1024
D_FF = 2816
RG_WIDTH = 512
RG_HEADS = 8
RG_HEAD_DIM = 64
CONV_WIDTH = 4
RG_C = 8.0
GLA_WIDTH = 512
GLA_HEADS = 4
GLA_DV = 128
GLA_DK = 64
GLA_KEY_WIDTH = 256
GLA_GATE_RANK = 16
GLA_GATE_NORMALIZER = 16.0
GLA_CHUNK = 32
N_MEM = 256
XA_HEADS = 4
XA_HEAD_DIM = 256
EPS = 1e-6
MASKED_SCORE = -1e30

OFF_RG_X = 0
OFF_RG_Y = 512
OFF_Q = 1024
OFF_K = 1280
OFF_V = 1536
OFF_G = 2048
OFF_A = 2560
D_IN = 2576

LANES = 128
SUBLANES = 8
A_PAD = LANES
P_WIDTH = OFF_A + A_PAD
PAIR_K = 2 * GLA_DK
PAIR_V = 2 * GLA_DV
VMEM_LIMIT = 56 * 1024 * 1024

ROW_TILE = 256
SAMPLE_ATTN_BATCH = 4


def _rms(x, g):
    return x * lax.rsqrt(jnp.mean(x * x, axis=-1, keepdims=True) + EPS) * g


def _mm(a, w):
    return jnp.dot(a.astype(BF16), w, preferred_element_type=F32)


def _mm_nt(a, b):
    return lax.dot_general(a.astype(BF16), b.astype(BF16), (((1,), (1,)), ((), ())),
                           preferred_element_type=F32)


def _mm_tn(a, b):
    return lax.dot_general(a.astype(BF16), b.astype(BF16), (((0,), (0,)), ((), ())),
                           preferred_element_type=F32)


def _silu(x):
    return x * jax.nn.sigmoid(x)


def _gelu_tanh(x):
    c = np.float32(np.sqrt(2.0 / np.pi))
    return x * (0.5 * (1.0 + jnp.tanh(c * (x + 0.044715 * (x * x * x)))))


def _softplus(x):
    return jnp.maximum(x, 0.0) + jnp.log1p(jnp.exp(-jnp.abs(x)))


def _ffn(x, norm_ref, wg_ref, wu_ref, wd_ref):
    h = _rms(x, norm_ref[...]).astype(BF16)
    g = jnp.dot(h, wg_ref[...], preferred_element_type=F32)
    u = jnp.dot(h, wu_ref[...], preferred_element_type=F32)
    return _mm(_silu(g) * u, wd_ref[...])


def _last_rows(buf_ref, x, seg):
    n = x.shape[0] // seg
    outs = []
    for j in range(x.shape[1] // LANES):
        buf_ref[j] = x[:, j * LANES:(j + 1) * LANES]
        outs.append(buf_ref[j, pl.ds(seg - 1, n, stride=seg), :])
    return jnp.concatenate(outs, axis=1)


def _shift_rows(x, s):
    return pltpu.roll(x, s, axis=0)


def _row_in_segment(shape, seg):
    return lax.broadcasted_iota(jnp.int32, shape, 0) & (seg - 1)


def _segment_cumsum(x, seg):
    pos = _row_in_segment(x.shape, seg)
    s = 1
    while s < seg:
        x = jnp.where(pos >= s, x + _shift_rows(x, s), x)
        s *= 2
    return x


def _segment_affine_scan(a, u, seg):
    pos = _row_in_segment(a.shape, seg)
    s = 1
    while s < seg:
        m = pos >= s
        u = jnp.where(m, a * _shift_rows(u, s) + u, u)
        a = jnp.where(m, a * _shift_rows(a, s), a)
        s *= 2
    return a, u


def _conv(xb, sh1, sh2, sh3, cw_ref, cb_ref):
    y = cb_ref[...] + sh3 * cw_ref[0:1, :]
    y = y + sh2 * cw_ref[1:2, :]
    y = y + sh1 * cw_ref[2:3, :]
    return y + xb * cw_ref[3:4, :]


def _rg_gates(xc, wgate_ref, ba_ref, bx_ref):
    half = RG_WIDTH // 2
    r, i = [], []
    for c in range(2):
        z = _mm(xc[:, c * half:(c + 1) * half], wgate_ref[c])
        r.append(z[:, :half])
        i.append(z[:, half:])
    r = jax.nn.sigmoid(jnp.concatenate(r, axis=1) + ba_ref[...])
    i = jax.nn.sigmoid(jnp.concatenate(i, axis=1) + bx_ref[...])
    return r, i


def _rg_decay_input(xc, r, i, lam_ref):
    log_a = (-RG_C * _softplus(-lam_ref[...])) * r
    a = jnp.exp(log_a)
    mult = jnp.sqrt(-jnp.tanh(log_a) * (a * a + 1.0))
    return a, mult * (i * xc)


def _gla_log_decay(a_low, wa2_ref, ba2_ref):
    z = _mm(a_low, wa2_ref[...]) + ba2_ref[...]
    return (-_softplus(-z)) / GLA_GATE_NORMALIZER


def _gla_intra(qt, kt, v, chunk):
    t = qt.shape[0]
    shift = int(np.log2(chunk))
    ri = lax.broadcasted_iota(jnp.int32, (t, t), 0)
    ci = lax.broadcasted_iota(jnp.int32, (t, t), 1)
    causal = (ri >= ci) & ((ri >> shift) == (ci >> shift))
    lane = lax.broadcasted_iota(jnp.int32, (t, PAIR_K), 1)
    outs = []
    for h in range(GLA_HEADS):
        p = h // 2
        qp = qt[:, p * PAIR_K:(p + 1) * PAIR_K]
        kp = kt[:, p * PAIR_K:(p + 1) * PAIR_K]
        mine = (lane >= GLA_DK) if (h % 2) else (lane < GLA_DK)
        s = _mm_nt(jnp.where(mine, qp, 0.0), kp)
        attn = jnp.where(causal, s, 0.0)
        outs.append(_mm(attn, v[:, h * GLA_DV:(h + 1) * GLA_DV].astype(BF16)))
    return outs


def _pair_blockdiag_mask():
    r = lax.broadcasted_iota(jnp.int32, (PAIR_K, PAIR_V), 0)
    c = lax.broadcasted_iota(jnp.int32, (PAIR_K, PAIR_V), 1)
    return (r < GLA_DK) == (c < GLA_DV)


def _gla_chunk_updates(k, v, b, b_last, chunk):
    bd_mask = _pair_blockdiag_mask()
    kv = {}
    for c in range(k.shape[0] // chunk):
        rows = slice(c * chunk, (c + 1) * chunk)
        for p in range(2):
            kl = slice(p * PAIR_K, (p + 1) * PAIR_K)
            kd = k[rows, kl] * jnp.exp(b_last[c:c + 1, kl] - b[rows, kl])
            kv[c, p] = jnp.where(bd_mask, _mm_tn(kd, v[rows, p * PAIR_V:(p + 1) * PAIR_V]), 0.0)
    return kv


def _gla_combine(intra, qt, states, chunk):
    n_chunks = qt.shape[0] // chunk
    heads = []
    for p in range(2):
        kl = slice(p * PAIR_K, (p + 1) * PAIR_K)
        inter = jnp.concatenate(
            [_mm(qt[c * chunk:(c + 1) * chunk, kl], states[c, p].astype(BF16)) for c in range(n_chunks)],
            axis=0)
        heads.append(intra[2 * p] + inter[:, :GLA_DV])
        heads.append(intra[2 * p + 1] + inter[:, GLA_DV:])
    return heads


def _gla_out(o_heads, g, norm_ref):
    outs = []
    for h in range(GLA_HEADS):
        outs.append(_rms(o_heads[h], norm_ref[:, h * GLA_DV:(h + 1) * GLA_DV]))
    return jnp.concatenate(outs, axis=1) * _silu(g)


def _softmax_rows(s):
    m = jnp.max(s, axis=-1, keepdims=True)
    e = jnp.exp(s - m)
    return e * (1.0 / jnp.sum(e, axis=-1, keepdims=True))


def _split_p(p_ref):
    xb = p_ref[:, OFF_RG_X:OFF_RG_Y]
    yb = p_ref[:, OFF_RG_Y:OFF_Q]
    q = p_ref[:, OFF_Q:OFF_K]
    k = p_ref[:, OFF_K:OFF_V]
    v = p_ref[:, OFF_V:OFF_G]
    g = p_ref[:, OFF_G:OFF_A]
    a_low = p_ref[:, OFF_A:P_WIDTH]
    return xb, yb, q, k, v, g, a_low


def _gla_prepare(q, k, log_a, chunk):
    b = _segment_cumsum(log_a, chunk)
    qt = (q * (GLA_DK ** -0.5)) * jnp.exp(b)
    kt = k * jnp.exp(-b)
    return b, qt, kt


def _mix_project(rg_out, gla_out, wout_ref):
    return (_mm(rg_out, wout_ref[0:RG_WIDTH, :]) + _mm(gla_out, wout_ref[RG_WIDTH:, :]))


def _ffn_in_kernel(x_ref, n1_ref, wg_ref, wu_ref, wd_ref, n2_ref, win_ref, x1_ref, p_ref):
    x = x_ref[...]
    x1 = x + 0.5 * _ffn(x, n1_ref, wg_ref, wu_ref, wd_ref)
    x1_ref[...] = x1
    p_ref[...] = _mm(_rms(x1, n2_ref[...]), win_ref[...])


def _mem_kv_kernel(m_ref, n_ref, wk_ref, wv_ref, k_ref, v_ref, kb_ref, vb_ref):
    h = _rms(m_ref[...], n_ref[...]).astype(BF16)
    k = jnp.dot(h, wk_ref[...], preferred_element_type=F32)
    v = jnp.dot(h, wv_ref[...], preferred_element_type=F32)
    for h in range(XA_HEADS):
        sl = slice(h * XA_HEAD_DIM, (h + 1) * XA_HEAD_DIM)
        k_ref[:, h, :] = k[:, sl]
        v_ref[:, h, :] = v[:, sl]
    kb_ref[...] = k.astype(BF16)
    vb_ref[...] = v.astype(BF16)


def _prompt_tail_kernel(
        x1_ref, p_ref, kb_ref, vb_ref,
        cw_ref, cb_ref, wgate_ref, ba_ref, bx_ref, lam_ref, rgn_ref,
        wa2_ref, ba2_ref, glan_ref, wout_ref,
        xn_ref, wcq_ref, wco_ref, n2_ref, wg_ref, wu_ref, wd_ref, fn_ref,
        y_ref, rgh_ref, s_ref,
        xpad_ref, h_ref, sbd_ref, b_ref):
    t_idx = pl.program_id(1)
    tt = x1_ref.shape[0]
    n_chunks = tt // GLA_CHUNK

    @pl.when(t_idx == 0)
    def _():
        xpad_ref[0:SUBLANES, :] = jnp.zeros((SUBLANES, RG_WIDTH), F32)
        h_ref[...] = jnp.zeros_like(h_ref)
        sbd_ref[...] = jnp.zeros_like(sbd_ref)

    xb, yb, q, k, v, g, a_low = _split_p(p_ref)

    xpad_ref[SUBLANES:SUBLANES + tt, :] = xb
    sh1 = xpad_ref[SUBLANES - 1:SUBLANES - 1 + tt, :]
    sh2 = xpad_ref[SUBLANES - 2:SUBLANES - 2 + tt, :]
    sh3 = xpad_ref[SUBLANES - 3:SUBLANES - 3 + tt, :]
    xc = _conv(xb, sh1, sh2, sh3, cw_ref, cb_ref)
    xpad_ref[0:SUBLANES, :] = xb[tt - SUBLANES:tt, :]
    r, i = _rg_gates(xc, wgate_ref, ba_ref, bx_ref)
    a, u = _rg_decay_input(xc, r, i, lam_ref)
    a_grp, h_grp = _segment_affine_scan(a, u, SUBLANES)
    carry = h_ref[0:1, :]
    groups = []
    for gi in range(tt // SUBLANES):
        rows = slice(gi * SUBLANES, (gi + 1) * SUBLANES)
        groups.append(h_grp[rows, :] + a_grp[rows, :] * carry)
        carry = groups[-1][SUBLANES - 1:SUBLANES, :]
    hs = jnp.concatenate(groups, axis=0)
    h_ref[...] = jnp.broadcast_to(carry, h_ref.shape)
    rg_out = _rms(hs * _gelu_tanh(yb), rgn_ref[...])

    log_a = _gla_log_decay(a_low, wa2_ref, ba2_ref)
    b, qt, kt = _gla_prepare(q, k, log_a, GLA_CHUNK)
    intra = _gla_intra(qt, kt, v, GLA_CHUNK)
    b_last = _last_rows(b_ref, b, GLA_CHUNK)
    dec_t = jnp.exp(b_last).T
    kv = _gla_chunk_updates(k, v, b, b_last, GLA_CHUNK)
    states = {}
    for p in range(2):
        s_bd = sbd_ref[p]
        for c in range(n_chunks):
            states[c, p] = s_bd
            s_bd = s_bd * dec_t[p * PAIR_K:(p + 1) * PAIR_K, c:c + 1] + kv[c, p]
        sbd_ref[p] = s_bd
    gla_out = _gla_out(_gla_combine(intra, qt, states, GLA_CHUNK), g, glan_ref)

    x2 = x1_ref[...] + _mix_project(rg_out, gla_out, wout_ref)

    qx = _mm(_rms(x2, xn_ref[...]), wcq_ref[...])
    head = lambda h: slice(h * XA_HEAD_DIM, (h + 1) * XA_HEAD_DIM)
    s = jnp.concatenate([_mm_nt(qx[:, head(h)], kb_ref[:, head(h)]) for h in range(XA_HEADS)], axis=0)
    pr = _softmax_rows(s * (XA_HEAD_DIM ** -0.5))
    o = jnp.concatenate([_mm(pr[h * tt:(h + 1) * tt, :], vb_ref[:, head(h)]) for h in range(XA_HEADS)], axis=1)
    x3 = x2 + _mm(o, wco_ref[...])

    x4 = x3 + 0.5 * _ffn(x3, n2_ref, wg_ref, wu_ref, wd_ref)
    y_ref[...] = _rms(x4, fn_ref[...])

    @pl.when(t_idx == pl.num_programs(1) - 1)
    def _():
        rgh_ref[...] = h_ref[...]
        row = lax.broadcasted_iota(jnp.int32, (PAIR_K, GLA_DV), 0)
        for p in range(2):
            s_bd = sbd_ref[p]
            s_ref[p * PAIR_K:(p + 1) * PAIR_K, :] = jnp.where(
                row < GLA_DK, s_bd[:, :GLA_DV], s_bd[:, GLA_DV:])


def _sample_mix_kernel(
        x1_ref, p_ref, econv_ref, eh_ref, s0_ref,
        cw_ref, cb_ref, wgate_ref, ba_ref, bx_ref, lam_ref, rgn_ref,
        wa2_ref, ba2_ref, glan_ref, wout_ref, xn_ref, wcq_ref,
        x2_ref, q_ref, rgh_ref, s_ref,
        hs_ref, b_ref, *, seq):
    rows_n = x1_ref.shape[0]
    nb = rows_n // seq

    xb, yb, q, k, v, g, a_low = _split_p(p_ref)

    pos = _row_in_segment(xb.shape, seq)
    econv = econv_ref[...]
    sh = []
    for j in range(1, CONV_WIDTH):
        sh.append(jnp.where(pos >= j, _shift_rows(xb, j), _shift_rows(econv, rows_n - seq + j)))
    xc = _conv(xb, sh[0], sh[1], sh[2], cw_ref, cb_ref)
    r, i = _rg_gates(xc, wgate_ref, ba_ref, bx_ref)
    a, u = _rg_decay_input(xc, r, i, lam_ref)
    u = u + a * eh_ref[...]
    _, hs = _segment_affine_scan(a, u, seq)
    rgh_ref[...] = _last_rows(hs_ref, hs, seq)
    rg_out = _rms(hs * _gelu_tanh(yb), rgn_ref[...])

    log_a = _gla_log_decay(a_low, wa2_ref, ba2_ref)
    b, qt, kt = _gla_prepare(q, k, log_a, seq)
    intra = _gla_intra(qt, kt, v, seq)
    b_last = _last_rows(b_ref, b, seq)
    dec_t = jnp.exp(b_last).T
    kv = _gla_chunk_updates(k, v, b, b_last, seq)
    row = lax.broadcasted_iota(jnp.int32, (PAIR_K, GLA_DV), 0)
    top = row < GLA_DK
    states = {}
    for c in range(nb):
        for p in range(2):
            kl = slice(p * PAIR_K, (p + 1) * PAIR_K)
            s_pair = s0_ref[c, kl, :]
            s_bd = jnp.concatenate([jnp.where(top, s_pair, 0.0), jnp.where(top, 0.0, s_pair)], axis=1)
            states[c, p] = s_bd
            s_new = s_bd * dec_t[kl, c:c + 1] + kv[c, p]
            s_ref[c, kl, :] = jnp.where(top, s_new[:, :GLA_DV], s_new[:, GLA_DV:])
    gla_out = _gla_out(_gla_combine(intra, qt, states, seq), g, glan_ref)

    x2 = x1_ref[...] + _mix_project(rg_out, gla_out, wout_ref)
    x2_ref[...] = x2
    q_ref[...] = _mm(_rms(x2, xn_ref[...]), wcq_ref[...])


def _sample_attn_kernel(q_ref, k_ref, v_ref, o_ref, *, seq):
    tiles = XA_HEAD_DIM // LANES
    group = tiles * XA_HEADS
    r = N_MEM * group
    nb = k_ref.shape[0] // r
    lane = lax.broadcasted_iota(jnp.int32, (XA_HEADS * seq, r), 1)
    head = lax.broadcasted_iota(jnp.int32, (XA_HEADS * seq, r), 0) // seq
    own = (lane & (group - 1)) == head
    for j in range(nb):
        q = q_ref[j * seq:(j + 1) * seq, :]
        q_parts = jnp.concatenate(
            [q[:, c * LANES:(c + 1) * LANES] for c in range(XA_HEADS * tiles)], axis=0)
        part = _mm_nt(q_parts, k_ref[j * r:(j + 1) * r, :])
        s = []
        for h in range(XA_HEADS):
            acc = part[h * tiles * seq:(h * tiles + 1) * seq, :]
            for c in range(1, tiles):
                blk = part[(h * tiles + c) * seq:(h * tiles + c + 1) * seq, :]
                acc = acc + pltpu.roll(blk, r - c * XA_HEADS, axis=1)
            s.append(acc)
        s = jnp.concatenate(s, axis=0) * (XA_HEAD_DIM ** -0.5)
        p = _softmax_rows(jnp.where(own, s, MASKED_SCORE))
        p_all = jnp.concatenate(
            [p] + [pltpu.roll(p, c * XA_HEADS, axis=1) for c in range(1, tiles)], axis=0)
        o = _mm(p_all, v_ref[j * r:(j + 1) * r, :].astype(BF16))
        for c in range(tiles):
            for h in range(XA_HEADS):
                col = h * XA_HEAD_DIM + c * LANES
                o_ref[j * seq:(j + 1) * seq, col:col + LANES] = (
                    o[(c * XA_HEADS + h) * seq:(c * XA_HEADS + h + 1) * seq, :])


def _out_tail_kernel(x2_ref, o_ref, wco_ref, n2_ref, wg_ref, wu_ref, wd_ref, fn_ref, y_ref):
    x3 = x2_ref[...] + _mm(o_ref[...], wco_ref[...])
    x4 = x3 + 0.5 * _ffn(x3, n2_ref, wg_ref, wu_ref, wd_ref)
    y_ref[...] = _rms(x4, fn_ref[...])


def _resident(arr):
    nd = arr.ndim
    return pl.BlockSpec(arr.shape, lambda *_: (0,) * nd, pipeline_mode=pl.Buffered(1))


def _params(sem):
    return pltpu.CompilerParams(dimension_semantics=sem, vmem_limit_bytes=VMEM_LIMIT)


def _ffn_in(x, weights):
    rows = x.shape[0]
    row_spec = lambda w: pl.BlockSpec((ROW_TILE, w), lambda i: (i, 0))
    return pl.pallas_call(
        _ffn_in_kernel,
        grid=(rows // ROW_TILE,),
        in_specs=[row_spec(D_MODEL)] + [_resident(w) for w in weights],
        out_specs=[row_spec(D_MODEL), row_spec(P_WIDTH)],
        out_shape=[jax.ShapeDtypeStruct((rows, D_MODEL), F32),
                   jax.ShapeDtypeStruct((rows, P_WIDTH), F32)],
        compiler_params=_params(("parallel",)),
        name="ffn_in",
    )(x, *weights)


def _mem_kv(mem, weights):
    rows = mem.shape[0]
    row_spec = pl.BlockSpec((ROW_TILE, D_MODEL), lambda i: (i, 0))
    head_spec = pl.BlockSpec((ROW_TILE, XA_HEADS, XA_HEAD_DIM), lambda i: (i, 0, 0))
    return pl.pallas_call(
        _mem_kv_kernel,
        grid=(rows // ROW_TILE,),
        in_specs=[row_spec] + [_resident(w) for w in weights],
        out_specs=[head_spec] * 2 + [row_spec] * 2,
        out_shape=[jax.ShapeDtypeStruct((rows, XA_HEADS, XA_HEAD_DIM), F32)] * 2
                  + [jax.ShapeDtypeStruct((rows, D_MODEL), BF16)] * 2,
        compiler_params=_params(("parallel",)),
        name="mem_kv",
    )(mem, *weights)


def _prompt_tail(x1, p, kb, vb, weights):
    batch, seq, _ = x1.shape
    tt = ROW_TILE
    tok = lambda w: pl.BlockSpec((None, tt, w), lambda b, t: (b, t, 0))
    per_seq = lambda r, w: pl.BlockSpec((None, r, w), lambda b, t: (b, 0, 0))
    return pl.pallas_call(
        _prompt_tail_kernel,
        grid=(batch, seq // tt),
        in_specs=[tok(D_MODEL), tok(P_WIDTH), per_seq(N_MEM, D_MODEL), per_seq(N_MEM, D_MODEL)]
                 + [_resident(w) for w in weights],
        out_specs=[tok(D_MODEL), per_seq(SUBLANES, RG_WIDTH), per_seq(GLA_KEY_WIDTH, GLA_DV)],
        out_shape=[jax.ShapeDtypeStruct((batch, seq, D_MODEL), F32),
                   jax.ShapeDtypeStruct((batch, SUBLANES, RG_WIDTH), F32),
                   jax.ShapeDtypeStruct((batch, GLA_KEY_WIDTH, GLA_DV), F32)],
        scratch_shapes=[pltpu.VMEM((SUBLANES + tt, RG_WIDTH), F32),
                        pltpu.VMEM((SUBLANES, RG_WIDTH), F32),
                        pltpu.VMEM((2, PAIR_K, PAIR_V), F32),
                        pltpu.VMEM((GLA_KEY_WIDTH // LANES, tt, LANES), F32)],
        compiler_params=_params(("parallel", "arbitrary")),
        name="prompt_tail",
    )(x1, p, kb, vb, *weights)


def _sample_mix(x1, p, econv, eh, s0, weights, seq):
    rows = x1.shape[0]
    nb = ROW_TILE // seq
    row_spec = lambda w: pl.BlockSpec((ROW_TILE, w), lambda i: (i, 0))
    state_spec = pl.BlockSpec((nb, GLA_KEY_WIDTH, GLA_DV), lambda i: (i, 0, 0))
    return pl.pallas_call(
        functools.partial(_sample_mix_kernel, seq=seq),
        grid=(rows // ROW_TILE,),
        in_specs=[row_spec(D_MODEL), row_spec(P_WIDTH), row_spec(RG_WIDTH), row_spec(RG_WIDTH),
                  state_spec] + [_resident(w) for w in weights],
        out_specs=[row_spec(D_MODEL), row_spec(D_MODEL),
                   pl.BlockSpec((nb, RG_WIDTH), lambda i: (i, 0)), state_spec],
        out_shape=[jax.ShapeDtypeStruct((rows, D_MODEL), F32),
                   jax.ShapeDtypeStruct((rows, D_MODEL), F32),
                   jax.ShapeDtypeStruct((rows // seq, RG_WIDTH), F32),
                   jax.ShapeDtypeStruct((rows // seq, GLA_KEY_WIDTH, GLA_DV), F32)],
        scratch_shapes=[pltpu.VMEM((RG_WIDTH // LANES, ROW_TILE, LANES), F32),
                        pltpu.VMEM((GLA_KEY_WIDTH // LANES, ROW_TILE, LANES), F32)],
        compiler_params=_params(("parallel",)),
        name="sample_mix",
    )(x1, p, econv, eh, s0, *weights)


def _head_interleaved_rows(x):
    b, m, h, dh = x.shape
    tiles = dh // LANES
    return (x.reshape(b, m, h, tiles, LANES).transpose(0, 1, 3, 2, 4)
            .reshape(b * m * tiles * h, LANES))


def _sample_attn(q, k, v, seq):
    rows = q.shape[0]
    nb = SAMPLE_ATTN_BATCH
    row_spec = pl.BlockSpec((nb * seq, D_MODEL), lambda i: (i, 0))
    kv_rows = k.shape[0] // (rows // seq)
    kv_spec = pl.BlockSpec((nb * kv_rows, LANES), lambda i: (i, 0))
    return pl.pallas_call(
        functools.partial(_sample_attn_kernel, seq=seq),
        grid=(rows // (nb * seq),),
        in_specs=[row_spec, kv_spec, kv_spec],
        out_specs=row_spec,
        out_shape=jax.ShapeDtypeStruct((rows, D_MODEL), F32),
        compiler_params=_params(("parallel",)),
        name="sample_attn",
    )(q, k, v)


def _out_tail(x2, o, weights):
    rows = x2.shape[0]
    row_spec = pl.BlockSpec((ROW_TILE, D_MODEL), lambda i: (i, 0))
    return pl.pallas_call(
        _out_tail_kernel,
        grid=(rows // ROW_TILE,),
        in_specs=[row_spec, row_spec] + [_resident(w) for w in weights],
        out_specs=row_spec,
        out_shape=jax.ShapeDtypeStruct((rows, D_MODEL), F32),
        compiler_params=_params(("parallel",)),
        name="out_tail",
    )(x2, o, *weights)


def _block_diag_heads(w):
    h, n, _ = w.shape
    eye = jnp.eye(h, dtype=w.dtype)
    return (eye[:, None, :, None] * w[:, :, None, :]).reshape(h * n, h * n)


def kernel(x_prompt, x_sample, cache_mem_k, cache_mem_v, state_conv, state_rglru, state_gla, mem_prompt, ffn1_norm, ffn1_w_gate, ffn1_w_up, ffn1_w_down, mix_norm, w_in, conv_w, conv_b, rg_w_a, rg_b_a, rg_w_x, rg_b_x, rg_lambda, rg_out_norm, gla_w_a2, gla_b_a2, gla_out_norm, w_out, xattn_norm, mem_norm, w_cq, w_ck, w_cv, w_co, ffn2_norm, ffn2_w_gate, ffn2_w_up, ffn2_w_down, final_norm):
    bp, tp, _ = x_prompt.shape
    bs, ts, _ = x_sample.shape
    row = lambda g: g.reshape(1, -1)
    bf = lambda w: w.astype(BF16)

    w_in_p = bf(jnp.pad(w_in, ((0, 0), (0, P_WIDTH - D_IN))))
    hp = RG_HEADS // 2
    w_gate_rg = bf(jnp.stack([
        jnp.concatenate([_block_diag_heads(rg_w_a[c * hp:(c + 1) * hp]),
                         _block_diag_heads(rg_w_x[c * hp:(c + 1) * hp])], axis=1)
        for c in range(2)]))
    w_a2_p = bf(jnp.pad(gla_w_a2, ((0, A_PAD - GLA_GATE_RANK), (0, 0))))
    ffn_in_w = (row(ffn1_norm), bf(ffn1_w_gate), bf(ffn1_w_up), bf(ffn1_w_down), row(mix_norm), w_in_p)
    mixer_w = (conv_w, row(conv_b), w_gate_rg, row(rg_b_a), row(rg_b_x), row(rg_lambda),
               row(rg_out_norm), w_a2_p, row(gla_b_a2), row(jnp.tile(gla_out_norm, GLA_HEADS)),
               bf(w_out))
    w_cq_b, w_co_b = bf(w_cq), bf(w_co)
    ffn2_w = (row(ffn2_norm), bf(ffn2_w_gate), bf(ffn2_w_up), bf(ffn2_w_down), row(final_norm))

    mem_k, mem_v, mem_kb, mem_vb = _mem_kv(mem_prompt.reshape(bp * N_MEM, D_MODEL),
                                           (row(mem_norm), bf(w_ck), bf(w_cv)))
    x1_p, p_p = _ffn_in(x_prompt.reshape(bp * tp, D_MODEL), ffn_in_w)
    p_p = p_p.reshape(bp, tp, P_WIDTH)
    y_p, rgh_p, s_p = _prompt_tail(
        x1_p.reshape(bp, tp, D_MODEL), p_p,
        mem_kb.reshape(bp, N_MEM, D_MODEL), mem_vb.reshape(bp, N_MEM, D_MODEL),
        mixer_w + (row(xattn_norm), w_cq_b, w_co_b) + ffn2_w)
    conv_p = p_p[:, tp - (CONV_WIDTH - 1):, OFF_RG_X:OFF_RG_Y]

    x1_s, p_s = _ffn_in(x_sample.reshape(bs * ts, D_MODEL), ffn_in_w)
    econv = jnp.pad(state_conv, ((0, 0), (ts - (CONV_WIDTH - 1), 0), (0, 0))).reshape(bs * ts, RG_WIDTH)
    eh = jnp.pad(state_rglru[:, None, :], ((0, 0), (0, ts - 1), (0, 0))).reshape(bs * ts, RG_WIDTH)
    x2_s, q_s, rgh_s, s_s = _sample_mix(
        x1_s, p_s, econv, eh, state_gla.reshape(bs, GLA_KEY_WIDTH, GLA_DV),
        mixer_w + (row(xattn_norm), w_cq_b), ts)
    o_s = _sample_attn(q_s, _head_interleaved_rows(cache_mem_k),
                       _head_interleaved_rows(cache_mem_v), ts)
    y_s = _out_tail(x2_s, o_s, (w_co_b,) + ffn2_w)
    conv_s = p_s.reshape(bs, ts, P_WIDTH)[:, ts - (CONV_WIDTH - 1):, OFF_RG_X:OFF_RG_Y]

    return (y_p, y_s.reshape(bs, ts, D_MODEL),
            mem_k.reshape(bp, N_MEM, XA_HEADS, XA_HEAD_DIM),
            mem_v.reshape(bp, N_MEM, XA_HEADS, XA_HEAD_DIM),
            conv_p, rgh_p[:, 0, :], s_p.reshape(bp, GLA_HEADS, GLA_DK, GLA_DV),
            conv_s, rgh_s, s_s.reshape(bs, GLA_HEADS, GLA_DK, GLA_DV))
```

```python
import functools

import jax
import jax.numpy as jnp
import numpy as np
from jax import lax
from jax.experimental import pallas as pl
from jax.experimental.pallas import tpu as pltpu

F32 = jnp.float32
BF16 = jnp.bfloat16

D_MODEL = 1024
D_FF = 2816
RG_WIDTH = 512
RG_HEADS = 8
RG_HEAD_DIM = 64
CONV_WIDTH = 4
RG_C = 8.0
GLA_WIDTH = 512
GLA_HEADS = 4
GLA_DV = 128
GLA_DK = 64
GLA_KEY_WIDTH = 256
GLA_GATE_RANK = 16
GLA_GATE_NORMALIZER = 16.0
GLA_CHUNK = 32
N_MEM = 256
XA_HEADS = 4
XA_HEAD_DIM = 256
EPS = 1e-6
MASKED_SCORE = -1e30

OFF_RG_X = 0
OFF_RG_Y = 512
OFF_Q = 1024
OFF_K = 1280
OFF_V = 1536
OFF_G = 2048
OFF_A = 2560
D_IN = 2576

LANES = 128
SUBLANES = 8
A_PAD = LANES
P_WIDTH = OFF_A + A_PAD
PAIR_K = 2 * GLA_DK
PAIR_V = 2 * GLA_DV
VMEM_LIMIT = 56 * 1024 * 1024

ROW_TILE = 256
FF_CHUNK = 256
FF_DOWN_GROUP = 4
SAMPLE_ATTN_BATCH = 4


def _rms(x, g):
    return x * lax.rsqrt(jnp.mean(x * x, axis=-1, keepdims=True) + EPS) * g


def _mm(a, w):
    return jnp.dot(a.astype(BF16), w, preferred_element_type=F32)


def _mm_nt(a, b):
    return lax.dot_general(a.astype(BF16), b.astype(BF16), (((1,), (1,)), ((), ())),
                           preferred_element_type=F32)


def _mm_tn(a, b):
    return lax.dot_general(a.astype(BF16), b.astype(BF16), (((0,), (0,)), ((), ())),
                           preferred_element_type=F32)


def _silu(x):
    return x * jax.nn.sigmoid(x)


def _gelu_tanh(x):
    c = np.float32(np.sqrt(2.0 / np.pi))
    return x * (0.5 * (1.0 + jnp.tanh(c * (x + 0.044715 * (x * x * x)))))


def _softplus(x):
    return jnp.maximum(x, 0.0) + jnp.log1p(jnp.exp(-jnp.abs(x)))


def _ffn(x, norm_ref, wg_ref, wu_ref, wd_ref):
    h = _rms(x, norm_ref[...]).astype(BF16)
    g = jnp.dot(h, wg_ref[...], preferred_element_type=F32)
    u = jnp.dot(h, wu_ref[...], preferred_element_type=F32)
    return _mm(_silu(g) * u, wd_ref[...])


def _last_rows(buf_ref, x, seg):
    n = x.shape[0] // seg
    outs = []
    for j in range(x.shape[1] // LANES):
        buf_ref[j] = x[:, j * LANES:(j + 1) * LANES]
        outs.append(buf_ref[j, pl.ds(seg - 1, n, stride=seg), :])
    return jnp.concatenate(outs, axis=1)


def _ffn_pieces(x, norm_ref, wg_ref, wu_ref, wd_ref, out):
    h = _rms(x, norm_ref[...]).astype(BF16)
    acc = None
    acts = []
    n = D_FF // FF_CHUNK
    for c in range(n):
        cols = slice(c * FF_CHUNK, (c + 1) * FF_CHUNK)
        g = jnp.dot(h, wg_ref[:, cols], preferred_element_type=F32)
        u = jnp.dot(h, wu_ref[:, cols], preferred_element_type=F32)
        acts.append((_silu(g) * u).astype(BF16))
        if len(acts) == FF_DOWN_GROUP or c == n - 1:
            lo = (c + 1 - len(acts)) * FF_CHUNK
            part = jnp.dot(jnp.concatenate(acts, axis=1), wd_ref[lo:(c + 1) * FF_CHUNK, :],
                           preferred_element_type=F32)
            acc = part if acc is None else acc + part
            acts = []
        yield
    out.append(acc)


def _shift_rows(x, s):
    return pltpu.roll(x, s, axis=0)


def _row_in_segment(shape, seg):
    return lax.broadcasted_iota(jnp.int32, shape, 0) & (seg - 1)


def _segment_cumsum(x, seg):
    pos = _row_in_segment(x.shape, seg)
    s = 1
    while s < seg:
        x = jnp.where(pos >= s, x + _shift_rows(x, s), x)
        s *= 2
    return x


def _segment_affine_scan(a, u, seg):
    pos = _row_in_segment(a.shape, seg)
    s = 1
    while s < seg:
        m = pos >= s
        u = jnp.where(m, a * _shift_rows(u, s) + u, u)
        a = jnp.where(m, a * _shift_rows(a, s), a)
        s *= 2
    return a, u


def _conv(xb, sh1, sh2, sh3, cw_ref, cb_ref):
    y = cb_ref[...] + sh3 * cw_ref[0:1, :]
    y = y + sh2 * cw_ref[1:2, :]
    y = y + sh1 * cw_ref[2:3, :]
    return y + xb * cw_ref[3:4, :]


def _rg_gates(xc, wgate_ref, ba_ref, bx_ref):
    half = RG_WIDTH // 2
    r, i = [], []
    for c in range(2):
        z = _mm(xc[:, c * half:(c + 1) * half], wgate_ref[c])
        r.append(z[:, :half])
        i.append(z[:, half:])
    r = jax.nn.sigmoid(jnp.concatenate(r, axis=1) + ba_ref[...])
    i = jax.nn.sigmoid(jnp.concatenate(i, axis=1) + bx_ref[...])
    return r, i


def _rg_decay_input(xc, r, i, lam_ref):
    log_a = (-RG_C * _softplus(-lam_ref[...])) * r
    a = jnp.exp(log_a)
    mult = jnp.sqrt(-jnp.tanh(log_a) * (a * a + 1.0))
    return a, mult * (i * xc)


def _gla_log_decay(a_low, wa2_ref, ba2_ref):
    z = _mm(a_low, wa2_ref[...]) + ba2_ref[...]
    return (-_softplus(-z)) / GLA_GATE_NORMALIZER


def _gla_intra(qt, kt, v, chunk):
    t = qt.shape[0]
    shift = int(np.log2(chunk))
    ri = lax.broadcasted_iota(jnp.int32, (t, t), 0)
    ci = lax.broadcasted_iota(jnp.int32, (t, t), 1)
    causal = (ri >= ci) & ((ri >> shift) == (ci >> shift))
    lane = lax.broadcasted_iota(jnp.int32, (t, PAIR_K), 1)
    outs = []
    for h in range(GLA_HEADS):
        p = h // 2
        qp = qt[:, p * PAIR_K:(p + 1) * PAIR_K]
        kp = kt[:, p * PAIR_K:(p + 1) * PAIR_K]
        mine = (lane >= GLA_DK) if (h % 2) else (lane < GLA_DK)
        s = _mm_nt(jnp.where(mine, qp, 0.0), kp)
        attn = jnp.where(causal, s, 0.0)
        outs.append(_mm(attn, v[:, h * GLA_DV:(h + 1) * GLA_DV].astype(BF16)))
    return outs


def _pair_blockdiag_mask():
    r = lax.broadcasted_iota(jnp.int32, (PAIR_K, PAIR_V), 0)
    c = lax.broadcasted_iota(jnp.int32, (PAIR_K, PAIR_V), 1)
    return (r < GLA_DK) == (c < GLA_DV)


def _gla_chunk_updates(k, v, b, b_last, chunk):
    bd_mask = _pair_blockdiag_mask()
    kv = {}
    for c in range(k.shape[0] // chunk):
        rows = slice(c * chunk, (c + 1) * chunk)
        for p in range(2):
            kl = slice(p * PAIR_K, (p + 1) * PAIR_K)
            kd = k[rows, kl] * jnp.exp(b_last[c:c + 1, kl] - b[rows, kl])
            kv[c, p] = jnp.where(bd_mask, _mm_tn(kd, v[rows, p * PAIR_V:(p + 1) * PAIR_V]), 0.0)
    return kv


def _gla_combine(intra, qt, states, chunk):
    n_chunks = qt.shape[0] // chunk
    heads = []
    for p in range(2):
        kl = slice(p * PAIR_K, (p + 1) * PAIR_K)
        inter = jnp.concatenate(
            [_mm(qt[c * chunk:(c + 1) * chunk, kl], states[c, p].astype(BF16)) for c in range(n_chunks)],
            axis=0)
        heads.append(intra[2 * p] + inter[:, :GLA_DV])
        heads.append(intra[2 * p + 1] + inter[:, GLA_DV:])
    return heads


def _gla_out(o_heads, g, norm_ref):
    outs = []
    for h in range(GLA_HEADS):
        outs.append(_rms(o_heads[h], norm_ref[:, h * GLA_DV:(h + 1) * GLA_DV]))
    return jnp.concatenate(outs, axis=1) * _silu(g)


def _softmax_rows(s):
    m = jnp.max(s, axis=-1, keepdims=True)
    e = jnp.exp(s - m)
    return e * (1.0 / jnp.sum(e, axis=-1, keepdims=True))


def _split_p(p_ref):
    xb = p_ref[:, OFF_RG_X:OFF_RG_Y]
    yb = p_ref[:, OFF_RG_Y:OFF_Q]
    q = p_ref[:, OFF_Q:OFF_K]
    k = p_ref[:, OFF_K:OFF_V]
    v = p_ref[:, OFF_V:OFF_G]
    g = p_ref[:, OFF_G:OFF_A]
    a_low = p_ref[:, OFF_A:P_WIDTH]
    return xb, yb, q, k, v, g, a_low


def _gla_prepare(q, k, log_a, chunk):
    b = _segment_cumsum(log_a, chunk)
    qt = (q * (GLA_DK ** -0.5)) * jnp.exp(b)
    kt = k * jnp.exp(-b)
    return b, qt, kt


def _mix_project(rg_out, gla_out, wout_ref):
    return (_mm(rg_out, wout_ref[0:RG_WIDTH, :]) + _mm(gla_out, wout_ref[RG_WIDTH:, :]))


def _ffn_in_kernel(x_ref, n1_ref, wg_ref, wu_ref, wd_ref, n2_ref, win_ref, x1_ref, p_ref):
    x = x_ref[...]
    x1 = x + 0.5 * _ffn(x, n1_ref, wg_ref, wu_ref, wd_ref)
    x1_ref[...] = x1
    p_ref[...] = _mm(_rms(x1, n2_ref[...]), win_ref[...])


def _mem_kv_kernel(m_ref, n_ref, wk_ref, wv_ref, k_ref, v_ref, kb_ref, vb_ref):
    h = _rms(m_ref[...], n_ref[...]).astype(BF16)
    k = jnp.dot(h, wk_ref[...], preferred_element_type=F32)
    v = jnp.dot(h, wv_ref[...], preferred_element_type=F32)
    for h in range(XA_HEADS):
        sl = slice(h * XA_HEAD_DIM, (h + 1) * XA_HEAD_DIM)
        k_ref[:, h, :] = k[:, sl]
        v_ref[:, h, :] = v[:, sl]
    kb_ref[...] = k.astype(BF16)
    vb_ref[...] = v.astype(BF16)


def _prompt_tail_kernel(
        x1_ref, p_ref, kb_ref, vb_ref,
        cw_ref, cb_ref, wgate_ref, ba_ref, bx_ref, lam_ref, rgn_ref,
        wa2_ref, ba2_ref, glan_ref, wout_ref,
        xn_ref, wcq_ref, wco_ref, n2_ref, wg_ref, wu_ref, wd_ref, fn_ref,
        y_ref, rgh_ref, s_ref,
        xpad_ref, h_ref, sbd_ref, b_ref, x3_ref, *, tiles_per_seq, n_tiles):
    step = pl.program_id(0)
    t_idx = jnp.minimum(step, n_tiles - 1) % tiles_per_seq
    slot = step % 2
    tt = x1_ref.shape[0]
    n_chunks = tt // GLA_CHUNK

    @pl.when(step == 0)
    def _():
        x3_ref[1] = jnp.zeros((tt, D_MODEL), F32)

    @pl.when(t_idx == 0)
    def _():
        xpad_ref[0:SUBLANES, :] = jnp.zeros((SUBLANES, RG_WIDTH), F32)
        h_ref[...] = jnp.zeros_like(h_ref)
        sbd_ref[...] = jnp.zeros_like(sbd_ref)

    x3_prev = x3_ref[1 - slot]
    ffn_out = []
    ffn = _ffn_pieces(x3_prev, n2_ref, wg_ref, wu_ref, wd_ref, ffn_out)
    next(ffn)

    xb, yb, q, k, v, g, a_low = _split_p(p_ref)

    xpad_ref[SUBLANES:SUBLANES + tt, :] = xb
    sh1 = xpad_ref[SUBLANES - 1:SUBLANES - 1 + tt, :]
    sh2 = xpad_ref[SUBLANES - 2:SUBLANES - 2 + tt, :]
    sh3 = xpad_ref[SUBLANES - 3:SUBLANES - 3 + tt, :]
    xc = _conv(xb, sh1, sh2, sh3, cw_ref, cb_ref)
    xpad_ref[0:SUBLANES, :] = xb[tt - SUBLANES:tt, :]
    next(ffn)
    r, i = _rg_gates(xc, wgate_ref, ba_ref, bx_ref)
    a, u = _rg_decay_input(xc, r, i, lam_ref)
    next(ffn)
    a_grp, h_grp = _segment_affine_scan(a, u, SUBLANES)
    next(ffn)
    carry = h_ref[0:1, :]
    groups = []
    for gi in range(tt // SUBLANES):
        rows = slice(gi * SUBLANES, (gi + 1) * SUBLANES)
        groups.append(h_grp[rows, :] + a_grp[rows, :] * carry)
        carry = groups[-1][SUBLANES - 1:SUBLANES, :]
    hs = jnp.concatenate(groups, axis=0)
    h_ref[...] = jnp.broadcast_to(carry, h_ref.shape)
    rg_out = _rms(hs * _gelu_tanh(yb), rgn_ref[...])
    next(ffn)

    log_a = _gla_log_decay(a_low, wa2_ref, ba2_ref)
    b, qt, kt = _gla_prepare(q, k, log_a, GLA_CHUNK)
    next(ffn)
    intra = _gla_intra(qt, kt, v, GLA_CHUNK)
    next(ffn)
    b_last = _last_rows(b_ref, b, GLA_CHUNK)
    dec_t = jnp.exp(b_last).T
    kv = _gla_chunk_updates(k, v, b, b_last, GLA_CHUNK)
    states = {}
    for p in range(2):
        s_bd = sbd_ref[p]
        for c in range(n_chunks):
            states[c, p] = s_bd
            s_bd = s_bd * dec_t[p * PAIR_K:(p + 1) * PAIR_K, c:c + 1] + kv[c, p]
        sbd_ref[p] = s_bd
    next(ffn)
    gla_out = _gla_out(_gla_combine(intra, qt, states, GLA_CHUNK), g, glan_ref)
    next(ffn)

    x2 = x1_ref[...] + _mix_project(rg_out, gla_out, wout_ref)

    qx = _mm(_rms(x2, xn_ref[...]), wcq_ref[...])
    next(ffn)
    head = lambda h: slice(h * XA_HEAD_DIM, (h + 1) * XA_HEAD_DIM)
    s = jnp.concatenate([_mm_nt(qx[:, head(h)], kb_ref[:, head(h)]) for h in range(XA_HEADS)], axis=0)
    pr = _softmax_rows(s * (XA_HEAD_DIM ** -0.5))
    next(ffn)
    o = jnp.concatenate([_mm(pr[h * tt:(h + 1) * tt, :], vb_ref[:, head(h)]) for h in range(XA_HEADS)], axis=1)
    x3_ref[slot] = x2 + _mm(o, wco_ref[...])
    for _ in ffn:
        pass
    y_ref[...] = _rms(x3_prev + 0.5 * ffn_out[0], fn_ref[...])

    @pl.when((t_idx == tiles_per_seq - 1) & (step < n_tiles))
    def _():
        rgh_ref[...] = h_ref[...]
        row = lax.broadcasted_iota(jnp.int32, (PAIR_K, GLA_DV), 0)
        for p in range(2):
            s_bd = sbd_ref[p]
            s_ref[p * PAIR_K:(p + 1) * PAIR_K, :] = jnp.where(
                row < GLA_DK, s_bd[:, :GLA_DV], s_bd[:, GLA_DV:])


def _sample_mix_kernel(
        x1_ref, p_ref, econv_ref, eh_ref, s0_ref,
        cw_ref, cb_ref, wgate_ref, ba_ref, bx_ref, lam_ref, rgn_ref,
        wa2_ref, ba2_ref, glan_ref, wout_ref, xn_ref, wcq_ref,
        x2_ref, q_ref, rgh_ref, s_ref,
        hs_ref, b_ref, *, seq):
    rows_n = x1_ref.shape[0]
    nb = rows_n // seq

    xb, yb, q, k, v, g, a_low = _split_p(p_ref)

    pos = _row_in_segment(xb.shape, seq)
    econv = econv_ref[...]
    sh = []
    for j in range(1, CONV_WIDTH):
        sh.append(jnp.where(pos >= j, _shift_rows(xb, j), _shift_rows(econv, rows_n - seq + j)))
    xc = _conv(xb, sh[0], sh[1], sh[2], cw_ref, cb_ref)
    r, i = _rg_gates(xc, wgate_ref, ba_ref, bx_ref)
    a, u = _rg_decay_input(xc, r, i, lam_ref)
    u = u + a * eh_ref[...]
    _, hs = _segment_affine_scan(a, u, seq)
    rgh_ref[...] = _last_rows(hs_ref, hs, seq)
    rg_out = _rms(hs * _gelu_tanh(yb), rgn_ref[...])

    log_a = _gla_log_decay(a_low, wa2_ref, ba2_ref)
    b, qt, kt = _gla_prepare(q, k, log_a, seq)
    intra = _gla_intra(qt, kt, v, seq)
    b_last = _last_rows(b_ref, b, seq)
    dec_t = jnp.exp(b_last).T
    kv = _gla_chunk_updates(k, v, b, b_last, seq)
    row = lax.broadcasted_iota(jnp.int32, (PAIR_K, GLA_DV), 0)
    top = row < GLA_DK
    states = {}
    for c in range(nb):
        for p in range(2):
            kl = slice(p * PAIR_K, (p + 1) * PAIR_K)
            s_pair = s0_ref[c, kl, :]
            s_bd = jnp.concatenate([jnp.where(top, s_pair, 0.0), jnp.where(top, 0.0, s_pair)], axis=1)
            states[c, p] = s_bd
            s_new = s_bd * dec_t[kl, c:c + 1] + kv[c, p]
            s_ref[c, kl, :] = jnp.where(top, s_new[:, :GLA_DV], s_new[:, GLA_DV:])
    gla_out = _gla_out(_gla_combine(intra, qt, states, seq), g, glan_ref)

    x2 = x1_ref[...] + _mix_project(rg_out, gla_out, wout_ref)
    x2_ref[...] = x2
    q_ref[...] = _mm(_rms(x2, xn_ref[...]), wcq_ref[...])


def _sample_attn_kernel(q_ref, k_ref, v_ref, o_ref, *, seq):
    tiles = XA_HEAD_DIM // LANES
    group = tiles * XA_HEADS
    r = N_MEM * group
    nb = k_ref.shape[0] // r
    lane = lax.broadcasted_iota(jnp.int32, (XA_HEADS * seq, r), 1)
    head = lax.broadcasted_iota(jnp.int32, (XA_HEADS * seq, r), 0) // seq
    own = (lane & (group - 1)) == head
    for j in range(nb):
        q = q_ref[j * seq:(j + 1) * seq, :]
        q_parts = jnp.concatenate(
            [q[:, c * LANES:(c + 1) * LANES] for c in range(XA_HEADS * tiles)], axis=0)
        part = _mm_nt(q_parts, k_ref[j * r:(j + 1) * r, :])
        s = []
        for h in range(XA_HEADS):
            acc = part[h * tiles * seq:(h * tiles + 1) * seq, :]
            for c in range(1, tiles):
                blk = part[(h * tiles + c) * seq:(h * tiles + c + 1) * seq, :]
                acc = acc + pltpu.roll(blk, r - c * XA_HEADS, axis=1)
            s.append(acc)
        s = jnp.concatenate(s, axis=0) * (XA_HEAD_DIM ** -0.5)
        p = _softmax_rows(jnp.where(own, s, MASKED_SCORE))
        p_all = jnp.concatenate(
            [p] + [pltpu.roll(p, c * XA_HEADS, axis=1) for c in range(1, tiles)], axis=0)
        o = _mm(p_all, v_ref[j * r:(j + 1) * r, :].astype(BF16))
        for c in range(tiles):
            for h in range(XA_HEADS):
                col = h * XA_HEAD_DIM + c * LANES
                o_ref[j * seq:(j + 1) * seq, col:col + LANES] = (
                    o[(c * XA_HEADS + h) * seq:(c * XA_HEADS + h + 1) * seq, :])


def _out_tail_kernel(x2_ref, o_ref, wco_ref, n2_ref, wg_ref, wu_ref, wd_ref, fn_ref, y_ref):
    x3 = x2_ref[...] + _mm(o_ref[...], wco_ref[...])
    x4 = x3 + 0.5 * _ffn(x3, n2_ref, wg_ref, wu_ref, wd_ref)
    y_ref[...] = _rms(x4, fn_ref[...])


def _resident(arr):
    nd = arr.ndim
    return pl.BlockSpec(arr.shape, lambda *_: (0,) * nd, pipeline_mode=pl.Buffered(1))


def _params(sem):
    return pltpu.CompilerParams(dimension_semantics=sem, vmem_limit_bytes=VMEM_LIMIT)


def _ffn_in(x, weights):
    rows = x.shape[0]
    row_spec = lambda w: pl.BlockSpec((ROW_TILE, w), lambda i: (i, 0))
    return pl.pallas_call(
        _ffn_in_kernel,
        grid=(rows // ROW_TILE,),
        in_specs=[row_spec(D_MODEL)] + [_resident(w) for w in weights],
        out_specs=[row_spec(D_MODEL), row_spec(P_WIDTH)],
        out_shape=[jax.ShapeDtypeStruct((rows, D_MODEL), F32),
                   jax.ShapeDtypeStruct((rows, P_WIDTH), F32)],
        compiler_params=_params(("parallel",)),
        name="ffn_in",
    )(x, *weights)


def _mem_kv(mem, weights):
    rows = mem.shape[0]
    row_spec = pl.BlockSpec((ROW_TILE, D_MODEL), lambda i: (i, 0))
    head_spec = pl.BlockSpec((ROW_TILE, XA_HEADS, XA_HEAD_DIM), lambda i: (i, 0, 0))
    return pl.pallas_call(
        _mem_kv_kernel,
        grid=(rows // ROW_TILE,),
        in_specs=[row_spec] + [_resident(w) for w in weights],
        out_specs=[head_spec] * 2 + [row_spec] * 2,
        out_shape=[jax.ShapeDtypeStruct((rows, XA_HEADS, XA_HEAD_DIM), F32)] * 2
                  + [jax.ShapeDtypeStruct((rows, D_MODEL), BF16)] * 2,
        compiler_params=_params(("parallel",)),
        name="mem_kv",
    )(mem, *weights)


def _prompt_tail(x1, p, kb, vb, weights):
    batch, seq, _ = x1.shape
    tt = ROW_TILE
    tiles_per_seq = seq // tt
    n_tiles = batch * tiles_per_seq
    cur = lambda s: jnp.minimum(s, n_tiles - 1)
    prev = lambda s: jnp.maximum(s - 1, 0)
    tok = lambda w: pl.BlockSpec(
        (None, tt, w), lambda s: (cur(s) // tiles_per_seq, cur(s) % tiles_per_seq, 0))
    per_seq = lambda r, w: pl.BlockSpec((None, r, w), lambda s: (cur(s) // tiles_per_seq, 0, 0))
    out_tok = pl.BlockSpec(
        (None, tt, D_MODEL), lambda s: (prev(s) // tiles_per_seq, prev(s) % tiles_per_seq, 0))
    return pl.pallas_call(
        functools.partial(_prompt_tail_kernel, tiles_per_seq=tiles_per_seq, n_tiles=n_tiles),
        grid=(n_tiles + 1,),
        in_specs=[tok(D_MODEL), tok(P_WIDTH), per_seq(N_MEM, D_MODEL), per_seq(N_MEM, D_MODEL)]
                 + [_resident(w) for w in weights],
        out_specs=[out_tok, per_seq(SUBLANES, RG_WIDTH), per_seq(GLA_KEY_WIDTH, GLA_DV)],
        out_shape=[jax.ShapeDtypeStruct((batch, seq, D_MODEL), F32),
                   jax.ShapeDtypeStruct((batch, SUBLANES, RG_WIDTH), F32),
                   jax.ShapeDtypeStruct((batch, GLA_KEY_WIDTH, GLA_DV), F32)],
        scratch_shapes=[pltpu.VMEM((SUBLANES + tt, RG_WIDTH), F32),
                        pltpu.VMEM((SUBLANES, RG_WIDTH), F32),
                        pltpu.VMEM((2, PAIR_K, PAIR_V), F32),
                        pltpu.VMEM((GLA_KEY_WIDTH // LANES, tt, LANES), F32),
                        pltpu.VMEM((2, tt, D_MODEL), F32)],
        compiler_params=_params(("arbitrary",)),
        name="prompt_tail",
    )(x1, p, kb, vb, *weights)


def _sample_mix(x1, p, econv, eh, s0, weights, seq):
    rows = x1.shape[0]
    nb = ROW_TILE // seq
    row_spec = lambda w: pl.BlockSpec((ROW_TILE, w), lambda i: (i, 0))
    state_spec = pl.BlockSpec((nb, GLA_KEY_WIDTH, GLA_DV), lambda i: (i, 0, 0))
    return pl.pallas_call(
        functools.partial(_sample_mix_kernel, seq=seq),
        grid=(rows // ROW_TILE,),
        in_specs=[row_spec(D_MODEL), row_spec(P_WIDTH), row_spec(RG_WIDTH), row_spec(RG_WIDTH),
                  state_spec] + [_resident(w) for w in weights],
        out_specs=[row_spec(D_MODEL), row_spec(D_MODEL),
                   pl.BlockSpec((nb, RG_WIDTH), lambda i: (i, 0)), state_spec],
        out_shape=[jax.ShapeDtypeStruct((rows, D_MODEL), F32),
                   jax.ShapeDtypeStruct((rows, D_MODEL), F32),
                   jax.ShapeDtypeStruct((rows // seq, RG_WIDTH), F32),
                   jax.ShapeDtypeStruct((rows // seq, GLA_KEY_WIDTH, GLA_DV), F32)],
        scratch_shapes=[pltpu.VMEM((RG_WIDTH // LANES, ROW_TILE, LANES), F32),
                        pltpu.VMEM((GLA_KEY_WIDTH // LANES, ROW_TILE, LANES), F32)],
        compiler_params=_params(("parallel",)),
        name="sample_mix",
    )(x1, p, econv, eh, s0, *weights)


def _head_interleaved_rows(x):
    b, m, h, dh = x.shape
    tiles = dh // LANES
    return (x.reshape(b, m, h, tiles, LANES).transpose(0, 1, 3, 2, 4)
            .reshape(b * m * tiles * h, LANES))


def _sample_attn(q, k, v, seq):
    rows = q.shape[0]
    nb = SAMPLE_ATTN_BATCH
    row_spec = pl.BlockSpec((nb * seq, D_MODEL), lambda i: (i, 0))
    kv_rows = k.shape[0] // (rows // seq)
    kv_spec = pl.BlockSpec((nb * kv_rows, LANES), lambda i: (i, 0))
    return pl.pallas_call(
        functools.partial(_sample_attn_kernel, seq=seq),
        grid=(rows // (nb * seq),),
        in_specs=[row_spec, kv_spec, kv_spec],
        out_specs=row_spec,
        out_shape=jax.ShapeDtypeStruct((rows, D_MODEL), F32),
        compiler_params=_params(("parallel",)),
        name="sample_attn",
    )(q, k, v)


def _out_tail(x2, o, weights):
    rows = x2.shape[0]
    row_spec = pl.BlockSpec((ROW_TILE, D_MODEL), lambda i: (i, 0))
    return pl.pallas_call(
        _out_tail_kernel,
        grid=(rows // ROW_TILE,),
        in_specs=[row_spec, row_spec] + [_resident(w) for w in weights],
        out_specs=row_spec,
        out_shape=jax.ShapeDtypeStruct((rows, D_MODEL), F32),
        compiler_params=_params(("parallel",)),
        name="out_tail",
    )(x2, o, *weights)


def _block_diag_heads(w):
    h, n, _ = w.shape
    eye = jnp.eye(h, dtype=w.dtype)
    return (eye[:, None, :, None] * w[:, :, None, :]).reshape(h * n, h * n)


def kernel(x_prompt, x_sample, cache_mem_k, cache_mem_v, state_conv, state_rglru, state_gla, mem_prompt, ffn1_norm, ffn1_w_gate, ffn1_w_up, ffn1_w_down, mix_norm, w_in, conv_w, conv_b, rg_w_a, rg_b_a, rg_w_x, rg_b_x, rg_lambda, rg_out_norm, gla_w_a2, gla_b_a2, gla_out_norm, w_out, xattn_norm, mem_norm, w_cq, w_ck, w_cv, w_co, ffn2_norm, ffn2_w_gate, ffn2_w_up, ffn2_w_down, final_norm):
    bp, tp, _ = x_prompt.shape
    bs, ts, _ = x_sample.shape
    row = lambda g: g.reshape(1, -1)
    bf = lambda w: w.astype(BF16)

    w_in_p = bf(jnp.pad(w_in, ((0, 0), (0, P_WIDTH - D_IN))))
    hp = RG_HEADS // 2
    w_gate_rg = bf(jnp.stack([
        jnp.concatenate([_block_diag_heads(rg_w_a[c * hp:(c + 1) * hp]),
                         _block_diag_heads(rg_w_x[c * hp:(c + 1) * hp])], axis=1)
        for c in range(2)]))
    w_a2_p = bf(jnp.pad(gla_w_a2, ((0, A_PAD - GLA_GATE_RANK), (0, 0))))
    ffn_in_w = (row(ffn1_norm), bf(ffn1_w_gate), bf(ffn1_w_up), bf(ffn1_w_down), row(mix_norm), w_in_p)
    mixer_w = (conv_w, row(conv_b), w_gate_rg, row(rg_b_a), row(rg_b_x), row(rg_lambda),
               row(rg_out_norm), w_a2_p, row(gla_b_a2), row(jnp.tile(gla_out_norm, GLA_HEADS)),
               bf(w_out))
    w_cq_b, w_co_b = bf(w_cq), bf(w_co)
    ffn2_w = (row(ffn2_norm), bf(ffn2_w_gate), bf(ffn2_w_up), bf(ffn2_w_down), row(final_norm))

    mem_k, mem_v, mem_kb, mem_vb = _mem_kv(mem_prompt.reshape(bp * N_MEM, D_MODEL),
                                           (row(mem_norm), bf(w_ck), bf(w_cv)))
    x1_p, p_p = _ffn_in(x_prompt.reshape(bp * tp, D_MODEL), ffn_in_w)
    p_p = p_p.reshape(bp, tp, P_WIDTH)
    y_p, rgh_p, s_p = _prompt_tail(
        x1_p.reshape(bp, tp, D_MODEL), p_p,
        mem_kb.reshape(bp, N_MEM, D_MODEL), mem_vb.reshape(bp, N_MEM, D_MODEL),
        mixer_w + (row(xattn_norm), w_cq_b, w_co_b) + ffn2_w)
    conv_p = p_p[:, tp - (CONV_WIDTH - 1):, OFF_RG_X:OFF_RG_Y]

    x1_s, p_s = _ffn_in(x_sample.reshape(bs * ts, D_MODEL), ffn_in_w)
    econv = jnp.pad(state_conv, ((0, 0), (ts - (CONV_WIDTH - 1), 0), (0, 0))).reshape(bs * ts, RG_WIDTH)
    eh = jnp.pad(state_rglru[:, None, :], ((0, 0), (0, ts - 1), (0, 0))).reshape(bs * ts, RG_WIDTH)
    x2_s, q_s, rgh_s, s_s = _sample_mix(
        x1_s, p_s, econv, eh, state_gla.reshape(bs, GLA_KEY_WIDTH, GLA_DV),
        mixer_w + (row(xattn_norm), w_cq_b), ts)
    o_s = _sample_attn(q_s, _head_interleaved_rows(cache_mem_k),
                       _head_interleaved_rows(cache_mem_v), ts)
    y_s = _out_tail(x2_s, o_s, (w_co_b,) + ffn2_w)
    conv_s = p_s.reshape(bs, ts, P_WIDTH)[:, ts - (CONV_WIDTH - 1):, OFF_RG_X:OFF_RG_Y]

    return (y_p, y_s.reshape(bs, ts, D_MODEL),
            mem_k.reshape(bp, N_MEM, XA_HEADS, XA_HEAD_DIM),
            mem_v.reshape(bp, N_MEM, XA_HEADS, XA_HEAD_DIM),
            conv_p, rgh_p[:, 0, :], s_p.reshape(bp, GLA_HEADS, GLA_DK, GLA_DV),
            conv_s, rgh_s, s_s.reshape(bs, GLA_HEADS, GLA_DK, GLA_DV))
```

```python
import functools

import jax
import jax.numpy as jnp
import numpy as np
from jax import lax
from jax.experimental import pallas as pl
from jax.experimental.pallas import tpu as pltpu

F32 = jnp.float32
BF16 = jnp.bfloat16

D_MODEL = 1024
D_FF = 2816
RG_WIDTH = 512
RG_HEADS = 8
RG_HEAD_DIM = 64
CONV_WIDTH = 4
RG_C = 8.0
GLA_WIDTH = 512
GLA_HEADS = 4
GLA_DV = 128
GLA_DK = 64
GLA_KEY_WIDTH = 256
GLA_GATE_RANK = 16
GLA_GATE_NORMALIZER = 16.0
GLA_CHUNK = 32
N_MEM = 256
XA_HEADS = 4
XA_HEAD_DIM = 256
EPS = 1e-6
MASKED_SCORE = -1e30

OFF_RG_X = 0
OFF_RG_Y = 512
OFF_Q = 1024
OFF_K = 1280
OFF_V = 1536
OFF_G = 2048
OFF_A = 2560
D_IN = 2576

LANES = 128
SUBLANES = 8
A_PAD = LANES
P_WIDTH = OFF_A + A_PAD
PAIR_K = 2 * GLA_DK
PAIR_V = 2 * GLA_DV
VMEM_LIMIT = 56 * 1024 * 1024

ROW_TILE = 256
FF_CHUNK = 256
FF_DOWN_GROUP = 2
FFN2_PLACEMENT = (1, 1, 1, 1, 1, 1, 1, 1, 1, 1, 1)
SAMPLE_ATTN_BATCH = 8


def _rms(x, g):
    return x * lax.rsqrt(jnp.mean(x * x, axis=-1, keepdims=True) + EPS) * g


def _mm(a, w):
    return jnp.dot(a.astype(BF16), w, preferred_element_type=F32)


def _mm_nt(a, b):
    return lax.dot_general(a.astype(BF16), b.astype(BF16), (((1,), (1,)), ((), ())),
                           preferred_element_type=F32)


def _mm_tn(a, b):
    return lax.dot_general(a.astype(BF16), b.astype(BF16), (((0,), (0,)), ((), ())),
                           preferred_element_type=F32)


def _silu(x):
    return x * jax.nn.sigmoid(x)


def _gelu_tanh(x):
    c = np.float32(np.sqrt(2.0 / np.pi))
    return x * (0.5 * (1.0 + jnp.tanh(c * (x + 0.044715 * (x * x * x)))))


def _softplus(x):
    return jnp.maximum(x, 0.0) + jnp.log1p(jnp.exp(-jnp.abs(x)))


def _ffn(x, norm_ref, wg_ref, wu_ref, wd_ref):
    h = _rms(x, norm_ref[...]).astype(BF16)
    g = jnp.dot(h, wg_ref[...], preferred_element_type=F32)
    u = jnp.dot(h, wu_ref[...], preferred_element_type=F32)
    return _mm(_silu(g) * u, wd_ref[...])


def _last_rows(buf_ref, x, seg):
    n = x.shape[0] // seg
    outs = []
    for j in range(x.shape[1] // LANES):
        buf_ref[j] = x[:, j * LANES:(j + 1) * LANES]
        outs.append(buf_ref[j, pl.ds(seg - 1, n, stride=seg), :])
    return jnp.concatenate(outs, axis=1)


def _ffn_pieces(x, norm_ref, wg_ref, wu_ref, wd_ref, out):
    h = _rms(x, norm_ref[...]).astype(BF16)
    acc = None
    acts = []
    n = D_FF // FF_CHUNK
    for c in range(n):
        cols = slice(c * FF_CHUNK, (c + 1) * FF_CHUNK)
        g = jnp.dot(h, wg_ref[:, cols], preferred_element_type=F32)
        u = jnp.dot(h, wu_ref[:, cols], preferred_element_type=F32)
        acts.append((_silu(g) * u).astype(BF16))
        if len(acts) == FF_DOWN_GROUP or c == n - 1:
            lo = (c + 1 - len(acts)) * FF_CHUNK
            part = jnp.dot(jnp.concatenate(acts, axis=1), wd_ref[lo:(c + 1) * FF_CHUNK, :],
                           preferred_element_type=F32)
            acc = part if acc is None else acc + part
            acts = []
        yield
    out.append(acc)


def _shift_rows(x, s):
    return pltpu.roll(x, s, axis=0)


def _row_in_segment(shape, seg):
    return lax.broadcasted_iota(jnp.int32, shape, 0) & (seg - 1)


def _segment_cumsum(x, seg):
    pos = _row_in_segment(x.shape, seg)
    s = 1
    while s < seg:
        x = jnp.where(pos >= s, x + _shift_rows(x, s), x)
        s *= 2
    return x


def _segment_affine_scan(a, u, seg):
    pos = _row_in_segment(a.shape, seg)
    s = 1
    while s < seg:
        m = pos >= s
        u = jnp.where(m, a * _shift_rows(u, s) + u, u)
        a = jnp.where(m, a * _shift_rows(a, s), a)
        s *= 2
    return a, u


def _conv(xb, sh1, sh2, sh3, cw_ref, cb_ref):
    y = cb_ref[...] + sh3 * cw_ref[0:1, :]
    y = y + sh2 * cw_ref[1:2, :]
    y = y + sh1 * cw_ref[2:3, :]
    return y + xb * cw_ref[3:4, :]


def _rg_gates(xc, wgate_ref, ba_ref, bx_ref):
    half = RG_WIDTH // 2
    r, i = [], []
    for c in range(2):
        z = _mm(xc[:, c * half:(c + 1) * half], wgate_ref[c])
        r.append(z[:, :half])
        i.append(z[:, half:])
    r = jax.nn.sigmoid(jnp.concatenate(r, axis=1) + ba_ref[...])
    i = jax.nn.sigmoid(jnp.concatenate(i, axis=1) + bx_ref[...])
    return r, i


def _rg_decay_input(xc, r, i, lam_ref):
    log_a = (-RG_C * _softplus(-lam_ref[...])) * r
    a = jnp.exp(log_a)
    mult = jnp.sqrt(-jnp.tanh(log_a) * (a * a + 1.0))
    return a, mult * (i * xc)


def _gla_log_decay(a_low, wa2_ref, ba2_ref):
    z = _mm(a_low, wa2_ref[...]) + ba2_ref[...]
    return (-_softplus(-z)) / GLA_GATE_NORMALIZER


def _gla_intra(qt, kt, v, chunk):
    t = qt.shape[0]
    shift = int(np.log2(chunk))
    ri = lax.broadcasted_iota(jnp.int32, (t, t), 0)
    ci = lax.broadcasted_iota(jnp.int32, (t, t), 1)
    causal = (ri >= ci) & ((ri >> shift) == (ci >> shift))
    lane = lax.broadcasted_iota(jnp.int32, (t, PAIR_K), 1)
    outs = []
    for h in range(GLA_HEADS):
        p = h // 2
        qp = qt[:, p * PAIR_K:(p + 1) * PAIR_K]
        kp = kt[:, p * PAIR_K:(p + 1) * PAIR_K]
        mine = (lane >= GLA_DK) if (h % 2) else (lane < GLA_DK)
        s = _mm_nt(jnp.where(mine, qp, 0.0), kp)
        attn = jnp.where(causal, s, 0.0)
        outs.append(_mm(attn, v[:, h * GLA_DV:(h + 1) * GLA_DV].astype(BF16)))
    return outs


def _pair_blockdiag_mask():
    r = lax.broadcasted_iota(jnp.int32, (PAIR_K, PAIR_V), 0)
    c = lax.broadcasted_iota(jnp.int32, (PAIR_K, PAIR_V), 1)
    return (r < GLA_DK) == (c < GLA_DV)


def _gla_chunk_updates(k, v, b, b_last, chunk):
    bd_mask = _pair_blockdiag_mask()
    kv = {}
    for c in range(k.shape[0] // chunk):
        rows = slice(c * chunk, (c + 1) * chunk)
        for p in range(2):
            kl = slice(p * PAIR_K, (p + 1) * PAIR_K)
            kd = k[rows, kl] * jnp.exp(b_last[c:c + 1, kl] - b[rows, kl])
            kv[c, p] = jnp.where(bd_mask, _mm_tn(kd, v[rows, p * PAIR_V:(p + 1) * PAIR_V]), 0.0)
    return kv


def _gla_combine(intra, qt, states, chunk):
    n_chunks = qt.shape[0] // chunk
    heads = []
    for p in range(2):
        kl = slice(p * PAIR_K, (p + 1) * PAIR_K)
        inter = jnp.concatenate(
            [_mm(qt[c * chunk:(c + 1) * chunk, kl], states[c, p].astype(BF16)) for c in range(n_chunks)],
            axis=0)
        heads.append(intra[2 * p] + inter[:, :GLA_DV])
        heads.append(intra[2 * p + 1] + inter[:, GLA_DV:])
    return heads


def _gla_out(o_heads, g, norm_ref):
    outs = []
    for h in range(GLA_HEADS):
        outs.append(_rms(o_heads[h], norm_ref[:, h * GLA_DV:(h + 1) * GLA_DV]))
    return jnp.concatenate(outs, axis=1) * _silu(g)


def _softmax_rows(s):
    m = jnp.max(s, axis=-1, keepdims=True)
    e = jnp.exp(s - m)
    return e * (1.0 / jnp.sum(e, axis=-1, keepdims=True))


def _split_p(p_ref):
    xb = p_ref[:, OFF_RG_X:OFF_RG_Y]
    yb = p_ref[:, OFF_RG_Y:OFF_Q]
    q = p_ref[:, OFF_Q:OFF_K]
    k = p_ref[:, OFF_K:OFF_V]
    v = p_ref[:, OFF_V:OFF_G]
    g = p_ref[:, OFF_G:OFF_A]
    a_low = p_ref[:, OFF_A:P_WIDTH]
    return xb, yb, q, k, v, g, a_low


def _gla_prepare(q, k, log_a, chunk):
    b = _segment_cumsum(log_a, chunk)
    qt = (q * (GLA_DK ** -0.5)) * jnp.exp(b)
    kt = k * jnp.exp(-b)
    return b, qt, kt


def _mix_project(rg_out, gla_out, wout_ref):
    return (_mm(rg_out, wout_ref[0:RG_WIDTH, :]) + _mm(gla_out, wout_ref[RG_WIDTH:, :]))


def _ffn_in_kernel(xa_ref, xb_ref, n1_ref, wg_ref, wu_ref, wd_ref, n2_ref, win_ref,
                   x1a_ref, pa_ref, x1b_ref, pb_ref, *, tiles_a):
    def run(x_ref, x1_ref, p_ref):
        x = x_ref[...]
        x1 = x + 0.5 * _ffn(x, n1_ref, wg_ref, wu_ref, wd_ref)
        x1_ref[...] = x1
        p_ref[...] = _mm(_rms(x1, n2_ref[...]), win_ref[...])

    step = pl.program_id(0)
    pl.when(step < tiles_a)(lambda: run(xa_ref, x1a_ref, pa_ref))
    pl.when(step >= tiles_a)(lambda: run(xb_ref, x1b_ref, pb_ref))


def _mem_kv_kernel(m_ref, n_ref, wk_ref, wv_ref, k_ref, v_ref, kb_ref, vb_ref):
    h = _rms(m_ref[...], n_ref[...]).astype(BF16)
    k = jnp.dot(h, wk_ref[...], preferred_element_type=F32)
    v = jnp.dot(h, wv_ref[...], preferred_element_type=F32)
    for h in range(XA_HEADS):
        sl = slice(h * XA_HEAD_DIM, (h + 1) * XA_HEAD_DIM)
        k_ref[:, h, :] = k[:, sl]
        v_ref[:, h, :] = v[:, sl]
    kb_ref[...] = k.astype(BF16)
    vb_ref[...] = v.astype(BF16)


def _prompt_tail_kernel(
        x1_ref, p_ref, kb_ref, vb_ref,
        cw_ref, cb_ref, wgate_ref, ba_ref, bx_ref, lam_ref, rgn_ref,
        wa2_ref, ba2_ref, glan_ref, wout_ref,
        xn_ref, wcq_ref, wco_ref, n2_ref, wg_ref, wu_ref, wd_ref, fn_ref,
        y_ref, rgh_ref, s_ref,
        xpad_ref, h_ref, sbd_ref, b_ref, x3_ref, *, tiles_per_seq, n_tiles):
    step = pl.program_id(0)
    t_idx = jnp.minimum(step, n_tiles - 1) % tiles_per_seq
    slot = step % 2
    tt = x1_ref.shape[0]
    n_chunks = tt // GLA_CHUNK

    @pl.when(step == 0)
    def _():
        x3_ref[1] = jnp.zeros((tt, D_MODEL), F32)

    @pl.when(t_idx == 0)
    def _():
        xpad_ref[0:SUBLANES, :] = jnp.zeros((SUBLANES, RG_WIDTH), F32)
        h_ref[...] = jnp.zeros_like(h_ref)
        sbd_ref[...] = jnp.zeros_like(sbd_ref)

    x3_prev = x3_ref[1 - slot]
    ffn_out = []
    ffn = _ffn_pieces(x3_prev, n2_ref, wg_ref, wu_ref, wd_ref, ffn_out)
    placement = iter(FFN2_PLACEMENT)

    def emit_ffn():
        for _ in range(next(placement)):
            next(ffn)

    emit_ffn()

    xb, yb, q, k, v, g, a_low = _split_p(p_ref)

    xpad_ref[SUBLANES:SUBLANES + tt, :] = xb
    sh1 = xpad_ref[SUBLANES - 1:SUBLANES - 1 + tt, :]
    sh2 = xpad_ref[SUBLANES - 2:SUBLANES - 2 + tt, :]
    sh3 = xpad_ref[SUBLANES - 3:SUBLANES - 3 + tt, :]
    xc = _conv(xb, sh1, sh2, sh3, cw_ref, cb_ref)
    xpad_ref[0:SUBLANES, :] = xb[tt - SUBLANES:tt, :]
    emit_ffn()
    r, i = _rg_gates(xc, wgate_ref, ba_ref, bx_ref)
    a, u = _rg_decay_input(xc, r, i, lam_ref)
    emit_ffn()
    a_grp, h_grp = _segment_affine_scan(a, u, SUBLANES)
    emit_ffn()
    carry = h_ref[0:1, :]
    groups = []
    for gi in range(tt // SUBLANES):
        rows = slice(gi * SUBLANES, (gi + 1) * SUBLANES)
        groups.append(h_grp[rows, :] + a_grp[rows, :] * carry)
        carry = groups[-1][SUBLANES - 1:SUBLANES, :]
    hs = jnp.concatenate(groups, axis=0)
    h_ref[...] = jnp.broadcast_to(carry, h_ref.shape)
    rg_out = _rms(hs * _gelu_tanh(yb), rgn_ref[...])
    emit_ffn()

    log_a = _gla_log_decay(a_low, wa2_ref, ba2_ref)
    b, qt, kt = _gla_prepare(q, k, log_a, GLA_CHUNK)
    emit_ffn()
    intra = _gla_intra(qt, kt, v, GLA_CHUNK)
    emit_ffn()
    b_last = _last_rows(b_ref, b, GLA_CHUNK)
    dec_t = jnp.exp(b_last).T
    kv = _gla_chunk_updates(k, v, b, b_last, GLA_CHUNK)
    states = {}
    for p in range(2):
        s_bd = sbd_ref[p]
        for c in range(n_chunks):
            states[c, p] = s_bd
            s_bd = s_bd * dec_t[p * PAIR_K:(p + 1) * PAIR_K, c:c + 1] + kv[c, p]
        sbd_ref[p] = s_bd
    emit_ffn()
    gla_out = _gla_out(_gla_combine(intra, qt, states, GLA_CHUNK), g, glan_ref)
    emit_ffn()

    x2 = x1_ref[...] + _mix_project(rg_out, gla_out, wout_ref)

    qx = _mm(_rms(x2, xn_ref[...]), wcq_ref[...])
    emit_ffn()
    head = lambda h: slice(h * XA_HEAD_DIM, (h + 1) * XA_HEAD_DIM)
    s = jnp.concatenate([_mm_nt(qx[:, head(h)], kb_ref[:, head(h)]) for h in range(XA_HEADS)], axis=0)
    pr = _softmax_rows(s * (XA_HEAD_DIM ** -0.5))
    emit_ffn()
    o = jnp.concatenate([_mm(pr[h * tt:(h + 1) * tt, :], vb_ref[:, head(h)]) for h in range(XA_HEADS)], axis=1)
    x3_ref[slot] = x2 + _mm(o, wco_ref[...])
    for _ in ffn:
        pass
    y_ref[...] = _rms(x3_prev + 0.5 * ffn_out[0], fn_ref[...])

    @pl.when((t_idx == tiles_per_seq - 1) & (step < n_tiles))
    def _():
        rgh_ref[...] = h_ref[...]
        row = lax.broadcasted_iota(jnp.int32, (PAIR_K, GLA_DV), 0)
        for p in range(2):
            s_bd = sbd_ref[p]
            s_ref[p * PAIR_K:(p + 1) * PAIR_K, :] = jnp.where(
                row < GLA_DK, s_bd[:, :GLA_DV], s_bd[:, GLA_DV:])


def _sample_mix_kernel(
        x1_ref, p_ref, econv_ref, eh_ref, s0_ref,
        cw_ref, cb_ref, wgate_ref, ba_ref, bx_ref, lam_ref, rgn_ref,
        wa2_ref, ba2_ref, glan_ref, wout_ref, xn_ref, wcq_ref,
        x2_ref, q_ref, rgh_ref, s_ref,
        hs_ref, b_ref, *, seq):
    rows_n = x1_ref.shape[0]
    nb = rows_n // seq

    xb, yb, q, k, v, g, a_low = _split_p(p_ref)

    pos = _row_in_segment(xb.shape, seq)
    econv = econv_ref[...]
    sh = []
    for j in range(1, CONV_WIDTH):
        sh.append(jnp.where(pos >= j, _shift_rows(xb, j), _shift_rows(econv, rows_n - seq + j)))
    xc = _conv(xb, sh[0], sh[1], sh[2], cw_ref, cb_ref)
    r, i = _rg_gates(xc, wgate_ref, ba_ref, bx_ref)
    a, u = _rg_decay_input(xc, r, i, lam_ref)
    u = u + a * eh_ref[...]
    _, hs = _segment_affine_scan(a, u, seq)
    rgh_ref[...] = _last_rows(hs_ref, hs, seq)
    rg_out = _rms(hs * _gelu_tanh(yb), rgn_ref[...])

    log_a = _gla_log_decay(a_low, wa2_ref, ba2_ref)
    b, qt, kt = _gla_prepare(q, k, log_a, seq)
    intra = _gla_intra(qt, kt, v, seq)
    b_last = _last_rows(b_ref, b, seq)
    dec_t = jnp.exp(b_last).T
    kv = _gla_chunk_updates(k, v, b, b_last, seq)
    row = lax.broadcasted_iota(jnp.int32, (PAIR_K, GLA_DV), 0)
    top = row < GLA_DK
    states = {}
    for c in range(nb):
        for p in range(2):
            kl = slice(p * PAIR_K, (p + 1) * PAIR_K)
            s_pair = s0_ref[c, kl, :]
            s_bd = jnp.concatenate([jnp.where(top, s_pair, 0.0), jnp.where(top, 0.0, s_pair)], axis=1)
            states[c, p] = s_bd
            s_new = s_bd * dec_t[kl, c:c + 1] + kv[c, p]
            s_ref[c, kl, :] = jnp.where(top, s_new[:, :GLA_DV], s_new[:, GLA_DV:])
    gla_out = _gla_out(_gla_combine(intra, qt, states, seq), g, glan_ref)

    x2 = x1_ref[...] + _mix_project(rg_out, gla_out, wout_ref)
    x2_ref[...] = x2
    q_ref[...] = _mm(_rms(x2, xn_ref[...]), wcq_ref[...])


def _sample_attn_tail_kernel(q_ref, k_ref, v_ref, x2_ref, wco_ref, n2_ref, wg_ref, wu_ref, wd_ref, fn_ref,
                             y_ref, o_ref, *, seq):
    tiles = XA_HEAD_DIM // LANES
    group = tiles * XA_HEADS
    r = N_MEM * group
    nb = k_ref.shape[0] // r
    hs = XA_HEADS * seq
    lane = lax.broadcasted_iota(jnp.int32, (nb * hs, r), 1)
    head = (lax.broadcasted_iota(jnp.int32, (nb * hs, r), 0) // seq) & (XA_HEADS - 1)
    own = (lane & (group - 1)) == head
    s = []
    for j in range(nb):
        q = q_ref[j * seq:(j + 1) * seq, :]
        q_parts = jnp.concatenate(
            [q[:, c * LANES:(c + 1) * LANES] for c in range(XA_HEADS * tiles)], axis=0)
        part = _mm_nt(q_parts, k_ref[j * r:(j + 1) * r, :])
        for h in range(XA_HEADS):
            acc = part[h * tiles * seq:(h * tiles + 1) * seq, :]
            for c in range(1, tiles):
                blk = part[(h * tiles + c) * seq:(h * tiles + c + 1) * seq, :]
                acc = acc + pltpu.roll(blk, r - c * XA_HEADS, axis=1)
            s.append(acc)
    p = _softmax_rows(jnp.where(own, jnp.concatenate(s, axis=0) * (XA_HEAD_DIM ** -0.5), MASKED_SCORE))
    for j in range(nb):
        pj = p[j * hs:(j + 1) * hs, :]
        p_all = jnp.concatenate(
            [pj] + [pltpu.roll(pj, c * XA_HEADS, axis=1) for c in range(1, tiles)], axis=0)
        o = _mm(p_all, v_ref[j * r:(j + 1) * r, :].astype(BF16))
        for c in range(tiles):
            for h in range(XA_HEADS):
                col = h * XA_HEAD_DIM + c * LANES
                o_ref[j * seq:(j + 1) * seq, col:col + LANES] = (
                    o[(c * XA_HEADS + h) * seq:(c * XA_HEADS + h + 1) * seq, :])
    x3 = x2_ref[...] + _mm(o_ref[...], wco_ref[...])
    x4 = x3 + 0.5 * _ffn(x3, n2_ref, wg_ref, wu_ref, wd_ref)
    y_ref[...] = _rms(x4, fn_ref[...])


def _resident(arr):
    nd = arr.ndim
    return pl.BlockSpec(arr.shape, lambda *_: (0,) * nd, pipeline_mode=pl.Buffered(1))


def _params(sem, flags=None):
    return pltpu.CompilerParams(dimension_semantics=sem, vmem_limit_bytes=VMEM_LIMIT, flags=flags)


def _ffn_in(xa, xb, weights):
    tiles_a = xa.shape[0] // ROW_TILE
    tiles_b = xb.shape[0] // ROW_TILE
    spec_a = lambda w: pl.BlockSpec((ROW_TILE, w), lambda i: (jnp.minimum(i, tiles_a - 1), 0))
    spec_b = lambda w: pl.BlockSpec((ROW_TILE, w), lambda i: (jnp.maximum(i - tiles_a, 0), 0))
    out = lambda x, w: jax.ShapeDtypeStruct((x.shape[0], w), F32)
    return pl.pallas_call(
        functools.partial(_ffn_in_kernel, tiles_a=tiles_a),
        grid=(tiles_a + tiles_b,),
        in_specs=[spec_a(D_MODEL), spec_b(D_MODEL)] + [_resident(w) for w in weights],
        out_specs=[spec_a(D_MODEL), spec_a(P_WIDTH), spec_b(D_MODEL), spec_b(P_WIDTH)],
        out_shape=[out(xa, D_MODEL), out(xa, P_WIDTH), out(xb, D_MODEL), out(xb, P_WIDTH)],
        compiler_params=_params(("arbitrary",)),
        name="ffn_in",
    )(xa, xb, *weights)


def _mem_kv(mem, weights):
    rows = mem.shape[0]
    row_spec = pl.BlockSpec((ROW_TILE, D_MODEL), lambda i: (i, 0))
    head_spec = pl.BlockSpec((ROW_TILE, XA_HEADS, XA_HEAD_DIM), lambda i: (i, 0, 0))
    return pl.pallas_call(
        _mem_kv_kernel,
        grid=(rows // ROW_TILE,),
        in_specs=[row_spec] + [_resident(w) for w in weights],
        out_specs=[head_spec] * 2 + [row_spec] * 2,
        out_shape=[jax.ShapeDtypeStruct((rows, XA_HEADS, XA_HEAD_DIM), F32)] * 2
                  + [jax.ShapeDtypeStruct((rows, D_MODEL), BF16)] * 2,
        compiler_params=_params(("parallel",)),
        name="mem_kv",
    )(mem, *weights)


def _prompt_tail(x1, p, kb, vb, weights):
    batch, seq, _ = x1.shape
    tt = ROW_TILE
    tiles_per_seq = seq // tt
    n_tiles = batch * tiles_per_seq
    cur = lambda s: jnp.minimum(s, n_tiles - 1)
    prev = lambda s: jnp.maximum(s - 1, 0)
    tok = lambda w: pl.BlockSpec(
        (None, tt, w), lambda s: (cur(s) // tiles_per_seq, cur(s) % tiles_per_seq, 0))
    per_seq = lambda r, w: pl.BlockSpec((None, r, w), lambda s: (cur(s) // tiles_per_seq, 0, 0))
    out_tok = pl.BlockSpec(
        (None, tt, D_MODEL), lambda s: (prev(s) // tiles_per_seq, prev(s) % tiles_per_seq, 0))
    return pl.pallas_call(
        functools.partial(_prompt_tail_kernel, tiles_per_seq=tiles_per_seq, n_tiles=n_tiles),
        grid=(n_tiles + 1,),
        in_specs=[tok(D_MODEL), tok(P_WIDTH), per_seq(N_MEM, D_MODEL), per_seq(N_MEM, D_MODEL)]
                 + [_resident(w) for w in weights],
        out_specs=[out_tok, per_seq(SUBLANES, RG_WIDTH), per_seq(GLA_KEY_WIDTH, GLA_DV)],
        out_shape=[jax.ShapeDtypeStruct((batch, seq, D_MODEL), F32),
                   jax.ShapeDtypeStruct((batch, SUBLANES, RG_WIDTH), F32),
                   jax.ShapeDtypeStruct((batch, GLA_KEY_WIDTH, GLA_DV), F32)],
        scratch_shapes=[pltpu.VMEM((SUBLANES + tt, RG_WIDTH), F32),
                        pltpu.VMEM((SUBLANES, RG_WIDTH), F32),
                        pltpu.VMEM((2, PAIR_K, PAIR_V), F32),
                        pltpu.VMEM((GLA_KEY_WIDTH // LANES, tt, LANES), F32),
                        pltpu.VMEM((2, tt, D_MODEL), F32)],
        compiler_params=_params(("arbitrary",)),
        name="prompt_tail",
    )(x1, p, kb, vb, *weights)


def _sample_mix(x1, p, econv, eh, s0, weights, seq):
    rows = x1.shape[0]
    nb = ROW_TILE // seq
    row_spec = lambda w: pl.BlockSpec((ROW_TILE, w), lambda i: (i, 0))
    state_spec = pl.BlockSpec((nb, GLA_KEY_WIDTH, GLA_DV), lambda i: (i, 0, 0))
    return pl.pallas_call(
        functools.partial(_sample_mix_kernel, seq=seq),
        grid=(rows // ROW_TILE,),
        in_specs=[row_spec(D_MODEL), row_spec(P_WIDTH), row_spec(RG_WIDTH), row_spec(RG_WIDTH),
                  state_spec] + [_resident(w) for w in weights],
        out_specs=[row_spec(D_MODEL), row_spec(D_MODEL),
                   pl.BlockSpec((nb, RG_WIDTH), lambda i: (i, 0)), state_spec],
        out_shape=[jax.ShapeDtypeStruct((rows, D_MODEL), F32),
                   jax.ShapeDtypeStruct((rows, D_MODEL), F32),
                   jax.ShapeDtypeStruct((rows // seq, RG_WIDTH), F32),
                   jax.ShapeDtypeStruct((rows // seq, GLA_KEY_WIDTH, GLA_DV), F32)],
        scratch_shapes=[pltpu.VMEM((RG_WIDTH // LANES, ROW_TILE, LANES), F32),
                        pltpu.VMEM((GLA_KEY_WIDTH // LANES, ROW_TILE, LANES), F32)],
        compiler_params=_params(("parallel",)),
        name="sample_mix",
    )(x1, p, econv, eh, s0, *weights)


def _head_interleaved_rows(x):
    b, m, h, dh = x.shape
    tiles = dh // LANES
    return (x.reshape(b, m, h, tiles, LANES).transpose(0, 1, 3, 2, 4)
            .reshape(b * m * tiles * h, LANES))


def _sample_attn_tail(q, k, v, x2, weights, seq):
    rows = q.shape[0]
    nb = SAMPLE_ATTN_BATCH
    row_spec = pl.BlockSpec((nb * seq, D_MODEL), lambda i: (i, 0))
    kv_rows = k.shape[0] // (rows // seq)
    kv_spec = pl.BlockSpec((nb * kv_rows, LANES), lambda i: (i, 0))
    return pl.pallas_call(
        functools.partial(_sample_attn_tail_kernel, seq=seq),
        grid=(rows // (nb * seq),),
        in_specs=[row_spec, kv_spec, kv_spec, row_spec] + [_resident(w) for w in weights],
        out_specs=row_spec,
        out_shape=jax.ShapeDtypeStruct((rows, D_MODEL), F32),
        scratch_shapes=[pltpu.VMEM((nb * seq, D_MODEL), F32)],
        compiler_params=_params(("parallel",)),
        name="sample_attn_tail",
    )(q, k, v, x2, *weights)


def _block_diag_heads(w):
    h, n, _ = w.shape
    eye = jnp.eye(h, dtype=w.dtype)
    return (eye[:, None, :, None] * w[:, :, None, :]).reshape(h * n, h * n)


def kernel(x_prompt, x_sample, cache_mem_k, cache_mem_v, state_conv, state_rglru, state_gla, mem_prompt, ffn1_norm, ffn1_w_gate, ffn1_w_up, ffn1_w_down, mix_norm, w_in, conv_w, conv_b, rg_w_a, rg_b_a, rg_w_x, rg_b_x, rg_lambda, rg_out_norm, gla_w_a2, gla_b_a2, gla_out_norm, w_out, xattn_norm, mem_norm, w_cq, w_ck, w_cv, w_co, ffn2_norm, ffn2_w_gate, ffn2_w_up, ffn2_w_down, final_norm):
    bp, tp, _ = x_prompt.shape
    bs, ts, _ = x_sample.shape
    row = lambda g: g.reshape(1, -1)
    bf = lambda w: w.astype(BF16)

    w_in_p = bf(jnp.pad(w_in, ((0, 0), (0, P_WIDTH - D_IN))))
    hp = RG_HEADS // 2
    w_gate_rg = bf(jnp.stack([
        jnp.concatenate([_block_diag_heads(rg_w_a[c * hp:(c + 1) * hp]),
                         _block_diag_heads(rg_w_x[c * hp:(c + 1) * hp])], axis=1)
        for c in range(2)]))
    w_a2_p = bf(jnp.pad(gla_w_a2, ((0, A_PAD - GLA_GATE_RANK), (0, 0))))
    ffn_in_w = (row(ffn1_norm), bf(ffn1_w_gate), bf(ffn1_w_up), bf(ffn1_w_down), row(mix_norm), w_in_p)
    mixer_w = (conv_w, row(conv_b), w_gate_rg, row(rg_b_a), row(rg_b_x), row(rg_lambda),
               row(rg_out_norm), w_a2_p, row(gla_b_a2), row(jnp.tile(gla_out_norm, GLA_HEADS)),
               bf(w_out))
    w_cq_b, w_co_b = bf(w_cq), bf(w_co)
    ffn2_w = (row(ffn2_norm), bf(ffn2_w_gate), bf(ffn2_w_up), bf(ffn2_w_down), row(final_norm))

    mem_k, mem_v, mem_kb, mem_vb = _mem_kv(mem_prompt.reshape(bp * N_MEM, D_MODEL),
                                           (row(mem_norm), bf(w_ck), bf(w_cv)))
    x1_p, p_p, x1_s, p_s = _ffn_in(x_prompt.reshape(bp * tp, D_MODEL),
                                   x_sample.reshape(bs * ts, D_MODEL), ffn_in_w)
    p_p = p_p.reshape(bp, tp, P_WIDTH)
    y_p, rgh_p, s_p = _prompt_tail(
        x1_p.reshape(bp, tp, D_MODEL), p_p,
        mem_kb.reshape(bp, N_MEM, D_MODEL), mem_vb.reshape(bp, N_MEM, D_MODEL),
        mixer_w + (row(xattn_norm), w_cq_b, w_co_b) + ffn2_w)
    conv_p = p_p[:, tp - (CONV_WIDTH - 1):, OFF_RG_X:OFF_RG_Y]

    econv = jnp.pad(state_conv, ((0, 0), (ts - (CONV_WIDTH - 1), 0), (0, 0))).reshape(bs * ts, RG_WIDTH)
    eh = jnp.pad(state_rglru[:, None, :], ((0, 0), (0, ts - 1), (0, 0))).reshape(bs * ts, RG_WIDTH)
    x2_s, q_s, rgh_s, s_s = _sample_mix(
        x1_s, p_s, econv, eh, state_gla.reshape(bs, GLA_KEY_WIDTH, GLA_DV),
        mixer_w + (row(xattn_norm), w_cq_b), ts)
    y_s = _sample_attn_tail(q_s, _head_interleaved_rows(cache_mem_k), _head_interleaved_rows(cache_mem_v),
                            x2_s, (w_co_b,) + ffn2_w, ts)
    conv_s = p_s.reshape(bs, ts, P_WIDTH)[:, ts - (CONV_WIDTH - 1):, OFF_RG_X:OFF_RG_Y]

    return (y_p, y_s.reshape(bs, ts, D_MODEL),
            mem_k.reshape(bp, N_MEM, XA_HEADS, XA_HEAD_DIM),
            mem_v.reshape(bp, N_MEM, XA_HEADS, XA_HEAD_DIM),
            conv_p, rgh_p[:, 0, :], s_p.reshape(bp, GLA_HEADS, GLA_DK, GLA_DV),
            conv_s, rgh_s, s_s.reshape(bs, GLA_HEADS, GLA_DK, GLA_DV))
```

```python
import functools

import jax
import jax.numpy as jnp
import numpy as np
from jax import lax
from jax.experimental import pallas as pl
from jax.experimental.pallas import tpu as pltpu

F32 = jnp.float32
BF16 = jnp.bfloat16

D_MODEL = 1024
D_FF = 2816
RG_WIDTH = 512
RG_HEADS = 8
RG_HEAD_DIM = 64
CONV_WIDTH = 4
RG_C = 8.0
GLA_WIDTH = 512
GLA_HEADS = 4
GLA_DV = 128
GLA_DK = 64
GLA_KEY_WIDTH = 256
GLA_GATE_RANK = 16
GLA_GATE_NORMALIZER = 16.0
GLA_CHUNK = 32
N_MEM = 256
XA_HEADS = 4
XA_HEAD_DIM = 256
EPS = 1e-6
MASKED_SCORE = -1e30

OFF_RG_X = 0
OFF_RG_Y = 512
OFF_Q = 1024
OFF_K = 1280
OFF_V = 1536
OFF_G = 2048
OFF_A = 2560
D_IN = 2576

LANES = 128
SUBLANES = 8
A_PAD = LANES
P_WIDTH = OFF_A + A_PAD
PAIR_K = 2 * GLA_DK
PAIR_V = 2 * GLA_DV
VMEM_LIMIT = 56 * 1024 * 1024

ROW_TILE = 256
FF_CHUNK = 256
FF_DOWN_GROUP = 2
FFN2_PLACEMENT = (1, 1, 1, 1, 1, 1, 1, 1, 1, 1, 1)
SAMPLE_ATTN_BATCH = 8
KV_STREAMS = 2


def _rms(x, g):
    return x * lax.rsqrt(jnp.mean(x * x, axis=-1, keepdims=True) + EPS) * g


def _mm(a, w):
    return jnp.dot(a.astype(BF16), w, preferred_element_type=F32)


def _mm_nt(a, b):
    return lax.dot_general(a.astype(BF16), b.astype(BF16), (((1,), (1,)), ((), ())),
                           preferred_element_type=F32)


def _mm_tn(a, b):
    return lax.dot_general(a.astype(BF16), b.astype(BF16), (((0,), (0,)), ((), ())),
                           preferred_element_type=F32)


def _silu(x):
    return x * jax.nn.sigmoid(x)


def _gelu_tanh(x):
    c = np.float32(np.sqrt(2.0 / np.pi))
    return x * (0.5 * (1.0 + jnp.tanh(c * (x + 0.044715 * (x * x * x)))))


def _softplus(x):
    return jnp.maximum(x, 0.0) + jnp.log1p(jnp.exp(-jnp.abs(x)))


def _ffn(x, norm_ref, wg_ref, wu_ref, wd_ref):
    h = _rms(x, norm_ref[...]).astype(BF16)
    g = jnp.dot(h, wg_ref[...], preferred_element_type=F32)
    u = jnp.dot(h, wu_ref[...], preferred_element_type=F32)
    return _mm(_silu(g) * u, wd_ref[...])


def _last_rows(buf_ref, x, seg):
    n = x.shape[0] // seg
    outs = []
    for j in range(x.shape[1] // LANES):
        buf_ref[j] = x[:, j * LANES:(j + 1) * LANES]
        outs.append(buf_ref[j, pl.ds(seg - 1, n, stride=seg), :])
    return jnp.concatenate(outs, axis=1)


def _ffn_pieces(x, norm_ref, wg_ref, wu_ref, wd_ref, out):
    h = _rms(x, norm_ref[...]).astype(BF16)
    acc = None
    acts = []
    n = D_FF // FF_CHUNK
    for c in range(n):
        cols = slice(c * FF_CHUNK, (c + 1) * FF_CHUNK)
        g = jnp.dot(h, wg_ref[:, cols], preferred_element_type=F32)
        u = jnp.dot(h, wu_ref[:, cols], preferred_element_type=F32)
        acts.append((_silu(g) * u).astype(BF16))
        if len(acts) == FF_DOWN_GROUP or c == n - 1:
            lo = (c + 1 - len(acts)) * FF_CHUNK
            part = jnp.dot(jnp.concatenate(acts, axis=1), wd_ref[lo:(c + 1) * FF_CHUNK, :],
                           preferred_element_type=F32)
            acc = part if acc is None else acc + part
            acts = []
        yield
    out.append(acc)


def _shift_rows(x, s):
    return pltpu.roll(x, s, axis=0)


def _row_in_segment(shape, seg):
    return lax.broadcasted_iota(jnp.int32, shape, 0) & (seg - 1)


def _segment_cumsum(x, seg):
    pos = _row_in_segment(x.shape, seg)
    s = 1
    while s < seg:
        x = jnp.where(pos >= s, x + _shift_rows(x, s), x)
        s *= 2
    return x


def _segment_affine_scan(a, u, seg):
    pos = _row_in_segment(a.shape, seg)
    s = 1
    while s < seg:
        m = pos >= s
        u = jnp.where(m, a * _shift_rows(u, s) + u, u)
        a = jnp.where(m, a * _shift_rows(a, s), a)
        s *= 2
    return a, u


def _conv(xb, sh1, sh2, sh3, cw_ref, cb_ref):
    y = cb_ref[...] + sh3 * cw_ref[0:1, :]
    y = y + sh2 * cw_ref[1:2, :]
    y = y + sh1 * cw_ref[2:3, :]
    return y + xb * cw_ref[3:4, :]


def _rg_gates(xc, wgate_ref, ba_ref, bx_ref):
    half = RG_WIDTH // 2
    r, i = [], []
    for c in range(2):
        z = _mm(xc[:, c * half:(c + 1) * half], wgate_ref[c])
        r.append(z[:, :half])
        i.append(z[:, half:])
    r = jax.nn.sigmoid(jnp.concatenate(r, axis=1) + ba_ref[...])
    i = jax.nn.sigmoid(jnp.concatenate(i, axis=1) + bx_ref[...])
    return r, i


def _rg_decay_input(xc, r, i, lam_ref):
    log_a = (-RG_C * _softplus(-lam_ref[...])) * r
    a = jnp.exp(log_a)
    mult = jnp.sqrt(-jnp.tanh(log_a) * (a * a + 1.0))
    return a, mult * (i * xc)


def _gla_log_decay(a_low, wa2_ref, ba2_ref):
    z = _mm(a_low, wa2_ref[...]) + ba2_ref[...]
    return (-_softplus(-z)) / GLA_GATE_NORMALIZER


def _gla_intra(qt, kt, v, chunk):
    t = qt.shape[0]
    shift = int(np.log2(chunk))
    ri = lax.broadcasted_iota(jnp.int32, (t, t), 0)
    ci = lax.broadcasted_iota(jnp.int32, (t, t), 1)
    causal = (ri >= ci) & ((ri >> shift) == (ci >> shift))
    lane = lax.broadcasted_iota(jnp.int32, (t, PAIR_K), 1)
    outs = []
    for h in range(GLA_HEADS):
        p = h // 2
        qp = qt[:, p * PAIR_K:(p + 1) * PAIR_K]
        kp = kt[:, p * PAIR_K:(p + 1) * PAIR_K]
        mine = (lane >= GLA_DK) if (h % 2) else (lane < GLA_DK)
        s = _mm_nt(jnp.where(mine, qp, 0.0), kp)
        attn = jnp.where(causal, s, 0.0)
        outs.append(_mm(attn, v[:, h * GLA_DV:(h + 1) * GLA_DV].astype(BF16)))
    return outs


def _pair_blockdiag_mask():
    r = lax.broadcasted_iota(jnp.int32, (PAIR_K, PAIR_V), 0)
    c = lax.broadcasted_iota(jnp.int32, (PAIR_K, PAIR_V), 1)
    return (r < GLA_DK) == (c < GLA_DV)


def _gla_chunk_updates(k, v, b, b_last, chunk):
    bd_mask = _pair_blockdiag_mask()
    kv = {}
    for c in range(k.shape[0] // chunk):
        rows = slice(c * chunk, (c + 1) * chunk)
        for p in range(2):
            kl = slice(p * PAIR_K, (p + 1) * PAIR_K)
            kd = k[rows, kl] * jnp.exp(b_last[c:c + 1, kl] - b[rows, kl])
            kv[c, p] = jnp.where(bd_mask, _mm_tn(kd, v[rows, p * PAIR_V:(p + 1) * PAIR_V]), 0.0)
    return kv


def _gla_combine(intra, qt, states, chunk):
    n_chunks = qt.shape[0] // chunk
    heads = []
    for p in range(2):
        kl = slice(p * PAIR_K, (p + 1) * PAIR_K)
        inter = jnp.concatenate(
            [_mm(qt[c * chunk:(c + 1) * chunk, kl], states[c, p].astype(BF16)) for c in range(n_chunks)],
            axis=0)
        heads.append(intra[2 * p] + inter[:, :GLA_DV])
        heads.append(intra[2 * p + 1] + inter[:, GLA_DV:])
    return heads


def _gla_out(o_heads, g, norm_ref):
    outs = []
    for h in range(GLA_HEADS):
        outs.append(_rms(o_heads[h], norm_ref[:, h * GLA_DV:(h + 1) * GLA_DV]))
    return jnp.concatenate(outs, axis=1) * _silu(g)


def _softmax_rows(s):
    m = jnp.max(s, axis=-1, keepdims=True)
    e = jnp.exp(s - m)
    return e * (1.0 / jnp.sum(e, axis=-1, keepdims=True))


def _split_p(p_ref):
    xb = p_ref[:, OFF_RG_X:OFF_RG_Y]
    yb = p_ref[:, OFF_RG_Y:OFF_Q]
    q = p_ref[:, OFF_Q:OFF_K]
    k = p_ref[:, OFF_K:OFF_V]
    v = p_ref[:, OFF_V:OFF_G]
    g = p_ref[:, OFF_G:OFF_A]
    a_low = p_ref[:, OFF_A:P_WIDTH]
    return xb, yb, q, k, v, g, a_low


def _gla_prepare(q, k, log_a, chunk):
    b = _segment_cumsum(log_a, chunk)
    qt = (q * (GLA_DK ** -0.5)) * jnp.exp(b)
    kt = k * jnp.exp(-b)
    return b, qt, kt


def _mix_project(rg_out, gla_out, wout_ref):
    return (_mm(rg_out, wout_ref[0:RG_WIDTH, :]) + _mm(gla_out, wout_ref[RG_WIDTH:, :]))


def _ffn_in_kernel(xa_ref, xb_ref, n1_ref, wg_ref, wu_ref, wd_ref, n2_ref, win_ref,
                   x1a_ref, pa_ref, x1b_ref, pb_ref, *, tiles_a):
    def run(x_ref, x1_ref, p_ref):
        x = x_ref[...]
        x1 = x + 0.5 * _ffn(x, n1_ref, wg_ref, wu_ref, wd_ref)
        x1_ref[...] = x1
        p_ref[...] = _mm(_rms(x1, n2_ref[...]), win_ref[...])

    step = pl.program_id(0)
    pl.when(step < tiles_a)(lambda: run(xa_ref, x1a_ref, pa_ref))
    pl.when(step >= tiles_a)(lambda: run(xb_ref, x1b_ref, pb_ref))


def _mem_kv_kernel(m_ref, n_ref, wk_ref, wv_ref, k_ref, v_ref, kb_ref, vb_ref):
    h = _rms(m_ref[...], n_ref[...]).astype(BF16)
    k = jnp.dot(h, wk_ref[...], preferred_element_type=F32)
    v = jnp.dot(h, wv_ref[...], preferred_element_type=F32)
    for h in range(XA_HEADS):
        sl = slice(h * XA_HEAD_DIM, (h + 1) * XA_HEAD_DIM)
        k_ref[:, h, :] = k[:, sl]
        v_ref[:, h, :] = v[:, sl]
    kb_ref[...] = k.astype(BF16)
    vb_ref[...] = v.astype(BF16)


def _prompt_tail_kernel(
        x1_ref, p_ref, kb_ref, vb_ref,
        cw_ref, cb_ref, wgate_ref, ba_ref, bx_ref, lam_ref, rgn_ref,
        wa2_ref, ba2_ref, glan_ref, wout_ref,
        xn_ref, wcq_ref, wco_ref, n2_ref, wg_ref, wu_ref, wd_ref, fn_ref,
        y_ref, rgh_ref, s_ref,
        xpad_ref, h_ref, sbd_ref, b_ref, x3_ref, *, tiles_per_seq, n_tiles):
    step = pl.program_id(0)
    t_idx = jnp.minimum(step, n_tiles - 1) % tiles_per_seq
    slot = step % 2
    tt = x1_ref.shape[0]
    n_chunks = tt // GLA_CHUNK

    @pl.when(step == 0)
    def _():
        x3_ref[1] = jnp.zeros((tt, D_MODEL), F32)

    @pl.when(t_idx == 0)
    def _():
        xpad_ref[0:SUBLANES, :] = jnp.zeros((SUBLANES, RG_WIDTH), F32)
        h_ref[...] = jnp.zeros_like(h_ref)
        sbd_ref[...] = jnp.zeros_like(sbd_ref)

    x3_prev = x3_ref[1 - slot]
    ffn_out = []
    ffn = _ffn_pieces(x3_prev, n2_ref, wg_ref, wu_ref, wd_ref, ffn_out)
    placement = iter(FFN2_PLACEMENT)

    def emit_ffn():
        for _ in range(next(placement)):
            next(ffn)

    emit_ffn()

    xb, yb, q, k, v, g, a_low = _split_p(p_ref)

    xpad_ref[SUBLANES:SUBLANES + tt, :] = xb
    sh1 = xpad_ref[SUBLANES - 1:SUBLANES - 1 + tt, :]
    sh2 = xpad_ref[SUBLANES - 2:SUBLANES - 2 + tt, :]
    sh3 = xpad_ref[SUBLANES - 3:SUBLANES - 3 + tt, :]
    xc = _conv(xb, sh1, sh2, sh3, cw_ref, cb_ref)
    xpad_ref[0:SUBLANES, :] = xb[tt - SUBLANES:tt, :]
    emit_ffn()
    r, i = _rg_gates(xc, wgate_ref, ba_ref, bx_ref)
    a, u = _rg_decay_input(xc, r, i, lam_ref)
    emit_ffn()
    a_grp, h_grp = _segment_affine_scan(a, u, SUBLANES)
    emit_ffn()
    carry = h_ref[0:1, :]
    groups = []
    for gi in range(tt // SUBLANES):
        rows = slice(gi * SUBLANES, (gi + 1) * SUBLANES)
        groups.append(h_grp[rows, :] + a_grp[rows, :] * carry)
        carry = groups[-1][SUBLANES - 1:SUBLANES, :]
    hs = jnp.concatenate(groups, axis=0)
    h_ref[...] = jnp.broadcast_to(carry, h_ref.shape)
    rg_out = _rms(hs * _gelu_tanh(yb), rgn_ref[...])
    emit_ffn()

    log_a = _gla_log_decay(a_low, wa2_ref, ba2_ref)
    b, qt, kt = _gla_prepare(q, k, log_a, GLA_CHUNK)
    emit_ffn()
    intra = _gla_intra(qt, kt, v, GLA_CHUNK)
    emit_ffn()
    b_last = _last_rows(b_ref, b, GLA_CHUNK)
    dec_t = jnp.exp(b_last).T
    kv = _gla_chunk_updates(k, v, b, b_last, GLA_CHUNK)
    states = {}
    for p in range(2):
        s_bd = sbd_ref[p]
        for c in range(n_chunks):
            states[c, p] = s_bd
            s_bd = s_bd * dec_t[p * PAIR_K:(p + 1) * PAIR_K, c:c + 1] + kv[c, p]
        sbd_ref[p] = s_bd
    emit_ffn()
    gla_out = _gla_out(_gla_combine(intra, qt, states, GLA_CHUNK), g, glan_ref)
    emit_ffn()

    x2 = x1_ref[...] + _mix_project(rg_out, gla_out, wout_ref)

    qx = _mm(_rms(x2, xn_ref[...]), wcq_ref[...])
    emit_ffn()
    head = lambda h: slice(h * XA_HEAD_DIM, (h + 1) * XA_HEAD_DIM)
    s = jnp.concatenate([_mm_nt(qx[:, head(h)], kb_ref[:, head(h)]) for h in range(XA_HEADS)], axis=0)
    pr = _softmax_rows(s * (XA_HEAD_DIM ** -0.5))
    emit_ffn()
    o = jnp.concatenate([_mm(pr[h * tt:(h + 1) * tt, :], vb_ref[:, head(h)]) for h in range(XA_HEADS)], axis=1)
    x3_ref[slot] = x2 + _mm(o, wco_ref[...])
    for _ in ffn:
        pass
    y_ref[...] = _rms(x3_prev + 0.5 * ffn_out[0], fn_ref[...])

    @pl.when((t_idx == tiles_per_seq - 1) & (step < n_tiles))
    def _():
        rgh_ref[...] = h_ref[...]
        row = lax.broadcasted_iota(jnp.int32, (PAIR_K, GLA_DV), 0)
        for p in range(2):
            s_bd = sbd_ref[p]
            s_ref[p * PAIR_K:(p + 1) * PAIR_K, :] = jnp.where(
                row < GLA_DK, s_bd[:, :GLA_DV], s_bd[:, GLA_DV:])


def _sample_mix_kernel(
        x1_ref, p_ref, econv_ref, eh_ref, s0_ref,
        cw_ref, cb_ref, wgate_ref, ba_ref, bx_ref, lam_ref, rgn_ref,
        wa2_ref, ba2_ref, glan_ref, wout_ref, xn_ref, wcq_ref,
        x2_ref, q_ref, rgh_ref, s_ref,
        hs_ref, b_ref, *, seq):
    rows_n = x1_ref.shape[0]
    nb = rows_n // seq

    xb, yb, q, k, v, g, a_low = _split_p(p_ref)

    pos = _row_in_segment(xb.shape, seq)
    econv = econv_ref[...]
    sh = []
    for j in range(1, CONV_WIDTH):
        sh.append(jnp.where(pos >= j, _shift_rows(xb, j), _shift_rows(econv, rows_n - seq + j)))
    xc = _conv(xb, sh[0], sh[1], sh[2], cw_ref, cb_ref)
    r, i = _rg_gates(xc, wgate_ref, ba_ref, bx_ref)
    a, u = _rg_decay_input(xc, r, i, lam_ref)
    u = u + a * eh_ref[...]
    _, hs = _segment_affine_scan(a, u, seq)
    rgh_ref[...] = _last_rows(hs_ref, hs, seq)
    rg_out = _rms(hs * _gelu_tanh(yb), rgn_ref[...])

    log_a = _gla_log_decay(a_low, wa2_ref, ba2_ref)
    b, qt, kt = _gla_prepare(q, k, log_a, seq)
    intra = _gla_intra(qt, kt, v, seq)
    b_last = _last_rows(b_ref, b, seq)
    dec_t = jnp.exp(b_last).T
    kv = _gla_chunk_updates(k, v, b, b_last, seq)
    row = lax.broadcasted_iota(jnp.int32, (PAIR_K, GLA_DV), 0)
    top = row < GLA_DK
    states = {}
    for c in range(nb):
        for p in range(2):
            kl = slice(p * PAIR_K, (p + 1) * PAIR_K)
            s_pair = s0_ref[c, kl, :]
            s_bd = jnp.concatenate([jnp.where(top, s_pair, 0.0), jnp.where(top, 0.0, s_pair)], axis=1)
            states[c, p] = s_bd
            s_new = s_bd * dec_t[kl, c:c + 1] + kv[c, p]
            s_ref[c, kl, :] = jnp.where(top, s_new[:, :GLA_DV], s_new[:, GLA_DV:])
    gla_out = _gla_out(_gla_combine(intra, qt, states, seq), g, glan_ref)

    x2 = x1_ref[...] + _mix_project(rg_out, gla_out, wout_ref)
    x2_ref[...] = x2
    q_ref[...] = _mm(_rms(x2, xn_ref[...]), wcq_ref[...])


def _sample_attn_tail_kernel(q_ref, *refs, seq, streams):
    tiles = XA_HEAD_DIM // LANES
    group = tiles * XA_HEADS
    r = N_MEM * group
    k_refs, v_refs = refs[:streams], refs[streams:2 * streams]
    x2_ref, wco_ref, n2_ref, wg_ref, wu_ref, wd_ref, fn_ref, y_ref, o_ref = refs[2 * streams:]
    per = k_refs[0].shape[0] // r
    nb = per * streams
    rows_of = lambda blocks, j: blocks[j // per][(j % per) * r:(j % per + 1) * r, :]
    hs = XA_HEADS * seq
    lane = lax.broadcasted_iota(jnp.int32, (nb * hs, r), 1)
    head = (lax.broadcasted_iota(jnp.int32, (nb * hs, r), 0) // seq) & (XA_HEADS - 1)
    own = (lane & (group - 1)) == head
    s = []
    for j in range(nb):
        q = q_ref[j * seq:(j + 1) * seq, :]
        q_parts = jnp.concatenate(
            [q[:, c * LANES:(c + 1) * LANES] for c in range(XA_HEADS * tiles)], axis=0)
        part = _mm_nt(q_parts, rows_of(k_refs, j))
        for h in range(XA_HEADS):
            acc = part[h * tiles * seq:(h * tiles + 1) * seq, :]
            for c in range(1, tiles):
                blk = part[(h * tiles + c) * seq:(h * tiles + c + 1) * seq, :]
                acc = acc + pltpu.roll(blk, r - c * XA_HEADS, axis=1)
            s.append(acc)
    p = _softmax_rows(jnp.where(own, jnp.concatenate(s, axis=0) * (XA_HEAD_DIM ** -0.5), MASKED_SCORE))
    for j in range(nb):
        pj = p[j * hs:(j + 1) * hs, :]
        p_all = jnp.concatenate(
            [pj] + [pltpu.roll(pj, c * XA_HEADS, axis=1) for c in range(1, tiles)], axis=0)
        o = _mm(p_all, rows_of(v_refs, j).astype(BF16))
        for c in range(tiles):
            for h in range(XA_HEADS):
                col = h * XA_HEAD_DIM + c * LANES
                o_ref[j * seq:(j + 1) * seq, col:col + LANES] = (
                    o[(c * XA_HEADS + h) * seq:(c * XA_HEADS + h + 1) * seq, :])
    x3 = x2_ref[...] + _mm(o_ref[...], wco_ref[...])
    x4 = x3 + 0.5 * _ffn(x3, n2_ref, wg_ref, wu_ref, wd_ref)
    y_ref[...] = _rms(x4, fn_ref[...])


def _resident(arr):
    nd = arr.ndim
    return pl.BlockSpec(arr.shape, lambda *_: (0,) * nd, pipeline_mode=pl.Buffered(1))


def _params(sem, flags=None):
    return pltpu.CompilerParams(dimension_semantics=sem, vmem_limit_bytes=VMEM_LIMIT, flags=flags)


def _ffn_in(xa, xb, weights):
    tiles_a = xa.shape[0] // ROW_TILE
    tiles_b = xb.shape[0] // ROW_TILE
    spec_a = lambda w: pl.BlockSpec((ROW_TILE, w), lambda i: (jnp.minimum(i, tiles_a - 1), 0))
    spec_b = lambda w: pl.BlockSpec((ROW_TILE, w), lambda i: (jnp.maximum(i - tiles_a, 0), 0))
    out = lambda x, w: jax.ShapeDtypeStruct((x.shape[0], w), F32)
    return pl.pallas_call(
        functools.partial(_ffn_in_kernel, tiles_a=tiles_a),
        grid=(tiles_a + tiles_b,),
        in_specs=[spec_a(D_MODEL), spec_b(D_MODEL)] + [_resident(w) for w in weights],
        out_specs=[spec_a(D_MODEL), spec_a(P_WIDTH), spec_b(D_MODEL), spec_b(P_WIDTH)],
        out_shape=[out(xa, D_MODEL), out(xa, P_WIDTH), out(xb, D_MODEL), out(xb, P_WIDTH)],
        compiler_params=_params(("arbitrary",)),
        name="ffn_in",
    )(xa, xb, *weights)


def _mem_kv(mem, weights):
    rows = mem.shape[0]
    row_spec = pl.BlockSpec((ROW_TILE, D_MODEL), lambda i: (i, 0))
    head_spec = pl.BlockSpec((ROW_TILE, XA_HEADS, XA_HEAD_DIM), lambda i: (i, 0, 0))
    return pl.pallas_call(
        _mem_kv_kernel,
        grid=(rows // ROW_TILE,),
        in_specs=[row_spec] + [_resident(w) for w in weights],
        out_specs=[head_spec] * 2 + [row_spec] * 2,
        out_shape=[jax.ShapeDtypeStruct((rows, XA_HEADS, XA_HEAD_DIM), F32)] * 2
                  + [jax.ShapeDtypeStruct((rows, D_MODEL), BF16)] * 2,
        compiler_params=_params(("parallel",)),
        name="mem_kv",
    )(mem, *weights)


def _prompt_tail(x1, p, kb, vb, weights):
    batch, seq, _ = x1.shape
    tt = ROW_TILE
    tiles_per_seq = seq // tt
    n_tiles = batch * tiles_per_seq
    cur = lambda s: jnp.minimum(s, n_tiles - 1)
    prev = lambda s: jnp.maximum(s - 1, 0)
    tok = lambda w: pl.BlockSpec(
        (None, tt, w), lambda s: (cur(s) // tiles_per_seq, cur(s) % tiles_per_seq, 0))
    per_seq = lambda r, w: pl.BlockSpec((None, r, w), lambda s: (cur(s) // tiles_per_seq, 0, 0))
    out_tok = pl.BlockSpec(
        (None, tt, D_MODEL), lambda s: (prev(s) // tiles_per_seq, prev(s) % tiles_per_seq, 0))
    return pl.pallas_call(
        functools.partial(_prompt_tail_kernel, tiles_per_seq=tiles_per_seq, n_tiles=n_tiles),
        grid=(n_tiles + 1,),
        in_specs=[tok(D_MODEL), tok(P_WIDTH), per_seq(N_MEM, D_MODEL), per_seq(N_MEM, D_MODEL)]
                 + [_resident(w) for w in weights],
        out_specs=[out_tok, per_seq(SUBLANES, RG_WIDTH), per_seq(GLA_KEY_WIDTH, GLA_DV)],
        out_shape=[jax.ShapeDtypeStruct((batch, seq, D_MODEL), F32),
                   jax.ShapeDtypeStruct((batch, SUBLANES, RG_WIDTH), F32),
                   jax.ShapeDtypeStruct((batch, GLA_KEY_WIDTH, GLA_DV), F32)],
        scratch_shapes=[pltpu.VMEM((SUBLANES + tt, RG_WIDTH), F32),
                        pltpu.VMEM((SUBLANES, RG_WIDTH), F32),
                        pltpu.VMEM((2, PAIR_K, PAIR_V), F32),
                        pltpu.VMEM((GLA_KEY_WIDTH // LANES, tt, LANES), F32),
                        pltpu.VMEM((2, tt, D_MODEL), F32)],
        compiler_params=_params(("arbitrary",)),
        name="prompt_tail",
    )(x1, p, kb, vb, *weights)


def _sample_mix(x1, p, econv, eh, s0, weights, seq):
    rows = x1.shape[0]
    nb = ROW_TILE // seq
    row_spec = lambda w: pl.BlockSpec((ROW_TILE, w), lambda i: (i, 0))
    state_spec = pl.BlockSpec((nb, GLA_KEY_WIDTH, GLA_DV), lambda i: (i, 0, 0))
    return pl.pallas_call(
        functools.partial(_sample_mix_kernel, seq=seq),
        grid=(rows // ROW_TILE,),
        in_specs=[row_spec(D_MODEL), row_spec(P_WIDTH), row_spec(RG_WIDTH), row_spec(RG_WIDTH),
                  state_spec] + [_resident(w) for w in weights],
        out_specs=[row_spec(D_MODEL), row_spec(D_MODEL),
                   pl.BlockSpec((nb, RG_WIDTH), lambda i: (i, 0)), state_spec],
        out_shape=[jax.ShapeDtypeStruct((rows, D_MODEL), F32),
                   jax.ShapeDtypeStruct((rows, D_MODEL), F32),
                   jax.ShapeDtypeStruct((rows // seq, RG_WIDTH), F32),
                   jax.ShapeDtypeStruct((rows // seq, GLA_KEY_WIDTH, GLA_DV), F32)],
        scratch_shapes=[pltpu.VMEM((RG_WIDTH // LANES, ROW_TILE, LANES), F32),
                        pltpu.VMEM((GLA_KEY_WIDTH // LANES, ROW_TILE, LANES), F32)],
        compiler_params=_params(("parallel",)),
        name="sample_mix",
    )(x1, p, econv, eh, s0, *weights)


def _head_interleaved_rows(x):
    b, m, h, dh = x.shape
    tiles = dh // LANES
    return (x.reshape(b, m, h, tiles, LANES).transpose(0, 1, 3, 2, 4)
            .reshape(b * m * tiles * h, LANES))


def _sample_attn_tail(q, k, v, x2, weights, seq):
    rows = q.shape[0]
    nb = SAMPLE_ATTN_BATCH
    row_spec = pl.BlockSpec((nb * seq, D_MODEL), lambda i: (i, 0))
    kv_rows = k.shape[0] // (rows // seq)
    n = KV_STREAMS
    kv_specs = [pl.BlockSpec((nb // n * kv_rows, LANES), lambda i, c=c: (n * i + c, 0)) for c in range(n)]
    return pl.pallas_call(
        functools.partial(_sample_attn_tail_kernel, seq=seq, streams=n),
        grid=(rows // (nb * seq),),
        in_specs=[row_spec] + kv_specs + kv_specs + [row_spec] + [_resident(w) for w in weights],
        out_specs=row_spec,
        out_shape=jax.ShapeDtypeStruct((rows, D_MODEL), F32),
        scratch_shapes=[pltpu.VMEM((nb * seq, D_MODEL), F32)],
        compiler_params=_params(("parallel",)),
        name="sample_attn_tail",
    )(q, *([k] * n), *([v] * n), x2, *weights)


def _block_diag_heads(w):
    h, n, _ = w.shape
    eye = jnp.eye(h, dtype=w.dtype)
    return (eye[:, None, :, None] * w[:, :, None, :]).reshape(h * n, h * n)


def kernel(x_prompt, x_sample, cache_mem_k, cache_mem_v, state_conv, state_rglru, state_gla, mem_prompt, ffn1_norm, ffn1_w_gate, ffn1_w_up, ffn1_w_down, mix_norm, w_in, conv_w, conv_b, rg_w_a, rg_b_a, rg_w_x, rg_b_x, rg_lambda, rg_out_norm, gla_w_a2, gla_b_a2, gla_out_norm, w_out, xattn_norm, mem_norm, w_cq, w_ck, w_cv, w_co, ffn2_norm, ffn2_w_gate, ffn2_w_up, ffn2_w_down, final_norm):
    bp, tp, _ = x_prompt.shape
    bs, ts, _ = x_sample.shape
    row = lambda g: g.reshape(1, -1)
    bf = lambda w: w.astype(BF16)

    w_in_p = bf(jnp.pad(w_in, ((0, 0), (0, P_WIDTH - D_IN))))
    hp = RG_HEADS // 2
    w_gate_rg = bf(jnp.stack([
        jnp.concatenate([_block_diag_heads(rg_w_a[c * hp:(c + 1) * hp]),
                         _block_diag_heads(rg_w_x[c * hp:(c + 1) * hp])], axis=1)
        for c in range(2)]))
    w_a2_p = bf(jnp.pad(gla_w_a2, ((0, A_PAD - GLA_GATE_RANK), (0, 0))))
    ffn_in_w = (row(ffn1_norm), bf(ffn1_w_gate), bf(ffn1_w_up), bf(ffn1_w_down), row(mix_norm), w_in_p)
    mixer_w = (conv_w, row(conv_b), w_gate_rg, row(rg_b_a), row(rg_b_x), row(rg_lambda),
               row(rg_out_norm), w_a2_p, row(gla_b_a2), row(jnp.tile(gla_out_norm, GLA_HEADS)),
               bf(w_out))
    w_cq_b, w_co_b = bf(w_cq), bf(w_co)
    ffn2_w = (row(ffn2_norm), bf(ffn2_w_gate), bf(ffn2_w_up), bf(ffn2_w_down), row(final_norm))

    mem_k, mem_v, mem_kb, mem_vb = _mem_kv(mem_prompt.reshape(bp * N_MEM, D_MODEL),
                                           (row(mem_norm), bf(w_ck), bf(w_cv)))
    x1_p, p_p, x1_s, p_s = _ffn_in(x_prompt.reshape(bp * tp, D_MODEL),
                                   x_sample.reshape(bs * ts, D_MODEL), ffn_in_w)
    p_p = p_p.reshape(bp, tp, P_WIDTH)
    y_p, rgh_p, s_p = _prompt_tail(
        x1_p.reshape(bp, tp, D_MODEL), p_p,
        mem_kb.reshape(bp, N_MEM, D_MODEL), mem_vb.reshape(bp, N_MEM, D_MODEL),
        mixer_w + (row(xattn_norm), w_cq_b, w_co_b) + ffn2_w)
    conv_p = p_p[:, tp - (CONV_WIDTH - 1):, OFF_RG_X:OFF_RG_Y]

    econv = jnp.pad(state_conv, ((0, 0), (ts - (CONV_WIDTH - 1), 0), (0, 0))).reshape(bs * ts, RG_WIDTH)
    eh = jnp.pad(state_rglru[:, None, :], ((0, 0), (0, ts - 1), (0, 0))).reshape(bs * ts, RG_WIDTH)
    x2_s, q_s, rgh_s, s_s = _sample_mix(
        x1_s, p_s, econv, eh, state_gla.reshape(bs, GLA_KEY_WIDTH, GLA_DV),
        mixer_w + (row(xattn_norm), w_cq_b), ts)
    y_s = _sample_attn_tail(q_s, _head_interleaved_rows(cache_mem_k), _head_interleaved_rows(cache_mem_v),
                            x2_s, (w_co_b,) + ffn2_w, ts)
    conv_s = p_s.reshape(bs, ts, P_WIDTH)[:, ts - (CONV_WIDTH - 1):, OFF_RG_X:OFF_RG_Y]

    return (y_p, y_s.reshape(bs, ts, D_MODEL),
            mem_k.reshape(bp, N_MEM, XA_HEADS, XA_HEAD_DIM),
            mem_v.reshape(bp, N_MEM, XA_HEADS, XA_HEAD_DIM),
            conv_p, rgh_p[:, 0, :], s_p.reshape(bp, GLA_HEADS, GLA_DK, GLA_DV),
            conv_s, rgh_s, s_s.reshape(bs, GLA_HEADS, GLA_DK, GLA_DV))
```

```python
import functools

import jax
import jax.numpy as jnp
import numpy as np
from jax import lax
from jax.experimental import pallas as pl
from jax.experimental.pallas import tpu as pltpu

F32 = jnp.float32
BF16 = jnp.bfloat16

D_MODEL = 1024
D_FF = 2816
RG_WIDTH = 512
RG_HEADS = 8
RG_HEAD_DIM = 64
CONV_WIDTH = 4
RG_C = 8.0
GLA_WIDTH = 512
GLA_HEADS = 4
GLA_DV = 128
GLA_DK = 64
GLA_KEY_WIDTH = 256
GLA_GATE_RANK = 16
GLA_GATE_NORMALIZER = 16.0
GLA_CHUNK = 32
N_MEM = 256
XA_HEADS = 4
XA_HEAD_DIM = 256
EPS = 1e-6
MASKED_SCORE = -1e30

OFF_RG_X = 0
OFF_RG_Y = 512
OFF_Q = 1024
OFF_K = 1280
OFF_V = 1536
OFF_G = 2048
OFF_A = 2560
D_IN = 2576

LANES = 128
SUBLANES = 8
A_PAD = LANES
P_WIDTH = OFF_A + A_PAD
PAIR_K = 2 * GLA_DK
PAIR_V = 2 * GLA_DV
VMEM_LIMIT = 58 * 1024 * 1024

ROW_TILE = 256
FFN_IN_TILES = 2
PROMPT_TILES = 2
FF_CHUNK = 256
FF_DOWN_GROUP = 2
FFN2_PLACEMENT = (1, 1, 1, 1, 1, 1, 1, 1, 1, 1, 1)
SAMPLE_ATTN_BATCH = 8


def _rms(x, g):
    return x * lax.rsqrt(jnp.mean(x * x, axis=-1, keepdims=True) + EPS) * g


def _mm(a, w):
    return jnp.dot(a.astype(BF16), w, preferred_element_type=F32)


def _mm_nt(a, b):
    return lax.dot_general(a.astype(BF16), b.astype(BF16), (((1,), (1,)), ((), ())),
                           preferred_element_type=F32)


def _mm_tn(a, b):
    return lax.dot_general(a.astype(BF16), b.astype(BF16), (((0,), (0,)), ((), ())),
                           preferred_element_type=F32)


def _silu(x):
    return x * jax.nn.sigmoid(x)


def _gelu_tanh(x):
    c = np.float32(np.sqrt(2.0 / np.pi))
    return x * (0.5 * (1.0 + jnp.tanh(c * (x + 0.044715 * (x * x * x)))))


def _softplus(x):
    return jnp.maximum(x, 0.0) + jnp.log1p(jnp.exp(-jnp.abs(x)))


def _ffn(x, norm_ref, wg_ref, wu_ref, wd_ref):
    h = _rms(x, norm_ref[...]).astype(BF16)
    g = jnp.dot(h, wg_ref[...], preferred_element_type=F32)
    u = jnp.dot(h, wu_ref[...], preferred_element_type=F32)
    return _mm(_silu(g) * u, wd_ref[...])


def _last_rows(buf_ref, x, seg):
    n = x.shape[0] // seg
    outs = []
    for j in range(x.shape[1] // LANES):
        buf_ref[j] = x[:, j * LANES:(j + 1) * LANES]
        outs.append(buf_ref[j, pl.ds(seg - 1, n, stride=seg), :])
    return jnp.concatenate(outs, axis=1)


def _ffn_pieces(x, norm_ref, wg_ref, wu_ref, wd_ref, out):
    h = _rms(x, norm_ref[...]).astype(BF16)
    acc = None
    acts = []
    n = D_FF // FF_CHUNK
    for c in range(n):
        cols = slice(c * FF_CHUNK, (c + 1) * FF_CHUNK)
        g = jnp.dot(h, wg_ref[:, cols], preferred_element_type=F32)
        u = jnp.dot(h, wu_ref[:, cols], preferred_element_type=F32)
        acts.append((_silu(g) * u).astype(BF16))
        if len(acts) == FF_DOWN_GROUP or c == n - 1:
            lo = (c + 1 - len(acts)) * FF_CHUNK
            part = jnp.dot(jnp.concatenate(acts, axis=1), wd_ref[lo:(c + 1) * FF_CHUNK, :],
                           preferred_element_type=F32)
            acc = part if acc is None else acc + part
            acts = []
        yield
    out.append(acc)


def _shift_rows(x, s):
    return pltpu.roll(x, s, axis=0)


def _row_in_segment(shape, seg):
    return lax.broadcasted_iota(jnp.int32, shape, 0) & (seg - 1)


def _segment_cumsum(x, seg):
    pos = _row_in_segment(x.shape, seg)
    s = 1
    while s < seg:
        x = jnp.where(pos >= s, x + _shift_rows(x, s), x)
        s *= 2
    return x


def _segment_affine_scan(a, u, seg):
    pos = _row_in_segment(a.shape, seg)
    s = 1
    while s < seg:
        m = pos >= s
        u = jnp.where(m, a * _shift_rows(u, s) + u, u)
        a = jnp.where(m, a * _shift_rows(a, s), a)
        s *= 2
    return a, u


def _conv(xb, sh1, sh2, sh3, cw_ref, cb_ref):
    y = cb_ref[...] + sh3 * cw_ref[0:1, :]
    y = y + sh2 * cw_ref[1:2, :]
    y = y + sh1 * cw_ref[2:3, :]
    return y + xb * cw_ref[3:4, :]


def _rg_gates(xc, wgate_ref, ba_ref, bx_ref):
    half = RG_WIDTH // 2
    r, i = [], []
    for c in range(2):
        z = _mm(xc[:, c * half:(c + 1) * half], wgate_ref[c])
        r.append(z[:, :half])
        i.append(z[:, half:])
    r = jax.nn.sigmoid(jnp.concatenate(r, axis=1) + ba_ref[...])
    i = jax.nn.sigmoid(jnp.concatenate(i, axis=1) + bx_ref[...])
    return r, i


def _rg_decay_input(xc, r, i, lam_ref):
    log_a = (-RG_C * _softplus(-lam_ref[...])) * r
    a = jnp.exp(log_a)
    mult = jnp.sqrt(-jnp.tanh(log_a) * (a * a + 1.0))
    return a, mult * (i * xc)


def _gla_log_decay(a_low, wa2_ref, ba2_ref):
    z = _mm(a_low, wa2_ref[...]) + ba2_ref[...]
    return (-_softplus(-z)) / GLA_GATE_NORMALIZER


def _gla_intra(qt, kt, v, chunk):
    t = qt.shape[0]
    shift = int(np.log2(chunk))
    ri = lax.broadcasted_iota(jnp.int32, (t, t), 0)
    ci = lax.broadcasted_iota(jnp.int32, (t, t), 1)
    causal = (ri >= ci) & ((ri >> shift) == (ci >> shift))
    lane = lax.broadcasted_iota(jnp.int32, (t, PAIR_K), 1)
    outs = []
    for h in range(GLA_HEADS):
        p = h // 2
        qp = qt[:, p * PAIR_K:(p + 1) * PAIR_K]
        kp = kt[:, p * PAIR_K:(p + 1) * PAIR_K]
        mine = (lane >= GLA_DK) if (h % 2) else (lane < GLA_DK)
        s = _mm_nt(jnp.where(mine, qp, 0.0), kp)
        attn = jnp.where(causal, s, 0.0)
        outs.append(_mm(attn, v[:, h * GLA_DV:(h + 1) * GLA_DV].astype(BF16)))
    return outs


def _pair_blockdiag_mask():
    r = lax.broadcasted_iota(jnp.int32, (PAIR_K, PAIR_V), 0)
    c = lax.broadcasted_iota(jnp.int32, (PAIR_K, PAIR_V), 1)
    return (r < GLA_DK) == (c < GLA_DV)


def _gla_chunk_updates(k, v, b, b_last, chunk):
    bd_mask = _pair_blockdiag_mask()
    kv = {}
    for c in range(k.shape[0] // chunk):
        rows = slice(c * chunk, (c + 1) * chunk)
        for p in range(2):
            kl = slice(p * PAIR_K, (p + 1) * PAIR_K)
            kd = k[rows, kl] * jnp.exp(b_last[c:c + 1, kl] - b[rows, kl])
            kv[c, p] = jnp.where(bd_mask, _mm_tn(kd, v[rows, p * PAIR_V:(p + 1) * PAIR_V]), 0.0)
    return kv


def _gla_combine(intra, qt, states, chunk):
    n_chunks = qt.shape[0] // chunk
    heads = []
    for p in range(2):
        kl = slice(p * PAIR_K, (p + 1) * PAIR_K)
        inter = jnp.concatenate(
            [_mm(qt[c * chunk:(c + 1) * chunk, kl], states[c, p]) for c in range(n_chunks)],
            axis=0)
        heads.append(intra[2 * p] + inter[:, :GLA_DV])
        heads.append(intra[2 * p + 1] + inter[:, GLA_DV:])
    return heads


def _gla_out(o_heads, g, norm_ref):
    outs = []
    for h in range(GLA_HEADS):
        outs.append(_rms(o_heads[h], norm_ref[:, h * GLA_DV:(h + 1) * GLA_DV]))
    return jnp.concatenate(outs, axis=1) * _silu(g)


def _softmax_rows(s):
    m = jnp.max(s, axis=-1, keepdims=True)
    e = jnp.exp(s - m)
    return e * (1.0 / jnp.sum(e, axis=-1, keepdims=True))


def _split_p(p_ref, rows=slice(None)):
    xb = p_ref[rows, OFF_RG_X:OFF_RG_Y]
    yb = p_ref[rows, OFF_RG_Y:OFF_Q]
    q = p_ref[rows, OFF_Q:OFF_K]
    k = p_ref[rows, OFF_K:OFF_V]
    v = p_ref[rows, OFF_V:OFF_G]
    g = p_ref[rows, OFF_G:OFF_A]
    a_low = p_ref[rows, OFF_A:P_WIDTH]
    return xb, yb, q, k, v, g, a_low


def _gla_prepare(q, k, log_a, chunk):
    b = _segment_cumsum(log_a, chunk)
    qt = (q * (GLA_DK ** -0.5)) * jnp.exp(b)
    kt = k * jnp.exp(-b)
    return b, qt, kt


def _mix_project(rg_out, gla_out, wout_ref):
    return (_mm(rg_out, wout_ref[0:RG_WIDTH, :]) + _mm(gla_out, wout_ref[RG_WIDTH:, :]))


def _ffn_in_kernel(x_ref, n1_ref, wg_ref, wu_ref, wd_ref, n2_ref, win_ref, x1_ref, p_ref):
    for sub in range(x_ref.shape[0] // ROW_TILE):
        rows = slice(sub * ROW_TILE, (sub + 1) * ROW_TILE)
        x = x_ref[rows, :]
        x1 = x + 0.5 * _ffn(x, n1_ref, wg_ref, wu_ref, wd_ref)
        x1_ref[rows, :] = x1
        p_ref[rows, :] = _mm(_rms(x1, n2_ref[...]), win_ref[...])


def _mem_kv_kernel(m_ref, n_ref, wk_ref, wv_ref, k_ref, v_ref, kb_ref, vb_ref):
    h = _rms(m_ref[...], n_ref[...]).astype(BF16)
    k = jnp.dot(h, wk_ref[...], preferred_element_type=F32)
    v = jnp.dot(h, wv_ref[...], preferred_element_type=F32)
    for h in range(XA_HEADS):
        sl = slice(h * XA_HEAD_DIM, (h + 1) * XA_HEAD_DIM)
        k_ref[:, h, :] = k[:, sl]
        v_ref[:, h, :] = v[:, sl]
    kb_ref[...] = k.astype(BF16)
    vb_ref[...] = v.astype(BF16)


def _prompt_tail_kernel(
        x1_ref, p_ref, kb_ref, vb_ref,
        cw_ref, cb_ref, wgate_ref, ba_ref, bx_ref, lam_ref, rgn_ref,
        wa2_ref, ba2_ref, glan_ref, wout_ref,
        xn_ref, wcq_ref, wco_ref, n2_ref, wg_ref, wu_ref, wd_ref, fn_ref,
        y_ref, rgh_ref, s_ref,
        xpad_ref, h_ref, sbd_ref, b_ref, x3_ref, *, steps_per_seq, n_steps):
    step = pl.program_id(0)
    t_idx = jnp.minimum(step, n_steps - 1) % steps_per_seq
    slot = step % 2
    tt = ROW_TILE
    subs = x1_ref.shape[0] // tt
    n_chunks = tt // GLA_CHUNK

    @pl.when(step == 0)
    def _():
        x3_ref[1] = jnp.zeros((subs, tt, D_MODEL), F32)

    @pl.when(t_idx == 0)
    def _():
        xpad_ref[0:SUBLANES, :] = jnp.zeros((SUBLANES, RG_WIDTH), F32)
        h_ref[...] = jnp.zeros_like(h_ref)
        sbd_ref[...] = jnp.zeros_like(sbd_ref)

    for sub in range(subs):
        rows = slice(sub * tt, (sub + 1) * tt)
        x3_prev = x3_ref[1 - slot, sub]
        ffn_out = []
        ffn = _ffn_pieces(x3_prev, n2_ref, wg_ref, wu_ref, wd_ref, ffn_out)
        placement = iter(FFN2_PLACEMENT)

        def emit_ffn():
            for _ in range(next(placement)):
                next(ffn)

        emit_ffn()

        xb, yb, q, k, v, g, a_low = _split_p(p_ref, rows)

        xpad_ref[SUBLANES:SUBLANES + tt, :] = xb
        sh1 = xpad_ref[SUBLANES - 1:SUBLANES - 1 + tt, :]
        sh2 = xpad_ref[SUBLANES - 2:SUBLANES - 2 + tt, :]
        sh3 = xpad_ref[SUBLANES - 3:SUBLANES - 3 + tt, :]
        xc = _conv(xb, sh1, sh2, sh3, cw_ref, cb_ref)
        xpad_ref[0:SUBLANES, :] = xb[tt - SUBLANES:tt, :]
        emit_ffn()
        r, i = _rg_gates(xc, wgate_ref, ba_ref, bx_ref)
        a, u = _rg_decay_input(xc, r, i, lam_ref)
        emit_ffn()
        a_grp, h_grp = _segment_affine_scan(a, u, SUBLANES)
        emit_ffn()
        carry = h_ref[0:1, :]
        groups = []
        for gi in range(tt // SUBLANES):
            grp = slice(gi * SUBLANES, (gi + 1) * SUBLANES)
            groups.append(h_grp[grp, :] + a_grp[grp, :] * carry)
            carry = groups[-1][SUBLANES - 1:SUBLANES, :]
        hs = jnp.concatenate(groups, axis=0)
        h_ref[...] = jnp.broadcast_to(carry, h_ref.shape)
        rg_out = _rms(hs * _gelu_tanh(yb), rgn_ref[...]).astype(BF16)
        emit_ffn()

        log_a = _gla_log_decay(a_low, wa2_ref, ba2_ref)
        b, qt, kt = _gla_prepare(q, k, log_a, GLA_CHUNK)
        emit_ffn()
        b_last = _last_rows(b_ref, b, GLA_CHUNK)
        dec_t = jnp.exp(b_last).T
        kv = _gla_chunk_updates(k, v, b, b_last, GLA_CHUNK)
        states = {}
        for p in range(2):
            s_bd = sbd_ref[p]
            for c in range(n_chunks):
                states[c, p] = s_bd.astype(BF16)
                s_bd = s_bd * dec_t[p * PAIR_K:(p + 1) * PAIR_K, c:c + 1] + kv[c, p]
            sbd_ref[p] = s_bd
        emit_ffn()
        intra = _gla_intra(qt, kt, v, GLA_CHUNK)
        emit_ffn()
        gla_out = _gla_out(_gla_combine(intra, qt, states, GLA_CHUNK), g, glan_ref).astype(BF16)
        emit_ffn()

        x2 = x1_ref[rows, :] + _mix_project(rg_out, gla_out, wout_ref)

        qx = _mm(_rms(x2, xn_ref[...]), wcq_ref[...])
        emit_ffn()
        head = lambda h: slice(h * XA_HEAD_DIM, (h + 1) * XA_HEAD_DIM)
        s = jnp.concatenate([_mm_nt(qx[:, head(h)], kb_ref[:, head(h)]) for h in range(XA_HEADS)], axis=0)
        pr = _softmax_rows(s * (XA_HEAD_DIM ** -0.5))
        emit_ffn()
        o = jnp.concatenate([_mm(pr[h * tt:(h + 1) * tt, :], vb_ref[:, head(h)]) for h in range(XA_HEADS)], axis=1)
        x3_ref[slot, sub] = x2 + _mm(o, wco_ref[...])
        for _ in ffn:
            pass
        y_ref[rows, :] = _rms(x3_prev + 0.5 * ffn_out[0], fn_ref[...])

    @pl.when((t_idx == steps_per_seq - 1) & (step < n_steps))
    def _():
        rgh_ref[...] = h_ref[...]
        row = lax.broadcasted_iota(jnp.int32, (PAIR_K, GLA_DV), 0)
        for p in range(2):
            s_bd = sbd_ref[p]
            s_ref[p * PAIR_K:(p + 1) * PAIR_K, :] = jnp.where(
                row < GLA_DK, s_bd[:, :GLA_DV], s_bd[:, GLA_DV:])


def _sample_mix_kernel(
        x1_ref, p_ref, econv_ref, eh_ref, s0_ref,
        cw_ref, cb_ref, wgate_ref, ba_ref, bx_ref, lam_ref, rgn_ref,
        wa2_ref, ba2_ref, glan_ref, wout_ref, xn_ref, wcq_ref,
        x2_ref, q_ref, rgh_ref, s_ref,
        hs_ref, b_ref, *, seq):
    rows_n = x1_ref.shape[0]
    nb = rows_n // seq

    xb, yb, q, k, v, g, a_low = _split_p(p_ref)

    pos = _row_in_segment(xb.shape, seq)
    econv = econv_ref[...]
    sh = []
    for j in range(1, CONV_WIDTH):
        sh.append(jnp.where(pos >= j, _shift_rows(xb, j), _shift_rows(econv, rows_n - seq + j)))
    xc = _conv(xb, sh[0], sh[1], sh[2], cw_ref, cb_ref)
    r, i = _rg_gates(xc, wgate_ref, ba_ref, bx_ref)
    a, u = _rg_decay_input(xc, r, i, lam_ref)
    u = u + a * eh_ref[...]
    _, hs = _segment_affine_scan(a, u, seq)
    rgh_ref[...] = _last_rows(hs_ref, hs, seq)
    rg_out = _rms(hs * _gelu_tanh(yb), rgn_ref[...]).astype(BF16)

    log_a = _gla_log_decay(a_low, wa2_ref, ba2_ref)
    b, qt, kt = _gla_prepare(q, k, log_a, seq)
    intra = _gla_intra(qt, kt, v, seq)
    b_last = _last_rows(b_ref, b, seq)
    dec_t = jnp.exp(b_last).T
    kv = _gla_chunk_updates(k, v, b, b_last, seq)
    row = lax.broadcasted_iota(jnp.int32, (PAIR_K, GLA_DV), 0)
    top = row < GLA_DK
    states = {}
    for c in range(nb):
        for p in range(2):
            kl = slice(p * PAIR_K, (p + 1) * PAIR_K)
            s_pair = s0_ref[c, kl, :]
            s_bd = jnp.concatenate([jnp.where(top, s_pair, 0.0), jnp.where(top, 0.0, s_pair)], axis=1)
            states[c, p] = s_bd.astype(BF16)
            s_new = s_bd * dec_t[kl, c:c + 1] + kv[c, p]
            s_ref[c, kl, :] = jnp.where(top, s_new[:, :GLA_DV], s_new[:, GLA_DV:])
    gla_out = _gla_out(_gla_combine(intra, qt, states, seq), g, glan_ref).astype(BF16)

    x2 = x1_ref[...] + _mix_project(rg_out, gla_out, wout_ref)
    x2_ref[...] = x2
    q_ref[...] = _mm(_rms(x2, xn_ref[...]), wcq_ref[...])


def _sample_attn_tail_kernel(q_ref, k_ref, v_ref, x2_ref, wco_ref, n2_ref, wg_ref, wu_ref, wd_ref, fn_ref,
                             y_ref, o_ref, *, seq):
    tiles = XA_HEAD_DIM // LANES
    group = tiles * XA_HEADS
    r = N_MEM * group
    nb = k_ref.shape[0] // r
    hs = XA_HEADS * seq
    lane = lax.broadcasted_iota(jnp.int32, (nb * hs, r), 1)
    head = (lax.broadcasted_iota(jnp.int32, (nb * hs, r), 0) // seq) & (XA_HEADS - 1)
    own = (lane & (group - 1)) == head
    s = []
    for j in range(nb):
        q = q_ref[j * seq:(j + 1) * seq, :]
        q_parts = jnp.concatenate(
            [q[:, c * LANES:(c + 1) * LANES] for c in range(XA_HEADS * tiles)], axis=0)
        part = _mm_nt(q_parts, k_ref[j * r:(j + 1) * r, :])
        for h in range(XA_HEADS):
            acc = part[h * tiles * seq:(h * tiles + 1) * seq, :]
            for c in range(1, tiles):
                blk = part[(h * tiles + c) * seq:(h * tiles + c + 1) * seq, :]
                acc = acc + pltpu.roll(blk, r - c * XA_HEADS, axis=1)
            s.append(acc)
    p = _softmax_rows(jnp.where(own, jnp.concatenate(s, axis=0) * (XA_HEAD_DIM ** -0.5), MASKED_SCORE))
    for j in range(nb):
        pj = p[j * hs:(j + 1) * hs, :]
        p_all = jnp.concatenate(
            [pj] + [pltpu.roll(pj, c * XA_HEADS, axis=1) for c in range(1, tiles)], axis=0)
        o = _mm(p_all, v_ref[j * r:(j + 1) * r, :].astype(BF16))
        for c in range(tiles):
            for h in range(XA_HEADS):
                col = h * XA_HEAD_DIM + c * LANES
                o_ref[j * seq:(j + 1) * seq, col:col + LANES] = (
                    o[(c * XA_HEADS + h) * seq:(c * XA_HEADS + h + 1) * seq, :])
    x3 = x2_ref[...] + _mm(o_ref[...], wco_ref[...])
    x4 = x3 + 0.5 * _ffn(x3, n2_ref, wg_ref, wu_ref, wd_ref)
    y_ref[...] = _rms(x4, fn_ref[...])


def _resident(arr):
    nd = arr.ndim
    return pl.BlockSpec(arr.shape, lambda *_: (0,) * nd, pipeline_mode=pl.Buffered(1))


def _params(sem, flags=None):
    return pltpu.CompilerParams(dimension_semantics=sem, vmem_limit_bytes=VMEM_LIMIT, flags=flags)


def _ffn_in(x, weights):
    rows = x.shape[0]
    block = FFN_IN_TILES * ROW_TILE
    row_spec = lambda w: pl.BlockSpec((block, w), lambda i: (i, 0))
    return pl.pallas_call(
        _ffn_in_kernel,
        grid=(rows // block,),
        in_specs=[row_spec(D_MODEL)] + [_resident(w) for w in weights],
        out_specs=[row_spec(D_MODEL), row_spec(P_WIDTH)],
        out_shape=[jax.ShapeDtypeStruct((rows, D_MODEL), F32),
                   jax.ShapeDtypeStruct((rows, P_WIDTH), F32)],
        compiler_params=_params(("parallel",)),
        name="ffn_in",
    )(x, *weights)


def _mem_kv(mem, weights):
    rows = mem.shape[0]
    row_spec = pl.BlockSpec((ROW_TILE, D_MODEL), lambda i: (i, 0))
    head_spec = pl.BlockSpec((ROW_TILE, XA_HEADS, XA_HEAD_DIM), lambda i: (i, 0, 0))
    return pl.pallas_call(
        _mem_kv_kernel,
        grid=(rows // ROW_TILE,),
        in_specs=[row_spec] + [_resident(w) for w in weights],
        out_specs=[head_spec] * 2 + [row_spec] * 2,
        out_shape=[jax.ShapeDtypeStruct((rows, XA_HEADS, XA_HEAD_DIM), F32)] * 2
                  + [jax.ShapeDtypeStruct((rows, D_MODEL), BF16)] * 2,
        compiler_params=_params(("parallel",)),
        name="mem_kv",
    )(mem, *weights)


def _prompt_tail(x1, p, kb, vb, weights):
    batch, seq, _ = x1.shape
    tt = ROW_TILE
    block = PROMPT_TILES * tt
    steps_per_seq = seq // block
    n_steps = batch * steps_per_seq
    cur = lambda s: jnp.minimum(s, n_steps - 1)
    prev = lambda s: jnp.maximum(s - 1, 0)
    tok = lambda w: pl.BlockSpec(
        (None, block, w), lambda s: (cur(s) // steps_per_seq, cur(s) % steps_per_seq, 0))
    per_seq = lambda r, w: pl.BlockSpec((None, r, w), lambda s: (cur(s) // steps_per_seq, 0, 0))
    out_tok = pl.BlockSpec(
        (None, block, D_MODEL), lambda s: (prev(s) // steps_per_seq, prev(s) % steps_per_seq, 0))
    return pl.pallas_call(
        functools.partial(_prompt_tail_kernel, steps_per_seq=steps_per_seq, n_steps=n_steps),
        grid=(n_steps + 1,),
        in_specs=[tok(D_MODEL), tok(P_WIDTH), per_seq(N_MEM, D_MODEL), per_seq(N_MEM, D_MODEL)]
                 + [_resident(w) for w in weights],
        out_specs=[out_tok, per_seq(SUBLANES, RG_WIDTH), per_seq(GLA_KEY_WIDTH, GLA_DV)],
        out_shape=[jax.ShapeDtypeStruct((batch, seq, D_MODEL), F32),
                   jax.ShapeDtypeStruct((batch, SUBLANES, RG_WIDTH), F32),
                   jax.ShapeDtypeStruct((batch, GLA_KEY_WIDTH, GLA_DV), F32)],
        scratch_shapes=[pltpu.VMEM((SUBLANES + tt, RG_WIDTH), F32),
                        pltpu.VMEM((SUBLANES, RG_WIDTH), F32),
                        pltpu.VMEM((2, PAIR_K, PAIR_V), F32),
                        pltpu.VMEM((GLA_KEY_WIDTH // LANES, tt, LANES), F32),
                        pltpu.VMEM((2, PROMPT_TILES, tt, D_MODEL), F32)],
        compiler_params=_params(("arbitrary",)),
        name="prompt_tail",
    )(x1, p, kb, vb, *weights)


def _sample_mix(x1, p, econv, eh, s0, weights, seq):
    rows = x1.shape[0]
    nb = ROW_TILE // seq
    row_spec = lambda w: pl.BlockSpec((ROW_TILE, w), lambda i: (i, 0))
    state_spec = pl.BlockSpec((nb, GLA_KEY_WIDTH, GLA_DV), lambda i: (i, 0, 0))
    return pl.pallas_call(
        functools.partial(_sample_mix_kernel, seq=seq),
        grid=(rows // ROW_TILE,),
        in_specs=[row_spec(D_MODEL), row_spec(P_WIDTH), row_spec(RG_WIDTH), row_spec(RG_WIDTH),
                  state_spec] + [_resident(w) for w in weights],
        out_specs=[row_spec(D_MODEL), row_spec(D_MODEL),
                   pl.BlockSpec((nb, RG_WIDTH), lambda i: (i, 0)), state_spec],
        out_shape=[jax.ShapeDtypeStruct((rows, D_MODEL), F32),
                   jax.ShapeDtypeStruct((rows, D_MODEL), F32),
                   jax.ShapeDtypeStruct((rows // seq, RG_WIDTH), F32),
                   jax.ShapeDtypeStruct((rows // seq, GLA_KEY_WIDTH, GLA_DV), F32)],
        scratch_shapes=[pltpu.VMEM((RG_WIDTH // LANES, ROW_TILE, LANES), F32),
                        pltpu.VMEM((GLA_KEY_WIDTH // LANES, ROW_TILE, LANES), F32)],
        compiler_params=_params(("parallel",)),
        name="sample_mix",
    )(x1, p, econv, eh, s0, *weights)


def _head_interleaved_rows(x):
    b, m, h, dh = x.shape
    tiles = dh // LANES
    return (x.reshape(b, m, h, tiles, LANES).transpose(0, 1, 3, 2, 4)
            .reshape(b * m * tiles * h, LANES))


def _sample_attn_tail(q, k, v, x2, weights, seq):
    rows = q.shape[0]
    nb = SAMPLE_ATTN_BATCH
    row_spec = pl.BlockSpec((nb * seq, D_MODEL), lambda i: (i, 0))
    kv_rows = k.shape[0] // (rows // seq)
    kv_spec = pl.BlockSpec((nb * kv_rows, LANES), lambda i: (i, 0))
    return pl.pallas_call(
        functools.partial(_sample_attn_tail_kernel, seq=seq),
        grid=(rows // (nb * seq),),
        in_specs=[row_spec, kv_spec, kv_spec, row_spec] + [_resident(w) for w in weights],
        out_specs=row_spec,
        out_shape=jax.ShapeDtypeStruct((rows, D_MODEL), F32),
        scratch_shapes=[pltpu.VMEM((nb * seq, D_MODEL), F32)],
        compiler_params=_params(("parallel",)),
        name="sample_attn_tail",
    )(q, k, v, x2, *weights)


def _block_diag_heads(w):
    h, n, _ = w.shape
    eye = jnp.eye(h, dtype=w.dtype)
    return (eye[:, None, :, None] * w[:, :, None, :]).reshape(h * n, h * n)


def kernel(x_prompt, x_sample, cache_mem_k, cache_mem_v, state_conv, state_rglru, state_gla, mem_prompt, ffn1_norm, ffn1_w_gate, ffn1_w_up, ffn1_w_down, mix_norm, w_in, conv_w, conv_b, rg_w_a, rg_b_a, rg_w_x, rg_b_x, rg_lambda, rg_out_norm, gla_w_a2, gla_b_a2, gla_out_norm, w_out, xattn_norm, mem_norm, w_cq, w_ck, w_cv, w_co, ffn2_norm, ffn2_w_gate, ffn2_w_up, ffn2_w_down, final_norm):
    bp, tp, _ = x_prompt.shape
    bs, ts, _ = x_sample.shape
    row = lambda g: g.reshape(1, -1)
    bf = lambda w: w.astype(BF16)

    w_in_p = bf(jnp.pad(w_in, ((0, 0), (0, P_WIDTH - D_IN))))
    hp = RG_HEADS // 2
    w_gate_rg = bf(jnp.stack([
        jnp.concatenate([_block_diag_heads(rg_w_a[c * hp:(c + 1) * hp]),
                         _block_diag_heads(rg_w_x[c * hp:(c + 1) * hp])], axis=1)
        for c in range(2)]))
    w_a2_p = bf(jnp.pad(gla_w_a2, ((0, A_PAD - GLA_GATE_RANK), (0, 0))))
    ffn_in_w = (row(ffn1_norm), bf(ffn1_w_gate), bf(ffn1_w_up), bf(ffn1_w_down), row(mix_norm), w_in_p)
    mixer_w = (conv_w, row(conv_b), w_gate_rg, row(rg_b_a), row(rg_b_x), row(rg_lambda),
               row(rg_out_norm), w_a2_p, row(gla_b_a2), row(jnp.tile(gla_out_norm, GLA_HEADS)),
               bf(w_out))
    w_cq_b, w_co_b = bf(w_cq), bf(w_co)
    ffn2_w = (row(ffn2_norm), bf(ffn2_w_gate), bf(ffn2_w_up), bf(ffn2_w_down), row(final_norm))

    mem_k, mem_v, mem_kb, mem_vb = _mem_kv(mem_prompt.reshape(bp * N_MEM, D_MODEL),
                                           (row(mem_norm), bf(w_ck), bf(w_cv)))
    x1_p, p_p = _ffn_in(x_prompt.reshape(bp * tp, D_MODEL), ffn_in_w)
    p_p = p_p.reshape(bp, tp, P_WIDTH)
    y_p, rgh_p, s_p = _prompt_tail(
        x1_p.reshape(bp, tp, D_MODEL), p_p,
        mem_kb.reshape(bp, N_MEM, D_MODEL), mem_vb.reshape(bp, N_MEM, D_MODEL),
        mixer_w + (row(xattn_norm), w_cq_b, w_co_b) + ffn2_w)
    conv_p = p_p[:, tp - (CONV_WIDTH - 1):, OFF_RG_X:OFF_RG_Y]

    x1_s, p_s = _ffn_in(x_sample.reshape(bs * ts, D_MODEL), ffn_in_w)
    econv = jnp.pad(state_conv, ((0, 0), (ts - (CONV_WIDTH - 1), 0), (0, 0))).reshape(bs * ts, RG_WIDTH)
    eh = jnp.pad(state_rglru[:, None, :], ((0, 0), (0, ts - 1), (0, 0))).reshape(bs * ts, RG_WIDTH)
    x2_s, q_s, rgh_s, s_s = _sample_mix(
        x1_s, p_s, econv, eh, state_gla.reshape(bs, GLA_KEY_WIDTH, GLA_DV),
        mixer_w + (row(xattn_norm), w_cq_b), ts)
    y_s = _sample_attn_tail(q_s, _head_interleaved_rows(cache_mem_k), _head_interleaved_rows(cache_mem_v),
                            x2_s, (w_co_b,) + ffn2_w, ts)
    conv_s = p_s.reshape(bs, ts, P_WIDTH)[:, ts - (CONV_WIDTH - 1):, OFF_RG_X:OFF_RG_Y]

    return (y_p, y_s.reshape(bs, ts, D_MODEL),
            mem_k.reshape(bp, N_MEM, XA_HEADS, XA_HEAD_DIM),
            mem_v.reshape(bp, N_MEM, XA_HEADS, XA_HEAD_DIM),
            conv_p, rgh_p[:, 0, :], s_p.reshape(bp, GLA_HEADS, GLA_DK, GLA_DV),
            conv_s, rgh_s, s_s.reshape(bs, GLA_HEADS, GLA_DK, GLA_DV))
```

```python
import functools

import jax
import jax.numpy as jnp
import numpy as np
from jax import lax
from jax.experimental import pallas as pl
from jax.experimental.pallas import tpu as pltpu

F32 = jnp.float32
BF16 = jnp.bfloat16

D_MODEL = 1024
D_FF = 2816
RG_WIDTH = 512
RG_HEADS = 8
RG_HEAD_DIM = 64
CONV_WIDTH = 4
RG_C = 8.0
GLA_WIDTH = 512
GLA_HEADS = 4
GLA_DV = 128
GLA_DK = 64
GLA_KEY_WIDTH = 256
GLA_GATE_RANK = 16
GLA_GATE_NORMALIZER = 16.0
GLA_CHUNK = 32
N_MEM = 256
XA_HEADS = 4
XA_HEAD_DIM = 256
EPS = 1e-6
MASKED_SCORE = -np.inf

OFF_RG_X = 0
OFF_RG_Y = 512
OFF_Q = 1024
OFF_K = 1280
OFF_V = 1536
OFF_G = 2048
OFF_A = 2560
D_IN = 2576

LANES = 128
SUBLANES = 8
A_PAD = LANES
P_WIDTH = OFF_A + A_PAD
PAIR_K = 2 * GLA_DK
PAIR_V = 2 * GLA_DV
VMEM_LIMIT = 58 * 1024 * 1024

ROW_TILE = 256
FFN_IN_TILES = 2
PROMPT_TILES = 2
FF_CHUNK = 256
FF_DOWN_GROUP = 2
FFN2_PLACEMENT = (1, 1, 1, 1, 1, 1, 1, 1, 1, 1, 1)
SAMPLE_ATTN_BATCH = 8


def _rms(x, g):
    return x * lax.rsqrt(jnp.mean(x * x, axis=-1, keepdims=True) + EPS) * g


def _mm(a, w):
    return jnp.dot(a.astype(BF16), w, preferred_element_type=F32)


def _mm_nt(a, b):
    return lax.dot_general(a.astype(BF16), b.astype(BF16), (((1,), (1,)), ((), ())),
                           preferred_element_type=F32)


def _mm_tn(a, b):
    return lax.dot_general(a.astype(BF16), b.astype(BF16), (((0,), (0,)), ((), ())),
                           preferred_element_type=F32)


def _silu(x):
    return x * jax.nn.sigmoid(x)


def _gelu_tanh(x):
    c = np.float32(np.sqrt(2.0 / np.pi))
    return x * (0.5 * (1.0 + jnp.tanh(c * (x + 0.044715 * (x * x * x)))))


def _softplus(x):
    return jnp.maximum(x, 0.0) + jnp.log1p(jnp.exp(-jnp.abs(x)))


def _ffn(x, norm_ref, wg_ref, wu_ref, wd_ref):
    h = _rms(x, norm_ref[...]).astype(BF16)
    g = jnp.dot(h, wg_ref[...], preferred_element_type=F32)
    u = jnp.dot(h, wu_ref[...], preferred_element_type=F32)
    return _mm(_silu(g) * u, wd_ref[...])


def _last_rows(buf_ref, x, seg):
    n = x.shape[0] // seg
    outs = []
    for j in range(x.shape[1] // LANES):
        buf_ref[j] = x[:, j * LANES:(j + 1) * LANES]
        outs.append(buf_ref[j, pl.ds(seg - 1, n, stride=seg), :])
    return jnp.concatenate(outs, axis=1)


def _ffn_pieces(x, norm_ref, wg_ref, wu_ref, wd_ref, out):
    h = _rms(x, norm_ref[...]).astype(BF16)
    acc = None
    acts = []
    n = D_FF // FF_CHUNK
    for c in range(n):
        cols = slice(c * FF_CHUNK, (c + 1) * FF_CHUNK)
        g = jnp.dot(h, wg_ref[:, cols], preferred_element_type=F32)
        u = jnp.dot(h, wu_ref[:, cols], preferred_element_type=F32)
        acts.append((_silu(g) * u).astype(BF16))
        if len(acts) == FF_DOWN_GROUP or c == n - 1:
            lo = (c + 1 - len(acts)) * FF_CHUNK
            part = jnp.dot(jnp.concatenate(acts, axis=1), wd_ref[lo:(c + 1) * FF_CHUNK, :],
                           preferred_element_type=F32)
            acc = part if acc is None else acc + part
            acts = []
        yield
    out.append(acc)


def _shift_rows(x, s):
    return pltpu.roll(x, s, axis=0)


def _row_in_segment(shape, seg):
    return lax.broadcasted_iota(jnp.int32, shape, 0) & (seg - 1)


def _segment_cumsum(x, seg):
    pos = _row_in_segment(x.shape, seg)
    s = 1
    while s < seg:
        x = jnp.where(pos >= s, x + _shift_rows(x, s), x)
        s *= 2
    return x


def _segment_affine_scan(a, u, seg):
    pos = _row_in_segment(a.shape, seg)
    s = 1
    while s < seg:
        m = pos >= s
        u = jnp.where(m, a * _shift_rows(u, s) + u, u)
        a = jnp.where(m, a * _shift_rows(a, s), a)
        s *= 2
    return a, u


def _conv(xb, sh1, sh2, sh3, cw_ref, cb_ref):
    y = cb_ref[...] + sh3 * cw_ref[0:1, :]
    y = y + sh2 * cw_ref[1:2, :]
    y = y + sh1 * cw_ref[2:3, :]
    return y + xb * cw_ref[3:4, :]


def _rg_gates(xc, wgate_ref, ba_ref, bx_ref):
    half = RG_WIDTH // 2
    r, i = [], []
    for c in range(2):
        z = _mm(xc[:, c * half:(c + 1) * half], wgate_ref[c])
        r.append(z[:, :half])
        i.append(z[:, half:])
    r = jax.nn.sigmoid(jnp.concatenate(r, axis=1) + ba_ref[...])
    i = jax.nn.sigmoid(jnp.concatenate(i, axis=1) + bx_ref[...])
    return r, i


def _rg_decay_input(xc, r, i, lam_ref):
    log_a = (-RG_C * _softplus(-lam_ref[...])) * r
    a = jnp.exp(log_a)
    mult = jnp.sqrt(-jnp.tanh(log_a) * (a * a + 1.0))
    return a, mult * (i * xc)


def _gla_log_decay(a_low, wa2_ref, ba2_ref):
    z = _mm(a_low, wa2_ref[...]) + ba2_ref[...]
    return (-_softplus(-z)) / GLA_GATE_NORMALIZER


def _gla_intra(qt, kt, v, chunk):
    t = qt.shape[0]
    shift = int(np.log2(chunk))
    ri = lax.broadcasted_iota(jnp.int32, (t, t), 0)
    ci = lax.broadcasted_iota(jnp.int32, (t, t), 1)
    causal = (ri >= ci) & ((ri >> shift) == (ci >> shift))
    lane = lax.broadcasted_iota(jnp.int32, (t, PAIR_K), 1)
    outs = []
    for h in range(GLA_HEADS):
        p = h // 2
        qp = qt[:, p * PAIR_K:(p + 1) * PAIR_K]
        kp = kt[:, p * PAIR_K:(p + 1) * PAIR_K]
        mine = (lane >= GLA_DK) if (h % 2) else (lane < GLA_DK)
        s = _mm_nt(jnp.where(mine, qp, 0.0), kp)
        attn = jnp.where(causal, s, 0.0)
        outs.append(_mm(attn, v[:, h * GLA_DV:(h + 1) * GLA_DV].astype(BF16)))
    return outs


def _pair_blockdiag_mask():
    r = lax.broadcasted_iota(jnp.int32, (PAIR_K, PAIR_V), 0)
    c = lax.broadcasted_iota(jnp.int32, (PAIR_K, PAIR_V), 1)
    return (r < GLA_DK) == (c < GLA_DV)


def _gla_chunk_updates(k, v, b, b_last, chunk):
    bd_mask = _pair_blockdiag_mask()
    kv = {}
    for c in range(k.shape[0] // chunk):
        rows = slice(c * chunk, (c + 1) * chunk)
        for p in range(2):
            kl = slice(p * PAIR_K, (p + 1) * PAIR_K)
            kd = k[rows, kl] * jnp.exp(b_last[c:c + 1, kl] - b[rows, kl])
            kv[c, p] = jnp.where(bd_mask, _mm_tn(kd, v[rows, p * PAIR_V:(p + 1) * PAIR_V]), 0.0)
    return kv


def _gla_combine(intra, qt, states, chunk):
    n_chunks = qt.shape[0] // chunk
    heads = []
    for p in range(2):
        kl = slice(p * PAIR_K, (p + 1) * PAIR_K)
        inter = jnp.concatenate(
            [_mm(qt[c * chunk:(c + 1) * chunk, kl], states[c, p]) for c in range(n_chunks)],
            axis=0)
        heads.append(intra[2 * p] + inter[:, :GLA_DV])
        heads.append(intra[2 * p + 1] + inter[:, GLA_DV:])
    return heads


def _gla_out(o_heads, g, norm_ref):
    outs = []
    for h in range(GLA_HEADS):
        outs.append(_rms(o_heads[h], norm_ref[:, h * GLA_DV:(h + 1) * GLA_DV]))
    return jnp.concatenate(outs, axis=1) * _silu(g)


def _softmax_rows(s):
    m = jnp.max(s, axis=-1, keepdims=True)
    e = jnp.exp(s - m)
    return e * (1.0 / jnp.sum(e, axis=-1, keepdims=True))


def _split_p(p_ref, rows=slice(None)):
    xb = p_ref[rows, OFF_RG_X:OFF_RG_Y]
    yb = p_ref[rows, OFF_RG_Y:OFF_Q]
    q = p_ref[rows, OFF_Q:OFF_K]
    k = p_ref[rows, OFF_K:OFF_V]
    v = p_ref[rows, OFF_V:OFF_G]
    g = p_ref[rows, OFF_G:OFF_A]
    a_low = p_ref[rows, OFF_A:P_WIDTH]
    return xb, yb, q, k, v, g, a_low


def _gla_prepare(q, k, log_a, chunk):
    b = _segment_cumsum(log_a, chunk)
    qt = (q * (GLA_DK ** -0.5)) * jnp.exp(b)
    kt = k * jnp.exp(-b)
    return b, qt, kt


def _mix_project(rg_out, gla_out, wout_ref):
    return (_mm(rg_out, wout_ref[0:RG_WIDTH, :]) + _mm(gla_out, wout_ref[RG_WIDTH:, :]))


def _ffn_in_kernel(x_ref, n1_ref, wg_ref, wu_ref, wd_ref, n2_ref, win_ref, x1_ref, p_ref):
    for sub in range(x_ref.shape[0] // ROW_TILE):
        rows = slice(sub * ROW_TILE, (sub + 1) * ROW_TILE)
        x = x_ref[rows, :]
        x1 = x + 0.5 * _ffn(x, n1_ref, wg_ref, wu_ref, wd_ref)
        x1_ref[rows, :] = x1
        p_ref[rows, :] = _mm(_rms(x1, n2_ref[...]), win_ref[...])


def _mem_kv_kernel(m_ref, n_ref, wk_ref, wv_ref, k_ref, v_ref, kb_ref, vb_ref):
    h = _rms(m_ref[...], n_ref[...]).astype(BF16)
    k = jnp.dot(h, wk_ref[...], preferred_element_type=F32)
    v = jnp.dot(h, wv_ref[...], preferred_element_type=F32)
    for h in range(XA_HEADS):
        sl = slice(h * XA_HEAD_DIM, (h + 1) * XA_HEAD_DIM)
        k_ref[:, h, :] = k[:, sl]
        v_ref[:, h, :] = v[:, sl]
    kb_ref[...] = k.astype(BF16)
    vb_ref[...] = v.astype(BF16)


def _prompt_tail_kernel(
        x1_ref, p_ref, kb_ref, vb_ref,
        cw_ref, cb_ref, wgate_ref, ba_ref, bx_ref, lam_ref, rgn_ref,
        wa2_ref, ba2_ref, glan_ref, wout_ref,
        xn_ref, wcq_ref, wco_ref, n2_ref, wg_ref, wu_ref, wd_ref, fn_ref,
        y_ref, rgh_ref, s_ref,
        xpad_ref, h_ref, sbd_ref, b_ref, x3_ref, *, steps_per_seq, n_steps):
    step = pl.program_id(0)
    t_idx = jnp.minimum(step, n_steps - 1) % steps_per_seq
    slot = step % 2
    tt = ROW_TILE
    subs = x1_ref.shape[0] // tt
    n_chunks = tt // GLA_CHUNK

    @pl.when(step == 0)
    def _():
        x3_ref[1] = jnp.zeros((subs, tt, D_MODEL), F32)

    @pl.when(t_idx == 0)
    def _():
        xpad_ref[0:SUBLANES, :] = jnp.zeros((SUBLANES, RG_WIDTH), F32)
        h_ref[...] = jnp.zeros_like(h_ref)
        sbd_ref[...] = jnp.zeros_like(sbd_ref)

    for sub in range(subs):
        rows = slice(sub * tt, (sub + 1) * tt)
        x3_prev = x3_ref[1 - slot, sub]
        ffn_out = []
        ffn = _ffn_pieces(x3_prev, n2_ref, wg_ref, wu_ref, wd_ref, ffn_out)
        placement = iter(FFN2_PLACEMENT)

        def emit_ffn():
            for _ in range(next(placement)):
                next(ffn)

        emit_ffn()

        xb, yb, q, k, v, g, a_low = _split_p(p_ref, rows)

        xpad_ref[SUBLANES:SUBLANES + tt, :] = xb
        sh1 = xpad_ref[SUBLANES - 1:SUBLANES - 1 + tt, :]
        sh2 = xpad_ref[SUBLANES - 2:SUBLANES - 2 + tt, :]
        sh3 = xpad_ref[SUBLANES - 3:SUBLANES - 3 + tt, :]
        xc = _conv(xb, sh1, sh2, sh3, cw_ref, cb_ref)
        xpad_ref[0:SUBLANES, :] = xb[tt - SUBLANES:tt, :]
        emit_ffn()
        r, i = _rg_gates(xc, wgate_ref, ba_ref, bx_ref)
        a, u = _rg_decay_input(xc, r, i, lam_ref)
        emit_ffn()
        a_grp, h_grp = _segment_affine_scan(a, u, SUBLANES)
        emit_ffn()
        carry = h_ref[0:1, :]
        groups = []
        for gi in range(tt // SUBLANES):
            grp = slice(gi * SUBLANES, (gi + 1) * SUBLANES)
            groups.append(h_grp[grp, :] + a_grp[grp, :] * carry)
            carry = groups[-1][SUBLANES - 1:SUBLANES, :]
        hs = jnp.concatenate(groups, axis=0)
        h_ref[...] = jnp.broadcast_to(carry, h_ref.shape)
        rg_out = _rms(hs * _gelu_tanh(yb), rgn_ref[...]).astype(BF16)
        emit_ffn()

        log_a = _gla_log_decay(a_low, wa2_ref, ba2_ref)
        b, qt, kt = _gla_prepare(q, k, log_a, GLA_CHUNK)
        emit_ffn()
        b_last = _last_rows(b_ref, b, GLA_CHUNK)
        dec_t = jnp.exp(b_last).T
        kv = _gla_chunk_updates(k, v, b, b_last, GLA_CHUNK)
        states = {}
        for p in range(2):
            s_bd = sbd_ref[p]
            for c in range(n_chunks):
                states[c, p] = s_bd.astype(BF16)
                s_bd = s_bd * dec_t[p * PAIR_K:(p + 1) * PAIR_K, c:c + 1] + kv[c, p]
            sbd_ref[p] = s_bd
        emit_ffn()
        intra = _gla_intra(qt, kt, v, GLA_CHUNK)
        emit_ffn()
        gla_out = _gla_out(_gla_combine(intra, qt, states, GLA_CHUNK), g, glan_ref).astype(BF16)
        emit_ffn()

        x2 = x1_ref[rows, :] + _mix_project(rg_out, gla_out, wout_ref)

        qx = _mm(_rms(x2, xn_ref[...]), wcq_ref[...])
        emit_ffn()
        head = lambda h: slice(h * XA_HEAD_DIM, (h + 1) * XA_HEAD_DIM)
        s = jnp.concatenate([_mm_nt(qx[:, head(h)], kb_ref[:, head(h)]) for h in range(XA_HEADS)], axis=0)
        pr = _softmax_rows(s * (XA_HEAD_DIM ** -0.5))
        emit_ffn()
        o = jnp.concatenate([_mm(pr[h * tt:(h + 1) * tt, :], vb_ref[:, head(h)]) for h in range(XA_HEADS)], axis=1)
        x3_ref[slot, sub] = x2 + _mm(o, wco_ref[...])
        for _ in ffn:
            pass
        y_ref[rows, :] = _rms(x3_prev + 0.5 * ffn_out[0], fn_ref[...])

    @pl.when((t_idx == steps_per_seq - 1) & (step < n_steps))
    def _():
        rgh_ref[...] = h_ref[...]
        row = lax.broadcasted_iota(jnp.int32, (PAIR_K, GLA_DV), 0)
        for p in range(2):
            s_bd = sbd_ref[p]
            s_ref[p * PAIR_K:(p + 1) * PAIR_K, :] = jnp.where(
                row < GLA_DK, s_bd[:, :GLA_DV], s_bd[:, GLA_DV:])


def _sample_mix_kernel(
        x1_ref, p_ref, econv_ref, eh_ref, s0_ref,
        cw_ref, cb_ref, wgate_ref, ba_ref, bx_ref, lam_ref, rgn_ref,
        wa2_ref, ba2_ref, glan_ref, wout_ref, xn_ref, wcq_ref,
        x2_ref, q_ref, rgh_ref, s_ref,
        hs_ref, b_ref, *, seq):
    rows_n = x1_ref.shape[0]
    nb = rows_n // seq

    xb, yb, q, k, v, g, a_low = _split_p(p_ref)

    pos = _row_in_segment(xb.shape, seq)
    econv = econv_ref[...]
    sh = []
    for j in range(1, CONV_WIDTH):
        sh.append(jnp.where(pos >= j, _shift_rows(xb, j), _shift_rows(econv, rows_n - seq + j)))
    xc = _conv(xb, sh[0], sh[1], sh[2], cw_ref, cb_ref)
    r, i = _rg_gates(xc, wgate_ref, ba_ref, bx_ref)
    a, u = _rg_decay_input(xc, r, i, lam_ref)
    u = u + a * eh_ref[...]
    _, hs = _segment_affine_scan(a, u, seq)
    rgh_ref[...] = _last_rows(hs_ref, hs, seq)
    rg_out = _rms(hs * _gelu_tanh(yb), rgn_ref[...]).astype(BF16)

    log_a = _gla_log_decay(a_low, wa2_ref, ba2_ref)
    b, qt, kt = _gla_prepare(q, k, log_a, seq)
    intra = _gla_intra(qt, kt, v, seq)
    b_last = _last_rows(b_ref, b, seq)
    dec_t = jnp.exp(b_last).T
    kv = _gla_chunk_updates(k, v, b, b_last, seq)
    row = lax.broadcasted_iota(jnp.int32, (PAIR_K, GLA_DV), 0)
    top = row < GLA_DK
    states = {}
    for c in range(nb):
        for p in range(2):
            kl = slice(p * PAIR_K, (p + 1) * PAIR_K)
            s_pair = s0_ref[c, kl, :]
            s_bd = jnp.concatenate([jnp.where(top, s_pair, 0.0), jnp.where(top, 0.0, s_pair)], axis=1)
            states[c, p] = s_bd.astype(BF16)
            s_new = s_bd * dec_t[kl, c:c + 1] + kv[c, p]
            s_ref[c, kl, :] = jnp.where(top, s_new[:, :GLA_DV], s_new[:, GLA_DV:])
    gla_out = _gla_out(_gla_combine(intra, qt, states, seq), g, glan_ref).astype(BF16)

    x2 = x1_ref[...] + _mix_project(rg_out, gla_out, wout_ref)
    x2_ref[...] = x2
    q_ref[...] = _mm(_rms(x2, xn_ref[...]), wcq_ref[...])


def _sample_attn_tail_kernel(q_ref, k_ref, v_ref, x2_ref, wco_ref, n2_ref, wg_ref, wu_ref, wd_ref, fn_ref,
                             y_ref, o_ref, *, seq):
    tiles = XA_HEAD_DIM // LANES
    group = tiles * XA_HEADS
    r = N_MEM * group
    nb = k_ref.shape[0] // r
    hs = XA_HEADS * seq
    lane = lax.broadcasted_iota(jnp.int32, (nb * hs, r), 1)
    head = (lax.broadcasted_iota(jnp.int32, (nb * hs, r), 0) // seq) & (XA_HEADS - 1)
    own = (lane & (group - 1)) == head
    s = []
    for j in range(nb):
        q = q_ref[j * seq:(j + 1) * seq, :]
        q_parts = jnp.concatenate(
            [q[:, c * LANES:(c + 1) * LANES] for c in range(XA_HEADS * tiles)], axis=0)
        part = _mm_nt(q_parts, k_ref[j * r:(j + 1) * r, :])
        for h in range(XA_HEADS):
            acc = part[h * tiles * seq:(h * tiles + 1) * seq, :]
            for c in range(1, tiles):
                blk = part[(h * tiles + c) * seq:(h * tiles + c + 1) * seq, :]
                acc = acc + pltpu.roll(blk, r - c * XA_HEADS, axis=1)
            s.append(acc)
    p = _softmax_rows(jnp.where(own, jnp.concatenate(s, axis=0) * (XA_HEAD_DIM ** -0.5), MASKED_SCORE))
    for j in range(nb):
        pj = p[j * hs:(j + 1) * hs, :]
        p_all = jnp.concatenate(
            [pj] + [pltpu.roll(pj, c * XA_HEADS, axis=1) for c in range(1, tiles)], axis=0)
        o = _mm(p_all, v_ref[j * r:(j + 1) * r, :].astype(BF16))
        for c in range(tiles):
            for h in range(XA_HEADS):
                col = h * XA_HEAD_DIM + c * LANES
                o_ref[j * seq:(j + 1) * seq, col:col + LANES] = (
                    o[(c * XA_HEADS + h) * seq:(c * XA_HEADS + h + 1) * seq, :])
    x3 = x2_ref[...] + _mm(o_ref[...], wco_ref[...])
    x4 = x3 + 0.5 * _ffn(x3, n2_ref, wg_ref, wu_ref, wd_ref)
    y_ref[...] = _rms(x4, fn_ref[...])


def _resident(arr):
    nd = arr.ndim
    return pl.BlockSpec(arr.shape, lambda *_: (0,) * nd, pipeline_mode=pl.Buffered(1))


def _params(sem, flags=None):
    return pltpu.CompilerParams(dimension_semantics=sem, vmem_limit_bytes=VMEM_LIMIT, flags=flags)


def _ffn_in(x, weights):
    rows = x.shape[0]
    block = FFN_IN_TILES * ROW_TILE
    row_spec = lambda w: pl.BlockSpec((block, w), lambda i: (i, 0))
    return pl.pallas_call(
        _ffn_in_kernel,
        grid=(rows // block,),
        in_specs=[row_spec(D_MODEL)] + [_resident(w) for w in weights],
        out_specs=[row_spec(D_MODEL), row_spec(P_WIDTH)],
        out_shape=[jax.ShapeDtypeStruct((rows, D_MODEL), F32),
                   jax.ShapeDtypeStruct((rows, P_WIDTH), F32)],
        compiler_params=_params(("parallel",)),
        name="ffn_in",
    )(x, *weights)


def _mem_kv(mem, weights):
    rows = mem.shape[0]
    row_spec = pl.BlockSpec((ROW_TILE, D_MODEL), lambda i: (i, 0))
    head_spec = pl.BlockSpec((ROW_TILE, XA_HEADS, XA_HEAD_DIM), lambda i: (i, 0, 0))
    return pl.pallas_call(
        _mem_kv_kernel,
        grid=(rows // ROW_TILE,),
        in_specs=[row_spec] + [_resident(w) for w in weights],
        out_specs=[head_spec] * 2 + [row_spec] * 2,
        out_shape=[jax.ShapeDtypeStruct((rows, XA_HEADS, XA_HEAD_DIM), F32)] * 2
                  + [jax.ShapeDtypeStruct((rows, D_MODEL), BF16)] * 2,
        compiler_params=_params(("parallel",)),
        name="mem_kv",
    )(mem, *weights)


def _prompt_tail(x1, p, kb, vb, weights):
    batch, seq, _ = x1.shape
    tt = ROW_TILE
    block = PROMPT_TILES * tt
    steps_per_seq = seq // block
    n_steps = batch * steps_per_seq
    cur = lambda s: jnp.minimum(s, n_steps - 1)
    prev = lambda s: jnp.maximum(s - 1, 0)
    tok = lambda w: pl.BlockSpec(
        (None, block, w), lambda s: (cur(s) // steps_per_seq, cur(s) % steps_per_seq, 0))
    per_seq = lambda r, w: pl.BlockSpec((None, r, w), lambda s: (cur(s) // steps_per_seq, 0, 0))
    out_tok = pl.BlockSpec(
        (None, block, D_MODEL), lambda s: (prev(s) // steps_per_seq, prev(s) % steps_per_seq, 0))
    return pl.pallas_call(
        functools.partial(_prompt_tail_kernel, steps_per_seq=steps_per_seq, n_steps=n_steps),
        grid=(n_steps + 1,),
        in_specs=[tok(D_MODEL), tok(P_WIDTH), per_seq(N_MEM, D_MODEL), per_seq(N_MEM, D_MODEL)]
                 + [_resident(w) for w in weights],
        out_specs=[out_tok, per_seq(SUBLANES, RG_WIDTH), per_seq(GLA_KEY_WIDTH, GLA_DV)],
        out_shape=[jax.ShapeDtypeStruct((batch, seq, D_MODEL), F32),
                   jax.ShapeDtypeStruct((batch, SUBLANES, RG_WIDTH), F32),
                   jax.ShapeDtypeStruct((batch, GLA_KEY_WIDTH, GLA_DV), F32)],
        scratch_shapes=[pltpu.VMEM((SUBLANES + tt, RG_WIDTH), F32),
                        pltpu.VMEM((SUBLANES, RG_WIDTH), F32),
                        pltpu.VMEM((2, PAIR_K, PAIR_V), F32),
                        pltpu.VMEM((GLA_KEY_WIDTH // LANES, tt, LANES), F32),
                        pltpu.VMEM((2, PROMPT_TILES, tt, D_MODEL), F32)],
        compiler_params=_params(("arbitrary",)),
        name="prompt_tail",
    )(x1, p, kb, vb, *weights)


def _sample_mix(x1, p, econv, eh, s0, weights, seq):
    rows = x1.shape[0]
    nb = ROW_TILE // seq
    row_spec = lambda w: pl.BlockSpec((ROW_TILE, w), lambda i: (i, 0))
    state_spec = pl.BlockSpec((nb, GLA_KEY_WIDTH, GLA_DV), lambda i: (i, 0, 0))
    return pl.pallas_call(
        functools.partial(_sample_mix_kernel, seq=seq),
        grid=(rows // ROW_TILE,),
        in_specs=[row_spec(D_MODEL), row_spec(P_WIDTH), row_spec(RG_WIDTH), row_spec(RG_WIDTH),
                  state_spec] + [_resident(w) for w in weights],
        out_specs=[row_spec(D_MODEL), row_spec(D_MODEL),
                   pl.BlockSpec((nb, RG_WIDTH), lambda i: (i, 0)), state_spec],
        out_shape=[jax.ShapeDtypeStruct((rows, D_MODEL), F32),
                   jax.ShapeDtypeStruct((rows, D_MODEL), F32),
                   jax.ShapeDtypeStruct((rows // seq, RG_WIDTH), F32),
                   jax.ShapeDtypeStruct((rows // seq, GLA_KEY_WIDTH, GLA_DV), F32)],
        scratch_shapes=[pltpu.VMEM((RG_WIDTH // LANES, ROW_TILE, LANES), F32),
                        pltpu.VMEM((GLA_KEY_WIDTH // LANES, ROW_TILE, LANES), F32)],
        compiler_params=_params(("parallel",)),
        name="sample_mix",
    )(x1, p, econv, eh, s0, *weights)


def _head_interleaved_rows(x):
    b, m, h, dh = x.shape
    tiles = dh // LANES
    return (x.reshape(b, m, h, tiles, LANES).transpose(0, 1, 3, 2, 4)
            .reshape(b * m * tiles * h, LANES))


def _sample_attn_tail(q, k, v, x2, weights, seq):
    rows = q.shape[0]
    nb = SAMPLE_ATTN_BATCH
    row_spec = pl.BlockSpec((nb * seq, D_MODEL), lambda i: (i, 0))
    kv_rows = k.shape[0] // (rows // seq)
    kv_spec = pl.BlockSpec((nb * kv_rows, LANES), lambda i: (i, 0))
    return pl.pallas_call(
        functools.partial(_sample_attn_tail_kernel, seq=seq),
        grid=(rows // (nb * seq),),
        in_specs=[row_spec, kv_spec, kv_spec, row_spec] + [_resident(w) for w in weights],
        out_specs=row_spec,
        out_shape=jax.ShapeDtypeStruct((rows, D_MODEL), F32),
        scratch_shapes=[pltpu.VMEM((nb * seq, D_MODEL), F32)],
        compiler_params=_params(("parallel",)),
        name="sample_attn_tail",
    )(q, k, v, x2, *weights)


def _block_diag_heads(w):
    h, n, _ = w.shape
    eye = jnp.eye(h, dtype=w.dtype)
    return (eye[:, None, :, None] * w[:, :, None, :]).reshape(h * n, h * n)


def kernel(x_prompt, x_sample, cache_mem_k, cache_mem_v, state_conv, state_rglru, state_gla, mem_prompt, ffn1_norm, ffn1_w_gate, ffn1_w_up, ffn1_w_down, mix_norm, w_in, conv_w, conv_b, rg_w_a, rg_b_a, rg_w_x, rg_b_x, rg_lambda, rg_out_norm, gla_w_a2, gla_b_a2, gla_out_norm, w_out, xattn_norm, mem_norm, w_cq, w_ck, w_cv, w_co, ffn2_norm, ffn2_w_gate, ffn2_w_up, ffn2_w_down, final_norm):
    bp, tp, _ = x_prompt.shape
    bs, ts, _ = x_sample.shape
    assert ts == SUBLANES, "each sample sequence must fill exactly one 8-row sublane group"
    assert tp % (PROMPT_TILES * ROW_TILE) == 0 and (bs * ts) % (FFN_IN_TILES * ROW_TILE) == 0
    assert bs % SAMPLE_ATTN_BATCH == 0 and (bp * N_MEM) % ROW_TILE == 0
    assert cache_mem_k.shape[1:] == (N_MEM, XA_HEADS, XA_HEAD_DIM) and mem_prompt.shape[1] == N_MEM
    row = lambda g: g.reshape(1, -1)
    bf = lambda w: w.astype(BF16)

    w_in_p = bf(jnp.pad(w_in, ((0, 0), (0, P_WIDTH - D_IN))))
    hp = RG_HEADS // 2
    w_gate_rg = bf(jnp.stack([
        jnp.concatenate([_block_diag_heads(rg_w_a[c * hp:(c + 1) * hp]),
                         _block_diag_heads(rg_w_x[c * hp:(c + 1) * hp])], axis=1)
        for c in range(2)]))
    w_a2_p = bf(jnp.pad(gla_w_a2, ((0, A_PAD - GLA_GATE_RANK), (0, 0))))
    ffn_in_w = (row(ffn1_norm), bf(ffn1_w_gate), bf(ffn1_w_up), bf(ffn1_w_down), row(mix_norm), w_in_p)
    mixer_w = (conv_w, row(conv_b), w_gate_rg, row(rg_b_a), row(rg_b_x), row(rg_lambda),
               row(rg_out_norm), w_a2_p, row(gla_b_a2), row(jnp.tile(gla_out_norm, GLA_HEADS)),
               bf(w_out))
    w_cq_b, w_co_b = bf(w_cq), bf(w_co)
    ffn2_w = (row(ffn2_norm), bf(ffn2_w_gate), bf(ffn2_w_up), bf(ffn2_w_down), row(final_norm))

    mem_k, mem_v, mem_kb, mem_vb = _mem_kv(mem_prompt.reshape(bp * N_MEM, D_MODEL),
                                           (row(mem_norm), bf(w_ck), bf(w_cv)))
    x1_p, p_p = _ffn_in(x_prompt.reshape(bp * tp, D_MODEL), ffn_in_w)
    p_p = p_p.reshape(bp, tp, P_WIDTH)
    y_p, rgh_p, s_p = _prompt_tail(
        x1_p.reshape(bp, tp, D_MODEL), p_p,
        mem_kb.reshape(bp, N_MEM, D_MODEL), mem_vb.reshape(bp, N_MEM, D_MODEL),
        mixer_w + (row(xattn_norm), w_cq_b, w_co_b) + ffn2_w)
    conv_p = p_p[:, tp - (CONV_WIDTH - 1):, OFF_RG_X:OFF_RG_Y]

    x1_s, p_s = _ffn_in(x_sample.reshape(bs * ts, D_MODEL), ffn_in_w)
    econv = jnp.pad(state_conv, ((0, 0), (ts - (CONV_WIDTH - 1), 0), (0, 0))).reshape(bs * ts, RG_WIDTH)
    eh = jnp.pad(state_rglru[:, None, :], ((0, 0), (0, ts - 1), (0, 0))).reshape(bs * ts, RG_WIDTH)
    x2_s, q_s, rgh_s, s_s = _sample_mix(
        x1_s, p_s, econv, eh, state_gla.reshape(bs, GLA_KEY_WIDTH, GLA_DV),
        mixer_w + (row(xattn_norm), w_cq_b), ts)
    y_s = _sample_attn_tail(q_s, _head_interleaved_rows(cache_mem_k), _head_interleaved_rows(cache_mem_v),
                            x2_s, (w_co_b,) + ffn2_w, ts)
    conv_s = p_s.reshape(bs, ts, P_WIDTH)[:, ts - (CONV_WIDTH - 1):, OFF_RG_X:OFF_RG_Y]

    return (y_p, y_s.reshape(bs, ts, D_MODEL),
            mem_k.reshape(bp, N_MEM, XA_HEADS, XA_HEAD_DIM),
            mem_v.reshape(bp, N_MEM, XA_HEADS, XA_HEAD_DIM),
            conv_p, rgh_p[:, 0, :], s_p.reshape(bp, GLA_HEADS, GLA_DK, GLA_DV),
            conv_s, rgh_s, s_s.reshape(bs, GLA_HEADS, GLA_DK, GLA_DV))
```

```python
import functools

import jax
import jax.numpy as jnp
import numpy as np
from jax import lax
from jax.experimental import pallas as pl
from jax.experimental.pallas import tpu as pltpu

F32 = jnp.float32
BF16 = jnp.bfloat16

D_MODEL = 1024
D_FF = 2816
RG_WIDTH = 512
RG_HEADS = 8
RG_HEAD_DIM = 64
CONV_WIDTH = 4
RG_C = 8.0
GLA_WIDTH = 512
GLA_HEADS = 4
GLA_DV = 128
GLA_DK = 64
GLA_KEY_WIDTH = 256
GLA_GATE_RANK = 16
GLA_GATE_NORMALIZER = 16.0
GLA_CHUNK = 32
N_MEM = 256
XA_HEADS = 4
XA_HEAD_DIM = 256
EPS = 1e-6
MASKED_SCORE = -np.inf

OFF_RG_X = 0
OFF_RG_Y = 512
OFF_Q = 1024
OFF_K = 1280
OFF_V = 1536
OFF_G = 2048
OFF_A = 2560
D_IN = 2576

LANES = 128
SUBLANES = 8
A_PAD = LANES
P_WIDTH = OFF_A + A_PAD
PAIR_K = 2 * GLA_DK
PAIR_V = 2 * GLA_DV
VMEM_LIMIT = 58 * 1024 * 1024

ROW_TILE = 256
FFN_IN_TILES = 2
PROMPT_TILES = 2
FF_CHUNK = 256
FF_DOWN_GROUP = 2
FFN2_PLACEMENT = (1, 1, 1, 1, 1, 1, 1, 1, 1, 1, 1)
SAMPLE_ATTN_BATCH = 8


def _rms(x, g):
    return x * lax.rsqrt(jnp.mean(x * x, axis=-1, keepdims=True) + EPS) * g


def _mm(a, w):
    return jnp.dot(a.astype(BF16), w, preferred_element_type=F32)


def _mm_nt(a, b):
    return lax.dot_general(a.astype(BF16), b.astype(BF16), (((1,), (1,)), ((), ())),
                           preferred_element_type=F32)


def _mm_tn(a, b):
    return lax.dot_general(a.astype(BF16), b.astype(BF16), (((0,), (0,)), ((), ())),
                           preferred_element_type=F32)


def _silu(x):
    return x * jax.nn.sigmoid(x)


def _gelu_tanh(x):
    c = np.float32(np.sqrt(2.0 / np.pi))
    return x * (0.5 * (1.0 + jnp.tanh(c * (x + 0.044715 * (x * x * x)))))


def _softplus(x):
    return jnp.maximum(x, 0.0) + jnp.log1p(jnp.exp(-jnp.abs(x)))


def _ffn(x, norm_ref, wg_ref, wu_ref, wd_ref):
    h = _rms(x, norm_ref[...]).astype(BF16)
    g = jnp.dot(h, wg_ref[...], preferred_element_type=F32)
    u = jnp.dot(h, wu_ref[...], preferred_element_type=F32)
    return _mm(_silu(g) * u, wd_ref[...])


def _last_rows(buf_ref, x, seg):
    n = x.shape[0] // seg
    outs = []
    for j in range(x.shape[1] // LANES):
        buf_ref[j] = x[:, j * LANES:(j + 1) * LANES]
        outs.append(buf_ref[j, pl.ds(seg - 1, n, stride=seg), :])
    return jnp.concatenate(outs, axis=1)


def _ffn_pieces(x, norm_ref, wg_ref, wu_ref, wd_ref, out):
    h = _rms(x, norm_ref[...]).astype(BF16)
    acc = None
    acts = []
    n = D_FF // FF_CHUNK
    for c in range(n):
        cols = slice(c * FF_CHUNK, (c + 1) * FF_CHUNK)
        g = jnp.dot(h, wg_ref[:, cols], preferred_element_type=F32)
        u = jnp.dot(h, wu_ref[:, cols], preferred_element_type=F32)
        acts.append((_silu(g) * u).astype(BF16))
        if len(acts) == FF_DOWN_GROUP or c == n - 1:
            lo = (c + 1 - len(acts)) * FF_CHUNK
            part = jnp.dot(jnp.concatenate(acts, axis=1), wd_ref[lo:(c + 1) * FF_CHUNK, :],
                           preferred_element_type=F32)
            acc = part if acc is None else acc + part
            acts = []
        yield
    out.append(acc)


def _shift_rows(x, s):
    return pltpu.roll(x, s, axis=0)


def _row_in_segment(shape, seg):
    return lax.broadcasted_iota(jnp.int32, shape, 0) & (seg - 1)


def _segment_cumsum(x, seg):
    pos = _row_in_segment(x.shape, seg)
    s = 1
    while s < seg:
        x = jnp.where(pos >= s, x + _shift_rows(x, s), x)
        s *= 2
    return x


def _segment_affine_scan(a, u, seg):
    pos = _row_in_segment(a.shape, seg)
    s = 1
    while s < seg:
        m = pos >= s
        u = jnp.where(m, a * _shift_rows(u, s) + u, u)
        a = jnp.where(m, a * _shift_rows(a, s), a)
        s *= 2
    return a, u


def _conv(xb, sh1, sh2, sh3, cw_ref, cb_ref):
    y = cb_ref[...] + sh3 * cw_ref[0:1, :]
    y = y + sh2 * cw_ref[1:2, :]
    y = y + sh1 * cw_ref[2:3, :]
    return y + xb * cw_ref[3:4, :]


def _rg_gates(xc, wgate_ref, ba_ref, bx_ref):
    half = RG_WIDTH // 2
    r, i = [], []
    for c in range(2):
        z = _mm(xc[:, c * half:(c + 1) * half], wgate_ref[c])
        r.append(z[:, :half])
        i.append(z[:, half:])
    r = jax.nn.sigmoid(jnp.concatenate(r, axis=1) + ba_ref[...])
    i = jax.nn.sigmoid(jnp.concatenate(i, axis=1) + bx_ref[...])
    return r, i


def _rg_decay_input(xc, r, i, lam_ref):
    log_a = (-RG_C * _softplus(-lam_ref[...])) * r
    a = jnp.exp(log_a)
    mult = jnp.sqrt(-jnp.tanh(log_a) * (a * a + 1.0))
    return a, mult * (i * xc)


def _gla_log_decay(a_low, wa2_ref, ba2_ref):
    z = _mm(a_low, wa2_ref[...]) + ba2_ref[...]
    return (-_softplus(-z)) / GLA_GATE_NORMALIZER


def _gla_intra(qt, kt, v, chunk):
    t = qt.shape[0]
    shift = int(np.log2(chunk))
    ri = lax.broadcasted_iota(jnp.int32, (t, t), 0)
    ci = lax.broadcasted_iota(jnp.int32, (t, t), 1)
    causal = (ri >= ci) & ((ri >> shift) == (ci >> shift))
    lane = lax.broadcasted_iota(jnp.int32, (t, PAIR_K), 1)
    outs = []
    for h in range(GLA_HEADS):
        p = h // 2
        qp = qt[:, p * PAIR_K:(p + 1) * PAIR_K]
        kp = kt[:, p * PAIR_K:(p + 1) * PAIR_K]
        mine = (lane >= GLA_DK) if (h % 2) else (lane < GLA_DK)
        s = _mm_nt(jnp.where(mine, qp, 0.0), kp)
        attn = jnp.where(causal, s, 0.0)
        outs.append(_mm(attn, v[:, h * GLA_DV:(h + 1) * GLA_DV].astype(BF16)))
    return outs


def _pair_blockdiag_mask():
    r = lax.broadcasted_iota(jnp.int32, (PAIR_K, PAIR_V), 0)
    c = lax.broadcasted_iota(jnp.int32, (PAIR_K, PAIR_V), 1)
    return (r < GLA_DK) == (c < GLA_DV)


def _gla_chunk_updates(k, v, b, b_last, chunk):
    bd_mask = _pair_blockdiag_mask()
    kv = {}
    for c in range(k.shape[0] // chunk):
        rows = slice(c * chunk, (c + 1) * chunk)
        for p in range(2):
            kl = slice(p * PAIR_K, (p + 1) * PAIR_K)
            kd = k[rows, kl] * jnp.exp(b_last[c:c + 1, kl] - b[rows, kl])
            kv[c, p] = jnp.where(bd_mask, _mm_tn(kd, v[rows, p * PAIR_V:(p + 1) * PAIR_V]), 0.0)
    return kv


def _gla_combine(intra, qt, states, chunk):
    n_chunks = qt.shape[0] // chunk
    heads = []
    for p in range(2):
        kl = slice(p * PAIR_K, (p + 1) * PAIR_K)
        inter = jnp.concatenate(
            [_mm(qt[c * chunk:(c + 1) * chunk, kl], states[c, p]) for c in range(n_chunks)],
            axis=0)
        heads.append(intra[2 * p] + inter[:, :GLA_DV])
        heads.append(intra[2 * p + 1] + inter[:, GLA_DV:])
    return heads


def _gla_out(o_heads, g, norm_ref):
    outs = []
    for h in range(GLA_HEADS):
        outs.append(_rms(o_heads[h], norm_ref[:, h * GLA_DV:(h + 1) * GLA_DV]))
    return jnp.concatenate(outs, axis=1) * _silu(g)


def _softmax_rows(s):
    m = jnp.max(s, axis=-1, keepdims=True)
    e = jnp.exp(s - m)
    return e * (1.0 / jnp.sum(e, axis=-1, keepdims=True))


def _split_p(p_ref, rows=slice(None)):
    xb = p_ref[rows, OFF_RG_X:OFF_RG_Y]
    yb = p_ref[rows, OFF_RG_Y:OFF_Q]
    q = p_ref[rows, OFF_Q:OFF_K]
    k = p_ref[rows, OFF_K:OFF_V]
    v = p_ref[rows, OFF_V:OFF_G]
    g = p_ref[rows, OFF_G:OFF_A]
    a_low = p_ref[rows, OFF_A:P_WIDTH]
    return xb, yb, q, k, v, g, a_low


def _gla_prepare(q, k, log_a, chunk):
    b = _segment_cumsum(log_a, chunk)
    qt = (q * (GLA_DK ** -0.5)) * jnp.exp(b)
    kt = k * jnp.exp(-b)
    return b, qt, kt


def _mix_project(rg_out, gla_out, wout_ref):
    return (_mm(rg_out, wout_ref[0:RG_WIDTH, :]) + _mm(gla_out, wout_ref[RG_WIDTH:, :]))


def _ffn_in_kernel(x_ref, n1_ref, wg_ref, wu_ref, wd_ref, n2_ref, win_ref, x1_ref, p_ref):
    for sub in range(x_ref.shape[0] // ROW_TILE):
        rows = slice(sub * ROW_TILE, (sub + 1) * ROW_TILE)
        x = x_ref[rows, :]
        x1 = x + 0.5 * _ffn(x, n1_ref, wg_ref, wu_ref, wd_ref)
        x1_ref[rows, :] = x1
        p_ref[rows, :] = _mm(_rms(x1, n2_ref[...]), win_ref[...])


def _mem_kv_kernel(m_ref, n_ref, wk_ref, wv_ref, k_ref, v_ref, kb_ref, vb_ref):
    h = _rms(m_ref[...], n_ref[...]).astype(BF16)
    k = jnp.dot(h, wk_ref[...], preferred_element_type=F32)
    v = jnp.dot(h, wv_ref[...], preferred_element_type=F32)
    for h in range(XA_HEADS):
        sl = slice(h * XA_HEAD_DIM, (h + 1) * XA_HEAD_DIM)
        k_ref[:, h, :] = k[:, sl]
        v_ref[:, h, :] = v[:, sl]
    kb_ref[...] = k.astype(BF16)
    vb_ref[...] = v.astype(BF16)


def _prompt_tail_kernel(
        x1_ref, p_ref, kb_ref, vb_ref,
        cw_ref, cb_ref, wgate_ref, ba_ref, bx_ref, lam_ref, rgn_ref,
        wa2_ref, ba2_ref, glan_ref, wout_ref,
        xn_ref, wcq_ref, wco_ref, n2_ref, wg_ref, wu_ref, wd_ref, fn_ref,
        y_ref, rgh_ref, s_ref,
        xpad_ref, h_ref, sbd_ref, b_ref, x3_ref, *, steps_per_seq, n_steps):
    step = pl.program_id(0)
    t_idx = jnp.minimum(step, n_steps - 1) % steps_per_seq
    slot = step % 2
    tt = ROW_TILE
    subs = x1_ref.shape[0] // tt
    n_chunks = tt // GLA_CHUNK

    @pl.when(t_idx == 0)
    def _():
        xpad_ref[0:SUBLANES, :] = jnp.zeros((SUBLANES, RG_WIDTH), F32)
        h_ref[...] = jnp.zeros_like(h_ref)
        sbd_ref[...] = jnp.zeros_like(sbd_ref)

    def stage2(sub):
        x3_prev = x3_ref[1 - slot, sub]
        out = []
        yield from _ffn_pieces(x3_prev, n2_ref, wg_ref, wu_ref, wd_ref, out)
        y_ref[sub * tt:(sub + 1) * tt, :] = _rms(x3_prev + 0.5 * out[0], fn_ref[...])

    def stage1(sub, emit_ffn):
        rows = slice(sub * tt, (sub + 1) * tt)
        xb, yb, q, k, v, g, a_low = _split_p(p_ref, rows)

        xpad_ref[SUBLANES:SUBLANES + tt, :] = xb
        sh1 = xpad_ref[SUBLANES - 1:SUBLANES - 1 + tt, :]
        sh2 = xpad_ref[SUBLANES - 2:SUBLANES - 2 + tt, :]
        sh3 = xpad_ref[SUBLANES - 3:SUBLANES - 3 + tt, :]
        xc = _conv(xb, sh1, sh2, sh3, cw_ref, cb_ref)
        xpad_ref[0:SUBLANES, :] = xb[tt - SUBLANES:tt, :]
        emit_ffn()
        r, i = _rg_gates(xc, wgate_ref, ba_ref, bx_ref)
        a, u = _rg_decay_input(xc, r, i, lam_ref)
        emit_ffn()
        a_grp, h_grp = _segment_affine_scan(a, u, SUBLANES)
        emit_ffn()
        carry = h_ref[0:1, :]
        groups = []
        for gi in range(tt // SUBLANES):
            grp = slice(gi * SUBLANES, (gi + 1) * SUBLANES)
            groups.append(h_grp[grp, :] + a_grp[grp, :] * carry)
            carry = groups[-1][SUBLANES - 1:SUBLANES, :]
        hs = jnp.concatenate(groups, axis=0)
        h_ref[...] = jnp.broadcast_to(carry, h_ref.shape)
        rg_out = _rms(hs * _gelu_tanh(yb), rgn_ref[...]).astype(BF16)
        emit_ffn()

        log_a = _gla_log_decay(a_low, wa2_ref, ba2_ref)
        b, qt, kt = _gla_prepare(q, k, log_a, GLA_CHUNK)
        emit_ffn()
        b_last = _last_rows(b_ref, b, GLA_CHUNK)
        dec_t = jnp.exp(b_last).T
        kv = _gla_chunk_updates(k, v, b, b_last, GLA_CHUNK)
        states = {}
        for p in range(2):
            s_bd = sbd_ref[p]
            for c in range(n_chunks):
                states[c, p] = s_bd.astype(BF16)
                s_bd = s_bd * dec_t[p * PAIR_K:(p + 1) * PAIR_K, c:c + 1] + kv[c, p]
            sbd_ref[p] = s_bd
        emit_ffn()
        intra = _gla_intra(qt, kt, v, GLA_CHUNK)
        emit_ffn()
        gla_out = _gla_out(_gla_combine(intra, qt, states, GLA_CHUNK), g, glan_ref).astype(BF16)
        emit_ffn()

        x2 = x1_ref[rows, :] + _mix_project(rg_out, gla_out, wout_ref)

        qx = _mm(_rms(x2, xn_ref[...]), wcq_ref[...])
        emit_ffn()
        head = lambda h: slice(h * XA_HEAD_DIM, (h + 1) * XA_HEAD_DIM)
        s = jnp.concatenate([_mm_nt(qx[:, head(h)], kb_ref[:, head(h)]) for h in range(XA_HEADS)], axis=0)
        pr = _softmax_rows(s * (XA_HEAD_DIM ** -0.5))
        emit_ffn()
        o = jnp.concatenate([_mm(pr[h * tt:(h + 1) * tt, :], vb_ref[:, head(h)]) for h in range(XA_HEADS)], axis=1)
        x3_ref[slot, sub] = x2 + _mm(o, wco_ref[...])

    def run(with_stage1, with_stage2):
        for sub in range(subs):
            if not with_stage1:
                for _ in stage2(sub):
                    pass
            elif not with_stage2:
                stage1(sub, lambda: None)
            else:
                ffn = stage2(sub)
                placement = iter(FFN2_PLACEMENT)

                def emit_ffn():
                    for _ in range(next(placement)):
                        next(ffn)

                emit_ffn()
                stage1(sub, emit_ffn)
                for _ in ffn:
                    pass

    pl.when(step == 0)(lambda: run(True, False))
    pl.when((step > 0) & (step < n_steps))(lambda: run(True, True))
    pl.when(step == n_steps)(lambda: run(False, True))

    @pl.when((t_idx == steps_per_seq - 1) & (step < n_steps))
    def _():
        rgh_ref[...] = h_ref[...]
        row = lax.broadcasted_iota(jnp.int32, (PAIR_K, GLA_DV), 0)
        for p in range(2):
            s_bd = sbd_ref[p]
            s_ref[p * PAIR_K:(p + 1) * PAIR_K, :] = jnp.where(
                row < GLA_DK, s_bd[:, :GLA_DV], s_bd[:, GLA_DV:])


def _sample_mix_kernel(
        x1_ref, p_ref, econv_ref, eh_ref, s0_ref,
        cw_ref, cb_ref, wgate_ref, ba_ref, bx_ref, lam_ref, rgn_ref,
        wa2_ref, ba2_ref, glan_ref, wout_ref, xn_ref, wcq_ref,
        x2_ref, q_ref, rgh_ref, s_ref,
        hs_ref, b_ref, *, seq):
    rows_n = x1_ref.shape[0]
    nb = rows_n // seq

    xb, yb, q, k, v, g, a_low = _split_p(p_ref)

    pos = _row_in_segment(xb.shape, seq)
    econv = econv_ref[...]
    sh = []
    for j in range(1, CONV_WIDTH):
        sh.append(jnp.where(pos >= j, _shift_rows(xb, j), _shift_rows(econv, rows_n - seq + j)))
    xc = _conv(xb, sh[0], sh[1], sh[2], cw_ref, cb_ref)
    r, i = _rg_gates(xc, wgate_ref, ba_ref, bx_ref)
    a, u = _rg_decay_input(xc, r, i, lam_ref)
    u = u + a * eh_ref[...]
    _, hs = _segment_affine_scan(a, u, seq)
    rgh_ref[...] = _last_rows(hs_ref, hs, seq)
    rg_out = _rms(hs * _gelu_tanh(yb), rgn_ref[...]).astype(BF16)

    log_a = _gla_log_decay(a_low, wa2_ref, ba2_ref)
    b, qt, kt = _gla_prepare(q, k, log_a, seq)
    intra = _gla_intra(qt, kt, v, seq)
    b_last = _last_rows(b_ref, b, seq)
    dec_t = jnp.exp(b_last).T
    kv = _gla_chunk_updates(k, v, b, b_last, seq)
    row = lax.broadcasted_iota(jnp.int32, (PAIR_K, GLA_DV), 0)
    top = row < GLA_DK
    states = {}
    for c in range(nb):
        for p in range(2):
            kl = slice(p * PAIR_K, (p + 1) * PAIR_K)
            s_pair = s0_ref[c, kl, :]
            s_bd = jnp.concatenate([jnp.where(top, s_pair, 0.0), jnp.where(top, 0.0, s_pair)], axis=1)
            states[c, p] = s_bd.astype(BF16)
            s_new = s_bd * dec_t[kl, c:c + 1] + kv[c, p]
            s_ref[c, kl, :] = jnp.where(top, s_new[:, :GLA_DV], s_new[:, GLA_DV:])
    gla_out = _gla_out(_gla_combine(intra, qt, states, seq), g, glan_ref).astype(BF16)

    x2 = x1_ref[...] + _mix_project(rg_out, gla_out, wout_ref)
    x2_ref[...] = x2
    q_ref[...] = _mm(_rms(x2, xn_ref[...]), wcq_ref[...])


def _sample_attn_tail_kernel(q_ref, k_ref, v_ref, x2_ref, wco_ref, n2_ref, wg_ref, wu_ref, wd_ref, fn_ref,
                             y_ref, o_ref, *, seq):
    tiles = XA_HEAD_DIM // LANES
    group = tiles * XA_HEADS
    r = N_MEM * group
    nb = k_ref.shape[0] // r
    hs = XA_HEADS * seq
    lane = lax.broadcasted_iota(jnp.int32, (nb * hs, r), 1)
    head = (lax.broadcasted_iota(jnp.int32, (nb * hs, r), 0) // seq) & (XA_HEADS - 1)
    own = (lane & (group - 1)) == head
    s = []
    for j in range(nb):
        q = q_ref[j * seq:(j + 1) * seq, :]
        q_parts = jnp.concatenate(
            [q[:, c * LANES:(c + 1) * LANES] for c in range(XA_HEADS * tiles)], axis=0)
        part = _mm_nt(q_parts, k_ref[j * r:(j + 1) * r, :])
        for h in range(XA_HEADS):
            acc = part[h * tiles * seq:(h * tiles + 1) * seq, :]
            for c in range(1, tiles):
                blk = part[(h * tiles + c) * seq:(h * tiles + c + 1) * seq, :]
                acc = acc + pltpu.roll(blk, r - c * XA_HEADS, axis=1)
            s.append(acc)
    p = _softmax_rows(jnp.where(own, jnp.concatenate(s, axis=0) * (XA_HEAD_DIM ** -0.5), MASKED_SCORE))
    for j in range(nb):
        pj = p[j * hs:(j + 1) * hs, :]
        p_all = jnp.concatenate(
            [pj] + [pltpu.roll(pj, c * XA_HEADS, axis=1) for c in range(1, tiles)], axis=0)
        o = _mm(p_all, v_ref[j * r:(j + 1) * r, :].astype(BF16))
        for c in range(tiles):
            for h in range(XA_HEADS):
                col = h * XA_HEAD_DIM + c * LANES
                o_ref[j * seq:(j + 1) * seq, col:col + LANES] = (
                    o[(c * XA_HEADS + h) * seq:(c * XA_HEADS + h + 1) * seq, :])
    x3 = x2_ref[...] + _mm(o_ref[...], wco_ref[...])
    x4 = x3 + 0.5 * _ffn(x3, n2_ref, wg_ref, wu_ref, wd_ref)
    y_ref[...] = _rms(x4, fn_ref[...])


def _resident(arr):
    nd = arr.ndim
    return pl.BlockSpec(arr.shape, lambda *_: (0,) * nd, pipeline_mode=pl.Buffered(1))


def _params(sem, flags=None):
    return pltpu.CompilerParams(dimension_semantics=sem, vmem_limit_bytes=VMEM_LIMIT, flags=flags)


def _ffn_in(x, weights):
    rows = x.shape[0]
    block = FFN_IN_TILES * ROW_TILE
    row_spec = lambda w: pl.BlockSpec((block, w), lambda i: (i, 0))
    return pl.pallas_call(
        _ffn_in_kernel,
        grid=(rows // block,),
        in_specs=[row_spec(D_MODEL)] + [_resident(w) for w in weights],
        out_specs=[row_spec(D_MODEL), row_spec(P_WIDTH)],
        out_shape=[jax.ShapeDtypeStruct((rows, D_MODEL), F32),
                   jax.ShapeDtypeStruct((rows, P_WIDTH), F32)],
        compiler_params=_params(("parallel",)),
        name="ffn_in",
    )(x, *weights)


def _mem_kv(mem, weights):
    rows = mem.shape[0]
    row_spec = pl.BlockSpec((ROW_TILE, D_MODEL), lambda i: (i, 0))
    head_spec = pl.BlockSpec((ROW_TILE, XA_HEADS, XA_HEAD_DIM), lambda i: (i, 0, 0))
    return pl.pallas_call(
        _mem_kv_kernel,
        grid=(rows // ROW_TILE,),
        in_specs=[row_spec] + [_resident(w) for w in weights],
        out_specs=[head_spec] * 2 + [row_spec] * 2,
        out_shape=[jax.ShapeDtypeStruct((rows, XA_HEADS, XA_HEAD_DIM), F32)] * 2
                  + [jax.ShapeDtypeStruct((rows, D_MODEL), BF16)] * 2,
        compiler_params=_params(("parallel",)),
        name="mem_kv",
    )(mem, *weights)


def _prompt_tail(x1, p, kb, vb, weights):
    batch, seq, _ = x1.shape
    tt = ROW_TILE
    block = PROMPT_TILES * tt
    steps_per_seq = seq // block
    n_steps = batch * steps_per_seq
    cur = lambda s: jnp.minimum(s, n_steps - 1)
    prev = lambda s: jnp.maximum(s - 1, 0)
    tok = lambda w: pl.BlockSpec(
        (None, block, w), lambda s: (cur(s) // steps_per_seq, cur(s) % steps_per_seq, 0))
    per_seq = lambda r, w: pl.BlockSpec((None, r, w), lambda s: (cur(s) // steps_per_seq, 0, 0))
    out_tok = pl.BlockSpec(
        (None, block, D_MODEL), lambda s: (prev(s) // steps_per_seq, prev(s) % steps_per_seq, 0))
    return pl.pallas_call(
        functools.partial(_prompt_tail_kernel, steps_per_seq=steps_per_seq, n_steps=n_steps),
        grid=(n_steps + 1,),
        in_specs=[tok(D_MODEL), tok(P_WIDTH), per_seq(N_MEM, D_MODEL), per_seq(N_MEM, D_MODEL)]
                 + [_resident(w) for w in weights],
        out_specs=[out_tok, per_seq(SUBLANES, RG_WIDTH), per_seq(GLA_KEY_WIDTH, GLA_DV)],
        out_shape=[jax.ShapeDtypeStruct((batch, seq, D_MODEL), F32),
                   jax.ShapeDtypeStruct((batch, SUBLANES, RG_WIDTH), F32),
                   jax.ShapeDtypeStruct((batch, GLA_KEY_WIDTH, GLA_DV), F32)],
        scratch_shapes=[pltpu.VMEM((SUBLANES + tt, RG_WIDTH), F32),
                        pltpu.VMEM((SUBLANES, RG_WIDTH), F32),
                        pltpu.VMEM((2, PAIR_K, PAIR_V), F32),
                        pltpu.VMEM((GLA_KEY_WIDTH // LANES, tt, LANES), F32),
                        pltpu.VMEM((2, PROMPT_TILES, tt, D_MODEL), F32)],
        compiler_params=_params(("arbitrary",)),
        name="prompt_tail",
    )(x1, p, kb, vb, *weights)


def _sample_mix(x1, p, econv, eh, s0, weights, seq):
    rows = x1.shape[0]
    nb = ROW_TILE // seq
    row_spec = lambda w: pl.BlockSpec((ROW_TILE, w), lambda i: (i, 0))
    state_spec = pl.BlockSpec((nb, GLA_KEY_WIDTH, GLA_DV), lambda i: (i, 0, 0))
    return pl.pallas_call(
        functools.partial(_sample_mix_kernel, seq=seq),
        grid=(rows // ROW_TILE,),
        in_specs=[row_spec(D_MODEL), row_spec(P_WIDTH), row_spec(RG_WIDTH), row_spec(RG_WIDTH),
                  state_spec] + [_resident(w) for w in weights],
        out_specs=[row_spec(D_MODEL), row_spec(D_MODEL),
                   pl.BlockSpec((nb, RG_WIDTH), lambda i: (i, 0)), state_spec],
        out_shape=[jax.ShapeDtypeStruct((rows, D_MODEL), F32),
                   jax.ShapeDtypeStruct((rows, D_MODEL), F32),
                   jax.ShapeDtypeStruct((rows // seq, RG_WIDTH), F32),
                   jax.ShapeDtypeStruct((rows // seq, GLA_KEY_WIDTH, GLA_DV), F32)],
        scratch_shapes=[pltpu.VMEM((RG_WIDTH // LANES, ROW_TILE, LANES), F32),
                        pltpu.VMEM((GLA_KEY_WIDTH // LANES, ROW_TILE, LANES), F32)],
        compiler_params=_params(("parallel",)),
        name="sample_mix",
    )(x1, p, econv, eh, s0, *weights)


def _head_interleaved_rows(x):
    b, m, h, dh = x.shape
    tiles = dh // LANES
    return (x.reshape(b, m, h, tiles, LANES).transpose(0, 1, 3, 2, 4)
            .reshape(b * m * tiles * h, LANES))


def _sample_attn_tail(q, k, v, x2, weights, seq):
    rows = q.shape[0]
    nb = SAMPLE_ATTN_BATCH
    row_spec = pl.BlockSpec((nb * seq, D_MODEL), lambda i: (i, 0))
    kv_rows = k.shape[0] // (rows // seq)
    kv_spec = pl.BlockSpec((nb * kv_rows, LANES), lambda i: (i, 0))
    return pl.pallas_call(
        functools.partial(_sample_attn_tail_kernel, seq=seq),
        grid=(rows // (nb * seq),),
        in_specs=[row_spec, kv_spec, kv_spec, row_spec] + [_resident(w) for w in weights],
        out_specs=row_spec,
        out_shape=jax.ShapeDtypeStruct((rows, D_MODEL), F32),
        scratch_shapes=[pltpu.VMEM((nb * seq, D_MODEL), F32)],
        compiler_params=_params(("parallel",)),
        name="sample_attn_tail",
    )(q, k, v, x2, *weights)


def _block_diag_heads(w):
    h, n, _ = w.shape
    eye = jnp.eye(h, dtype=w.dtype)
    return (eye[:, None, :, None] * w[:, :, None, :]).reshape(h * n, h * n)


def kernel(x_prompt, x_sample, cache_mem_k, cache_mem_v, state_conv, state_rglru, state_gla, mem_prompt, ffn1_norm, ffn1_w_gate, ffn1_w_up, ffn1_w_down, mix_norm, w_in, conv_w, conv_b, rg_w_a, rg_b_a, rg_w_x, rg_b_x, rg_lambda, rg_out_norm, gla_w_a2, gla_b_a2, gla_out_norm, w_out, xattn_norm, mem_norm, w_cq, w_ck, w_cv, w_co, ffn2_norm, ffn2_w_gate, ffn2_w_up, ffn2_w_down, final_norm):
    bp, tp, _ = x_prompt.shape
    bs, ts, _ = x_sample.shape
    assert ts == SUBLANES, "each sample sequence must fill exactly one 8-row sublane group"
    assert tp % (PROMPT_TILES * ROW_TILE) == 0 and (bs * ts) % (FFN_IN_TILES * ROW_TILE) == 0
    assert bs % SAMPLE_ATTN_BATCH == 0 and (bp * N_MEM) % ROW_TILE == 0
    assert cache_mem_k.shape[1:] == (N_MEM, XA_HEADS, XA_HEAD_DIM) and mem_prompt.shape[1] == N_MEM
    row = lambda g: g.reshape(1, -1)
    bf = lambda w: w.astype(BF16)

    w_in_p = bf(jnp.pad(w_in, ((0, 0), (0, P_WIDTH - D_IN))))
    hp = RG_HEADS // 2
    w_gate_rg = bf(jnp.stack([
        jnp.concatenate([_block_diag_heads(rg_w_a[c * hp:(c + 1) * hp]),
                         _block_diag_heads(rg_w_x[c * hp:(c + 1) * hp])], axis=1)
        for c in range(2)]))
    w_a2_p = bf(jnp.pad(gla_w_a2, ((0, A_PAD - GLA_GATE_RANK), (0, 0))))
    ffn_in_w = (row(ffn1_norm), bf(ffn1_w_gate), bf(ffn1_w_up), bf(ffn1_w_down), row(mix_norm), w_in_p)
    mixer_w = (conv_w, row(conv_b), w_gate_rg, row(rg_b_a), row(rg_b_x), row(rg_lambda),
               row(rg_out_norm), w_a2_p, row(gla_b_a2), row(jnp.tile(gla_out_norm, GLA_HEADS)),
               bf(w_out))
    w_cq_b, w_co_b = bf(w_cq), bf(w_co)
    ffn2_w = (row(ffn2_norm), bf(ffn2_w_gate), bf(ffn2_w_up), bf(ffn2_w_down), row(final_norm))

    mem_k, mem_v, mem_kb, mem_vb = _mem_kv(mem_prompt.reshape(bp * N_MEM, D_MODEL),
                                           (row(mem_norm), bf(w_ck), bf(w_cv)))
    x1_p, p_p = _ffn_in(x_prompt.reshape(bp * tp, D_MODEL), ffn_in_w)
    p_p = p_p.reshape(bp, tp, P_WIDTH)
    y_p, rgh_p, s_p = _prompt_tail(
        x1_p.reshape(bp, tp, D_MODEL), p_p,
        mem_kb.reshape(bp, N_MEM, D_MODEL), mem_vb.reshape(bp, N_MEM, D_MODEL),
        mixer_w + (row(xattn_norm), w_cq_b, w_co_b) + ffn2_w)
    conv_p = p_p[:, tp - (CONV_WIDTH - 1):, OFF_RG_X:OFF_RG_Y]

    x1_s, p_s = _ffn_in(x_sample.reshape(bs * ts, D_MODEL), ffn_in_w)
    econv = jnp.pad(state_conv, ((0, 0), (ts - (CONV_WIDTH - 1), 0), (0, 0))).reshape(bs * ts, RG_WIDTH)
    eh = jnp.pad(state_rglru[:, None, :], ((0, 0), (0, ts - 1), (0, 0))).reshape(bs * ts, RG_WIDTH)
    x2_s, q_s, rgh_s, s_s = _sample_mix(
        x1_s, p_s, econv, eh, state_gla.reshape(bs, GLA_KEY_WIDTH, GLA_DV),
        mixer_w + (row(xattn_norm), w_cq_b), ts)
    y_s = _sample_attn_tail(q_s, _head_interleaved_rows(cache_mem_k), _head_interleaved_rows(cache_mem_v),
                            x2_s, (w_co_b,) + ffn2_w, ts)
    conv_s = p_s.reshape(bs, ts, P_WIDTH)[:, ts - (CONV_WIDTH - 1):, OFF_RG_X:OFF_RG_Y]

    return (y_p, y_s.reshape(bs, ts, D_MODEL),
            mem_k.reshape(bp, N_MEM, XA_HEADS, XA_HEAD_DIM),
            mem_v.reshape(bp, N_MEM, XA_HEADS, XA_HEAD_DIM),
            conv_p, rgh_p[:, 0, :], s_p.reshape(bp, GLA_HEADS, GLA_DK, GLA_DV),
            conv_s, rgh_s, s_s.reshape(bs, GLA_HEADS, GLA_DK, GLA_DV))
```

```python
import functools

import jax
import jax.numpy as jnp
import numpy as np
from jax import lax
from jax.experimental import pallas as pl
from jax.experimental.pallas import tpu as pltpu

F32 = jnp.float32
BF16 = jnp.bfloat16

D_MODEL = 1024
D_FF = 2816
RG_WIDTH = 512
RG_HEADS = 8
RG_HEAD_DIM = 64
CONV_WIDTH = 4
RG_C = 8.0
GLA_WIDTH = 512
GLA_HEADS = 4
GLA_DV = 128
GLA_DK = 64
GLA_KEY_WIDTH = 256
GLA_GATE_RANK = 16
GLA_GATE_NORMALIZER = 16.0
GLA_CHUNK = 32
N_MEM = 256
XA_HEADS = 4
XA_HEAD_DIM = 256
EPS = 1e-6
MASKED_SCORE = -np.inf

OFF_RG_X = 0
OFF_RG_Y = 512
OFF_Q = 1024
OFF_K = 1280
OFF_V = 1536
OFF_G = 2048
OFF_A = 2560
D_IN = 2576

LANES = 128
SUBLANES = 8
A_PAD = LANES
P_WIDTH = OFF_A + A_PAD
PAIR_K = 2 * GLA_DK
PAIR_V = 2 * GLA_DV
VMEM_LIMIT = 58 * 1024 * 1024

ROW_TILE = 256
FFN_IN_TILES = 2
PROMPT_TILES = 2
FF_CHUNK = 256
FF_DOWN_GROUP = 4
FFN2_PLACEMENT = (1, 1, 1, 1, 1, 1, 1, 1, 1, 1, 1)
SAMPLE_ATTN_BATCH = 8


def _rms(x, g):
    return x * lax.rsqrt(jnp.mean(x * x, axis=-1, keepdims=True) + EPS) * g


def _mm(a, w):
    return jnp.dot(a.astype(BF16), w, preferred_element_type=F32)


def _mm_nt(a, b):
    return lax.dot_general(a.astype(BF16), b.astype(BF16), (((1,), (1,)), ((), ())),
                           preferred_element_type=F32)


def _mm_tn(a, b):
    return lax.dot_general(a.astype(BF16), b.astype(BF16), (((0,), (0,)), ((), ())),
                           preferred_element_type=F32)


def _silu(x):
    return x * jax.nn.sigmoid(x)


def _gelu_tanh(x):
    c = np.float32(np.sqrt(2.0 / np.pi))
    return x * (0.5 * (1.0 + jnp.tanh(c * (x + 0.044715 * (x * x * x)))))


def _softplus(x):
    return jnp.maximum(x, 0.0) + jnp.log1p(jnp.exp(-jnp.abs(x)))


def _ffn(x, norm_ref, wg_ref, wu_ref, wd_ref):
    h = _rms(x, norm_ref[...]).astype(BF16)
    g = jnp.dot(h, wg_ref[...], preferred_element_type=F32)
    u = jnp.dot(h, wu_ref[...], preferred_element_type=F32)
    return _mm(_silu(g) * u, wd_ref[...])


def _last_rows(buf_ref, x, seg):
    n = x.shape[0] // seg
    outs = []
    for j in range(x.shape[1] // LANES):
        buf_ref[j] = x[:, j * LANES:(j + 1) * LANES]
        outs.append(buf_ref[j, pl.ds(seg - 1, n, stride=seg), :])
    return jnp.concatenate(outs, axis=1)


def _ffn_pieces(x, norm_ref, wg_ref, wu_ref, wd_ref, out):
    h = _rms(x, norm_ref[...]).astype(BF16)
    acc = None
    acts = []
    n = D_FF // FF_CHUNK
    for c in range(n):
        cols = slice(c * FF_CHUNK, (c + 1) * FF_CHUNK)
        g = jnp.dot(h, wg_ref[:, cols], preferred_element_type=F32)
        u = jnp.dot(h, wu_ref[:, cols], preferred_element_type=F32)
        acts.append((_silu(g) * u).astype(BF16))
        if len(acts) == FF_DOWN_GROUP or c == n - 1:
            lo = (c + 1 - len(acts)) * FF_CHUNK
            part = jnp.dot(jnp.concatenate(acts, axis=1), wd_ref[lo:(c + 1) * FF_CHUNK, :],
                           preferred_element_type=F32)
            acc = part if acc is None else acc + part
            acts = []
        yield
    out.append(acc)


def _shift_rows(x, s):
    return pltpu.roll(x, s, axis=0)


def _row_in_segment(shape, seg):
    return lax.broadcasted_iota(jnp.int32, shape, 0) & (seg - 1)


def _segment_cumsum(x, seg):
    pos = _row_in_segment(x.shape, seg)
    s = 1
    while s < seg:
        x = jnp.where(pos >= s, x + _shift_rows(x, s), x)
        s *= 2
    return x


def _segment_affine_scan(a, u, seg):
    pos = _row_in_segment(a.shape, seg)
    s = 1
    while s < seg:
        m = pos >= s
        u = jnp.where(m, a * _shift_rows(u, s) + u, u)
        a = jnp.where(m, a * _shift_rows(a, s), a)
        s *= 2
    return a, u


def _conv(xb, sh1, sh2, sh3, cw_ref, cb_ref):
    y = cb_ref[...] + sh3 * cw_ref[0:1, :]
    y = y + sh2 * cw_ref[1:2, :]
    y = y + sh1 * cw_ref[2:3, :]
    return y + xb * cw_ref[3:4, :]


def _rg_gates(xc, wgate_ref, ba_ref, bx_ref):
    half = RG_WIDTH // 2
    r, i = [], []
    for c in range(2):
        z = _mm(xc[:, c * half:(c + 1) * half], wgate_ref[c])
        r.append(z[:, :half])
        i.append(z[:, half:])
    r = jax.nn.sigmoid(jnp.concatenate(r, axis=1) + ba_ref[...])
    i = jax.nn.sigmoid(jnp.concatenate(i, axis=1) + bx_ref[...])
    return r, i


def _rg_decay_input(xc, r, i, lam_ref):
    log_a = (-RG_C * _softplus(-lam_ref[...])) * r
    a = jnp.exp(log_a)
    mult = jnp.sqrt(-jnp.tanh(log_a) * (a * a + 1.0))
    return a, mult * (i * xc)


def _gla_log_decay(a_low, wa2_ref, ba2_ref):
    z = _mm(a_low, wa2_ref[...]) + ba2_ref[...]
    return (-_softplus(-z)) / GLA_GATE_NORMALIZER


def _gla_intra(qt, kt, v, chunk):
    t = qt.shape[0]
    shift = int(np.log2(chunk))
    ri = lax.broadcasted_iota(jnp.int32, (t, t), 0)
    ci = lax.broadcasted_iota(jnp.int32, (t, t), 1)
    causal = (ri >= ci) & ((ri >> shift) == (ci >> shift))
    lane = lax.broadcasted_iota(jnp.int32, (t, PAIR_K), 1)
    outs = []
    for h in range(GLA_HEADS):
        p = h // 2
        qp = qt[:, p * PAIR_K:(p + 1) * PAIR_K]
        kp = kt[:, p * PAIR_K:(p + 1) * PAIR_K]
        mine = (lane >= GLA_DK) if (h % 2) else (lane < GLA_DK)
        s = _mm_nt(jnp.where(mine, qp, 0.0), kp)
        attn = jnp.where(causal, s, 0.0)
        outs.append(_mm(attn, v[:, h * GLA_DV:(h + 1) * GLA_DV].astype(BF16)))
    return outs


def _pair_blockdiag_mask():
    r = lax.broadcasted_iota(jnp.int32, (PAIR_K, PAIR_V), 0)
    c = lax.broadcasted_iota(jnp.int32, (PAIR_K, PAIR_V), 1)
    return (r < GLA_DK) == (c < GLA_DV)


def _gla_chunk_updates(k, v, b, b_last, chunk):
    bd_mask = _pair_blockdiag_mask()
    kv = {}
    for c in range(k.shape[0] // chunk):
        rows = slice(c * chunk, (c + 1) * chunk)
        for p in range(2):
            kl = slice(p * PAIR_K, (p + 1) * PAIR_K)
            kd = k[rows, kl] * jnp.exp(b_last[c:c + 1, kl] - b[rows, kl])
            kv[c, p] = jnp.where(bd_mask, _mm_tn(kd, v[rows, p * PAIR_V:(p + 1) * PAIR_V]), 0.0)
    return kv


def _gla_combine(intra, qt, states, chunk):
    n_chunks = qt.shape[0] // chunk
    heads = []
    for p in range(2):
        kl = slice(p * PAIR_K, (p + 1) * PAIR_K)
        inter = jnp.concatenate(
            [_mm(qt[c * chunk:(c + 1) * chunk, kl], states[c, p]) for c in range(n_chunks)],
            axis=0)
        heads.append(intra[2 * p] + inter[:, :GLA_DV])
        heads.append(intra[2 * p + 1] + inter[:, GLA_DV:])
    return heads


def _gla_out(o_heads, g, norm_ref):
    outs = []
    for h in range(GLA_HEADS):
        outs.append(_rms(o_heads[h], norm_ref[:, h * GLA_DV:(h + 1) * GLA_DV]))
    return jnp.concatenate(outs, axis=1) * _silu(g)


def _softmax_rows(s):
    m = jnp.max(s, axis=-1, keepdims=True)
    e = jnp.exp(s - m)
    return e * (1.0 / jnp.sum(e, axis=-1, keepdims=True))


def _split_p(p_ref, rows=slice(None)):
    xb = p_ref[rows, OFF_RG_X:OFF_RG_Y]
    yb = p_ref[rows, OFF_RG_Y:OFF_Q]
    q = p_ref[rows, OFF_Q:OFF_K]
    k = p_ref[rows, OFF_K:OFF_V]
    v = p_ref[rows, OFF_V:OFF_G]
    g = p_ref[rows, OFF_G:OFF_A]
    a_low = p_ref[rows, OFF_A:P_WIDTH]
    return xb, yb, q, k, v, g, a_low


def _gla_prepare(q, k, log_a, chunk):
    b = _segment_cumsum(log_a, chunk)
    qt = (q * (GLA_DK ** -0.5)) * jnp.exp(b)
    kt = k * jnp.exp(-b)
    return b, qt, kt


def _mix_project(rg_out, gla_out, wout_ref):
    return (_mm(rg_out, wout_ref[0:RG_WIDTH, :]) + _mm(gla_out, wout_ref[RG_WIDTH:, :]))


def _ffn_in_kernel(x_ref, n1_ref, wg_ref, wu_ref, wd_ref, n2_ref, win_ref, x1_ref, p_ref):
    for sub in range(x_ref.shape[0] // ROW_TILE):
        rows = slice(sub * ROW_TILE, (sub + 1) * ROW_TILE)
        x = x_ref[rows, :]
        x1 = x + 0.5 * _ffn(x, n1_ref, wg_ref, wu_ref, wd_ref)
        x1_ref[rows, :] = x1
        p_ref[rows, :] = _mm(_rms(x1, n2_ref[...]), win_ref[...])


def _mem_kv_kernel(m_ref, n_ref, wk_ref, wv_ref, k_ref, v_ref, kb_ref, vb_ref):
    h = _rms(m_ref[...], n_ref[...]).astype(BF16)
    k = jnp.dot(h, wk_ref[...], preferred_element_type=F32)
    v = jnp.dot(h, wv_ref[...], preferred_element_type=F32)
    for h in range(XA_HEADS):
        sl = slice(h * XA_HEAD_DIM, (h + 1) * XA_HEAD_DIM)
        k_ref[:, h, :] = k[:, sl]
        v_ref[:, h, :] = v[:, sl]
    kb_ref[...] = k.astype(BF16)
    vb_ref[...] = v.astype(BF16)


def _prompt_tail_kernel(
        x1_ref, p_ref, kb_ref, vb_ref,
        cw_ref, cb_ref, wgate_ref, ba_ref, bx_ref, lam_ref, rgn_ref,
        wa2_ref, ba2_ref, glan_ref, wout_ref,
        xn_ref, wcq_ref, wco_ref, n2_ref, wg_ref, wu_ref, wd_ref, fn_ref,
        y_ref, rgh_ref, s_ref,
        xpad_ref, h_ref, sbd_ref, b_ref, x3_ref, *, steps_per_seq, n_steps):
    step = pl.program_id(0)
    t_idx = jnp.minimum(step, n_steps - 1) % steps_per_seq
    slot = step % 2
    tt = ROW_TILE
    subs = x1_ref.shape[0] // tt
    n_chunks = tt // GLA_CHUNK

    @pl.when(step == 0)
    def _():
        x3_ref[1] = jnp.zeros((subs, tt, D_MODEL), F32)

    @pl.when(t_idx == 0)
    def _():
        xpad_ref[0:SUBLANES, :] = jnp.zeros((SUBLANES, RG_WIDTH), F32)
        h_ref[...] = jnp.zeros_like(h_ref)
        sbd_ref[...] = jnp.zeros_like(sbd_ref)

    for sub in range(subs):
        rows = slice(sub * tt, (sub + 1) * tt)
        x3_prev = x3_ref[1 - slot, sub]
        ffn_out = []
        ffn = _ffn_pieces(x3_prev, n2_ref, wg_ref, wu_ref, wd_ref, ffn_out)
        placement = iter(FFN2_PLACEMENT)

        def emit_ffn():
            for _ in range(next(placement)):
                next(ffn)

        emit_ffn()

        xb, yb, q, k, v, g, a_low = _split_p(p_ref, rows)

        xpad_ref[SUBLANES:SUBLANES + tt, :] = xb
        sh1 = xpad_ref[SUBLANES - 1:SUBLANES - 1 + tt, :]
        sh2 = xpad_ref[SUBLANES - 2:SUBLANES - 2 + tt, :]
        sh3 = xpad_ref[SUBLANES - 3:SUBLANES - 3 + tt, :]
        xc = _conv(xb, sh1, sh2, sh3, cw_ref, cb_ref)
        xpad_ref[0:SUBLANES, :] = xb[tt - SUBLANES:tt, :]
        emit_ffn()
        r, i = _rg_gates(xc, wgate_ref, ba_ref, bx_ref)
        a, u = _rg_decay_input(xc, r, i, lam_ref)
        emit_ffn()
        a_grp, h_grp = _segment_affine_scan(a, u, SUBLANES)
        emit_ffn()
        carry = h_ref[0:1, :]
        groups = []
        for gi in range(tt // SUBLANES):
            grp = slice(gi * SUBLANES, (gi + 1) * SUBLANES)
            groups.append(h_grp[grp, :] + a_grp[grp, :] * carry)
            carry = groups[-1][SUBLANES - 1:SUBLANES, :]
        hs = jnp.concatenate(groups, axis=0)
        h_ref[...] = jnp.broadcast_to(carry, h_ref.shape)
        rg_out = _rms(hs * _gelu_tanh(yb), rgn_ref[...]).astype(BF16)
        emit_ffn()

        log_a = _gla_log_decay(a_low, wa2_ref, ba2_ref)
        b, qt, kt = _gla_prepare(q, k, log_a, GLA_CHUNK)
        emit_ffn()
        b_last = _last_rows(b_ref, b, GLA_CHUNK)
        dec_t = jnp.exp(b_last).T
        kv = _gla_chunk_updates(k, v, b, b_last, GLA_CHUNK)
        states = {}
        for p in range(2):
            s_bd = sbd_ref[p]
            for c in range(n_chunks):
                states[c, p] = s_bd.astype(BF16)
                s_bd = s_bd * dec_t[p * PAIR_K:(p + 1) * PAIR_K, c:c + 1] + kv[c, p]
            sbd_ref[p] = s_bd
        emit_ffn()
        intra = _gla_intra(qt, kt, v, GLA_CHUNK)
        emit_ffn()
        gla_out = _gla_out(_gla_combine(intra, qt, states, GLA_CHUNK), g, glan_ref).astype(BF16)
        emit_ffn()

        x2 = x1_ref[rows, :] + _mix_project(rg_out, gla_out, wout_ref)

        qx = _mm(_rms(x2, xn_ref[...]), wcq_ref[...])
        emit_ffn()
        head = lambda h: slice(h * XA_HEAD_DIM, (h + 1) * XA_HEAD_DIM)
        s = jnp.concatenate([_mm_nt(qx[:, head(h)], kb_ref[:, head(h)]) for h in range(XA_HEADS)], axis=0)
        pr = _softmax_rows(s * (XA_HEAD_DIM ** -0.5))
        emit_ffn()
        o = jnp.concatenate([_mm(pr[h * tt:(h + 1) * tt, :], vb_ref[:, head(h)]) for h in range(XA_HEADS)], axis=1)
        x3_ref[slot, sub] = x2 + _mm(o, wco_ref[...])
        for _ in ffn:
            pass
        y_ref[rows, :] = _rms(x3_prev + 0.5 * ffn_out[0], fn_ref[...])

    @pl.when((t_idx == steps_per_seq - 1) & (step < n_steps))
    def _():
        rgh_ref[...] = h_ref[...]
        row = lax.broadcasted_iota(jnp.int32, (PAIR_K, GLA_DV), 0)
        for p in range(2):
            s_bd = sbd_ref[p]
            s_ref[p * PAIR_K:(p + 1) * PAIR_K, :] = jnp.where(
                row < GLA_DK, s_bd[:, :GLA_DV], s_bd[:, GLA_DV:])


def _sample_mix_kernel(
        x1_ref, p_ref, econv_ref, eh_ref, s0_ref,
        cw_ref, cb_ref, wgate_ref, ba_ref, bx_ref, lam_ref, rgn_ref,
        wa2_ref, ba2_ref, glan_ref, wout_ref, xn_ref, wcq_ref,
        x2_ref, q_ref, rgh_ref, s_ref,
        hs_ref, b_ref, *, seq):
    rows_n = x1_ref.shape[0]
    nb = rows_n // seq

    xb, yb, q, k, v, g, a_low = _split_p(p_ref)

    pos = _row_in_segment(xb.shape, seq)
    econv = econv_ref[...]
    sh = []
    for j in range(1, CONV_WIDTH):
        sh.append(jnp.where(pos >= j, _shift_rows(xb, j), _shift_rows(econv, rows_n - seq + j)))
    xc = _conv(xb, sh[0], sh[1], sh[2], cw_ref, cb_ref)
    r, i = _rg_gates(xc, wgate_ref, ba_ref, bx_ref)
    a, u = _rg_decay_input(xc, r, i, lam_ref)
    u = u + a * eh_ref[...]
    _, hs = _segment_affine_scan(a, u, seq)
    rgh_ref[...] = _last_rows(hs_ref, hs, seq)
    rg_out = _rms(hs * _gelu_tanh(yb), rgn_ref[...]).astype(BF16)

    log_a = _gla_log_decay(a_low, wa2_ref, ba2_ref)
    b, qt, kt = _gla_prepare(q, k, log_a, seq)
    intra = _gla_intra(qt, kt, v, seq)
    b_last = _last_rows(b_ref, b, seq)
    dec_t = jnp.exp(b_last).T
    kv = _gla_chunk_updates(k, v, b, b_last, seq)
    row = lax.broadcasted_iota(jnp.int32, (PAIR_K, GLA_DV), 0)
    top = row < GLA_DK
    states = {}
    for c in range(nb):
        for p in range(2):
            kl = slice(p * PAIR_K, (p + 1) * PAIR_K)
            s_pair = s0_ref[c, kl, :]
            s_bd = jnp.concatenate([jnp.where(top, s_pair, 0.0), jnp.where(top, 0.0, s_pair)], axis=1)
            states[c, p] = s_bd.astype(BF16)
            s_new = s_bd * dec_t[kl, c:c + 1] + kv[c, p]
            s_ref[c, kl, :] = jnp.where(top, s_new[:, :GLA_DV], s_new[:, GLA_DV:])
    gla_out = _gla_out(_gla_combine(intra, qt, states, seq), g, glan_ref).astype(BF16)

    x2 = x1_ref[...] + _mix_project(rg_out, gla_out, wout_ref)
    x2_ref[...] = x2
    q_ref[...] = _mm(_rms(x2, xn_ref[...]), wcq_ref[...])


def _sample_attn_tail_kernel(q_ref, k_ref, v_ref, x2_ref, wco_ref, n2_ref, wg_ref, wu_ref, wd_ref, fn_ref,
                             y_ref, o_ref, *, seq):
    tiles = XA_HEAD_DIM // LANES
    group = tiles * XA_HEADS
    r = N_MEM * group
    nb = k_ref.shape[0] // r
    hs = XA_HEADS * seq
    lane = lax.broadcasted_iota(jnp.int32, (nb * hs, r), 1)
    head = (lax.broadcasted_iota(jnp.int32, (nb * hs, r), 0) // seq) & (XA_HEADS - 1)
    own = (lane & (group - 1)) == head
    s = []
    for j in range(nb):
        q = q_ref[j * seq:(j + 1) * seq, :]
        q_parts = jnp.concatenate(
            [q[:, c * LANES:(c + 1) * LANES] for c in range(XA_HEADS * tiles)], axis=0)
        part = _mm_nt(q_parts, k_ref[j * r:(j + 1) * r, :])
        for h in range(XA_HEADS):
            acc = part[h * tiles * seq:(h * tiles + 1) * seq, :]
            for c in range(1, tiles):
                blk = part[(h * tiles + c) * seq:(h * tiles + c + 1) * seq, :]
                acc = acc + pltpu.roll(blk, r - c * XA_HEADS, axis=1)
            s.append(acc)
    p = _softmax_rows(jnp.where(own, jnp.concatenate(s, axis=0) * (XA_HEAD_DIM ** -0.5), MASKED_SCORE))
    for j in range(nb):
        pj = p[j * hs:(j + 1) * hs, :]
        p_all = jnp.concatenate(
            [pj] + [pltpu.roll(pj, c * XA_HEADS, axis=1) for c in range(1, tiles)], axis=0)
        o = _mm(p_all, v_ref[j * r:(j + 1) * r, :].astype(BF16))
        for c in range(tiles):
            for h in range(XA_HEADS):
                col = h * XA_HEAD_DIM + c * LANES
                o_ref[j * seq:(j + 1) * seq, col:col + LANES] = (
                    o[(c * XA_HEADS + h) * seq:(c * XA_HEADS + h + 1) * seq, :])
    x3 = x2_ref[...] + _mm(o_ref[...], wco_ref[...])
    x4 = x3 + 0.5 * _ffn(x3, n2_ref, wg_ref, wu_ref, wd_ref)
    y_ref[...] = _rms(x4, fn_ref[...])


def _resident(arr):
    nd = arr.ndim
    return pl.BlockSpec(arr.shape, lambda *_: (0,) * nd, pipeline_mode=pl.Buffered(1))


def _params(sem, flags=None):
    return pltpu.CompilerParams(dimension_semantics=sem, vmem_limit_bytes=VMEM_LIMIT, flags=flags)


def _ffn_in(x, weights):
    rows = x.shape[0]
    block = FFN_IN_TILES * ROW_TILE
    row_spec = lambda w: pl.BlockSpec((block, w), lambda i: (i, 0))
    return pl.pallas_call(
        _ffn_in_kernel,
        grid=(rows // block,),
        in_specs=[row_spec(D_MODEL)] + [_resident(w) for w in weights],
        out_specs=[row_spec(D_MODEL), row_spec(P_WIDTH)],
        out_shape=[jax.ShapeDtypeStruct((rows, D_MODEL), F32),
                   jax.ShapeDtypeStruct((rows, P_WIDTH), F32)],
        compiler_params=_params(("parallel",)),
        name="ffn_in",
    )(x, *weights)


def _mem_kv(mem, weights):
    rows = mem.shape[0]
    row_spec = pl.BlockSpec((ROW_TILE, D_MODEL), lambda i: (i, 0))
    head_spec = pl.BlockSpec((ROW_TILE, XA_HEADS, XA_HEAD_DIM), lambda i: (i, 0, 0))
    return pl.pallas_call(
        _mem_kv_kernel,
        grid=(rows // ROW_TILE,),
        in_specs=[row_spec] + [_resident(w) for w in weights],
        out_specs=[head_spec] * 2 + [row_spec] * 2,
        out_shape=[jax.ShapeDtypeStruct((rows, XA_HEADS, XA_HEAD_DIM), F32)] * 2
                  + [jax.ShapeDtypeStruct((rows, D_MODEL), BF16)] * 2,
        compiler_params=_params(("parallel",)),
        name="mem_kv",
    )(mem, *weights)


def _prompt_tail(x1, p, kb, vb, weights):
    batch, seq, _ = x1.shape
    tt = ROW_TILE
    block = PROMPT_TILES * tt
    steps_per_seq = seq // block
    n_steps = batch * steps_per_seq
    cur = lambda s: jnp.minimum(s, n_steps - 1)
    prev = lambda s: jnp.maximum(s - 1, 0)
    tok = lambda w: pl.BlockSpec(
        (None, block, w), lambda s: (cur(s) // steps_per_seq, cur(s) % steps_per_seq, 0))
    per_seq = lambda r, w: pl.BlockSpec((None, r, w), lambda s: (cur(s) // steps_per_seq, 0, 0))
    out_tok = pl.BlockSpec(
        (None, block, D_MODEL), lambda s: (prev(s) // steps_per_seq, prev(s) % steps_per_seq, 0))
    return pl.pallas_call(
        functools.partial(_prompt_tail_kernel, steps_per_seq=steps_per_seq, n_steps=n_steps),
        grid=(n_steps + 1,),
        in_specs=[tok(D_MODEL), tok(P_WIDTH), per_seq(N_MEM, D_MODEL), per_seq(N_MEM, D_MODEL)]
                 + [_resident(w) for w in weights],
        out_specs=[out_tok, per_seq(SUBLANES, RG_WIDTH), per_seq(GLA_KEY_WIDTH, GLA_DV)],
        out_shape=[jax.ShapeDtypeStruct((batch, seq, D_MODEL), F32),
                   jax.ShapeDtypeStruct((batch, SUBLANES, RG_WIDTH), F32),
                   jax.ShapeDtypeStruct((batch, GLA_KEY_WIDTH, GLA_DV), F32)],
        scratch_shapes=[pltpu.VMEM((SUBLANES + tt, RG_WIDTH), F32),
                        pltpu.VMEM((SUBLANES, RG_WIDTH), F32),
                        pltpu.VMEM((2, PAIR_K, PAIR_V), F32),
                        pltpu.VMEM((GLA_KEY_WIDTH // LANES, tt, LANES), F32),
                        pltpu.VMEM((2, PROMPT_TILES, tt, D_MODEL), F32)],
        compiler_params=_params(("arbitrary",)),
        name="prompt_tail",
    )(x1, p, kb, vb, *weights)


def _sample_mix(x1, p, econv, eh, s0, weights, seq):
    rows = x1.shape[0]
    nb = ROW_TILE // seq
    row_spec = lambda w: pl.BlockSpec((ROW_TILE, w), lambda i: (i, 0))
    state_spec = pl.BlockSpec((nb, GLA_KEY_WIDTH, GLA_DV), lambda i: (i, 0, 0))
    return pl.pallas_call(
        functools.partial(_sample_mix_kernel, seq=seq),
        grid=(rows // ROW_TILE,),
        in_specs=[row_spec(D_MODEL), row_spec(P_WIDTH), row_spec(RG_WIDTH), row_spec(RG_WIDTH),
                  state_spec] + [_resident(w) for w in weights],
        out_specs=[row_spec(D_MODEL), row_spec(D_MODEL),
                   pl.BlockSpec((nb, RG_WIDTH), lambda i: (i, 0)), state_spec],
        out_shape=[jax.ShapeDtypeStruct((rows, D_MODEL), F32),
                   jax.ShapeDtypeStruct((rows, D_MODEL), F32),
                   jax.ShapeDtypeStruct((rows // seq, RG_WIDTH), F32),
                   jax.ShapeDtypeStruct((rows // seq, GLA_KEY_WIDTH, GLA_DV), F32)],
        scratch_shapes=[pltpu.VMEM((RG_WIDTH // LANES, ROW_TILE, LANES), F32),
                        pltpu.VMEM((GLA_KEY_WIDTH // LANES, ROW_TILE, LANES), F32)],
        compiler_params=_params(("parallel",)),
        name="sample_mix",
    )(x1, p, econv, eh, s0, *weights)


def _head_interleaved_rows(x):
    b, m, h, dh = x.shape
    tiles = dh // LANES
    return (x.reshape(b, m, h, tiles, LANES).transpose(0, 1, 3, 2, 4)
            .reshape(b * m * tiles * h, LANES))


def _sample_attn_tail(q, k, v, x2, weights, seq):
    rows = q.shape[0]
    nb = SAMPLE_ATTN_BATCH
    row_spec = pl.BlockSpec((nb * seq, D_MODEL), lambda i: (i, 0))
    kv_rows = k.shape[0] // (rows // seq)
    kv_spec = pl.BlockSpec((nb * kv_rows, LANES), lambda i: (i, 0))
    return pl.pallas_call(
        functools.partial(_sample_attn_tail_kernel, seq=seq),
        grid=(rows // (nb * seq),),
        in_specs=[row_spec, kv_spec, kv_spec, row_spec] + [_resident(w) for w in weights],
        out_specs=row_spec,
        out_shape=jax.ShapeDtypeStruct((rows, D_MODEL), F32),
        scratch_shapes=[pltpu.VMEM((nb * seq, D_MODEL), F32)],
        compiler_params=_params(("parallel",)),
        name="sample_attn_tail",
    )(q, k, v, x2, *weights)


def _block_diag_heads(w):
    h, n, _ = w.shape
    eye = jnp.eye(h, dtype=w.dtype)
    return (eye[:, None, :, None] * w[:, :, None, :]).reshape(h * n, h * n)


def kernel(x_prompt, x_sample, cache_mem_k, cache_mem_v, state_conv, state_rglru, state_gla, mem_prompt, ffn1_norm, ffn1_w_gate, ffn1_w_up, ffn1_w_down, mix_norm, w_in, conv_w, conv_b, rg_w_a, rg_b_a, rg_w_x, rg_b_x, rg_lambda, rg_out_norm, gla_w_a2, gla_b_a2, gla_out_norm, w_out, xattn_norm, mem_norm, w_cq, w_ck, w_cv, w_co, ffn2_norm, ffn2_w_gate, ffn2_w_up, ffn2_w_down, final_norm):
    bp, tp, _ = x_prompt.shape
    bs, ts, _ = x_sample.shape
    assert ts == SUBLANES, "each sample sequence must fill exactly one 8-row sublane group"
    assert tp % (PROMPT_TILES * ROW_TILE) == 0 and (bs * ts) % (FFN_IN_TILES * ROW_TILE) == 0
    assert bs % SAMPLE_ATTN_BATCH == 0 and (bp * N_MEM) % ROW_TILE == 0
    assert cache_mem_k.shape[1:] == (N_MEM, XA_HEADS, XA_HEAD_DIM) and mem_prompt.shape[1] == N_MEM
    row = lambda g: g.reshape(1, -1)
    bf = lambda w: w.astype(BF16)

    w_in_p = bf(jnp.pad(w_in, ((0, 0), (0, P_WIDTH - D_IN))))
    hp = RG_HEADS // 2
    w_gate_rg = bf(jnp.stack([
        jnp.concatenate([_block_diag_heads(rg_w_a[c * hp:(c + 1) * hp]),
                         _block_diag_heads(rg_w_x[c * hp:(c + 1) * hp])], axis=1)
        for c in range(2)]))
    w_a2_p = bf(jnp.pad(gla_w_a2, ((0, A_PAD - GLA_GATE_RANK), (0, 0))))
    ffn_in_w = (row(ffn1_norm), bf(ffn1_w_gate), bf(ffn1_w_up), bf(ffn1_w_down), row(mix_norm), w_in_p)
    mixer_w = (conv_w, row(conv_b), w_gate_rg, row(rg_b_a), row(rg_b_x), row(rg_lambda),
               row(rg_out_norm), w_a2_p, row(gla_b_a2), row(jnp.tile(gla_out_norm, GLA_HEADS)),
               bf(w_out))
    w_cq_b, w_co_b = bf(w_cq), bf(w_co)
    ffn2_w = (row(ffn2_norm), bf(ffn2_w_gate), bf(ffn2_w_up), bf(ffn2_w_down), row(final_norm))

    mem_k, mem_v, mem_kb, mem_vb = _mem_kv(mem_prompt.reshape(bp * N_MEM, D_MODEL),
                                           (row(mem_norm), bf(w_ck), bf(w_cv)))
    x1_p, p_p = _ffn_in(x_prompt.reshape(bp * tp, D_MODEL), ffn_in_w)
    p_p = p_p.reshape(bp, tp, P_WIDTH)
    y_p, rgh_p, s_p = _prompt_tail(
        x1_p.reshape(bp, tp, D_MODEL), p_p,
        mem_kb.reshape(bp, N_MEM, D_MODEL), mem_vb.reshape(bp, N_MEM, D_MODEL),
        mixer_w + (row(xattn_norm), w_cq_b, w_co_b) + ffn2_w)
    conv_p = p_p[:, tp - (CONV_WIDTH - 1):, OFF_RG_X:OFF_RG_Y]

    x1_s, p_s = _ffn_in(x_sample.reshape(bs * ts, D_MODEL), ffn_in_w)
    econv = jnp.pad(state_conv, ((0, 0), (ts - (CONV_WIDTH - 1), 0), (0, 0))).reshape(bs * ts, RG_WIDTH)
    eh = jnp.pad(state_rglru[:, None, :], ((0, 0), (0, ts - 1), (0, 0))).reshape(bs * ts, RG_WIDTH)
    x2_s, q_s, rgh_s, s_s = _sample_mix(
        x1_s, p_s, econv, eh, state_gla.reshape(bs, GLA_KEY_WIDTH, GLA_DV),
        mixer_w + (row(xattn_norm), w_cq_b), ts)
    y_s = _sample_attn_tail(q_s, _head_interleaved_rows(cache_mem_k), _head_interleaved_rows(cache_mem_v),
                            x2_s, (w_co_b,) + ffn2_w, ts)
    conv_s = p_s.reshape(bs, ts, P_WIDTH)[:, ts - (CONV_WIDTH - 1):, OFF_RG_X:OFF_RG_Y]

    return (y_p, y_s.reshape(bs, ts, D_MODEL),
            mem_k.reshape(bp, N_MEM, XA_HEADS, XA_HEAD_DIM),
            mem_v.reshape(bp, N_MEM, XA_HEADS, XA_HEAD_DIM),
            conv_p, rgh_p[:, 0, :], s_p.reshape(bp, GLA_HEADS, GLA_DK, GLA_DV),
            conv_s, rgh_s, s_s.reshape(bs, GLA_HEADS, GLA_DK, GLA_DV))
```

```python
import functools

import jax
import jax.numpy as jnp
import numpy as np
from jax import lax
from jax.experimental import pallas as pl
from jax.experimental.pallas import tpu as pltpu

F32 = jnp.float32
BF16 = jnp.bfloat16

D_MODEL = 1024
D_FF = 2816
RG_WIDTH = 512
RG_HEADS = 8
RG_HEAD_DIM = 64
CONV_WIDTH = 4
RG_C = 8.0
GLA_WIDTH = 512
GLA_HEADS = 4
GLA_DV = 128
GLA_DK = 64
GLA_KEY_WIDTH = 256
GLA_GATE_RANK = 16
GLA_GATE_NORMALIZER = 16.0
GLA_CHUNK = 32
N_MEM = 256
XA_HEADS = 4
XA_HEAD_DIM = 256
EPS = 1e-6
MASKED_SCORE = -np.inf

OFF_RG_X = 0
OFF_RG_Y = 512
OFF_Q = 1024
OFF_K = 1280
OFF_V = 1536
OFF_G = 2048
OFF_A = 2560
D_IN = 2576

LANES = 128
SUBLANES = 8
BF16_SUBLANES = 16
A_PAD = LANES
P_WIDTH = OFF_A + A_PAD
PAIR_K = 2 * GLA_DK
PAIR_V = 2 * GLA_DV
VMEM_LIMIT = 58 * 1024 * 1024

ROW_TILE = 256
FFN_IN_TILES = 2
PROMPT_TILES = 2
FF_CHUNK = 256
FF_DOWN_GROUP = 4
FFN2_PLACEMENT = (1, 1, 1, 1, 1, 1, 1, 1, 1, 1, 1)
SAMPLE_ATTN_BATCH = 8


def _rms(x, g):
    return x * lax.rsqrt(jnp.mean(x * x, axis=-1, keepdims=True) + EPS) * g


def _mm(a, w):
    return jnp.dot(a.astype(BF16), w, preferred_element_type=F32)


def _mm_nt(a, b):
    return lax.dot_general(a.astype(BF16), b.astype(BF16), (((1,), (1,)), ((), ())),
                           preferred_element_type=F32)


def _mm_tn(a, b):
    return lax.dot_general(a.astype(BF16), b.astype(BF16), (((0,), (0,)), ((), ())),
                           preferred_element_type=F32)


def _silu(x):
    return x * jax.nn.sigmoid(x)


def _gelu_tanh(x):
    c = np.float32(np.sqrt(2.0 / np.pi))
    return x * (0.5 * (1.0 + jnp.tanh(c * (x + 0.044715 * (x * x * x)))))


def _softplus(x):
    return jnp.maximum(x, 0.0) + jnp.log1p(jnp.exp(-jnp.abs(x)))


def _ffn(x, norm_ref, wg_ref, wu_ref, wd_ref):
    h = _rms(x, norm_ref[...]).astype(BF16)
    g = jnp.dot(h, wg_ref[...], preferred_element_type=F32)
    u = jnp.dot(h, wu_ref[...], preferred_element_type=F32)
    return _mm(_silu(g) * u, wd_ref[...])


def _last_rows(buf_ref, x, seg):
    n = x.shape[0] // seg
    outs = []
    for j in range(x.shape[1] // LANES):
        buf_ref[j] = x[:, j * LANES:(j + 1) * LANES]
        outs.append(buf_ref[j, pl.ds(seg - 1, n, stride=seg), :])
    return jnp.concatenate(outs, axis=1)


def _ffn_pieces(x, norm_ref, wg_ref, wu_ref, wd_ref, out):
    h = _rms(x, norm_ref[...]).astype(BF16)
    acc = None
    acts = []
    n = D_FF // FF_CHUNK
    for c in range(n):
        cols = slice(c * FF_CHUNK, (c + 1) * FF_CHUNK)
        g = jnp.dot(h, wg_ref[:, cols], preferred_element_type=F32)
        u = jnp.dot(h, wu_ref[:, cols], preferred_element_type=F32)
        acts.append((_silu(g) * u).astype(BF16))
        if len(acts) == FF_DOWN_GROUP or c == n - 1:
            lo = (c + 1 - len(acts)) * FF_CHUNK
            part = jnp.dot(jnp.concatenate(acts, axis=1), wd_ref[lo:(c + 1) * FF_CHUNK, :],
                           preferred_element_type=F32)
            acc = part if acc is None else acc + part
            acts = []
        yield
    out.append(acc)


def _shift_rows(x, s):
    return pltpu.roll(x, s, axis=0)


def _row_in_segment(shape, seg):
    return lax.broadcasted_iota(jnp.int32, shape, 0) & (seg - 1)


def _segment_cumsum(x, seg):
    pos = _row_in_segment(x.shape, seg)
    s = 1
    while s < seg:
        x = jnp.where(pos >= s, x + _shift_rows(x, s), x)
        s *= 2
    return x


def _segment_affine_scan(a, u, seg):
    pos = _row_in_segment(a.shape, seg)
    s = 1
    while s < seg:
        m = pos >= s
        u = jnp.where(m, a * _shift_rows(u, s) + u, u)
        a = jnp.where(m, a * _shift_rows(a, s), a)
        s *= 2
    return a, u


def _conv(xb, sh1, sh2, sh3, cw_ref, cb_ref):
    y = cb_ref[...] + sh3 * cw_ref[0:1, :]
    y = y + sh2 * cw_ref[1:2, :]
    y = y + sh1 * cw_ref[2:3, :]
    return y + xb * cw_ref[3:4, :]


def _rg_gates(xc, wgate_ref, ba_ref, bx_ref):
    half = RG_WIDTH // 2
    r, i = [], []
    for c in range(2):
        z = _mm(xc[:, c * half:(c + 1) * half], wgate_ref[c])
        r.append(z[:, :half])
        i.append(z[:, half:])
    r = jax.nn.sigmoid(jnp.concatenate(r, axis=1) + ba_ref[...])
    i = jax.nn.sigmoid(jnp.concatenate(i, axis=1) + bx_ref[...])
    return r, i


def _rg_decay_input(xc, r, i, lam_ref):
    log_a = (-RG_C * _softplus(-lam_ref[...])) * r
    a = jnp.exp(log_a)
    mult = jnp.sqrt(-jnp.tanh(log_a) * (a * a + 1.0))
    return a, mult * (i * xc)


def _gla_log_decay(a_low, wa2_ref, ba2_ref):
    z = _mm(a_low, wa2_ref[...]) + ba2_ref[...]
    return (-_softplus(-z)) / GLA_GATE_NORMALIZER


def _gla_intra(qt, kt, v, chunk):
    t = qt.shape[0]
    shift = int(np.log2(chunk))
    ri = lax.broadcasted_iota(jnp.int32, (t, t), 0)
    ci = lax.broadcasted_iota(jnp.int32, (t, t), 1)
    causal = (ri >= ci) & ((ri >> shift) == (ci >> shift))
    lane = lax.broadcasted_iota(jnp.int32, (t, PAIR_K), 1)
    outs = []
    for h in range(GLA_HEADS):
        p = h // 2
        qp = qt[:, p * PAIR_K:(p + 1) * PAIR_K]
        kp = kt[:, p * PAIR_K:(p + 1) * PAIR_K]
        mine = (lane >= GLA_DK) if (h % 2) else (lane < GLA_DK)
        s = _mm_nt(jnp.where(mine, qp, 0.0), kp)
        attn = jnp.where(causal, s, 0.0)
        outs.append(_mm(attn, v[:, h * GLA_DV:(h + 1) * GLA_DV].astype(BF16)))
    return outs


def _pair_blockdiag_mask():
    r = lax.broadcasted_iota(jnp.int32, (PAIR_K, PAIR_V), 0)
    c = lax.broadcasted_iota(jnp.int32, (PAIR_K, PAIR_V), 1)
    return (r < GLA_DK) == (c < GLA_DV)


def _gla_chunk_updates(k, v, b, b_last, chunk):
    bd_mask = _pair_blockdiag_mask()
    kv = {}
    for c in range(k.shape[0] // chunk):
        rows = slice(c * chunk, (c + 1) * chunk)
        for p in range(2):
            kl = slice(p * PAIR_K, (p + 1) * PAIR_K)
            kd = k[rows, kl] * jnp.exp(b_last[c:c + 1, kl] - b[rows, kl])
            kv[c, p] = jnp.where(bd_mask, _mm_tn(kd, v[rows, p * PAIR_V:(p + 1) * PAIR_V]), 0.0)
    return kv


def _gla_combine(intra, qt, states, chunk):
    n_chunks = qt.shape[0] // chunk
    heads = []
    for p in range(2):
        kl = slice(p * PAIR_K, (p + 1) * PAIR_K)
        inter = jnp.concatenate(
            [_mm(qt[c * chunk:(c + 1) * chunk, kl], states[c, p]) for c in range(n_chunks)],
            axis=0)
        heads.append(intra[2 * p] + inter[:, :GLA_DV])
        heads.append(intra[2 * p + 1] + inter[:, GLA_DV:])
    return heads


def _gla_out(o_heads, g, norm_ref):
    outs = []
    for h in range(GLA_HEADS):
        outs.append(_rms(o_heads[h], norm_ref[:, h * GLA_DV:(h + 1) * GLA_DV]))
    return jnp.concatenate(outs, axis=1) * _silu(g)


def _softmax_rows(s):
    m = jnp.max(s, axis=-1, keepdims=True)
    e = jnp.exp(s - m)
    return e * (1.0 / jnp.sum(e, axis=-1, keepdims=True))


def _split_p(p_ref, rows=slice(None)):
    xb = p_ref[rows, OFF_RG_X:OFF_RG_Y]
    yb = p_ref[rows, OFF_RG_Y:OFF_Q]
    q = p_ref[rows, OFF_Q:OFF_K]
    k = p_ref[rows, OFF_K:OFF_V]
    v = p_ref[rows, OFF_V:OFF_G]
    g = p_ref[rows, OFF_G:OFF_A]
    a_low = p_ref[rows, OFF_A:P_WIDTH]
    return xb, yb, q, k, v, g, a_low


def _gla_prepare(q, k, log_a, chunk):
    b = _segment_cumsum(log_a, chunk)
    qt = (q * (GLA_DK ** -0.5)) * jnp.exp(b)
    kt = k * jnp.exp(-b)
    return b, qt, kt


def _mix_project(rg_out, gla_out, wout_ref):
    return (_mm(rg_out, wout_ref[0:RG_WIDTH, :]) + _mm(gla_out, wout_ref[RG_WIDTH:, :]))


def _ffn_in_kernel(x_ref, n1_ref, wg_ref, wu_ref, wd_ref, n2_ref, win_ref, *refs):
    n_cast = (len(refs) - 2) // 2
    cast_src, (x1_ref, p_ref), cast_dst = refs[:n_cast], refs[n_cast:n_cast + 2], refs[n_cast + 2:]
    for sub in range(x_ref.shape[0] // ROW_TILE):
        rows = slice(sub * ROW_TILE, (sub + 1) * ROW_TILE)
        x = x_ref[rows, :]
        x1 = x + 0.5 * _ffn(x, n1_ref, wg_ref, wu_ref, wd_ref)
        x1_ref[rows, :] = x1
        p_ref[rows, :] = _mm(_rms(x1, n2_ref[...]), win_ref[...])
    for src, dst in zip(cast_src, cast_dst):
        dst[...] = src[...].astype(BF16)


def _mem_kv_kernel(m_ref, n_ref, wk_ref, wv_ref, k_ref, v_ref, kb_ref, vb_ref):
    h = _rms(m_ref[...], n_ref[...]).astype(BF16)
    k = jnp.dot(h, wk_ref[...], preferred_element_type=F32)
    v = jnp.dot(h, wv_ref[...], preferred_element_type=F32)
    for h in range(XA_HEADS):
        sl = slice(h * XA_HEAD_DIM, (h + 1) * XA_HEAD_DIM)
        k_ref[:, h, :] = k[:, sl]
        v_ref[:, h, :] = v[:, sl]
    kb_ref[...] = k.astype(BF16)
    vb_ref[...] = v.astype(BF16)


def _prompt_tail_kernel(
        x1_ref, p_ref, kb_ref, vb_ref,
        cw_ref, cb_ref, wgate_ref, ba_ref, bx_ref, lam_ref, rgn_ref,
        wa2_ref, ba2_ref, glan_ref, wout_ref,
        xn_ref, wcq_ref, wco_ref, n2_ref, wg_ref, wu_ref, wd_ref, fn_ref,
        y_ref, rgh_ref, s_ref,
        xpad_ref, h_ref, sbd_ref, b_ref, x3_ref, *, steps_per_seq, n_steps):
    step = pl.program_id(0)
    t_idx = jnp.minimum(step, n_steps - 1) % steps_per_seq
    slot = step % 2
    tt = ROW_TILE
    subs = x1_ref.shape[0] // tt
    n_chunks = tt // GLA_CHUNK

    @pl.when(step == 0)
    def _():
        x3_ref[1] = jnp.zeros((subs, tt, D_MODEL), F32)

    @pl.when(t_idx == 0)
    def _():
        xpad_ref[0:SUBLANES, :] = jnp.zeros((SUBLANES, RG_WIDTH), F32)
        h_ref[...] = jnp.zeros_like(h_ref)
        sbd_ref[...] = jnp.zeros_like(sbd_ref)

    for sub in range(subs):
        rows = slice(sub * tt, (sub + 1) * tt)
        x3_prev = x3_ref[1 - slot, sub]
        ffn_out = []
        ffn = _ffn_pieces(x3_prev, n2_ref, wg_ref, wu_ref, wd_ref, ffn_out)
        placement = iter(FFN2_PLACEMENT)

        def emit_ffn():
            for _ in range(next(placement)):
                next(ffn)

        emit_ffn()

        xb, yb, q, k, v, g, a_low = _split_p(p_ref, rows)

        xpad_ref[SUBLANES:SUBLANES + tt, :] = xb
        sh1 = xpad_ref[SUBLANES - 1:SUBLANES - 1 + tt, :]
        sh2 = xpad_ref[SUBLANES - 2:SUBLANES - 2 + tt, :]
        sh3 = xpad_ref[SUBLANES - 3:SUBLANES - 3 + tt, :]
        xc = _conv(xb, sh1, sh2, sh3, cw_ref, cb_ref)
        xpad_ref[0:SUBLANES, :] = xb[tt - SUBLANES:tt, :]
        emit_ffn()
        r, i = _rg_gates(xc, wgate_ref, ba_ref, bx_ref)
        a, u = _rg_decay_input(xc, r, i, lam_ref)
        emit_ffn()
        a_grp, h_grp = _segment_affine_scan(a, u, SUBLANES)
        emit_ffn()
        carry = h_ref[0:1, :]
        groups = []
        for gi in range(tt // SUBLANES):
            grp = slice(gi * SUBLANES, (gi + 1) * SUBLANES)
            groups.append(h_grp[grp, :] + a_grp[grp, :] * carry)
            carry = groups[-1][SUBLANES - 1:SUBLANES, :]
        hs = jnp.concatenate(groups, axis=0)
        h_ref[...] = jnp.broadcast_to(carry, h_ref.shape)
        rg_out = _rms(hs * _gelu_tanh(yb), rgn_ref[...]).astype(BF16)
        emit_ffn()

        log_a = _gla_log_decay(a_low, wa2_ref, ba2_ref)
        b, qt, kt = _gla_prepare(q, k, log_a, GLA_CHUNK)
        emit_ffn()
        b_last = _last_rows(b_ref, b, GLA_CHUNK)
        dec_t = jnp.exp(b_last).T
        kv = _gla_chunk_updates(k, v, b, b_last, GLA_CHUNK)
        states = {}
        for p in range(2):
            s_bd = sbd_ref[p]
            for c in range(n_chunks):
                states[c, p] = s_bd.astype(BF16)
                s_bd = s_bd * dec_t[p * PAIR_K:(p + 1) * PAIR_K, c:c + 1] + kv[c, p]
            sbd_ref[p] = s_bd
        emit_ffn()
        intra = _gla_intra(qt, kt, v, GLA_CHUNK)
        emit_ffn()
        gla_out = _gla_out(_gla_combine(intra, qt, states, GLA_CHUNK), g, glan_ref).astype(BF16)
        emit_ffn()

        x2 = x1_ref[rows, :] + _mix_project(rg_out, gla_out, wout_ref)

        qx = _mm(_rms(x2, xn_ref[...]), wcq_ref[...])
        emit_ffn()
        head = lambda h: slice(h * XA_HEAD_DIM, (h + 1) * XA_HEAD_DIM)
        s = jnp.concatenate([_mm_nt(qx[:, head(h)], kb_ref[:, head(h)]) for h in range(XA_HEADS)], axis=0)
        pr = _softmax_rows(s * (XA_HEAD_DIM ** -0.5))
        emit_ffn()
        o = jnp.concatenate([_mm(pr[h * tt:(h + 1) * tt, :], vb_ref[:, head(h)]) for h in range(XA_HEADS)], axis=1)
        x3_ref[slot, sub] = x2 + _mm(o, wco_ref[...])
        for _ in ffn:
            pass
        y_ref[rows, :] = _rms(x3_prev + 0.5 * ffn_out[0], fn_ref[...])

    @pl.when((t_idx == steps_per_seq - 1) & (step < n_steps))
    def _():
        rgh_ref[...] = h_ref[...]
        row = lax.broadcasted_iota(jnp.int32, (PAIR_K, GLA_DV), 0)
        for p in range(2):
            s_bd = sbd_ref[p]
            s_ref[p * PAIR_K:(p + 1) * PAIR_K, :] = jnp.where(
                row < GLA_DK, s_bd[:, :GLA_DV], s_bd[:, GLA_DV:])


def _sample_mix_kernel(
        x1_ref, p_ref, econv_ref, eh_ref, s0_ref,
        cw_ref, cb_ref, wgate_ref, ba_ref, bx_ref, lam_ref, rgn_ref,
        wa2_ref, ba2_ref, glan_ref, wout_ref, xn_ref, wcq_ref,
        x2_ref, q_ref, rgh_ref, s_ref,
        hs_ref, b_ref, *, seq):
    rows_n = x1_ref.shape[0]
    nb = rows_n // seq

    xb, yb, q, k, v, g, a_low = _split_p(p_ref)

    pos = _row_in_segment(xb.shape, seq)
    econv = econv_ref[...]
    sh = []
    for j in range(1, CONV_WIDTH):
        sh.append(jnp.where(pos >= j, _shift_rows(xb, j), _shift_rows(econv, rows_n - seq + j)))
    xc = _conv(xb, sh[0], sh[1], sh[2], cw_ref, cb_ref)
    r, i = _rg_gates(xc, wgate_ref, ba_ref, bx_ref)
    a, u = _rg_decay_input(xc, r, i, lam_ref)
    u = u + a * eh_ref[...]
    _, hs = _segment_affine_scan(a, u, seq)
    rgh_ref[...] = _last_rows(hs_ref, hs, seq)
    rg_out = _rms(hs * _gelu_tanh(yb), rgn_ref[...]).astype(BF16)

    log_a = _gla_log_decay(a_low, wa2_ref, ba2_ref)
    b, qt, kt = _gla_prepare(q, k, log_a, seq)
    intra = _gla_intra(qt, kt, v, seq)
    b_last = _last_rows(b_ref, b, seq)
    dec_t = jnp.exp(b_last).T
    kv = _gla_chunk_updates(k, v, b, b_last, seq)
    row = lax.broadcasted_iota(jnp.int32, (PAIR_K, GLA_DV), 0)
    top = row < GLA_DK
    states = {}
    for c in range(nb):
        for p in range(2):
            kl = slice(p * PAIR_K, (p + 1) * PAIR_K)
            s_pair = s0_ref[c, kl, :]
            s_bd = jnp.concatenate([jnp.where(top, s_pair, 0.0), jnp.where(top, 0.0, s_pair)], axis=1)
            states[c, p] = s_bd.astype(BF16)
            s_new = s_bd * dec_t[kl, c:c + 1] + kv[c, p]
            s_ref[c, kl, :] = jnp.where(top, s_new[:, :GLA_DV], s_new[:, GLA_DV:])
    gla_out = _gla_out(_gla_combine(intra, qt, states, seq), g, glan_ref).astype(BF16)

    x2 = x1_ref[...] + _mix_project(rg_out, gla_out, wout_ref)
    x2_ref[...] = x2
    q_ref[...] = _mm(_rms(x2, xn_ref[...]), wcq_ref[...])


def _sample_attn_tail_kernel(q_ref, k_ref, v_ref, x2_ref, wco_ref, n2_ref, wg_ref, wu_ref, wd_ref, fn_ref,
                             y_ref, o_ref, *, seq):
    tiles = XA_HEAD_DIM // LANES
    group = tiles * XA_HEADS
    r = N_MEM * group
    nb = k_ref.shape[0] // r
    hs = XA_HEADS * seq
    lane = lax.broadcasted_iota(jnp.int32, (nb * hs, r), 1)
    head = (lax.broadcasted_iota(jnp.int32, (nb * hs, r), 0) // seq) & (XA_HEADS - 1)
    own = (lane & (group - 1)) == head
    s = []
    for j in range(nb):
        q = q_ref[j * seq:(j + 1) * seq, :]
        q_parts = jnp.concatenate(
            [q[:, c * LANES:(c + 1) * LANES] for c in range(XA_HEADS * tiles)], axis=0)
        part = _mm_nt(q_parts, k_ref[j * r:(j + 1) * r, :])
        for h in range(XA_HEADS):
            acc = part[h * tiles * seq:(h * tiles + 1) * seq, :]
            for c in range(1, tiles):
                blk = part[(h * tiles + c) * seq:(h * tiles + c + 1) * seq, :]
                acc = acc + pltpu.roll(blk, r - c * XA_HEADS, axis=1)
            s.append(acc)
    p = _softmax_rows(jnp.where(own, jnp.concatenate(s, axis=0) * (XA_HEAD_DIM ** -0.5), MASKED_SCORE))
    for j in range(nb):
        pj = p[j * hs:(j + 1) * hs, :]
        p_all = jnp.concatenate(
            [pj] + [pltpu.roll(pj, c * XA_HEADS, axis=1) for c in range(1, tiles)], axis=0)
        o = _mm(p_all, v_ref[j * r:(j + 1) * r, :].astype(BF16))
        for c in range(tiles):
            for h in range(XA_HEADS):
                col = h * XA_HEAD_DIM + c * LANES
                o_ref[j * seq:(j + 1) * seq, col:col + LANES] = (
                    o[(c * XA_HEADS + h) * seq:(c * XA_HEADS + h + 1) * seq, :])
    x3 = x2_ref[...] + _mm(o_ref[...], wco_ref[...])
    x4 = x3 + 0.5 * _ffn(x3, n2_ref, wg_ref, wu_ref, wd_ref)
    y_ref[...] = _rms(x4, fn_ref[...])


def _resident(arr):
    nd = arr.ndim
    return pl.BlockSpec(arr.shape, lambda *_: (0,) * nd, pipeline_mode=pl.Buffered(1))


def _params(sem, flags=None):
    return pltpu.CompilerParams(dimension_semantics=sem, vmem_limit_bytes=VMEM_LIMIT, flags=flags)


def _slab_spec(w, steps):
    per, revisit = w.shape[0] // steps, 1
    while per % BF16_SUBLANES:
        per, revisit = per * 2, revisit * 2
    return pl.BlockSpec((per, w.shape[1]), lambda i: (i // revisit, 0))


def _ffn_in(x, weights, cast=()):
    rows = x.shape[0]
    block = FFN_IN_TILES * ROW_TILE
    steps = rows // block
    row_spec = lambda w: pl.BlockSpec((block, w), lambda i: (i, 0))
    slabs = [_slab_spec(w, steps) for w in cast]
    return pl.pallas_call(
        _ffn_in_kernel,
        grid=(steps,),
        in_specs=[row_spec(D_MODEL)] + [_resident(w) for w in weights] + slabs,
        out_specs=[row_spec(D_MODEL), row_spec(P_WIDTH)] + slabs,
        out_shape=[jax.ShapeDtypeStruct((rows, D_MODEL), F32),
                   jax.ShapeDtypeStruct((rows, P_WIDTH), F32)]
                  + [jax.ShapeDtypeStruct(w.shape, BF16) for w in cast],
        compiler_params=_params(("arbitrary",)),
        name="ffn_in",
    )(x, *weights, *cast)


def _mem_kv(mem, weights):
    rows = mem.shape[0]
    row_spec = pl.BlockSpec((ROW_TILE, D_MODEL), lambda i: (i, 0))
    head_spec = pl.BlockSpec((ROW_TILE, XA_HEADS, XA_HEAD_DIM), lambda i: (i, 0, 0))
    return pl.pallas_call(
        _mem_kv_kernel,
        grid=(rows // ROW_TILE,),
        in_specs=[row_spec] + [_resident(w) for w in weights],
        out_specs=[head_spec] * 2 + [row_spec] * 2,
        out_shape=[jax.ShapeDtypeStruct((rows, XA_HEADS, XA_HEAD_DIM), F32)] * 2
                  + [jax.ShapeDtypeStruct((rows, D_MODEL), BF16)] * 2,
        compiler_params=_params(("parallel",)),
        name="mem_kv",
    )(mem, *weights)


def _prompt_tail(x1, p, kb, vb, weights):
    batch, seq, _ = x1.shape
    tt = ROW_TILE
    block = PROMPT_TILES * tt
    steps_per_seq = seq // block
    n_steps = batch * steps_per_seq
    cur = lambda s: jnp.minimum(s, n_steps - 1)
    prev = lambda s: jnp.maximum(s - 1, 0)
    tok = lambda w: pl.BlockSpec(
        (None, block, w), lambda s: (cur(s) // steps_per_seq, cur(s) % steps_per_seq, 0))
    per_seq = lambda r, w: pl.BlockSpec((None, r, w), lambda s: (cur(s) // steps_per_seq, 0, 0))
    out_tok = pl.BlockSpec(
        (None, block, D_MODEL), lambda s: (prev(s) // steps_per_seq, prev(s) % steps_per_seq, 0))
    return pl.pallas_call(
        functools.partial(_prompt_tail_kernel, steps_per_seq=steps_per_seq, n_steps=n_steps),
        grid=(n_steps + 1,),
        in_specs=[tok(D_MODEL), tok(P_WIDTH), per_seq(N_MEM, D_MODEL), per_seq(N_MEM, D_MODEL)]
                 + [_resident(w) for w in weights],
        out_specs=[out_tok, per_seq(SUBLANES, RG_WIDTH), per_seq(GLA_KEY_WIDTH, GLA_DV)],
        out_shape=[jax.ShapeDtypeStruct((batch, seq, D_MODEL), F32),
                   jax.ShapeDtypeStruct((batch, SUBLANES, RG_WIDTH), F32),
                   jax.ShapeDtypeStruct((batch, GLA_KEY_WIDTH, GLA_DV), F32)],
        scratch_shapes=[pltpu.VMEM((SUBLANES + tt, RG_WIDTH), F32),
                        pltpu.VMEM((SUBLANES, RG_WIDTH), F32),
                        pltpu.VMEM((2, PAIR_K, PAIR_V), F32),
                        pltpu.VMEM((GLA_KEY_WIDTH // LANES, tt, LANES), F32),
                        pltpu.VMEM((2, PROMPT_TILES, tt, D_MODEL), F32)],
        compiler_params=_params(("arbitrary",)),
        name="prompt_tail",
    )(x1, p, kb, vb, *weights)


def _sample_mix(x1, p, econv, eh, s0, weights, seq):
    rows = x1.shape[0]
    nb = ROW_TILE // seq
    row_spec = lambda w: pl.BlockSpec((ROW_TILE, w), lambda i: (i, 0))
    state_spec = pl.BlockSpec((nb, GLA_KEY_WIDTH, GLA_DV), lambda i: (i, 0, 0))
    return pl.pallas_call(
        functools.partial(_sample_mix_kernel, seq=seq),
        grid=(rows // ROW_TILE,),
        in_specs=[row_spec(D_MODEL), row_spec(P_WIDTH), row_spec(RG_WIDTH), row_spec(RG_WIDTH),
                  state_spec] + [_resident(w) for w in weights],
        out_specs=[row_spec(D_MODEL), row_spec(D_MODEL),
                   pl.BlockSpec((nb, RG_WIDTH), lambda i: (i, 0)), state_spec],
        out_shape=[jax.ShapeDtypeStruct((rows, D_MODEL), F32),
                   jax.ShapeDtypeStruct((rows, D_MODEL), F32),
                   jax.ShapeDtypeStruct((rows // seq, RG_WIDTH), F32),
                   jax.ShapeDtypeStruct((rows // seq, GLA_KEY_WIDTH, GLA_DV), F32)],
        scratch_shapes=[pltpu.VMEM((RG_WIDTH // LANES, ROW_TILE, LANES), F32),
                        pltpu.VMEM((GLA_KEY_WIDTH // LANES, ROW_TILE, LANES), F32)],
        compiler_params=_params(("parallel",)),
        name="sample_mix",
    )(x1, p, econv, eh, s0, *weights)


def _head_interleaved_rows(x):
    b, m, h, dh = x.shape
    tiles = dh // LANES
    return (x.reshape(b, m, h, tiles, LANES).transpose(0, 1, 3, 2, 4)
            .reshape(b * m * tiles * h, LANES))


def _sample_attn_tail(q, k, v, x2, weights, seq):
    rows = q.shape[0]
    nb = SAMPLE_ATTN_BATCH
    row_spec = pl.BlockSpec((nb * seq, D_MODEL), lambda i: (i, 0))
    kv_rows = k.shape[0] // (rows // seq)
    kv_spec = pl.BlockSpec((nb * kv_rows, LANES), lambda i: (i, 0))
    return pl.pallas_call(
        functools.partial(_sample_attn_tail_kernel, seq=seq),
        grid=(rows // (nb * seq),),
        in_specs=[row_spec, kv_spec, kv_spec, row_spec] + [_resident(w) for w in weights],
        out_specs=row_spec,
        out_shape=jax.ShapeDtypeStruct((rows, D_MODEL), F32),
        scratch_shapes=[pltpu.VMEM((nb * seq, D_MODEL), F32)],
        compiler_params=_params(("parallel",)),
        name="sample_attn_tail",
    )(q, k, v, x2, *weights)


def _block_diag_heads(w):
    h, n, _ = w.shape
    eye = jnp.eye(h, dtype=w.dtype)
    return (eye[:, None, :, None] * w[:, :, None, :]).reshape(h * n, h * n)


def kernel(x_prompt, x_sample, cache_mem_k, cache_mem_v, state_conv, state_rglru, state_gla, mem_prompt, ffn1_norm, ffn1_w_gate, ffn1_w_up, ffn1_w_down, mix_norm, w_in, conv_w, conv_b, rg_w_a, rg_b_a, rg_w_x, rg_b_x, rg_lambda, rg_out_norm, gla_w_a2, gla_b_a2, gla_out_norm, w_out, xattn_norm, mem_norm, w_cq, w_ck, w_cv, w_co, ffn2_norm, ffn2_w_gate, ffn2_w_up, ffn2_w_down, final_norm):
    bp, tp, _ = x_prompt.shape
    bs, ts, _ = x_sample.shape
    assert ts == SUBLANES, "each sample sequence must fill exactly one 8-row sublane group"
    assert tp % (PROMPT_TILES * ROW_TILE) == 0 and (bs * ts) % (FFN_IN_TILES * ROW_TILE) == 0
    assert bs % SAMPLE_ATTN_BATCH == 0 and (bp * N_MEM) % ROW_TILE == 0
    assert cache_mem_k.shape[1:] == (N_MEM, XA_HEADS, XA_HEAD_DIM) and mem_prompt.shape[1] == N_MEM
    row = lambda g: g.reshape(1, -1)
    bf = lambda w: w.astype(BF16)

    w_in_p = bf(jnp.pad(w_in, ((0, 0), (0, P_WIDTH - D_IN))))
    hp = RG_HEADS // 2
    w_gate_rg = bf(jnp.stack([
        jnp.concatenate([_block_diag_heads(rg_w_a[c * hp:(c + 1) * hp]),
                         _block_diag_heads(rg_w_x[c * hp:(c + 1) * hp])], axis=1)
        for c in range(2)]))
    w_a2_p = bf(jnp.pad(gla_w_a2, ((0, A_PAD - GLA_GATE_RANK), (0, 0))))
    ffn_in_w = (row(ffn1_norm), bf(ffn1_w_gate), bf(ffn1_w_up), bf(ffn1_w_down), row(mix_norm), w_in_p)

    mem_k, mem_v, mem_kb, mem_vb = _mem_kv(mem_prompt.reshape(bp * N_MEM, D_MODEL),
                                           (row(mem_norm), bf(w_ck), bf(w_cv)))
    x1_p, p_p, w2_gate, w2_up, w2_down, w_out_b, w_cq_b, w_co_b = _ffn_in(
        x_prompt.reshape(bp * tp, D_MODEL), ffn_in_w,
        cast=(ffn2_w_gate, ffn2_w_up, ffn2_w_down, w_out, w_cq, w_co))
    mixer_w = (conv_w, row(conv_b), w_gate_rg, row(rg_b_a), row(rg_b_x), row(rg_lambda),
               row(rg_out_norm), w_a2_p, row(gla_b_a2), row(jnp.tile(gla_out_norm, GLA_HEADS)),
               w_out_b)
    ffn2_w = (row(ffn2_norm), w2_gate, w2_up, w2_down, row(final_norm))
    p_p = p_p.reshape(bp, tp, P_WIDTH)
    y_p, rgh_p, s_p = _prompt_tail(
        x1_p.reshape(bp, tp, D_MODEL), p_p,
        mem_kb.reshape(bp, N_MEM, D_MODEL), mem_vb.reshape(bp, N_MEM, D_MODEL),
        mixer_w + (row(xattn_norm), w_cq_b, w_co_b) + ffn2_w)
    conv_p = p_p[:, tp - (CONV_WIDTH - 1):, OFF_RG_X:OFF_RG_Y]

    x1_s, p_s = _ffn_in(x_sample.reshape(bs * ts, D_MODEL), ffn_in_w)
    econv = jnp.pad(state_conv, ((0, 0), (ts - (CONV_WIDTH - 1), 0), (0, 0))).reshape(bs * ts, RG_WIDTH)
    eh = jnp.pad(state_rglru[:, None, :], ((0, 0), (0, ts - 1), (0, 0))).reshape(bs * ts, RG_WIDTH)
    x2_s, q_s, rgh_s, s_s = _sample_mix(
        x1_s, p_s, econv, eh, state_gla.reshape(bs, GLA_KEY_WIDTH, GLA_DV),
        mixer_w + (row(xattn_norm), w_cq_b), ts)
    y_s = _sample_attn_tail(q_s, _head_interleaved_rows(cache_mem_k), _head_interleaved_rows(cache_mem_v),
                            x2_s, (w_co_b,) + ffn2_w, ts)
    conv_s = p_s.reshape(bs, ts, P_WIDTH)[:, ts - (CONV_WIDTH - 1):, OFF_RG_X:OFF_RG_Y]

    return (y_p, y_s.reshape(bs, ts, D_MODEL),
            mem_k.reshape(bp, N_MEM, XA_HEADS, XA_HEAD_DIM),
            mem_v.reshape(bp, N_MEM, XA_HEADS, XA_HEAD_DIM),
            conv_p, rgh_p[:, 0, :], s_p.reshape(bp, GLA_HEADS, GLA_DK, GLA_DV),
            conv_s, rgh_s, s_s.reshape(bs, GLA_HEADS, GLA_DK, GLA_DV))
```

```python
import functools

import jax
import jax.numpy as jnp
import numpy as np
from jax import lax
from jax.experimental import pallas as pl
from jax.experimental.pallas import tpu as pltpu

F32 = jnp.float32
BF16 = jnp.bfloat16

D_MODEL = 1024
D_FF = 2816
RG_WIDTH = 512
RG_HEADS = 8
RG_HEAD_DIM = 64
CONV_WIDTH = 4
RG_C = 8.0
GLA_WIDTH = 512
GLA_HEADS = 4
GLA_DV = 128
GLA_DK = 64
GLA_KEY_WIDTH = 256
GLA_GATE_RANK = 16
GLA_GATE_NORMALIZER = 16.0
GLA_CHUNK = 32
N_MEM = 256
XA_HEADS = 4
XA_HEAD_DIM = 256
EPS = 1e-6
MASKED_SCORE = -np.inf

OFF_RG_X = 0
OFF_RG_Y = 512
OFF_Q = 1024
OFF_K = 1280
OFF_V = 1536
OFF_G = 2048
OFF_A = 2560
D_IN = 2576

LANES = 128
SUBLANES = 8
BF16_SUBLANES = 16
A_PAD = LANES
P_WIDTH = OFF_A + A_PAD
PAIR_K = 2 * GLA_DK
PAIR_V = 2 * GLA_DV
VMEM_LIMIT = 58 * 1024 * 1024

ROW_TILE = 256
FFN_IN_TILES = 2
PROMPT_TILES = 2
FF_CHUNK = 256
FF_DOWN_GROUP = 4
FFN2_PLACEMENT = (1, 1, 1, 1, 1, 1, 1, 1, 1, 1, 1)
SAMPLE_ATTN_BATCH = 8


def _rms(x, g):
    return x * lax.rsqrt(jnp.mean(x * x, axis=-1, keepdims=True) + EPS) * g


def _mm(a, w):
    return jnp.dot(a.astype(BF16), w, preferred_element_type=F32)


def _mm_nt(a, b):
    return lax.dot_general(a.astype(BF16), b.astype(BF16), (((1,), (1,)), ((), ())),
                           preferred_element_type=F32)


def _mm_tn(a, b):
    return lax.dot_general(a.astype(BF16), b.astype(BF16), (((0,), (0,)), ((), ())),
                           preferred_element_type=F32)


def _silu(x):
    return x * jax.nn.sigmoid(x)


def _gelu_tanh(x):
    c = np.float32(np.sqrt(2.0 / np.pi))
    return x * (0.5 * (1.0 + jnp.tanh(c * (x + 0.044715 * (x * x * x)))))


def _softplus(x):
    return jnp.maximum(x, 0.0) + jnp.log1p(jnp.exp(-jnp.abs(x)))


def _ffn(x, norm_ref, wg_ref, wu_ref, wd_ref):
    h = _rms(x, norm_ref[...]).astype(BF16)
    g = jnp.dot(h, wg_ref[...], preferred_element_type=F32)
    u = jnp.dot(h, wu_ref[...], preferred_element_type=F32)
    return _mm(_silu(g) * u, wd_ref[...])


def _last_rows(buf_ref, x, seg):
    n = x.shape[0] // seg
    outs = []
    for j in range(x.shape[1] // LANES):
        buf_ref[j] = x[:, j * LANES:(j + 1) * LANES]
        outs.append(buf_ref[j, pl.ds(seg - 1, n, stride=seg), :])
    return jnp.concatenate(outs, axis=1)


def _ffn_pieces(x, norm_ref, wg_ref, wu_ref, wd_ref, out):
    h = _rms(x, norm_ref[...]).astype(BF16)
    acc = None
    acts = []
    n = D_FF // FF_CHUNK
    for c in range(n):
        cols = slice(c * FF_CHUNK, (c + 1) * FF_CHUNK)
        g = jnp.dot(h, wg_ref[:, cols], preferred_element_type=F32)
        u = jnp.dot(h, wu_ref[:, cols], preferred_element_type=F32)
        acts.append((_silu(g) * u).astype(BF16))
        if len(acts) == FF_DOWN_GROUP or c == n - 1:
            lo = (c + 1 - len(acts)) * FF_CHUNK
            part = jnp.dot(jnp.concatenate(acts, axis=1), wd_ref[lo:(c + 1) * FF_CHUNK, :],
                           preferred_element_type=F32)
            acc = part if acc is None else acc + part
            acts = []
        yield
    out.append(acc)


def _shift_rows(x, s):
    return pltpu.roll(x, s, axis=0)


def _row_in_segment(shape, seg):
    return lax.broadcasted_iota(jnp.int32, shape, 0) & (seg - 1)


def _segment_cumsum(x, seg):
    pos = _row_in_segment(x.shape, seg)
    s = 1
    while s < seg:
        x = jnp.where(pos >= s, x + _shift_rows(x, s), x)
        s *= 2
    return x


def _segment_affine_scan(a, u, seg):
    pos = _row_in_segment(a.shape, seg)
    s = 1
    while s < seg:
        m = pos >= s
        u = jnp.where(m, a * _shift_rows(u, s) + u, u)
        a = jnp.where(m, a * _shift_rows(a, s), a)
        s *= 2
    return a, u


def _conv(xb, sh1, sh2, sh3, cw_ref, cb_ref):
    y = cb_ref[...] + sh3 * cw_ref[0:1, :]
    y = y + sh2 * cw_ref[1:2, :]
    y = y + sh1 * cw_ref[2:3, :]
    return y + xb * cw_ref[3:4, :]


def _rg_gates(xc, wgate_ref, ba_ref, bx_ref):
    half = RG_WIDTH // 2
    r, i = [], []
    for c in range(2):
        z = _mm(xc[:, c * half:(c + 1) * half], wgate_ref[c])
        r.append(z[:, :half])
        i.append(z[:, half:])
    r = jax.nn.sigmoid(jnp.concatenate(r, axis=1) + ba_ref[...])
    i = jax.nn.sigmoid(jnp.concatenate(i, axis=1) + bx_ref[...])
    return r, i


def _rg_decay_input(xc, r, i, lam_ref):
    log_a = (-RG_C * _softplus(-lam_ref[...])) * r
    a = jnp.exp(log_a)
    mult = jnp.sqrt(-jnp.tanh(log_a) * (a * a + 1.0))
    return a, mult * (i * xc)


def _gla_log_decay(a_low, wa2_ref, ba2_ref):
    z = _mm(a_low, wa2_ref[...]) + ba2_ref[...]
    return (-_softplus(-z)) / GLA_GATE_NORMALIZER


def _gla_intra(qt, kt, v, chunk):
    t = qt.shape[0]
    shift = int(np.log2(chunk))
    ri = lax.broadcasted_iota(jnp.int32, (t, t), 0)
    ci = lax.broadcasted_iota(jnp.int32, (t, t), 1)
    causal = (ri >= ci) & ((ri >> shift) == (ci >> shift))
    lane = lax.broadcasted_iota(jnp.int32, (t, PAIR_K), 1)
    outs = []
    for h in range(GLA_HEADS):
        p = h // 2
        qp = qt[:, p * PAIR_K:(p + 1) * PAIR_K]
        kp = kt[:, p * PAIR_K:(p + 1) * PAIR_K]
        mine = (lane >= GLA_DK) if (h % 2) else (lane < GLA_DK)
        s = _mm_nt(jnp.where(mine, qp, 0.0), kp)
        attn = jnp.where(causal, s, 0.0)
        outs.append(_mm(attn, v[:, h * GLA_DV:(h + 1) * GLA_DV].astype(BF16)))
    return outs


def _pair_blockdiag_mask():
    r = lax.broadcasted_iota(jnp.int32, (PAIR_K, PAIR_V), 0)
    c = lax.broadcasted_iota(jnp.int32, (PAIR_K, PAIR_V), 1)
    return (r < GLA_DK) == (c < GLA_DV)


def _gla_chunk_updates(k, v, b, b_last, chunk):
    bd_mask = _pair_blockdiag_mask()
    kv = {}
    for c in range(k.shape[0] // chunk):
        rows = slice(c * chunk, (c + 1) * chunk)
        for p in range(2):
            kl = slice(p * PAIR_K, (p + 1) * PAIR_K)
            kd = k[rows, kl] * jnp.exp(b_last[c:c + 1, kl] - b[rows, kl])
            kv[c, p] = jnp.where(bd_mask, _mm_tn(kd, v[rows, p * PAIR_V:(p + 1) * PAIR_V]), 0.0)
    return kv


def _gla_combine(intra, qt, states, chunk):
    n_chunks = qt.shape[0] // chunk
    heads = []
    for p in range(2):
        kl = slice(p * PAIR_K, (p + 1) * PAIR_K)
        inter = jnp.concatenate(
            [_mm(qt[c * chunk:(c + 1) * chunk, kl], states[c, p]) for c in range(n_chunks)],
            axis=0)
        heads.append(intra[2 * p] + inter[:, :GLA_DV])
        heads.append(intra[2 * p + 1] + inter[:, GLA_DV:])
    return heads


def _gla_out(o_heads, g, norm_ref):
    outs = []
    for h in range(GLA_HEADS):
        outs.append(_rms(o_heads[h], norm_ref[:, h * GLA_DV:(h + 1) * GLA_DV]))
    return jnp.concatenate(outs, axis=1) * _silu(g)


def _softmax_rows(s):
    m = jnp.max(s, axis=-1, keepdims=True)
    e = jnp.exp(s - m)
    return e * (1.0 / jnp.sum(e, axis=-1, keepdims=True))


def _split_p(p_ref, rows=slice(None)):
    xb = p_ref[rows, OFF_RG_X:OFF_RG_Y]
    yb = p_ref[rows, OFF_RG_Y:OFF_Q]
    q = p_ref[rows, OFF_Q:OFF_K]
    k = p_ref[rows, OFF_K:OFF_V]
    v = p_ref[rows, OFF_V:OFF_G]
    g = p_ref[rows, OFF_G:OFF_A]
    a_low = p_ref[rows, OFF_A:P_WIDTH]
    return xb, yb, q, k, v, g, a_low


def _gla_prepare(q, k, log_a, chunk):
    b = _segment_cumsum(log_a, chunk)
    qt = (q * (GLA_DK ** -0.5)) * jnp.exp(b)
    kt = k * jnp.exp(-b)
    return b, qt, kt


def _mix_project(rg_out, gla_out, wout_ref):
    return (_mm(rg_out, wout_ref[0:RG_WIDTH, :]) + _mm(gla_out, wout_ref[RG_WIDTH:, :]))


def _ffn_in_kernel(x_ref, n1_ref, wg_ref, wu_ref, wd_ref, n2_ref, win_ref, *refs):
    n_cast = (len(refs) - 2) // 2
    cast_src, (x1_ref, p_ref), cast_dst = refs[:n_cast], refs[n_cast:n_cast + 2], refs[n_cast + 2:]
    for sub in range(x_ref.shape[0] // ROW_TILE):
        rows = slice(sub * ROW_TILE, (sub + 1) * ROW_TILE)
        x = x_ref[rows, :]
        x1 = x + 0.5 * _ffn(x, n1_ref, wg_ref, wu_ref, wd_ref)
        x1_ref[rows, :] = x1
        p_ref[rows, :] = _mm(_rms(x1, n2_ref[...]), win_ref[...])
    for src, dst in zip(cast_src, cast_dst):
        dst[...] = src[...].astype(BF16)


def _mem_kv_kernel(m_ref, n_ref, wk_ref, wv_ref, *refs):
    n_cast = (len(refs) - 4) // 2
    cast_src, (k_ref, v_ref, kb_ref, vb_ref), cast_dst = refs[:n_cast], refs[n_cast:n_cast + 4], refs[n_cast + 4:]
    h = _rms(m_ref[...], n_ref[...]).astype(BF16)
    k = jnp.dot(h, wk_ref[...], preferred_element_type=F32)
    v = jnp.dot(h, wv_ref[...], preferred_element_type=F32)
    for h in range(XA_HEADS):
        sl = slice(h * XA_HEAD_DIM, (h + 1) * XA_HEAD_DIM)
        k_ref[:, h, :] = k[:, sl]
        v_ref[:, h, :] = v[:, sl]
    kb_ref[...] = k.astype(BF16)
    vb_ref[...] = v.astype(BF16)
    for src, dst in zip(cast_src, cast_dst):
        dst[...] = src[...].astype(BF16)


def _prompt_tail_kernel(
        x1_ref, p_ref, kb_ref, vb_ref,
        cw_ref, cb_ref, wgate_ref, ba_ref, bx_ref, lam_ref, rgn_ref,
        wa2_ref, ba2_ref, glan_ref, wout_ref,
        xn_ref, wcq_ref, wco_ref, n2_ref, wg_ref, wu_ref, wd_ref, fn_ref,
        y_ref, rgh_ref, s_ref,
        xpad_ref, h_ref, sbd_ref, b_ref, x3_ref, *, steps_per_seq, n_steps):
    step = pl.program_id(0)
    t_idx = jnp.minimum(step, n_steps - 1) % steps_per_seq
    slot = step % 2
    tt = ROW_TILE
    subs = x1_ref.shape[0] // tt
    n_chunks = tt // GLA_CHUNK

    @pl.when(step == 0)
    def _():
        x3_ref[1] = jnp.zeros((subs, tt, D_MODEL), F32)

    @pl.when(t_idx == 0)
    def _():
        xpad_ref[0:SUBLANES, :] = jnp.zeros((SUBLANES, RG_WIDTH), F32)
        h_ref[...] = jnp.zeros_like(h_ref)
        sbd_ref[...] = jnp.zeros_like(sbd_ref)

    for sub in range(subs):
        rows = slice(sub * tt, (sub + 1) * tt)
        x3_prev = x3_ref[1 - slot, sub]
        ffn_out = []
        ffn = _ffn_pieces(x3_prev, n2_ref, wg_ref, wu_ref, wd_ref, ffn_out)
        placement = iter(FFN2_PLACEMENT)

        def emit_ffn():
            for _ in range(next(placement)):
                next(ffn)

        emit_ffn()

        xb, yb, q, k, v, g, a_low = _split_p(p_ref, rows)

        xpad_ref[SUBLANES:SUBLANES + tt, :] = xb
        sh1 = xpad_ref[SUBLANES - 1:SUBLANES - 1 + tt, :]
        sh2 = xpad_ref[SUBLANES - 2:SUBLANES - 2 + tt, :]
        sh3 = xpad_ref[SUBLANES - 3:SUBLANES - 3 + tt, :]
        xc = _conv(xb, sh1, sh2, sh3, cw_ref, cb_ref)
        xpad_ref[0:SUBLANES, :] = xb[tt - SUBLANES:tt, :]
        emit_ffn()
        r, i = _rg_gates(xc, wgate_ref, ba_ref, bx_ref)
        a, u = _rg_decay_input(xc, r, i, lam_ref)
        emit_ffn()
        a_grp, h_grp = _segment_affine_scan(a, u, SUBLANES)
        emit_ffn()
        carry = h_ref[0:1, :]
        groups = []
        for gi in range(tt // SUBLANES):
            grp = slice(gi * SUBLANES, (gi + 1) * SUBLANES)
            groups.append(h_grp[grp, :] + a_grp[grp, :] * carry)
            carry = groups[-1][SUBLANES - 1:SUBLANES, :]
        hs = jnp.concatenate(groups, axis=0)
        h_ref[...] = jnp.broadcast_to(carry, h_ref.shape)
        rg_out = _rms(hs * _gelu_tanh(yb), rgn_ref[...]).astype(BF16)
        emit_ffn()

        log_a = _gla_log_decay(a_low, wa2_ref, ba2_ref)
        b, qt, kt = _gla_prepare(q, k, log_a, GLA_CHUNK)
        emit_ffn()
        b_last = _last_rows(b_ref, b, GLA_CHUNK)
        dec_t = jnp.exp(b_last).T
        kv = _gla_chunk_updates(k, v, b, b_last, GLA_CHUNK)
        states = {}
        for p in range(2):
            s_bd = sbd_ref[p]
            for c in range(n_chunks):
                states[c, p] = s_bd.astype(BF16)
                s_bd = s_bd * dec_t[p * PAIR_K:(p + 1) * PAIR_K, c:c + 1] + kv[c, p]
            sbd_ref[p] = s_bd
        emit_ffn()
        intra = _gla_intra(qt, kt, v, GLA_CHUNK)
        emit_ffn()
        gla_out = _gla_out(_gla_combine(intra, qt, states, GLA_CHUNK), g, glan_ref).astype(BF16)
        emit_ffn()

        x2 = x1_ref[rows, :] + _mix_project(rg_out, gla_out, wout_ref)

        qx = _mm(_rms(x2, xn_ref[...]), wcq_ref[...])
        emit_ffn()
        head = lambda h: slice(h * XA_HEAD_DIM, (h + 1) * XA_HEAD_DIM)
        s = jnp.concatenate([_mm_nt(qx[:, head(h)], kb_ref[:, head(h)]) for h in range(XA_HEADS)], axis=0)
        pr = _softmax_rows(s * (XA_HEAD_DIM ** -0.5))
        emit_ffn()
        o = jnp.concatenate([_mm(pr[h * tt:(h + 1) * tt, :], vb_ref[:, head(h)]) for h in range(XA_HEADS)], axis=1)
        x3_ref[slot, sub] = x2 + _mm(o, wco_ref[...])
        for _ in ffn:
            pass
        y_ref[rows, :] = _rms(x3_prev + 0.5 * ffn_out[0], fn_ref[...])

    @pl.when((t_idx == steps_per_seq - 1) & (step < n_steps))
    def _():
        rgh_ref[...] = h_ref[...]
        row = lax.broadcasted_iota(jnp.int32, (PAIR_K, GLA_DV), 0)
        for p in range(2):
            s_bd = sbd_ref[p]
            s_ref[p * PAIR_K:(p + 1) * PAIR_K, :] = jnp.where(
                row < GLA_DK, s_bd[:, :GLA_DV], s_bd[:, GLA_DV:])


def _sample_mix_kernel(
        x1_ref, p_ref, econv_ref, eh_ref, s0_ref,
        cw_ref, cb_ref, wgate_ref, ba_ref, bx_ref, lam_ref, rgn_ref,
        wa2_ref, ba2_ref, glan_ref, wout_ref, xn_ref, wcq_ref,
        x2_ref, q_ref, rgh_ref, s_ref,
        hs_ref, b_ref, *, seq):
    rows_n = x1_ref.shape[0]
    nb = rows_n // seq

    xb, yb, q, k, v, g, a_low = _split_p(p_ref)

    pos = _row_in_segment(xb.shape, seq)
    econv = econv_ref[...]
    sh = []
    for j in range(1, CONV_WIDTH):
        sh.append(jnp.where(pos >= j, _shift_rows(xb, j), _shift_rows(econv, rows_n - seq + j)))
    xc = _conv(xb, sh[0], sh[1], sh[2], cw_ref, cb_ref)
    r, i = _rg_gates(xc, wgate_ref, ba_ref, bx_ref)
    a, u = _rg_decay_input(xc, r, i, lam_ref)
    u = u + a * eh_ref[...]
    _, hs = _segment_affine_scan(a, u, seq)
    rgh_ref[...] = _last_rows(hs_ref, hs, seq)
    rg_out = _rms(hs * _gelu_tanh(yb), rgn_ref[...]).astype(BF16)

    log_a = _gla_log_decay(a_low, wa2_ref, ba2_ref)
    b, qt, kt = _gla_prepare(q, k, log_a, seq)
    intra = _gla_intra(qt, kt, v, seq)
    b_last = _last_rows(b_ref, b, seq)
    dec_t = jnp.exp(b_last).T
    kv = _gla_chunk_updates(k, v, b, b_last, seq)
    row = lax.broadcasted_iota(jnp.int32, (PAIR_K, GLA_DV), 0)
    top = row < GLA_DK
    states = {}
    for c in range(nb):
        for p in range(2):
            kl = slice(p * PAIR_K, (p + 1) * PAIR_K)
            s_pair = s0_ref[c, kl, :]
            s_bd = jnp.concatenate([jnp.where(top, s_pair, 0.0), jnp.where(top, 0.0, s_pair)], axis=1)
            states[c, p] = s_bd.astype(BF16)
            s_new = s_bd * dec_t[kl, c:c + 1] + kv[c, p]
            s_ref[c, kl, :] = jnp.where(top, s_new[:, :GLA_DV], s_new[:, GLA_DV:])
    gla_out = _gla_out(_gla_combine(intra, qt, states, seq), g, glan_ref).astype(BF16)

    x2 = x1_ref[...] + _mix_project(rg_out, gla_out, wout_ref)
    x2_ref[...] = x2
    q_ref[...] = _mm(_rms(x2, xn_ref[...]), wcq_ref[...])


def _sample_attn_tail_kernel(q_ref, k_ref, v_ref, x2_ref, wco_ref, n2_ref, wg_ref, wu_ref, wd_ref, fn_ref,
                             y_ref, o_ref, *, seq):
    tiles = XA_HEAD_DIM // LANES
    group = tiles * XA_HEADS
    r = N_MEM * group
    nb = k_ref.shape[0] // r
    hs = XA_HEADS * seq
    lane = lax.broadcasted_iota(jnp.int32, (nb * hs, r), 1)
    head = (lax.broadcasted_iota(jnp.int32, (nb * hs, r), 0) // seq) & (XA_HEADS - 1)
    own = (lane & (group - 1)) == head
    s = []
    for j in range(nb):
        q = q_ref[j * seq:(j + 1) * seq, :]
        q_parts = jnp.concatenate(
            [q[:, c * LANES:(c + 1) * LANES] for c in range(XA_HEADS * tiles)], axis=0)
        part = _mm_nt(q_parts, k_ref[j * r:(j + 1) * r, :])
        for h in range(XA_HEADS):
            acc = part[h * tiles * seq:(h * tiles + 1) * seq, :]
            for c in range(1, tiles):
                blk = part[(h * tiles + c) * seq:(h * tiles + c + 1) * seq, :]
                acc = acc + pltpu.roll(blk, r - c * XA_HEADS, axis=1)
            s.append(acc)
    p = _softmax_rows(jnp.where(own, jnp.concatenate(s, axis=0) * (XA_HEAD_DIM ** -0.5), MASKED_SCORE))
    for j in range(nb):
        pj = p[j * hs:(j + 1) * hs, :]
        p_all = jnp.concatenate(
            [pj] + [pltpu.roll(pj, c * XA_HEADS, axis=1) for c in range(1, tiles)], axis=0)
        o = _mm(p_all, v_ref[j * r:(j + 1) * r, :].astype(BF16))
        for c in range(tiles):
            for h in range(XA_HEADS):
                col = h * XA_HEAD_DIM + c * LANES
                o_ref[j * seq:(j + 1) * seq, col:col + LANES] = (
                    o[(c * XA_HEADS + h) * seq:(c * XA_HEADS + h + 1) * seq, :])
    x3 = x2_ref[...] + _mm(o_ref[...], wco_ref[...])
    x4 = x3 + 0.5 * _ffn(x3, n2_ref, wg_ref, wu_ref, wd_ref)
    y_ref[...] = _rms(x4, fn_ref[...])


def _resident(arr):
    nd = arr.ndim
    return pl.BlockSpec(arr.shape, lambda *_: (0,) * nd, pipeline_mode=pl.Buffered(1))


def _params(sem, flags=None):
    return pltpu.CompilerParams(dimension_semantics=sem, vmem_limit_bytes=VMEM_LIMIT, flags=flags)


def _slab_spec(w, steps):
    per, revisit = w.shape[0] // steps, 1
    while per % BF16_SUBLANES:
        per, revisit = per * 2, revisit * 2
    return pl.BlockSpec((per, w.shape[1]), lambda i: (i // revisit, 0))


def _ffn_in(x, weights, cast=()):
    rows = x.shape[0]
    block = FFN_IN_TILES * ROW_TILE
    steps = rows // block
    row_spec = lambda w: pl.BlockSpec((block, w), lambda i: (i, 0))
    slabs = [_slab_spec(w, steps) for w in cast]
    return pl.pallas_call(
        _ffn_in_kernel,
        grid=(steps,),
        in_specs=[row_spec(D_MODEL)] + [_resident(w) for w in weights] + slabs,
        out_specs=[row_spec(D_MODEL), row_spec(P_WIDTH)] + slabs,
        out_shape=[jax.ShapeDtypeStruct((rows, D_MODEL), F32),
                   jax.ShapeDtypeStruct((rows, P_WIDTH), F32)]
                  + [jax.ShapeDtypeStruct(w.shape, BF16) for w in cast],
        compiler_params=_params(("arbitrary",)),
        name="ffn_in",
    )(x, *weights, *cast)


def _mem_kv(mem, weights, cast=()):
    rows = mem.shape[0]
    steps = rows // ROW_TILE
    row_spec = pl.BlockSpec((ROW_TILE, D_MODEL), lambda i: (i, 0))
    head_spec = pl.BlockSpec((ROW_TILE, XA_HEADS, XA_HEAD_DIM), lambda i: (i, 0, 0))
    slabs = [_slab_spec(w, steps) for w in cast]
    return pl.pallas_call(
        _mem_kv_kernel,
        grid=(steps,),
        in_specs=[row_spec] + [_resident(w) for w in weights] + slabs,
        out_specs=[head_spec] * 2 + [row_spec] * 2 + slabs,
        out_shape=[jax.ShapeDtypeStruct((rows, XA_HEADS, XA_HEAD_DIM), F32)] * 2
                  + [jax.ShapeDtypeStruct((rows, D_MODEL), BF16)] * 2
                  + [jax.ShapeDtypeStruct(w.shape, BF16) for w in cast],
        compiler_params=_params(("arbitrary",)),
        name="mem_kv",
    )(mem, *weights, *cast)


def _prompt_tail(x1, p, kb, vb, weights):
    batch, seq, _ = x1.shape
    tt = ROW_TILE
    block = PROMPT_TILES * tt
    steps_per_seq = seq // block
    n_steps = batch * steps_per_seq
    cur = lambda s: jnp.minimum(s, n_steps - 1)
    prev = lambda s: jnp.maximum(s - 1, 0)
    tok = lambda w: pl.BlockSpec(
        (None, block, w), lambda s: (cur(s) // steps_per_seq, cur(s) % steps_per_seq, 0))
    per_seq = lambda r, w: pl.BlockSpec((None, r, w), lambda s: (cur(s) // steps_per_seq, 0, 0))
    out_tok = pl.BlockSpec(
        (None, block, D_MODEL), lambda s: (prev(s) // steps_per_seq, prev(s) % steps_per_seq, 0))
    return pl.pallas_call(
        functools.partial(_prompt_tail_kernel, steps_per_seq=steps_per_seq, n_steps=n_steps),
        grid=(n_steps + 1,),
        in_specs=[tok(D_MODEL), tok(P_WIDTH), per_seq(N_MEM, D_MODEL), per_seq(N_MEM, D_MODEL)]
                 + [_resident(w) for w in weights],
        out_specs=[out_tok, per_seq(SUBLANES, RG_WIDTH), per_seq(GLA_KEY_WIDTH, GLA_DV)],
        out_shape=[jax.ShapeDtypeStruct((batch, seq, D_MODEL), F32),
                   jax.ShapeDtypeStruct((batch, SUBLANES, RG_WIDTH), F32),
                   jax.ShapeDtypeStruct((batch, GLA_KEY_WIDTH, GLA_DV), F32)],
        scratch_shapes=[pltpu.VMEM((SUBLANES + tt, RG_WIDTH), F32),
                        pltpu.VMEM((SUBLANES, RG_WIDTH), F32),
                        pltpu.VMEM((2, PAIR_K, PAIR_V), F32),
                        pltpu.VMEM((GLA_KEY_WIDTH // LANES, tt, LANES), F32),
                        pltpu.VMEM((2, PROMPT_TILES, tt, D_MODEL), F32)],
        compiler_params=_params(("arbitrary",)),
        name="prompt_tail",
    )(x1, p, kb, vb, *weights)


def _sample_mix(x1, p, econv, eh, s0, weights, seq):
    rows = x1.shape[0]
    nb = ROW_TILE // seq
    row_spec = lambda w: pl.BlockSpec((ROW_TILE, w), lambda i: (i, 0))
    state_spec = pl.BlockSpec((nb, GLA_KEY_WIDTH, GLA_DV), lambda i: (i, 0, 0))
    return pl.pallas_call(
        functools.partial(_sample_mix_kernel, seq=seq),
        grid=(rows // ROW_TILE,),
        in_specs=[row_spec(D_MODEL), row_spec(P_WIDTH), row_spec(RG_WIDTH), row_spec(RG_WIDTH),
                  state_spec] + [_resident(w) for w in weights],
        out_specs=[row_spec(D_MODEL), row_spec(D_MODEL),
                   pl.BlockSpec((nb, RG_WIDTH), lambda i: (i, 0)), state_spec],
        out_shape=[jax.ShapeDtypeStruct((rows, D_MODEL), F32),
                   jax.ShapeDtypeStruct((rows, D_MODEL), F32),
                   jax.ShapeDtypeStruct((rows // seq, RG_WIDTH), F32),
                   jax.ShapeDtypeStruct((rows // seq, GLA_KEY_WIDTH, GLA_DV), F32)],
        scratch_shapes=[pltpu.VMEM((RG_WIDTH // LANES, ROW_TILE, LANES), F32),
                        pltpu.VMEM((GLA_KEY_WIDTH // LANES, ROW_TILE, LANES), F32)],
        compiler_params=_params(("parallel",)),
        name="sample_mix",
    )(x1, p, econv, eh, s0, *weights)


def _head_interleaved_rows(x):
    b, m, h, dh = x.shape
    tiles = dh // LANES
    return (x.reshape(b, m, h, tiles, LANES).transpose(0, 1, 3, 2, 4)
            .reshape(b * m * tiles * h, LANES))


def _sample_attn_tail(q, k, v, x2, weights, seq):
    rows = q.shape[0]
    nb = SAMPLE_ATTN_BATCH
    row_spec = pl.BlockSpec((nb * seq, D_MODEL), lambda i: (i, 0))
    kv_rows = k.shape[0] // (rows // seq)
    kv_spec = pl.BlockSpec((nb * kv_rows, LANES), lambda i: (i, 0))
    return pl.pallas_call(
        functools.partial(_sample_attn_tail_kernel, seq=seq),
        grid=(rows // (nb * seq),),
        in_specs=[row_spec, kv_spec, kv_spec, row_spec] + [_resident(w) for w in weights],
        out_specs=row_spec,
        out_shape=jax.ShapeDtypeStruct((rows, D_MODEL), F32),
        scratch_shapes=[pltpu.VMEM((nb * seq, D_MODEL), F32)],
        compiler_params=_params(("parallel",)),
        name="sample_attn_tail",
    )(q, k, v, x2, *weights)


def _block_diag_heads(w):
    h, n, _ = w.shape
    eye = jnp.eye(h, dtype=w.dtype)
    return (eye[:, None, :, None] * w[:, :, None, :]).reshape(h * n, h * n)


def kernel(x_prompt, x_sample, cache_mem_k, cache_mem_v, state_conv, state_rglru, state_gla, mem_prompt, ffn1_norm, ffn1_w_gate, ffn1_w_up, ffn1_w_down, mix_norm, w_in, conv_w, conv_b, rg_w_a, rg_b_a, rg_w_x, rg_b_x, rg_lambda, rg_out_norm, gla_w_a2, gla_b_a2, gla_out_norm, w_out, xattn_norm, mem_norm, w_cq, w_ck, w_cv, w_co, ffn2_norm, ffn2_w_gate, ffn2_w_up, ffn2_w_down, final_norm):
    bp, tp, _ = x_prompt.shape
    bs, ts, _ = x_sample.shape
    assert ts == SUBLANES, "each sample sequence must fill exactly one 8-row sublane group"
    assert tp % (PROMPT_TILES * ROW_TILE) == 0 and (bs * ts) % (FFN_IN_TILES * ROW_TILE) == 0
    assert bs % SAMPLE_ATTN_BATCH == 0 and (bp * N_MEM) % ROW_TILE == 0
    assert cache_mem_k.shape[1:] == (N_MEM, XA_HEADS, XA_HEAD_DIM) and mem_prompt.shape[1] == N_MEM
    row = lambda g: g.reshape(1, -1)
    bf = lambda w: w.astype(BF16)

    w_in_p = bf(jnp.pad(w_in, ((0, 0), (0, P_WIDTH - D_IN))))
    hp = RG_HEADS // 2
    w_gate_rg = bf(jnp.stack([
        jnp.concatenate([_block_diag_heads(rg_w_a[c * hp:(c + 1) * hp]),
                         _block_diag_heads(rg_w_x[c * hp:(c + 1) * hp])], axis=1)
        for c in range(2)]))
    w_a2_p = bf(jnp.pad(gla_w_a2, ((0, A_PAD - GLA_GATE_RANK), (0, 0))))

    mem_k, mem_v, mem_kb, mem_vb, w1_gate, w1_up, w1_down = _mem_kv(
        mem_prompt.reshape(bp * N_MEM, D_MODEL), (row(mem_norm), bf(w_ck), bf(w_cv)),
        cast=(ffn1_w_gate, ffn1_w_up, ffn1_w_down))
    ffn_in_w = (row(ffn1_norm), w1_gate, w1_up, w1_down, row(mix_norm), w_in_p)
    x1_p, p_p, w2_gate, w2_up, w2_down, w_out_b, w_cq_b, w_co_b = _ffn_in(
        x_prompt.reshape(bp * tp, D_MODEL), ffn_in_w,
        cast=(ffn2_w_gate, ffn2_w_up, ffn2_w_down, w_out, w_cq, w_co))
    mixer_w = (conv_w, row(conv_b), w_gate_rg, row(rg_b_a), row(rg_b_x), row(rg_lambda),
               row(rg_out_norm), w_a2_p, row(gla_b_a2), row(jnp.tile(gla_out_norm, GLA_HEADS)),
               w_out_b)
    ffn2_w = (row(ffn2_norm), w2_gate, w2_up, w2_down, row(final_norm))
    p_p = p_p.reshape(bp, tp, P_WIDTH)
    y_p, rgh_p, s_p = _prompt_tail(
        x1_p.reshape(bp, tp, D_MODEL), p_p,
        mem_kb.reshape(bp, N_MEM, D_MODEL), mem_vb.reshape(bp, N_MEM, D_MODEL),
        mixer_w + (row(xattn_norm), w_cq_b, w_co_b) + ffn2_w)
    conv_p = p_p[:, tp - (CONV_WIDTH - 1):, OFF_RG_X:OFF_RG_Y]

    x1_s, p_s = _ffn_in(x_sample.reshape(bs * ts, D_MODEL), ffn_in_w)
    econv = jnp.pad(state_conv, ((0, 0), (ts - (CONV_WIDTH - 1), 0), (0, 0))).reshape(bs * ts, RG_WIDTH)
    eh = jnp.pad(state_rglru[:, None, :], ((0, 0), (0, ts - 1), (0, 0))).reshape(bs * ts, RG_WIDTH)
    x2_s, q_s, rgh_s, s_s = _sample_mix(
        x1_s, p_s, econv, eh, state_gla.reshape(bs, GLA_KEY_WIDTH, GLA_DV),
        mixer_w + (row(xattn_norm), w_cq_b), ts)
    y_s = _sample_attn_tail(q_s, _head_interleaved_rows(cache_mem_k), _head_interleaved_rows(cache_mem_v),
                            x2_s, (w_co_b,) + ffn2_w, ts)
    conv_s = p_s.reshape(bs, ts, P_WIDTH)[:, ts - (CONV_WIDTH - 1):, OFF_RG_X:OFF_RG_Y]

    return (y_p, y_s.reshape(bs, ts, D_MODEL),
            mem_k.reshape(bp, N_MEM, XA_HEADS, XA_HEAD_DIM),
            mem_v.reshape(bp, N_MEM, XA_HEADS, XA_HEAD_DIM),
            conv_p, rgh_p[:, 0, :], s_p.reshape(bp, GLA_HEADS, GLA_DK, GLA_DV),
            conv_s, rgh_s, s_s.reshape(bs, GLA_HEADS, GLA_DK, GLA_DV))
```

```python
import functools

import jax
import jax.numpy as jnp
import numpy as np
from jax import lax
from jax.experimental import pallas as pl
from jax.experimental.pallas import tpu as pltpu

F32 = jnp.float32
BF16 = jnp.bfloat16

D_MODEL = 1024
D_FF = 2816
RG_WIDTH = 512
RG_HEADS = 8
RG_HEAD_DIM = 64
CONV_WIDTH = 4
RG_C = 8.0
GLA_WIDTH = 512
GLA_HEADS = 4
GLA_DV = 128
GLA_DK = 64
GLA_KEY_WIDTH = 256
GLA_GATE_RANK = 16
GLA_GATE_NORMALIZER = 16.0
GLA_CHUNK = 32
N_MEM = 256
XA_HEADS = 4
XA_HEAD_DIM = 256
EPS = 1e-6
MASKED_SCORE = -np.inf

OFF_RG_X = 0
OFF_RG_Y = 512
OFF_Q = 1024
OFF_K = 1280
OFF_V = 1536
OFF_G = 2048
OFF_A = 2560
D_IN = 2576

LANES = 128
SUBLANES = 8
BF16_SUBLANES = 16
A_PAD = LANES
P_WIDTH = OFF_A + A_PAD
PAIR_K = 2 * GLA_DK
PAIR_V = 2 * GLA_DV
VMEM_LIMIT = 58 * 1024 * 1024

ROW_TILE = 256
FFN_IN_TILES = 2
PROMPT_TILES = 2
FF_CHUNK = 256
FF_DOWN_GROUP = 4
FFN2_PLACEMENT = (1, 1, 1, 1, 1, 1, 1, 1, 1, 1, 1)
SAMPLE_ATTN_BATCH = 8


def _rms(x, g):
    return x * lax.rsqrt(jnp.mean(x * x, axis=-1, keepdims=True) + EPS) * g


def _mm(a, w):
    return jnp.dot(a.astype(BF16), w, preferred_element_type=F32)


def _mm_nt(a, b):
    return lax.dot_general(a.astype(BF16), b.astype(BF16), (((1,), (1,)), ((), ())),
                           preferred_element_type=F32)


def _mm_tn(a, b):
    return lax.dot_general(a.astype(BF16), b.astype(BF16), (((0,), (0,)), ((), ())),
                           preferred_element_type=F32)


def _silu(x):
    return x * jax.nn.sigmoid(x)


def _gelu_tanh(x):
    c = np.float32(np.sqrt(2.0 / np.pi))
    return x * (0.5 * (1.0 + jnp.tanh(c * (x + 0.044715 * (x * x * x)))))


def _softplus(x):
    return jnp.maximum(x, 0.0) + jnp.log1p(jnp.exp(-jnp.abs(x)))


def _ffn(x, norm_ref, wg_ref, wu_ref, wd_ref):
    h = _rms(x, norm_ref[...]).astype(BF16)
    g = jnp.dot(h, wg_ref[...], preferred_element_type=F32)
    u = jnp.dot(h, wu_ref[...], preferred_element_type=F32)
    return _mm(_silu(g) * u, wd_ref[...])


def _last_rows(buf_ref, x, seg):
    n = x.shape[0] // seg
    outs = []
    for j in range(x.shape[1] // LANES):
        buf_ref[j] = x[:, j * LANES:(j + 1) * LANES]
        outs.append(buf_ref[j, pl.ds(seg - 1, n, stride=seg), :])
    return jnp.concatenate(outs, axis=1)


def _ffn_pieces(x, norm_ref, wg_ref, wu_ref, wd_ref, out):
    h = _rms(x, norm_ref[...]).astype(BF16)
    acc = None
    acts = []
    n = D_FF // FF_CHUNK
    for c in range(n):
        cols = slice(c * FF_CHUNK, (c + 1) * FF_CHUNK)
        g = jnp.dot(h, wg_ref[:, cols], preferred_element_type=F32)
        u = jnp.dot(h, wu_ref[:, cols], preferred_element_type=F32)
        acts.append((_silu(g) * u).astype(BF16))
        if len(acts) == FF_DOWN_GROUP or c == n - 1:
            lo = (c + 1 - len(acts)) * FF_CHUNK
            part = jnp.dot(jnp.concatenate(acts, axis=1), wd_ref[lo:(c + 1) * FF_CHUNK, :],
                           preferred_element_type=F32)
            acc = part if acc is None else acc + part
            acts = []
        yield
    out.append(acc)


def _shift_rows(x, s):
    return pltpu.roll(x, s, axis=0)


def _row_in_segment(shape, seg):
    return lax.broadcasted_iota(jnp.int32, shape, 0) & (seg - 1)


def _segment_cumsum(x, seg):
    pos = _row_in_segment(x.shape, seg)
    s = 1
    while s < seg:
        x = jnp.where(pos >= s, x + _shift_rows(x, s), x)
        s *= 2
    return x


def _segment_affine_scan(a, u, seg):
    pos = _row_in_segment(a.shape, seg)
    s = 1
    while s < seg:
        m = pos >= s
        u = jnp.where(m, a * _shift_rows(u, s) + u, u)
        a = jnp.where(m, a * _shift_rows(a, s), a)
        s *= 2
    return a, u


def _conv(xb, sh1, sh2, sh3, cw_ref, cb_ref):
    y = cb_ref[...] + sh3 * cw_ref[0:1, :]
    y = y + sh2 * cw_ref[1:2, :]
    y = y + sh1 * cw_ref[2:3, :]
    return y + xb * cw_ref[3:4, :]


def _rg_gates(xc, wgate_ref, ba_ref, bx_ref):
    half = RG_WIDTH // 2
    r, i = [], []
    for c in range(2):
        z = _mm(xc[:, c * half:(c + 1) * half], wgate_ref[c])
        r.append(z[:, :half])
        i.append(z[:, half:])
    r = jax.nn.sigmoid(jnp.concatenate(r, axis=1) + ba_ref[...])
    i = jax.nn.sigmoid(jnp.concatenate(i, axis=1) + bx_ref[...])
    return r, i


def _rg_decay_input(xc, r, i, lam_ref):
    log_a = (-RG_C * _softplus(-lam_ref[...])) * r
    a = jnp.exp(log_a)
    mult = jnp.sqrt(-jnp.tanh(log_a) * (a * a + 1.0))
    return a, mult * (i * xc)


def _gla_log_decay(a_low, wa2_ref, ba2_ref):
    z = _mm(a_low, wa2_ref[...]) + ba2_ref[...]
    return (-_softplus(-z)) / GLA_GATE_NORMALIZER


def _gla_intra(qt, kt, v, chunk):
    t = qt.shape[0]
    shift = int(np.log2(chunk))
    ri = lax.broadcasted_iota(jnp.int32, (t, t), 0)
    ci = lax.broadcasted_iota(jnp.int32, (t, t), 1)
    causal = (ri >= ci) & ((ri >> shift) == (ci >> shift))
    lane = lax.broadcasted_iota(jnp.int32, (t, PAIR_K), 1)
    outs = []
    for h in range(GLA_HEADS):
        p = h // 2
        qp = qt[:, p * PAIR_K:(p + 1) * PAIR_K]
        kp = kt[:, p * PAIR_K:(p + 1) * PAIR_K]
        mine = (lane >= GLA_DK) if (h % 2) else (lane < GLA_DK)
        s = _mm_nt(jnp.where(mine, qp, 0.0), kp)
        attn = jnp.where(causal, s, 0.0)
        outs.append(_mm(attn, v[:, h * GLA_DV:(h + 1) * GLA_DV].astype(BF16)))
    return outs


def _pair_blockdiag_mask():
    r = lax.broadcasted_iota(jnp.int32, (PAIR_K, PAIR_V), 0)
    c = lax.broadcasted_iota(jnp.int32, (PAIR_K, PAIR_V), 1)
    return (r < GLA_DK) == (c < GLA_DV)


def _gla_chunk_updates(k, v, b, b_last, chunk):
    bd_mask = _pair_blockdiag_mask()
    kv = {}
    for c in range(k.shape[0] // chunk):
        rows = slice(c * chunk, (c + 1) * chunk)
        for p in range(2):
            kl = slice(p * PAIR_K, (p + 1) * PAIR_K)
            kd = k[rows, kl] * jnp.exp(b_last[c:c + 1, kl] - b[rows, kl])
            kv[c, p] = jnp.where(bd_mask, _mm_tn(kd, v[rows, p * PAIR_V:(p + 1) * PAIR_V]), 0.0)
    return kv


def _gla_combine(intra, qt, states, chunk):
    n_chunks = qt.shape[0] // chunk
    heads = []
    for p in range(2):
        kl = slice(p * PAIR_K, (p + 1) * PAIR_K)
        inter = jnp.concatenate(
            [_mm(qt[c * chunk:(c + 1) * chunk, kl], states[c, p]) for c in range(n_chunks)],
            axis=0)
        heads.append(intra[2 * p] + inter[:, :GLA_DV])
        heads.append(intra[2 * p + 1] + inter[:, GLA_DV:])
    return heads


def _gla_out(o_heads, g, norm_ref):
    outs = []
    for h in range(GLA_HEADS):
        outs.append(_rms(o_heads[h], norm_ref[:, h * GLA_DV:(h + 1) * GLA_DV]))
    return jnp.concatenate(outs, axis=1) * _silu(g)


def _softmax_rows(s):
    m = jnp.max(s, axis=-1, keepdims=True)
    e = jnp.exp(s - m)
    return e * (1.0 / jnp.sum(e, axis=-1, keepdims=True))


def _split_p(p_ref, rows=slice(None)):
    xb = p_ref[rows, OFF_RG_X:OFF_RG_Y]
    yb = p_ref[rows, OFF_RG_Y:OFF_Q]
    q = p_ref[rows, OFF_Q:OFF_K]
    k = p_ref[rows, OFF_K:OFF_V]
    v = p_ref[rows, OFF_V:OFF_G]
    g = p_ref[rows, OFF_G:OFF_A]
    a_low = p_ref[rows, OFF_A:P_WIDTH]
    return xb, yb, q, k, v, g, a_low


def _gla_prepare(q, k, log_a, chunk):
    b = _segment_cumsum(log_a, chunk)
    qt = (q * (GLA_DK ** -0.5)) * jnp.exp(b)
    kt = k * jnp.exp(-b)
    return b, qt, kt


def _mix_project(rg_out, gla_out, wout_ref):
    return (_mm(rg_out, wout_ref[0:RG_WIDTH, :]) + _mm(gla_out, wout_ref[RG_WIDTH:, :]))


def _ffn_in_kernel(x_ref, n1_ref, wg_ref, wu_ref, wd_ref, n2_ref, win_ref, *refs):
    n_cast = (len(refs) - 2) // 2
    cast_src, (x1_ref, p_ref), cast_dst = refs[:n_cast], refs[n_cast:n_cast + 2], refs[n_cast + 2:]
    for sub in range(x_ref.shape[0] // ROW_TILE):
        rows = slice(sub * ROW_TILE, (sub + 1) * ROW_TILE)
        x = x_ref[rows, :]
        x1 = x + 0.5 * _ffn(x, n1_ref, wg_ref, wu_ref, wd_ref)
        x1_ref[rows, :] = x1
        p_ref[rows, :] = _mm(_rms(x1, n2_ref[...]), win_ref[...])
    for src, dst in zip(cast_src, cast_dst):
        dst[...] = src[...].astype(BF16)


def _mem_kv_kernel(m_ref, n_ref, wk_ref, wv_ref, *refs):
    n_cast = (len(refs) - 4) // 2
    cast_src, (k_ref, v_ref, kb_ref, vb_ref), cast_dst = refs[:n_cast], refs[n_cast:n_cast + 4], refs[n_cast + 4:]
    h = _rms(m_ref[...], n_ref[...]).astype(BF16)
    k = jnp.dot(h, wk_ref[...], preferred_element_type=F32)
    v = jnp.dot(h, wv_ref[...], preferred_element_type=F32)
    for h in range(XA_HEADS):
        sl = slice(h * XA_HEAD_DIM, (h + 1) * XA_HEAD_DIM)
        k_ref[:, h, :] = k[:, sl]
        v_ref[:, h, :] = v[:, sl]
    kb_ref[...] = k.astype(BF16)
    vb_ref[...] = v.astype(BF16)
    for src, dst in zip(cast_src, cast_dst):
        dst[...] = src[...].astype(BF16)


def _prompt_tail_kernel(
        x1_ref, p_ref, kb_ref, vb_ref,
        cw_ref, cb_ref, wgate_ref, ba_ref, bx_ref, lam_ref, rgn_ref,
        wa2_ref, ba2_ref, glan_ref, wout_ref,
        xn_ref, wcq_ref, wco_ref, n2_ref, wg_ref, wu_ref, wd_ref, fn_ref,
        y_ref, rgh_ref, s_ref,
        xpad_ref, h_ref, sbd_ref, b_ref, x3_ref, *, steps_per_seq, n_steps):
    step = pl.program_id(0)
    t_idx = jnp.minimum(step, n_steps - 1) % steps_per_seq
    slot = step % 2
    tt = ROW_TILE
    subs = x1_ref.shape[0] // tt
    n_chunks = tt // GLA_CHUNK

    @pl.when(step == 0)
    def _():
        x3_ref[1] = jnp.zeros((subs, tt, D_MODEL), F32)

    @pl.when(t_idx == 0)
    def _():
        xpad_ref[0:SUBLANES, :] = jnp.zeros((SUBLANES, RG_WIDTH), F32)
        h_ref[...] = jnp.zeros_like(h_ref)
        sbd_ref[...] = jnp.zeros_like(sbd_ref)

    for sub in range(subs):
        rows = slice(sub * tt, (sub + 1) * tt)
        x3_prev = x3_ref[1 - slot, sub]
        ffn_out = []
        ffn = _ffn_pieces(x3_prev, n2_ref, wg_ref, wu_ref, wd_ref, ffn_out)
        placement = iter(FFN2_PLACEMENT)

        def emit_ffn():
            for _ in range(next(placement)):
                next(ffn)

        emit_ffn()

        xb, yb, q, k, v, g, a_low = _split_p(p_ref, rows)

        xpad_ref[SUBLANES:SUBLANES + tt, :] = xb
        sh1 = xpad_ref[SUBLANES - 1:SUBLANES - 1 + tt, :]
        sh2 = xpad_ref[SUBLANES - 2:SUBLANES - 2 + tt, :]
        sh3 = xpad_ref[SUBLANES - 3:SUBLANES - 3 + tt, :]
        xc = _conv(xb, sh1, sh2, sh3, cw_ref, cb_ref)
        xpad_ref[0:SUBLANES, :] = xb[tt - SUBLANES:tt, :]
        emit_ffn()
        r, i = _rg_gates(xc, wgate_ref, ba_ref, bx_ref)
        a, u = _rg_decay_input(xc, r, i, lam_ref)
        emit_ffn()
        a_grp, h_grp = _segment_affine_scan(a, u, SUBLANES)
        emit_ffn()
        carry = h_ref[0:1, :]
        groups = []
        for gi in range(tt // SUBLANES):
            grp = slice(gi * SUBLANES, (gi + 1) * SUBLANES)
            groups.append(h_grp[grp, :] + a_grp[grp, :] * carry)
            carry = groups[-1][SUBLANES - 1:SUBLANES, :]
        hs = jnp.concatenate(groups, axis=0)
        h_ref[...] = jnp.broadcast_to(carry, h_ref.shape)
        rg_out = _rms(hs * _gelu_tanh(yb), rgn_ref[...]).astype(BF16)
        emit_ffn()

        log_a = _gla_log_decay(a_low, wa2_ref, ba2_ref)
        b, qt, kt = _gla_prepare(q, k, log_a, GLA_CHUNK)
        emit_ffn()
        b_last = _last_rows(b_ref, b, GLA_CHUNK)
        dec_t = jnp.exp(b_last).T
        kv = _gla_chunk_updates(k, v, b, b_last, GLA_CHUNK)
        states = {}
        for p in range(2):
            s_bd = sbd_ref[p]
            for c in range(n_chunks):
                states[c, p] = s_bd.astype(BF16)
                s_bd = s_bd * dec_t[p * PAIR_K:(p + 1) * PAIR_K, c:c + 1] + kv[c, p]
            sbd_ref[p] = s_bd
        emit_ffn()
        intra = _gla_intra(qt, kt, v, GLA_CHUNK)
        emit_ffn()
        gla_out = _gla_out(_gla_combine(intra, qt, states, GLA_CHUNK), g, glan_ref).astype(BF16)
        emit_ffn()

        x2 = x1_ref[rows, :] + _mix_project(rg_out, gla_out, wout_ref)

        qx = _mm(_rms(x2, xn_ref[...]), wcq_ref[...])
        emit_ffn()
        head = lambda h: slice(h * XA_HEAD_DIM, (h + 1) * XA_HEAD_DIM)
        s = jnp.concatenate([_mm_nt(qx[:, head(h)], kb_ref[:, head(h)]) for h in range(XA_HEADS)], axis=0)
        pr = _softmax_rows(s * (XA_HEAD_DIM ** -0.5))
        emit_ffn()
        o = jnp.concatenate([_mm(pr[h * tt:(h + 1) * tt, :], vb_ref[:, head(h)]) for h in range(XA_HEADS)], axis=1)
        x3_ref[slot, sub] = x2 + _mm(o, wco_ref[...])
        for _ in ffn:
            pass
        y_ref[rows, :] = _rms(x3_prev + 0.5 * ffn_out[0], fn_ref[...])

    @pl.when((t_idx == steps_per_seq - 1) & (step < n_steps))
    def _():
        rgh_ref[...] = h_ref[...]
        row = lax.broadcasted_iota(jnp.int32, (PAIR_K, GLA_DV), 0)
        for p in range(2):
            s_bd = sbd_ref[p]
            s_ref[p * PAIR_K:(p + 1) * PAIR_K, :] = jnp.where(
                row < GLA_DK, s_bd[:, :GLA_DV], s_bd[:, GLA_DV:])


def _sample_mix_kernel(
        x1_ref, p_ref, econv_ref, eh_ref, s0_ref,
        cw_ref, cb_ref, wgate_ref, ba_ref, bx_ref, lam_ref, rgn_ref,
        wa2_ref, ba2_ref, glan_ref, wout_ref, xn_ref, wcq_ref,
        x2_ref, q_ref, rgh_ref, s_ref,
        hs_ref, b_ref, *, seq):
    rows_n = x1_ref.shape[0]
    nb = rows_n // seq

    xb, yb, q, k, v, g, a_low = _split_p(p_ref)

    pos = _row_in_segment(xb.shape, seq)
    econv = econv_ref[...]
    sh = []
    for j in range(1, CONV_WIDTH):
        sh.append(jnp.where(pos >= j, _shift_rows(xb, j), _shift_rows(econv, rows_n - seq + j)))
    xc = _conv(xb, sh[0], sh[1], sh[2], cw_ref, cb_ref)
    r, i = _rg_gates(xc, wgate_ref, ba_ref, bx_ref)
    a, u = _rg_decay_input(xc, r, i, lam_ref)
    u = u + a * eh_ref[...]
    _, hs = _segment_affine_scan(a, u, seq)
    rgh_ref[...] = _last_rows(hs_ref, hs, seq)
    rg_out = _rms(hs * _gelu_tanh(yb), rgn_ref[...]).astype(BF16)

    log_a = _gla_log_decay(a_low, wa2_ref, ba2_ref)
    b, qt, kt = _gla_prepare(q, k, log_a, seq)
    intra = _gla_intra(qt, kt, v, seq)
    b_last = _last_rows(b_ref, b, seq)
    dec_t = jnp.exp(b_last).T
    kv = _gla_chunk_updates(k, v, b, b_last, seq)
    row = lax.broadcasted_iota(jnp.int32, (PAIR_K, GLA_DV), 0)
    top = row < GLA_DK
    states = {}
    for c in range(nb):
        for p in range(2):
            kl = slice(p * PAIR_K, (p + 1) * PAIR_K)
            s_pair = s0_ref[c, kl, :]
            s_bd = jnp.concatenate([jnp.where(top, s_pair, 0.0), jnp.where(top, 0.0, s_pair)], axis=1)
            states[c, p] = s_bd.astype(BF16)
            s_new = s_bd * dec_t[kl, c:c + 1] + kv[c, p]
            s_ref[c, kl, :] = jnp.where(top, s_new[:, :GLA_DV], s_new[:, GLA_DV:])
    gla_out = _gla_out(_gla_combine(intra, qt, states, seq), g, glan_ref).astype(BF16)

    x2 = x1_ref[...] + _mix_project(rg_out, gla_out, wout_ref)
    x2_ref[...] = x2
    q_ref[...] = _mm(_rms(x2, xn_ref[...]), wcq_ref[...])


def _sample_attn_tail_kernel(q_ref, k_ref, v_ref, x2_ref, wco_ref, n2_ref, wg_ref, wu_ref, wd_ref, fn_ref,
                             y_ref, o_ref, *, seq):
    tiles = XA_HEAD_DIM // LANES
    group = tiles * XA_HEADS
    r = N_MEM * group
    nb = k_ref.shape[0] // r
    hs = XA_HEADS * seq
    lane = lax.broadcasted_iota(jnp.int32, (nb * hs, r), 1)
    head = (lax.broadcasted_iota(jnp.int32, (nb * hs, r), 0) // seq) & (XA_HEADS - 1)
    own = (lane & (group - 1)) == head
    s = []
    for j in range(nb):
        q = q_ref[j * seq:(j + 1) * seq, :]
        q_parts = jnp.concatenate(
            [q[:, c * LANES:(c + 1) * LANES] for c in range(XA_HEADS * tiles)], axis=0)
        part = _mm_nt(q_parts, k_ref[j * r:(j + 1) * r, :])
        for h in range(XA_HEADS):
            acc = part[h * tiles * seq:(h * tiles + 1) * seq, :]
            for c in range(1, tiles):
                blk = part[(h * tiles + c) * seq:(h * tiles + c + 1) * seq, :]
                acc = acc + pltpu.roll(blk, r - c * XA_HEADS, axis=1)
            s.append(acc)
    p = _softmax_rows(jnp.where(own, jnp.concatenate(s, axis=0) * (XA_HEAD_DIM ** -0.5), MASKED_SCORE))
    for j in range(nb):
        pj = p[j * hs:(j + 1) * hs, :]
        p_all = jnp.concatenate(
            [pj] + [pltpu.roll(pj, c * XA_HEADS, axis=1) for c in range(1, tiles)], axis=0)
        o = _mm(p_all, v_ref[j * r:(j + 1) * r, :].astype(BF16))
        for c in range(tiles):
            for h in range(XA_HEADS):
                col = h * XA_HEAD_DIM + c * LANES
                o_ref[j * seq:(j + 1) * seq, col:col + LANES] = (
                    o[(c * XA_HEADS + h) * seq:(c * XA_HEADS + h + 1) * seq, :])
    x3 = x2_ref[...] + _mm(o_ref[...], wco_ref[...])
    x4 = x3 + 0.5 * _ffn(x3, n2_ref, wg_ref, wu_ref, wd_ref)
    y_ref[...] = _rms(x4, fn_ref[...])


def _resident(arr):
    nd = arr.ndim
    return pl.BlockSpec(arr.shape, lambda *_: (0,) * nd, pipeline_mode=pl.Buffered(1))


def _params(sem):
    return pltpu.CompilerParams(dimension_semantics=sem, vmem_limit_bytes=VMEM_LIMIT)


def _slab_spec(w, steps):
    per, revisit = w.shape[0] // steps, 1
    while per % BF16_SUBLANES:
        per, revisit = per * 2, revisit * 2
    return pl.BlockSpec((per, w.shape[1]), lambda i: (i // revisit, 0))


def _ffn_in(x, weights, cast=()):
    rows = x.shape[0]
    block = FFN_IN_TILES * ROW_TILE
    steps = rows // block
    row_spec = lambda w: pl.BlockSpec((block, w), lambda i: (i, 0))
    slabs = [_slab_spec(w, steps) for w in cast]
    return pl.pallas_call(
        _ffn_in_kernel,
        grid=(steps,),
        in_specs=[row_spec(D_MODEL)] + [_resident(w) for w in weights] + slabs,
        out_specs=[row_spec(D_MODEL), row_spec(P_WIDTH)] + slabs,
        out_shape=[jax.ShapeDtypeStruct((rows, D_MODEL), F32),
                   jax.ShapeDtypeStruct((rows, P_WIDTH), F32)]
                  + [jax.ShapeDtypeStruct(w.shape, BF16) for w in cast],
        compiler_params=_params(("arbitrary",)),
        name="ffn_in",
    )(x, *weights, *cast)


def _mem_kv(mem, weights, cast=()):
    rows = mem.shape[0]
    steps = rows // ROW_TILE
    row_spec = pl.BlockSpec((ROW_TILE, D_MODEL), lambda i: (i, 0))
    head_spec = pl.BlockSpec((ROW_TILE, XA_HEADS, XA_HEAD_DIM), lambda i: (i, 0, 0))
    slabs = [_slab_spec(w, steps) for w in cast]
    return pl.pallas_call(
        _mem_kv_kernel,
        grid=(steps,),
        in_specs=[row_spec] + [_resident(w) for w in weights] + slabs,
        out_specs=[head_spec] * 2 + [row_spec] * 2 + slabs,
        out_shape=[jax.ShapeDtypeStruct((rows, XA_HEADS, XA_HEAD_DIM), F32)] * 2
                  + [jax.ShapeDtypeStruct((rows, D_MODEL), BF16)] * 2
                  + [jax.ShapeDtypeStruct(w.shape, BF16) for w in cast],
        compiler_params=_params(("arbitrary",)),
        name="mem_kv",
    )(mem, *weights, *cast)


def _prompt_tail(x1, p, kb, vb, weights):
    batch, seq, _ = x1.shape
    tt = ROW_TILE
    block = PROMPT_TILES * tt
    steps_per_seq = seq // block
    n_steps = batch * steps_per_seq
    cur = lambda s: jnp.minimum(s, n_steps - 1)
    prev = lambda s: jnp.maximum(s - 1, 0)
    tok = lambda w: pl.BlockSpec(
        (None, block, w), lambda s: (cur(s) // steps_per_seq, cur(s) % steps_per_seq, 0))
    per_seq = lambda r, w: pl.BlockSpec((None, r, w), lambda s: (cur(s) // steps_per_seq, 0, 0))
    out_tok = pl.BlockSpec(
        (None, block, D_MODEL), lambda s: (prev(s) // steps_per_seq, prev(s) % steps_per_seq, 0))
    return pl.pallas_call(
        functools.partial(_prompt_tail_kernel, steps_per_seq=steps_per_seq, n_steps=n_steps),
        grid=(n_steps + 1,),
        in_specs=[tok(D_MODEL), tok(P_WIDTH), per_seq(N_MEM, D_MODEL), per_seq(N_MEM, D_MODEL)]
                 + [_resident(w) for w in weights],
        out_specs=[out_tok, per_seq(SUBLANES, RG_WIDTH), per_seq(GLA_KEY_WIDTH, GLA_DV)],
        out_shape=[jax.ShapeDtypeStruct((batch, seq, D_MODEL), F32),
                   jax.ShapeDtypeStruct((batch, SUBLANES, RG_WIDTH), F32),
                   jax.ShapeDtypeStruct((batch, GLA_KEY_WIDTH, GLA_DV), F32)],
        scratch_shapes=[pltpu.VMEM((SUBLANES + tt, RG_WIDTH), F32),
                        pltpu.VMEM((SUBLANES, RG_WIDTH), F32),
                        pltpu.VMEM((2, PAIR_K, PAIR_V), F32),
                        pltpu.VMEM((GLA_KEY_WIDTH // LANES, tt, LANES), F32),
                        pltpu.VMEM((2, PROMPT_TILES, tt, D_MODEL), F32)],
        compiler_params=_params(("arbitrary",)),
        name="prompt_tail",
    )(x1, p, kb, vb, *weights)


def _sample_mix(x1, p, econv, eh, s0, weights, seq):
    rows = x1.shape[0]
    nb = ROW_TILE // seq
    row_spec = lambda w: pl.BlockSpec((ROW_TILE, w), lambda i: (i, 0))
    state_spec = pl.BlockSpec((nb, GLA_KEY_WIDTH, GLA_DV), lambda i: (i, 0, 0))
    return pl.pallas_call(
        functools.partial(_sample_mix_kernel, seq=seq),
        grid=(rows // ROW_TILE,),
        in_specs=[row_spec(D_MODEL), row_spec(P_WIDTH), row_spec(RG_WIDTH), row_spec(RG_WIDTH),
                  state_spec] + [_resident(w) for w in weights],
        out_specs=[row_spec(D_MODEL), row_spec(D_MODEL),
                   pl.BlockSpec((nb, RG_WIDTH), lambda i: (i, 0)), state_spec],
        out_shape=[jax.ShapeDtypeStruct((rows, D_MODEL), F32),
                   jax.ShapeDtypeStruct((rows, D_MODEL), F32),
                   jax.ShapeDtypeStruct((rows // seq, RG_WIDTH), F32),
                   jax.ShapeDtypeStruct((rows // seq, GLA_KEY_WIDTH, GLA_DV), F32)],
        scratch_shapes=[pltpu.VMEM((RG_WIDTH // LANES, ROW_TILE, LANES), F32),
                        pltpu.VMEM((GLA_KEY_WIDTH // LANES, ROW_TILE, LANES), F32)],
        compiler_params=_params(("parallel",)),
        name="sample_mix",
    )(x1, p, econv, eh, s0, *weights)


def _head_interleaved_rows(x):
    b, m, h, dh = x.shape
    tiles = dh // LANES
    return (x.reshape(b, m, h, tiles, LANES).transpose(0, 1, 3, 2, 4)
            .reshape(b * m * tiles * h, LANES))


def _sample_attn_tail(q, k, v, x2, weights, seq):
    rows = q.shape[0]
    nb = SAMPLE_ATTN_BATCH
    row_spec = pl.BlockSpec((nb * seq, D_MODEL), lambda i: (i, 0))
    kv_rows = k.shape[0] // (rows // seq)
    kv_spec = pl.BlockSpec((nb * kv_rows, LANES), lambda i: (i, 0))
    return pl.pallas_call(
        functools.partial(_sample_attn_tail_kernel, seq=seq),
        grid=(rows // (nb * seq),),
        in_specs=[row_spec, kv_spec, kv_spec, row_spec] + [_resident(w) for w in weights],
        out_specs=row_spec,
        out_shape=jax.ShapeDtypeStruct((rows, D_MODEL), F32),
        scratch_shapes=[pltpu.VMEM((nb * seq, D_MODEL), F32)],
        compiler_params=_params(("parallel",)),
        name="sample_attn_tail",
    )(q, k, v, x2, *weights)


def _block_diag_heads(w):
    h, n, _ = w.shape
    eye = jnp.eye(h, dtype=w.dtype)
    return (eye[:, None, :, None] * w[:, :, None, :]).reshape(h * n, h * n)


def kernel(x_prompt, x_sample, cache_mem_k, cache_mem_v, state_conv, state_rglru, state_gla, mem_prompt, ffn1_norm, ffn1_w_gate, ffn1_w_up, ffn1_w_down, mix_norm, w_in, conv_w, conv_b, rg_w_a, rg_b_a, rg_w_x, rg_b_x, rg_lambda, rg_out_norm, gla_w_a2, gla_b_a2, gla_out_norm, w_out, xattn_norm, mem_norm, w_cq, w_ck, w_cv, w_co, ffn2_norm, ffn2_w_gate, ffn2_w_up, ffn2_w_down, final_norm):
    bp, tp, _ = x_prompt.shape
    bs, ts, _ = x_sample.shape
    assert ts == SUBLANES, "each sample sequence must fill exactly one 8-row sublane group"
    assert tp % (PROMPT_TILES * ROW_TILE) == 0 and (bs * ts) % (FFN_IN_TILES * ROW_TILE) == 0
    assert bs % SAMPLE_ATTN_BATCH == 0 and (bp * N_MEM) % ROW_TILE == 0
    assert cache_mem_k.shape[1:] == (N_MEM, XA_HEADS, XA_HEAD_DIM) and mem_prompt.shape[1] == N_MEM
    row = lambda g: g.reshape(1, -1)
    bf = lambda w: w.astype(BF16)

    w_in_p = bf(jnp.pad(w_in, ((0, 0), (0, P_WIDTH - D_IN))))
    hp = RG_HEADS // 2
    w_gate_rg = bf(jnp.stack([
        jnp.concatenate([_block_diag_heads(rg_w_a[c * hp:(c + 1) * hp]),
                         _block_diag_heads(rg_w_x[c * hp:(c + 1) * hp])], axis=1)
        for c in range(2)]))
    w_a2_p = bf(jnp.pad(gla_w_a2, ((0, A_PAD - GLA_GATE_RANK), (0, 0))))

    mem_k, mem_v, mem_kb, mem_vb, w1_gate, w1_up, w1_down = _mem_kv(
        mem_prompt.reshape(bp * N_MEM, D_MODEL), (row(mem_norm), bf(w_ck), bf(w_cv)),
        cast=(ffn1_w_gate, ffn1_w_up, ffn1_w_down))
    ffn_in_w = (row(ffn1_norm), w1_gate, w1_up, w1_down, row(mix_norm), w_in_p)
    x1_p, p_p, w2_gate, w2_up, w2_down, w_out_b, w_cq_b, w_co_b = _ffn_in(
        x_prompt.reshape(bp * tp, D_MODEL), ffn_in_w,
        cast=(ffn2_w_gate, ffn2_w_up, ffn2_w_down, w_out, w_cq, w_co))
    mixer_w = (conv_w, row(conv_b), w_gate_rg, row(rg_b_a), row(rg_b_x), row(rg_lambda),
               row(rg_out_norm), w_a2_p, row(gla_b_a2), row(jnp.tile(gla_out_norm, GLA_HEADS)),
               w_out_b)
    ffn2_w = (row(ffn2_norm), w2_gate, w2_up, w2_down, row(final_norm))
    p_p = p_p.reshape(bp, tp, P_WIDTH)
    y_p, rgh_p, s_p = _prompt_tail(
        x1_p.reshape(bp, tp, D_MODEL), p_p,
        mem_kb.reshape(bp, N_MEM, D_MODEL), mem_vb.reshape(bp, N_MEM, D_MODEL),
        mixer_w + (row(xattn_norm), w_cq_b, w_co_b) + ffn2_w)
    conv_p = p_p[:, tp - (CONV_WIDTH - 1):, OFF_RG_X:OFF_RG_Y]

    x1_s, p_s = _ffn_in(x_sample.reshape(bs * ts, D_MODEL), ffn_in_w)
    econv = jnp.pad(state_conv, ((0, 0), (ts - (CONV_WIDTH - 1), 0), (0, 0))).reshape(bs * ts, RG_WIDTH)
    eh = jnp.pad(state_rglru[:, None, :], ((0, 0), (0, ts - 1), (0, 0))).reshape(bs * ts, RG_WIDTH)
    x2_s, q_s, rgh_s, s_s = _sample_mix(
        x1_s, p_s, econv, eh, state_gla.reshape(bs, GLA_KEY_WIDTH, GLA_DV),
        mixer_w + (row(xattn_norm), w_cq_b), ts)
    y_s = _sample_attn_tail(q_s, _head_interleaved_rows(cache_mem_k), _head_interleaved_rows(cache_mem_v),
                            x2_s, (w_co_b,) + ffn2_w, ts)
    conv_s = p_s.reshape(bs, ts, P_WIDTH)[:, ts - (CONV_WIDTH - 1):, OFF_RG_X:OFF_RG_Y]

    return (y_p, y_s.reshape(bs, ts, D_MODEL),
            mem_k.reshape(bp, N_MEM, XA_HEADS, XA_HEAD_DIM),
            mem_v.reshape(bp, N_MEM, XA_HEADS, XA_HEAD_DIM),
            conv_p, rgh_p[:, 0, :], s_p.reshape(bp, GLA_HEADS, GLA_DK, GLA_DV),
            conv_s, rgh_s, s_s.reshape(bs, GLA_HEADS, GLA_DK, GLA_DV))
```

```python
import functools

import jax
import jax.numpy as jnp
import numpy as np
from jax import lax
from jax.experimental import pallas as pl
from jax.experimental.pallas import tpu as pltpu

F32 = jnp.float32
BF16 = jnp.bfloat16

D_MODEL = 1024
D_FF = 2816
RG_WIDTH = 512
RG_HEADS = 8
RG_HEAD_DIM = 64
CONV_WIDTH = 4
RG_C = 8.0
GLA_WIDTH = 512
GLA_HEADS = 4
GLA_DV = 128
GLA_DK = 64
GLA_KEY_WIDTH = 256
GLA_GATE_RANK = 16
GLA_GATE_NORMALIZER = 16.0
GLA_CHUNK = 32
N_MEM = 256
XA_HEADS = 4
XA_HEAD_DIM = 256
EPS = 1e-6
MASKED_SCORE = -np.inf

OFF_RG_X = 0
OFF_RG_Y = 512
OFF_Q = 1024
OFF_K = 1280
OFF_V = 1536
OFF_G = 2048
OFF_A = 2560
D_IN = 2576

LANES = 128
SUBLANES = 8
BF16_SUBLANES = 16
A_PAD = LANES
P_WIDTH = OFF_A + A_PAD
PAIR_K = 2 * GLA_DK
PAIR_V = 2 * GLA_DV
VMEM_LIMIT = 58 * 1024 * 1024

ROW_TILE = 256
FFN_IN_TILES = 2
PROMPT_TILES = 2
FF_CHUNK = 256
FF_DOWN_GROUP = 4
FFN2_PLACEMENT = (1, 1, 1, 1, 1, 1, 1, 1, 1, 1, 1)
SAMPLE_ATTN_BATCH = 8


def _rms(x, g):
    return x * lax.rsqrt(jnp.mean(x * x, axis=-1, keepdims=True) + EPS) * g


def _mm(a, w):
    return jnp.dot(a.astype(BF16), w, preferred_element_type=F32)


def _mm_nt(a, b):
    return lax.dot_general(a.astype(BF16), b.astype(BF16), (((1,), (1,)), ((), ())),
                           preferred_element_type=F32)


def _mm_tn(a, b):
    return lax.dot_general(a.astype(BF16), b.astype(BF16), (((0,), (0,)), ((), ())),
                           preferred_element_type=F32)


def _silu(x):
    return x * jax.nn.sigmoid(x)


def _gelu_tanh(x):
    c = np.float32(np.sqrt(2.0 / np.pi))
    return x * (0.5 * (1.0 + jnp.tanh(c * (x + 0.044715 * (x * x * x)))))


def _softplus(x):
    return jnp.maximum(x, 0.0) + jnp.log1p(jnp.exp(-jnp.abs(x)))


def _ffn(x, norm_ref, wg_ref, wu_ref, wd_ref):
    h = _rms(x, norm_ref[...]).astype(BF16)
    g = jnp.dot(h, wg_ref[...], preferred_element_type=F32)
    u = jnp.dot(h, wu_ref[...], preferred_element_type=F32)
    return _mm(_silu(g) * u, wd_ref[...])


def _last_rows(buf_ref, x, seg):
    n = x.shape[0] // seg
    outs = []
    for j in range(x.shape[1] // LANES):
        buf_ref[j] = x[:, j * LANES:(j + 1) * LANES]
        outs.append(buf_ref[j, pl.ds(seg - 1, n, stride=seg), :])
    return jnp.concatenate(outs, axis=1)


def _ffn_pieces(x, norm_ref, wg_ref, wu_ref, wd_ref, out):
    h = _rms(x, norm_ref[...]).astype(BF16)
    acc = None
    acts = []
    n = D_FF // FF_CHUNK
    for c in range(n):
        cols = slice(c * FF_CHUNK, (c + 1) * FF_CHUNK)
        g = jnp.dot(h, wg_ref[:, cols], preferred_element_type=F32)
        u = jnp.dot(h, wu_ref[:, cols], preferred_element_type=F32)
        acts.append((_silu(g) * u).astype(BF16))
        if len(acts) == FF_DOWN_GROUP or c == n - 1:
            lo = (c + 1 - len(acts)) * FF_CHUNK
            part = jnp.dot(jnp.concatenate(acts, axis=1), wd_ref[lo:(c + 1) * FF_CHUNK, :],
                           preferred_element_type=F32)
            acc = part if acc is None else acc + part
            acts = []
        yield
    out.append(acc)


def _shift_rows(x, s):
    return pltpu.roll(x, s, axis=0)


def _row_in_segment(shape, seg):
    return lax.broadcasted_iota(jnp.int32, shape, 0) & (seg - 1)


def _segment_cumsum(x, seg):
    pos = _row_in_segment(x.shape, seg)
    s = 1
    while s < seg:
        x = jnp.where(pos >= s, x + _shift_rows(x, s), x)
        s *= 2
    return x


def _segment_affine_scan(a, u, seg):
    pos = _row_in_segment(a.shape, seg)
    s = 1
    while s < seg:
        m = pos >= s
        u = jnp.where(m, a * _shift_rows(u, s) + u, u)
        a = jnp.where(m, a * _shift_rows(a, s), a)
        s *= 2
    return a, u


def _conv(xb, sh1, sh2, sh3, cw_ref, cb_ref):
    y = cb_ref[...] + sh3 * cw_ref[0:1, :]
    y = y + sh2 * cw_ref[1:2, :]
    y = y + sh1 * cw_ref[2:3, :]
    return y + xb * cw_ref[3:4, :]


def _rg_gates(xc, wgate_ref, ba_ref, bx_ref):
    half = RG_WIDTH // 2
    r, i = [], []
    for c in range(2):
        z = _mm(xc[:, c * half:(c + 1) * half], wgate_ref[c])
        r.append(z[:, :half])
        i.append(z[:, half:])
    r = jax.nn.sigmoid(jnp.concatenate(r, axis=1) + ba_ref[...])
    i = jax.nn.sigmoid(jnp.concatenate(i, axis=1) + bx_ref[...])
    return r, i


def _rg_decay_input(xc, r, i, lam_ref):
    log_a = (-RG_C * _softplus(-lam_ref[...])) * r
    a = jnp.exp(log_a)
    mult = jnp.sqrt(-jnp.tanh(log_a) * (a * a + 1.0))
    return a, mult * (i * xc)


def _gla_log_decay(a_low, wa2_ref, ba2_ref):
    z = _mm(a_low, wa2_ref[...]) + ba2_ref[...]
    return (-_softplus(-z)) / GLA_GATE_NORMALIZER


def _gla_intra(qt, kt, v, chunk):
    t = qt.shape[0]
    shift = int(np.log2(chunk))
    ri = lax.broadcasted_iota(jnp.int32, (t, t), 0)
    ci = lax.broadcasted_iota(jnp.int32, (t, t), 1)
    causal = (ri >= ci) & ((ri >> shift) == (ci >> shift))
    lane = lax.broadcasted_iota(jnp.int32, (t, PAIR_K), 1)
    outs = []
    for h in range(GLA_HEADS):
        p = h // 2
        qp = qt[:, p * PAIR_K:(p + 1) * PAIR_K]
        kp = kt[:, p * PAIR_K:(p + 1) * PAIR_K]
        mine = (lane >= GLA_DK) if (h % 2) else (lane < GLA_DK)
        s = _mm_nt(jnp.where(mine, qp, 0.0), kp)
        attn = jnp.where(causal, s, 0.0)
        outs.append(_mm(attn, v[:, h * GLA_DV:(h + 1) * GLA_DV].astype(BF16)))
    return outs


def _pair_blockdiag_mask():
    r = lax.broadcasted_iota(jnp.int32, (PAIR_K, PAIR_V), 0)
    c = lax.broadcasted_iota(jnp.int32, (PAIR_K, PAIR_V), 1)
    return (r < GLA_DK) == (c < GLA_DV)


def _gla_chunk_updates(k, v, b, b_last, chunk):
    bd_mask = _pair_blockdiag_mask()
    kv = {}
    for c in range(k.shape[0] // chunk):
        rows = slice(c * chunk, (c + 1) * chunk)
        for p in range(2):
            kl = slice(p * PAIR_K, (p + 1) * PAIR_K)
            kd = k[rows, kl] * jnp.exp(b_last[c:c + 1, kl] - b[rows, kl])
            kv[c, p] = jnp.where(bd_mask, _mm_tn(kd, v[rows, p * PAIR_V:(p + 1) * PAIR_V]), 0.0)
    return kv


def _gla_combine(intra, qt, states, chunk):
    n_chunks = qt.shape[0] // chunk
    heads = []
    for p in range(2):
        kl = slice(p * PAIR_K, (p + 1) * PAIR_K)
        inter = jnp.concatenate(
            [_mm(qt[c * chunk:(c + 1) * chunk, kl], states[c, p]) for c in range(n_chunks)],
            axis=0)
        heads.append(intra[2 * p] + inter[:, :GLA_DV])
        heads.append(intra[2 * p + 1] + inter[:, GLA_DV:])
    return heads


def _gla_out(o_heads, g, norm_ref):
    outs = []
    for h in range(GLA_HEADS):
        outs.append(_rms(o_heads[h], norm_ref[:, h * GLA_DV:(h + 1) * GLA_DV]))
    return jnp.concatenate(outs, axis=1) * _silu(g)


def _softmax_rows(s):
    m = jnp.max(s, axis=-1, keepdims=True)
    e = jnp.exp(s - m)
    return e * (1.0 / jnp.sum(e, axis=-1, keepdims=True))


def _split_p(p_ref, rows=slice(None)):
    xb = p_ref[rows, OFF_RG_X:OFF_RG_Y]
    yb = p_ref[rows, OFF_RG_Y:OFF_Q]
    q = p_ref[rows, OFF_Q:OFF_K]
    k = p_ref[rows, OFF_K:OFF_V]
    v = p_ref[rows, OFF_V:OFF_G]
    g = p_ref[rows, OFF_G:OFF_A]
    a_low = p_ref[rows, OFF_A:P_WIDTH]
    return xb, yb, q, k, v, g, a_low


def _gla_prepare(q, k, log_a, chunk):
    b = _segment_cumsum(log_a, chunk)
    qt = (q * (GLA_DK ** -0.5)) * jnp.exp(b)
    kt = k * jnp.exp(-b)
    return b, qt, kt


def _mix_project(rg_out, gla_out, wout_ref):
    return (_mm(rg_out, wout_ref[0:RG_WIDTH, :]) + _mm(gla_out, wout_ref[RG_WIDTH:, :]))


def _ffn_in_kernel(xa_ref, xb_ref, n1_ref, wg_ref, wu_ref, wd_ref, n2_ref, win_ref, *refs, steps_a):
    from_a = pl.program_id(0) < steps_a
    n_cast = (len(refs) - 2) // 2
    cast_src, (x1_ref, p_ref), cast_dst = refs[:n_cast], refs[n_cast:n_cast + 2], refs[n_cast + 2:]
    for sub in range(xa_ref.shape[0] // ROW_TILE):
        rows = slice(sub * ROW_TILE, (sub + 1) * ROW_TILE)
        x = jnp.where(from_a, xa_ref[rows, :], xb_ref[rows, :])
        x1 = x + 0.5 * _ffn(x, n1_ref, wg_ref, wu_ref, wd_ref)
        x1_ref[rows, :] = x1
        p_ref[rows, :] = _mm(_rms(x1, n2_ref[...]), win_ref[...])
    for src, dst in zip(cast_src, cast_dst):
        dst[...] = src[...].astype(BF16)


def _mem_kv_kernel(m_ref, n_ref, wk_ref, wv_ref, *refs):
    n_cast = (len(refs) - 4) // 2
    cast_src, (k_ref, v_ref, kb_ref, vb_ref), cast_dst = refs[:n_cast], refs[n_cast:n_cast + 4], refs[n_cast + 4:]
    h = _rms(m_ref[...], n_ref[...]).astype(BF16)
    k = jnp.dot(h, wk_ref[...], preferred_element_type=F32)
    v = jnp.dot(h, wv_ref[...], preferred_element_type=F32)
    for h in range(XA_HEADS):
        sl = slice(h * XA_HEAD_DIM, (h + 1) * XA_HEAD_DIM)
        k_ref[:, h, :] = k[:, sl]
        v_ref[:, h, :] = v[:, sl]
    kb_ref[...] = k.astype(BF16)
    vb_ref[...] = v.astype(BF16)
    for src, dst in zip(cast_src, cast_dst):
        dst[...] = src[...].astype(BF16)


def _prompt_tail_kernel(
        x1_ref, p_ref, kb_ref, vb_ref,
        cw_ref, cb_ref, wgate_ref, ba_ref, bx_ref, lam_ref, rgn_ref,
        wa2_ref, ba2_ref, glan_ref, wout_ref,
        xn_ref, wcq_ref, wco_ref, n2_ref, wg_ref, wu_ref, wd_ref, fn_ref,
        y_ref, rgh_ref, s_ref,
        xpad_ref, h_ref, sbd_ref, b_ref, x3_ref, *, steps_per_seq, n_steps):
    step = pl.program_id(0)
    t_idx = jnp.minimum(step, n_steps - 1) % steps_per_seq
    slot = step % 2
    tt = ROW_TILE
    subs = x1_ref.shape[0] // tt
    n_chunks = tt // GLA_CHUNK

    @pl.when(step == 0)
    def _():
        x3_ref[1] = jnp.zeros((subs, tt, D_MODEL), F32)

    @pl.when(t_idx == 0)
    def _():
        xpad_ref[0:SUBLANES, :] = jnp.zeros((SUBLANES, RG_WIDTH), F32)
        h_ref[...] = jnp.zeros_like(h_ref)
        sbd_ref[...] = jnp.zeros_like(sbd_ref)

    for sub in range(subs):
        rows = slice(sub * tt, (sub + 1) * tt)
        x3_prev = x3_ref[1 - slot, sub]
        ffn_out = []
        ffn = _ffn_pieces(x3_prev, n2_ref, wg_ref, wu_ref, wd_ref, ffn_out)
        placement = iter(FFN2_PLACEMENT)

        def emit_ffn():
            for _ in range(next(placement)):
                next(ffn)

        emit_ffn()

        xb, yb, q, k, v, g, a_low = _split_p(p_ref, rows)

        xpad_ref[SUBLANES:SUBLANES + tt, :] = xb
        sh1 = xpad_ref[SUBLANES - 1:SUBLANES - 1 + tt, :]
        sh2 = xpad_ref[SUBLANES - 2:SUBLANES - 2 + tt, :]
        sh3 = xpad_ref[SUBLANES - 3:SUBLANES - 3 + tt, :]
        xc = _conv(xb, sh1, sh2, sh3, cw_ref, cb_ref)
        xpad_ref[0:SUBLANES, :] = xb[tt - SUBLANES:tt, :]
        emit_ffn()
        r, i = _rg_gates(xc, wgate_ref, ba_ref, bx_ref)
        a, u = _rg_decay_input(xc, r, i, lam_ref)
        emit_ffn()
        a_grp, h_grp = _segment_affine_scan(a, u, SUBLANES)
        emit_ffn()
        carry = h_ref[0:1, :]
        groups = []
        for gi in range(tt // SUBLANES):
            grp = slice(gi * SUBLANES, (gi + 1) * SUBLANES)
            groups.append(h_grp[grp, :] + a_grp[grp, :] * carry)
            carry = groups[-1][SUBLANES - 1:SUBLANES, :]
        hs = jnp.concatenate(groups, axis=0)
        h_ref[...] = jnp.broadcast_to(carry, h_ref.shape)
        rg_out = _rms(hs * _gelu_tanh(yb), rgn_ref[...]).astype(BF16)
        emit_ffn()

        log_a = _gla_log_decay(a_low, wa2_ref, ba2_ref)
        b, qt, kt = _gla_prepare(q, k, log_a, GLA_CHUNK)
        emit_ffn()
        b_last = _last_rows(b_ref, b, GLA_CHUNK)
        dec_t = jnp.exp(b_last).T
        kv = _gla_chunk_updates(k, v, b, b_last, GLA_CHUNK)
        states = {}
        for p in range(2):
            s_bd = sbd_ref[p]
            for c in range(n_chunks):
                states[c, p] = s_bd.astype(BF16)
                s_bd = s_bd * dec_t[p * PAIR_K:(p + 1) * PAIR_K, c:c + 1] + kv[c, p]
            sbd_ref[p] = s_bd
        emit_ffn()
        intra = _gla_intra(qt, kt, v, GLA_CHUNK)
        emit_ffn()
        gla_out = _gla_out(_gla_combine(intra, qt, states, GLA_CHUNK), g, glan_ref).astype(BF16)
        emit_ffn()

        x2 = x1_ref[rows, :] + _mix_project(rg_out, gla_out, wout_ref)

        qx = _mm(_rms(x2, xn_ref[...]), wcq_ref[...])
        emit_ffn()
        head = lambda h: slice(h * XA_HEAD_DIM, (h + 1) * XA_HEAD_DIM)
        s = jnp.concatenate([_mm_nt(qx[:, head(h)], kb_ref[:, head(h)]) for h in range(XA_HEADS)], axis=0)
        pr = _softmax_rows(s * (XA_HEAD_DIM ** -0.5))
        emit_ffn()
        o = jnp.concatenate([_mm(pr[h * tt:(h + 1) * tt, :], vb_ref[:, head(h)]) for h in range(XA_HEADS)], axis=1)
        x3_ref[slot, sub] = x2 + _mm(o, wco_ref[...])
        for _ in ffn:
            pass
        y_ref[rows, :] = _rms(x3_prev + 0.5 * ffn_out[0], fn_ref[...])

    @pl.when((t_idx == steps_per_seq - 1) & (step < n_steps))
    def _():
        rgh_ref[...] = h_ref[...]
        row = lax.broadcasted_iota(jnp.int32, (PAIR_K, GLA_DV), 0)
        for p in range(2):
            s_bd = sbd_ref[p]
            s_ref[p * PAIR_K:(p + 1) * PAIR_K, :] = jnp.where(
                row < GLA_DK, s_bd[:, :GLA_DV], s_bd[:, GLA_DV:])


def _sample_mix_kernel(
        x1_ref, p_ref, econv_ref, eh_ref, s0_ref,
        cw_ref, cb_ref, wgate_ref, ba_ref, bx_ref, lam_ref, rgn_ref,
        wa2_ref, ba2_ref, glan_ref, wout_ref, xn_ref, wcq_ref,
        x2_ref, q_ref, rgh_ref, s_ref,
        hs_ref, b_ref, *, seq):
    rows_n = x1_ref.shape[0]
    nb = rows_n // seq

    xb, yb, q, k, v, g, a_low = _split_p(p_ref)

    pos = _row_in_segment(xb.shape, seq)
    econv = econv_ref[...]
    sh = []
    for j in range(1, CONV_WIDTH):
        sh.append(jnp.where(pos >= j, _shift_rows(xb, j), _shift_rows(econv, rows_n - seq + j)))
    xc = _conv(xb, sh[0], sh[1], sh[2], cw_ref, cb_ref)
    r, i = _rg_gates(xc, wgate_ref, ba_ref, bx_ref)
    a, u = _rg_decay_input(xc, r, i, lam_ref)
    u = u + a * eh_ref[...]
    _, hs = _segment_affine_scan(a, u, seq)
    rgh_ref[...] = _last_rows(hs_ref, hs, seq)
    rg_out = _rms(hs * _gelu_tanh(yb), rgn_ref[...]).astype(BF16)

    log_a = _gla_log_decay(a_low, wa2_ref, ba2_ref)
    b, qt, kt = _gla_prepare(q, k, log_a, seq)
    intra = _gla_intra(qt, kt, v, seq)
    b_last = _last_rows(b_ref, b, seq)
    dec_t = jnp.exp(b_last).T
    kv = _gla_chunk_updates(k, v, b, b_last, seq)
    row = lax.broadcasted_iota(jnp.int32, (PAIR_K, GLA_DV), 0)
    top = row < GLA_DK
    states = {}
    for c in range(nb):
        for p in range(2):
            kl = slice(p * PAIR_K, (p + 1) * PAIR_K)
            s_pair = s0_ref[c, kl, :]
            s_bd = jnp.concatenate([jnp.where(top, s_pair, 0.0), jnp.where(top, 0.0, s_pair)], axis=1)
            states[c, p] = s_bd.astype(BF16)
            s_new = s_bd * dec_t[kl, c:c + 1] + kv[c, p]
            s_ref[c, kl, :] = jnp.where(top, s_new[:, :GLA_DV], s_new[:, GLA_DV:])
    gla_out = _gla_out(_gla_combine(intra, qt, states, seq), g, glan_ref).astype(BF16)

    x2 = x1_ref[...] + _mix_project(rg_out, gla_out, wout_ref)
    x2_ref[...] = x2
    q_ref[...] = _mm(_rms(x2, xn_ref[...]), wcq_ref[...])


def _sample_attn_tail_kernel(q_ref, k_ref, v_ref, x2_ref, wco_ref, n2_ref, wg_ref, wu_ref, wd_ref, fn_ref,
                             y_ref, o_ref, *, seq):
    tiles = XA_HEAD_DIM // LANES
    group = tiles * XA_HEADS
    r = N_MEM * group
    nb = k_ref.shape[0] // r
    hs = XA_HEADS * seq
    lane = lax.broadcasted_iota(jnp.int32, (nb * hs, r), 1)
    head = (lax.broadcasted_iota(jnp.int32, (nb * hs, r), 0) // seq) & (XA_HEADS - 1)
    own = (lane & (group - 1)) == head
    s = []
    for j in range(nb):
        q = q_ref[j * seq:(j + 1) * seq, :]
        q_parts = jnp.concatenate(
            [q[:, c * LANES:(c + 1) * LANES] for c in range(XA_HEADS * tiles)], axis=0)
        part = _mm_nt(q_parts, k_ref[j * r:(j + 1) * r, :])
        for h in range(XA_HEADS):
            acc = part[h * tiles * seq:(h * tiles + 1) * seq, :]
            for c in range(1, tiles):
                blk = part[(h * tiles + c) * seq:(h * tiles + c + 1) * seq, :]
                acc = acc + pltpu.roll(blk, r - c * XA_HEADS, axis=1)
            s.append(acc)
    p = _softmax_rows(jnp.where(own, jnp.concatenate(s, axis=0) * (XA_HEAD_DIM ** -0.5), MASKED_SCORE))
    for j in range(nb):
        pj = p[j * hs:(j + 1) * hs, :]
        p_all = jnp.concatenate(
            [pj] + [pltpu.roll(pj, c * XA_HEADS, axis=1) for c in range(1, tiles)], axis=0)
        o = _mm(p_all, v_ref[j * r:(j + 1) * r, :].astype(BF16))
        for c in range(tiles):
            for h in range(XA_HEADS):
                col = h * XA_HEAD_DIM + c * LANES
                o_ref[j * seq:(j + 1) * seq, col:col + LANES] = (
                    o[(c * XA_HEADS + h) * seq:(c * XA_HEADS + h + 1) * seq, :])
    x3 = x2_ref[...] + _mm(o_ref[...], wco_ref[...])
    x4 = x3 + 0.5 * _ffn(x3, n2_ref, wg_ref, wu_ref, wd_ref)
    y_ref[...] = _rms(x4, fn_ref[...])


def _resident(arr):
    nd = arr.ndim
    return pl.BlockSpec(arr.shape, lambda *_: (0,) * nd, pipeline_mode=pl.Buffered(1))


def _params(sem):
    return pltpu.CompilerParams(dimension_semantics=sem, vmem_limit_bytes=VMEM_LIMIT)


def _slab_spec(w, steps):
    per, revisit = w.shape[0] // steps, 1
    while per % BF16_SUBLANES:
        per, revisit = per * 2, revisit * 2
    return pl.BlockSpec((per, w.shape[1]), lambda i: (jnp.minimum(i, steps - 1) // revisit, 0))


def _ffn_in(xa, xb, weights, cast=()):
    block = FFN_IN_TILES * ROW_TILE
    steps_a, steps_b = xa.shape[0] // block, xb.shape[0] // block
    rows = xa.shape[0] + xb.shape[0]
    spec_a = pl.BlockSpec((block, D_MODEL), lambda i: (jnp.minimum(i, steps_a - 1), 0))
    spec_b = pl.BlockSpec((block, D_MODEL), lambda i: (jnp.maximum(i - steps_a, 0), 0))
    row_spec = lambda w: pl.BlockSpec((block, w), lambda i: (i, 0))
    slabs = [_slab_spec(w, steps_a) for w in cast]
    return pl.pallas_call(
        functools.partial(_ffn_in_kernel, steps_a=steps_a),
        grid=(steps_a + steps_b,),
        in_specs=[spec_a, spec_b] + [_resident(w) for w in weights] + slabs,
        out_specs=[row_spec(D_MODEL), row_spec(P_WIDTH)] + slabs,
        out_shape=[jax.ShapeDtypeStruct((rows, D_MODEL), F32),
                   jax.ShapeDtypeStruct((rows, P_WIDTH), F32)]
                  + [jax.ShapeDtypeStruct(w.shape, BF16) for w in cast],
        compiler_params=_params(("arbitrary",)),
        name="ffn_in",
    )(xa, xb, *weights, *cast)


def _mem_kv(mem, weights, cast=()):
    rows = mem.shape[0]
    steps = rows // ROW_TILE
    row_spec = pl.BlockSpec((ROW_TILE, D_MODEL), lambda i: (i, 0))
    head_spec = pl.BlockSpec((ROW_TILE, XA_HEADS, XA_HEAD_DIM), lambda i: (i, 0, 0))
    slabs = [_slab_spec(w, steps) for w in cast]
    return pl.pallas_call(
        _mem_kv_kernel,
        grid=(steps,),
        in_specs=[row_spec] + [_resident(w) for w in weights] + slabs,
        out_specs=[head_spec] * 2 + [row_spec] * 2 + slabs,
        out_shape=[jax.ShapeDtypeStruct((rows, XA_HEADS, XA_HEAD_DIM), F32)] * 2
                  + [jax.ShapeDtypeStruct((rows, D_MODEL), BF16)] * 2
                  + [jax.ShapeDtypeStruct(w.shape, BF16) for w in cast],
        compiler_params=_params(("arbitrary",)),
        name="mem_kv",
    )(mem, *weights, *cast)


def _prompt_tail(x1, p, kb, vb, weights, seq):
    batch = kb.shape[0]
    tt = ROW_TILE
    block = PROMPT_TILES * tt
    steps_per_seq = seq // block
    n_steps = batch * steps_per_seq
    cur = lambda s: jnp.minimum(s, n_steps - 1)
    prev = lambda s: jnp.maximum(s - 1, 0)
    tok = lambda w: pl.BlockSpec((block, w), lambda s: (cur(s), 0))
    per_seq = lambda r, w: pl.BlockSpec((None, r, w), lambda s: (cur(s) // steps_per_seq, 0, 0))
    out_tok = pl.BlockSpec((block, D_MODEL), lambda s: (prev(s), 0))
    return pl.pallas_call(
        functools.partial(_prompt_tail_kernel, steps_per_seq=steps_per_seq, n_steps=n_steps),
        grid=(n_steps + 1,),
        in_specs=[tok(D_MODEL), tok(P_WIDTH), per_seq(N_MEM, D_MODEL), per_seq(N_MEM, D_MODEL)]
                 + [_resident(w) for w in weights],
        out_specs=[out_tok, per_seq(SUBLANES, RG_WIDTH), per_seq(GLA_KEY_WIDTH, GLA_DV)],
        out_shape=[jax.ShapeDtypeStruct((batch * seq, D_MODEL), F32),
                   jax.ShapeDtypeStruct((batch, SUBLANES, RG_WIDTH), F32),
                   jax.ShapeDtypeStruct((batch, GLA_KEY_WIDTH, GLA_DV), F32)],
        scratch_shapes=[pltpu.VMEM((SUBLANES + tt, RG_WIDTH), F32),
                        pltpu.VMEM((SUBLANES, RG_WIDTH), F32),
                        pltpu.VMEM((2, PAIR_K, PAIR_V), F32),
                        pltpu.VMEM((GLA_KEY_WIDTH // LANES, tt, LANES), F32),
                        pltpu.VMEM((2, PROMPT_TILES, tt, D_MODEL), F32)],
        compiler_params=_params(("arbitrary",)),
        name="prompt_tail",
    )(x1, p, kb, vb, *weights)


def _sample_mix(x1, p, econv, eh, s0, weights, seq):
    rows = econv.shape[0]
    first = (x1.shape[0] - rows) // ROW_TILE
    nb = ROW_TILE // seq
    row_spec = lambda w: pl.BlockSpec((ROW_TILE, w), lambda i: (i, 0))
    tail_spec = lambda w: pl.BlockSpec((ROW_TILE, w), lambda i: (first + i, 0))
    state_spec = pl.BlockSpec((nb, GLA_KEY_WIDTH, GLA_DV), lambda i: (i, 0, 0))
    return pl.pallas_call(
        functools.partial(_sample_mix_kernel, seq=seq),
        grid=(rows // ROW_TILE,),
        in_specs=[tail_spec(D_MODEL), tail_spec(P_WIDTH), row_spec(RG_WIDTH), row_spec(RG_WIDTH),
                  state_spec] + [_resident(w) for w in weights],
        out_specs=[row_spec(D_MODEL), row_spec(D_MODEL),
                   pl.BlockSpec((nb, RG_WIDTH), lambda i: (i, 0)), state_spec],
        out_shape=[jax.ShapeDtypeStruct((rows, D_MODEL), F32),
                   jax.ShapeDtypeStruct((rows, D_MODEL), F32),
                   jax.ShapeDtypeStruct((rows // seq, RG_WIDTH), F32),
                   jax.ShapeDtypeStruct((rows // seq, GLA_KEY_WIDTH, GLA_DV), F32)],
        scratch_shapes=[pltpu.VMEM((RG_WIDTH // LANES, ROW_TILE, LANES), F32),
                        pltpu.VMEM((GLA_KEY_WIDTH // LANES, ROW_TILE, LANES), F32)],
        compiler_params=_params(("parallel",)),
        name="sample_mix",
    )(x1, p, econv, eh, s0, *weights)


def _head_interleaved_rows(x):
    b, m, h, dh = x.shape
    tiles = dh // LANES
    return (x.reshape(b, m, h, tiles, LANES).transpose(0, 1, 3, 2, 4)
            .reshape(b * m * tiles * h, LANES))


def _sample_attn_tail(q, k, v, x2, weights, seq):
    rows = q.shape[0]
    nb = SAMPLE_ATTN_BATCH
    row_spec = pl.BlockSpec((nb * seq, D_MODEL), lambda i: (i, 0))
    kv_rows = k.shape[0] // (rows // seq)
    kv_spec = pl.BlockSpec((nb * kv_rows, LANES), lambda i: (i, 0))
    return pl.pallas_call(
        functools.partial(_sample_attn_tail_kernel, seq=seq),
        grid=(rows // (nb * seq),),
        in_specs=[row_spec, kv_spec, kv_spec, row_spec] + [_resident(w) for w in weights],
        out_specs=row_spec,
        out_shape=jax.ShapeDtypeStruct((rows, D_MODEL), F32),
        scratch_shapes=[pltpu.VMEM((nb * seq, D_MODEL), F32)],
        compiler_params=_params(("parallel",)),
        name="sample_attn_tail",
    )(q, k, v, x2, *weights)


def _block_diag_heads(w):
    h, n, _ = w.shape
    eye = jnp.eye(h, dtype=w.dtype)
    return (eye[:, None, :, None] * w[:, :, None, :]).reshape(h * n, h * n)


def kernel(x_prompt, x_sample, cache_mem_k, cache_mem_v, state_conv, state_rglru, state_gla, mem_prompt, ffn1_norm, ffn1_w_gate, ffn1_w_up, ffn1_w_down, mix_norm, w_in, conv_w, conv_b, rg_w_a, rg_b_a, rg_w_x, rg_b_x, rg_lambda, rg_out_norm, gla_w_a2, gla_b_a2, gla_out_norm, w_out, xattn_norm, mem_norm, w_cq, w_ck, w_cv, w_co, ffn2_norm, ffn2_w_gate, ffn2_w_up, ffn2_w_down, final_norm):
    bp, tp, _ = x_prompt.shape
    bs, ts, _ = x_sample.shape
    assert ts == SUBLANES, "each sample sequence must fill exactly one 8-row sublane group"
    assert tp % (PROMPT_TILES * ROW_TILE) == 0 and (bs * ts) % (FFN_IN_TILES * ROW_TILE) == 0
    assert bs % SAMPLE_ATTN_BATCH == 0 and (bp * N_MEM) % ROW_TILE == 0
    assert cache_mem_k.shape[1:] == (N_MEM, XA_HEADS, XA_HEAD_DIM) and mem_prompt.shape[1] == N_MEM
    row = lambda g: g.reshape(1, -1)
    bf = lambda w: w.astype(BF16)

    w_in_p = bf(jnp.pad(w_in, ((0, 0), (0, P_WIDTH - D_IN))))
    hp = RG_HEADS // 2
    w_gate_rg = bf(jnp.stack([
        jnp.concatenate([_block_diag_heads(rg_w_a[c * hp:(c + 1) * hp]),
                         _block_diag_heads(rg_w_x[c * hp:(c + 1) * hp])], axis=1)
        for c in range(2)]))
    w_a2_p = bf(jnp.pad(gla_w_a2, ((0, A_PAD - GLA_GATE_RANK), (0, 0))))

    mem_k, mem_v, mem_kb, mem_vb, w1_gate, w1_up, w1_down = _mem_kv(
        mem_prompt.reshape(bp * N_MEM, D_MODEL), (row(mem_norm), bf(w_ck), bf(w_cv)),
        cast=(ffn1_w_gate, ffn1_w_up, ffn1_w_down))
    ffn_in_w = (row(ffn1_norm), w1_gate, w1_up, w1_down, row(mix_norm), w_in_p)
    x1, p, w2_gate, w2_up, w2_down, w_out_b, w_cq_b, w_co_b = _ffn_in(
        x_prompt.reshape(bp * tp, D_MODEL), x_sample.reshape(bs * ts, D_MODEL), ffn_in_w,
        cast=(ffn2_w_gate, ffn2_w_up, ffn2_w_down, w_out, w_cq, w_co))
    mixer_w = (conv_w, row(conv_b), w_gate_rg, row(rg_b_a), row(rg_b_x), row(rg_lambda),
               row(rg_out_norm), w_a2_p, row(gla_b_a2), row(jnp.tile(gla_out_norm, GLA_HEADS)),
               w_out_b)
    ffn2_w = (row(ffn2_norm), w2_gate, w2_up, w2_down, row(final_norm))
    y_p, rgh_p, s_p = _prompt_tail(
        x1, p, mem_kb.reshape(bp, N_MEM, D_MODEL), mem_vb.reshape(bp, N_MEM, D_MODEL),
        mixer_w + (row(xattn_norm), w_cq_b, w_co_b) + ffn2_w, tp)
    p_p = p[:bp * tp].reshape(bp, tp, P_WIDTH)
    p_s = p[bp * tp:]
    conv_p = p_p[:, tp - (CONV_WIDTH - 1):, OFF_RG_X:OFF_RG_Y]

    econv = jnp.pad(state_conv, ((0, 0), (ts - (CONV_WIDTH - 1), 0), (0, 0))).reshape(bs * ts, RG_WIDTH)
    eh = jnp.pad(state_rglru[:, None, :], ((0, 0), (0, ts - 1), (0, 0))).reshape(bs * ts, RG_WIDTH)
    x2_s, q_s, rgh_s, s_s = _sample_mix(
        x1, p, econv, eh, state_gla.reshape(bs, GLA_KEY_WIDTH, GLA_DV),
        mixer_w + (row(xattn_norm), w_cq_b), ts)
    y_s = _sample_attn_tail(q_s, _head_interleaved_rows(cache_mem_k), _head_interleaved_rows(cache_mem_v),
                            x2_s, (w_co_b,) + ffn2_w, ts)
    conv_s = p_s.reshape(bs, ts, P_WIDTH)[:, ts - (CONV_WIDTH - 1):, OFF_RG_X:OFF_RG_Y]

    return (y_p.reshape(bp, tp, D_MODEL), y_s.reshape(bs, ts, D_MODEL),
            mem_k.reshape(bp, N_MEM, XA_HEADS, XA_HEAD_DIM),
            mem_v.reshape(bp, N_MEM, XA_HEADS, XA_HEAD_DIM),
            conv_p, rgh_p[:, 0, :], s_p.reshape(bp, GLA_HEADS, GLA_DK, GLA_DV),
            conv_s, rgh_s, s_s.reshape(bs, GLA_HEADS, GLA_DK, GLA_DV))
```

```python
import functools

import jax
import jax.numpy as jnp
import numpy as np
from jax import lax
from jax.experimental import pallas as pl
from jax.experimental.pallas import tpu as pltpu

F32 = jnp.float32
BF16 = jnp.bfloat16

D_MODEL = 1024
D_FF = 2816
RG_WIDTH = 512
RG_HEADS = 8
RG_HEAD_DIM = 64
CONV_WIDTH = 4
RG_C = 8.0
GLA_WIDTH = 512
GLA_HEADS = 4
GLA_DV = 128
GLA_DK = 64
GLA_KEY_WIDTH = 256
GLA_GATE_RANK = 16
GLA_GATE_NORMALIZER = 16.0
GLA_CHUNK = 32
N_MEM = 256
XA_HEADS = 4
XA_HEAD_DIM = 256
EPS = 1e-6
MASKED_SCORE = -np.inf

OFF_RG_X = 0
OFF_RG_Y = 512
OFF_Q = 1024
OFF_K = 1280
OFF_V = 1536
OFF_G = 2048
OFF_A = 2560
D_IN = 2576

LANES = 128
SUBLANES = 8
BF16_SUBLANES = 16
A_PAD = LANES
P_WIDTH = OFF_A + A_PAD
PAIR_K = 2 * GLA_DK
PAIR_V = 2 * GLA_DV
VMEM_LIMIT = 58 * 1024 * 1024

ROW_TILE = 256
FFN_IN_TILES = 2
PROMPT_TILES = 2
FF_CHUNK = 256
FF_DOWN_GROUP = 4
FFN2_PLACEMENT = (1, 1, 1, 1, 1, 1, 1, 1, 1, 1, 1)
SAMPLE_ATTN_BATCH = 8


def _rms(x, g):
    return x * lax.rsqrt(jnp.mean(x * x, axis=-1, keepdims=True) + EPS) * g


def _mm(a, w):
    return jnp.dot(a.astype(BF16), w, preferred_element_type=F32)


def _mm_nt(a, b):
    return lax.dot_general(a.astype(BF16), b.astype(BF16), (((1,), (1,)), ((), ())),
                           preferred_element_type=F32)


def _mm_tn(a, b):
    return lax.dot_general(a.astype(BF16), b.astype(BF16), (((0,), (0,)), ((), ())),
                           preferred_element_type=F32)


def _silu(x):
    return x * jax.nn.sigmoid(x)


def _gelu_tanh(x):
    c = np.float32(np.sqrt(2.0 / np.pi))
    return x * (0.5 * (1.0 + jnp.tanh(c * (x + 0.044715 * (x * x * x)))))


def _softplus(x):
    return jnp.maximum(x, 0.0) + jnp.log1p(jnp.exp(-jnp.abs(x)))


def _ffn(x, norm_ref, wg_ref, wu_ref, wd_ref):
    h = _rms(x, norm_ref[...]).astype(BF16)
    g = jnp.dot(h, wg_ref[...], preferred_element_type=F32)
    u = jnp.dot(h, wu_ref[...], preferred_element_type=F32)
    return _mm(_silu(g) * u, wd_ref[...])


def _last_rows(buf_ref, x, seg):
    n = x.shape[0] // seg
    outs = []
    for j in range(x.shape[1] // LANES):
        buf_ref[j] = x[:, j * LANES:(j + 1) * LANES]
        outs.append(buf_ref[j, pl.ds(seg - 1, n, stride=seg), :])
    return jnp.concatenate(outs, axis=1)


def _ffn_pieces(x, norm_ref, wg_ref, wu_ref, wd_ref, out):
    h = _rms(x, norm_ref[...]).astype(BF16)
    acc = None
    acts = []
    n = D_FF // FF_CHUNK
    for c in range(n):
        cols = slice(c * FF_CHUNK, (c + 1) * FF_CHUNK)
        g = jnp.dot(h, wg_ref[:, cols], preferred_element_type=F32)
        u = jnp.dot(h, wu_ref[:, cols], preferred_element_type=F32)
        acts.append((_silu(g) * u).astype(BF16))
        if len(acts) == FF_DOWN_GROUP or c == n - 1:
            lo = (c + 1 - len(acts)) * FF_CHUNK
            part = jnp.dot(jnp.concatenate(acts, axis=1), wd_ref[lo:(c + 1) * FF_CHUNK, :],
                           preferred_element_type=F32)
            acc = part if acc is None else acc + part
            acts = []
        yield
    out.append(acc)


def _shift_rows(x, s):
    return pltpu.roll(x, s, axis=0)


def _row_in_segment(shape, seg):
    return lax.broadcasted_iota(jnp.int32, shape, 0) & (seg - 1)


def _segment_cumsum(x, seg):
    pos = _row_in_segment(x.shape, seg)
    s = 1
    while s < seg:
        x = jnp.where(pos >= s, x + _shift_rows(x, s), x)
        s *= 2
    return x


def _segment_affine_scan(a, u, seg):
    pos = _row_in_segment(a.shape, seg)
    s = 1
    while s < seg:
        m = pos >= s
        u = jnp.where(m, a * _shift_rows(u, s) + u, u)
        a = jnp.where(m, a * _shift_rows(a, s), a)
        s *= 2
    return a, u


def _conv(xb, sh1, sh2, sh3, cw_ref, cb_ref):
    y = cb_ref[...] + sh3 * cw_ref[0:1, :]
    y = y + sh2 * cw_ref[1:2, :]
    y = y + sh1 * cw_ref[2:3, :]
    return y + xb * cw_ref[3:4, :]


def _rg_gates(xc, wgate_ref, ba_ref, bx_ref):
    half = RG_WIDTH // 2
    r, i = [], []
    for c in range(2):
        z = _mm(xc[:, c * half:(c + 1) * half], wgate_ref[c])
        r.append(z[:, :half])
        i.append(z[:, half:])
    r = jax.nn.sigmoid(jnp.concatenate(r, axis=1) + ba_ref[...])
    i = jax.nn.sigmoid(jnp.concatenate(i, axis=1) + bx_ref[...])
    return r, i


def _rg_decay_input(xc, r, i, lam_ref):
    log_a = (-RG_C * _softplus(-lam_ref[...])) * r
    a = jnp.exp(log_a)
    mult = jnp.sqrt(-jnp.tanh(log_a) * (a * a + 1.0))
    return a, mult * (i * xc)


def _gla_log_decay(a_low, wa2_ref, ba2_ref):
    z = _mm(a_low, wa2_ref[...]) + ba2_ref[...]
    return (-_softplus(-z)) / GLA_GATE_NORMALIZER


def _gla_intra(qt, kt, v, chunk):
    t = qt.shape[0]
    shift = int(np.log2(chunk))
    ri = lax.broadcasted_iota(jnp.int32, (t, t), 0)
    ci = lax.broadcasted_iota(jnp.int32, (t, t), 1)
    causal = (ri >= ci) & ((ri >> shift) == (ci >> shift))
    lane = lax.broadcasted_iota(jnp.int32, (t, PAIR_K), 1)
    outs = []
    for h in range(GLA_HEADS):
        p = h // 2
        qp = qt[:, p * PAIR_K:(p + 1) * PAIR_K]
        kp = kt[:, p * PAIR_K:(p + 1) * PAIR_K]
        mine = (lane >= GLA_DK) if (h % 2) else (lane < GLA_DK)
        s = _mm_nt(jnp.where(mine, qp, 0.0), kp)
        attn = jnp.where(causal, s, 0.0)
        outs.append(_mm(attn, v[:, h * GLA_DV:(h + 1) * GLA_DV].astype(BF16)))
    return outs


def _pair_blockdiag_mask():
    r = lax.broadcasted_iota(jnp.int32, (PAIR_K, PAIR_V), 0)
    c = lax.broadcasted_iota(jnp.int32, (PAIR_K, PAIR_V), 1)
    return (r < GLA_DK) == (c < GLA_DV)


def _gla_chunk_updates(k, v, b, b_last, chunk):
    bd_mask = _pair_blockdiag_mask()
    kv = {}
    for c in range(k.shape[0] // chunk):
        rows = slice(c * chunk, (c + 1) * chunk)
        for p in range(2):
            kl = slice(p * PAIR_K, (p + 1) * PAIR_K)
            kd = k[rows, kl] * jnp.exp(b_last[c:c + 1, kl] - b[rows, kl])
            kv[c, p] = jnp.where(bd_mask, _mm_tn(kd, v[rows, p * PAIR_V:(p + 1) * PAIR_V]), 0.0)
    return kv


def _gla_combine(intra, qt, states, chunk):
    n_chunks = qt.shape[0] // chunk
    heads = []
    for p in range(2):
        kl = slice(p * PAIR_K, (p + 1) * PAIR_K)
        inter = jnp.concatenate(
            [_mm(qt[c * chunk:(c + 1) * chunk, kl], states[c, p]) for c in range(n_chunks)],
            axis=0)
        heads.append(intra[2 * p] + inter[:, :GLA_DV])
        heads.append(intra[2 * p + 1] + inter[:, GLA_DV:])
    return heads


def _gla_out(o_heads, g, norm_ref):
    outs = []
    for h in range(GLA_HEADS):
        outs.append(_rms(o_heads[h], norm_ref[:, h * GLA_DV:(h + 1) * GLA_DV]))
    return jnp.concatenate(outs, axis=1) * _silu(g)


def _softmax_rows(s):
    m = jnp.max(s, axis=-1, keepdims=True)
    e = jnp.exp(s - m)
    return e * (1.0 / jnp.sum(e, axis=-1, keepdims=True))


def _split_p(p_ref, rows=slice(None)):
    xb = p_ref[rows, OFF_RG_X:OFF_RG_Y]
    yb = p_ref[rows, OFF_RG_Y:OFF_Q]
    q = p_ref[rows, OFF_Q:OFF_K]
    k = p_ref[rows, OFF_K:OFF_V]
    v = p_ref[rows, OFF_V:OFF_G]
    g = p_ref[rows, OFF_G:OFF_A]
    a_low = p_ref[rows, OFF_A:P_WIDTH]
    return xb, yb, q, k, v, g, a_low


def _gla_prepare(q, k, log_a, chunk):
    b = _segment_cumsum(log_a, chunk)
    qt = (q * (GLA_DK ** -0.5)) * jnp.exp(b)
    kt = k * jnp.exp(-b)
    return b, qt, kt


def _mix_project(rg_out, gla_out, wout_ref):
    return (_mm(rg_out, wout_ref[0:RG_WIDTH, :]) + _mm(gla_out, wout_ref[RG_WIDTH:, :]))


def _ffn_in_kernel(xa_ref, xb_ref, n1_ref, wg_ref, wu_ref, wd_ref, n2_ref, win_ref, *refs, steps_a):
    from_a = pl.program_id(0) < steps_a
    n_cast = (len(refs) - 2) // 2
    cast_src, (x1_ref, p_ref), cast_dst = refs[:n_cast], refs[n_cast:n_cast + 2], refs[n_cast + 2:]
    for sub in range(xa_ref.shape[0] // ROW_TILE):
        rows = slice(sub * ROW_TILE, (sub + 1) * ROW_TILE)
        x = jnp.where(from_a, xa_ref[rows, :], xb_ref[rows, :])
        x1 = x + 0.5 * _ffn(x, n1_ref, wg_ref, wu_ref, wd_ref)
        x1_ref[rows, :] = x1
        p_ref[rows, :] = _mm(_rms(x1, n2_ref[...]), win_ref[...])
    for src, dst in zip(cast_src, cast_dst):
        dst[...] = src[...].astype(BF16)


def _mem_kv_kernel(m_ref, n_ref, wk_ref, wv_ref, *refs):
    n_cast = (len(refs) - 4) // 2
    cast_src, (k_ref, v_ref, kb_ref, vb_ref), cast_dst = refs[:n_cast], refs[n_cast:n_cast + 4], refs[n_cast + 4:]
    h = _rms(m_ref[...], n_ref[...]).astype(BF16)
    k = jnp.dot(h, wk_ref[...], preferred_element_type=F32)
    v = jnp.dot(h, wv_ref[...], preferred_element_type=F32)
    for h in range(XA_HEADS):
        sl = slice(h * XA_HEAD_DIM, (h + 1) * XA_HEAD_DIM)
        k_ref[:, h, :] = k[:, sl]
        v_ref[:, h, :] = v[:, sl]
    kb_ref[...] = k.astype(BF16)
    vb_ref[...] = v.astype(BF16)
    for src, dst in zip(cast_src, cast_dst):
        dst[...] = src[...].astype(BF16)


def _prompt_tail_kernel(
        x1_ref, p_ref, kb_ref, vb_ref,
        cw_ref, cb_ref, wgate_ref, ba_ref, bx_ref, lam_ref, rgn_ref,
        wa2_ref, ba2_ref, glan_ref, wout_ref,
        xn_ref, wcq_ref, wco_ref, n2_ref, wg_ref, wu_ref, wd_ref, fn_ref,
        y_ref, rgh_ref, s_ref,
        xpad_ref, h_ref, sbd_ref, b_ref, x3_ref, *, steps_per_seq, n_steps):
    step = pl.program_id(0)
    t_idx = jnp.minimum(step, n_steps - 1) % steps_per_seq
    slot = step % 2
    tt = ROW_TILE
    subs = x1_ref.shape[0] // tt
    n_chunks = tt // GLA_CHUNK

    @pl.when(step == 0)
    def _():
        x3_ref[1] = jnp.zeros((subs, tt, D_MODEL), F32)

    @pl.when(t_idx == 0)
    def _():
        xpad_ref[0:SUBLANES, :] = jnp.zeros((SUBLANES, RG_WIDTH), F32)
        h_ref[...] = jnp.zeros_like(h_ref)
        sbd_ref[...] = jnp.zeros_like(sbd_ref)

    for sub in range(subs):
        rows = slice(sub * tt, (sub + 1) * tt)
        x3_prev = x3_ref[1 - slot, sub]
        ffn_out = []
        ffn = _ffn_pieces(x3_prev, n2_ref, wg_ref, wu_ref, wd_ref, ffn_out)
        placement = iter(FFN2_PLACEMENT)

        def emit_ffn():
            for _ in range(next(placement)):
                next(ffn)

        emit_ffn()

        xb, yb, q, k, v, g, a_low = _split_p(p_ref, rows)

        xpad_ref[SUBLANES:SUBLANES + tt, :] = xb
        sh1 = xpad_ref[SUBLANES - 1:SUBLANES - 1 + tt, :]
        sh2 = xpad_ref[SUBLANES - 2:SUBLANES - 2 + tt, :]
        sh3 = xpad_ref[SUBLANES - 3:SUBLANES - 3 + tt, :]
        xc = _conv(xb, sh1, sh2, sh3, cw_ref, cb_ref)
        xpad_ref[0:SUBLANES, :] = xb[tt - SUBLANES:tt, :]
        emit_ffn()
        r, i = _rg_gates(xc, wgate_ref, ba_ref, bx_ref)
        a, u = _rg_decay_input(xc, r, i, lam_ref)
        emit_ffn()
        a_grp, h_grp = _segment_affine_scan(a, u, SUBLANES)
        emit_ffn()
        carry = h_ref[0:1, :]
        groups = []
        for gi in range(tt // SUBLANES):
            grp = slice(gi * SUBLANES, (gi + 1) * SUBLANES)
            groups.append(h_grp[grp, :] + a_grp[grp, :] * carry)
            carry = groups[-1][SUBLANES - 1:SUBLANES, :]
        hs = jnp.concatenate(groups, axis=0)
        h_ref[...] = jnp.broadcast_to(carry, h_ref.shape)
        rg_out = _rms(hs * _gelu_tanh(yb), rgn_ref[...]).astype(BF16)
        emit_ffn()

        log_a = _gla_log_decay(a_low, wa2_ref, ba2_ref)
        b, qt, kt = _gla_prepare(q, k, log_a, GLA_CHUNK)
        emit_ffn()
        b_last = _last_rows(b_ref, b, GLA_CHUNK)
        dec_t = jnp.exp(b_last).T
        kv = _gla_chunk_updates(k, v, b, b_last, GLA_CHUNK)
        states = {}
        for p in range(2):
            s_bd = sbd_ref[p]
            for c in range(n_chunks):
                states[c, p] = s_bd.astype(BF16)
                s_bd = s_bd * dec_t[p * PAIR_K:(p + 1) * PAIR_K, c:c + 1] + kv[c, p]
            sbd_ref[p] = s_bd
        emit_ffn()
        intra = _gla_intra(qt, kt, v, GLA_CHUNK)
        emit_ffn()
        gla_out = _gla_out(_gla_combine(intra, qt, states, GLA_CHUNK), g, glan_ref).astype(BF16)
        emit_ffn()

        x2 = x1_ref[rows, :] + _mix_project(rg_out, gla_out, wout_ref)

        qx = _mm(_rms(x2, xn_ref[...]), wcq_ref[...])
        emit_ffn()
        head = lambda h: slice(h * XA_HEAD_DIM, (h + 1) * XA_HEAD_DIM)
        s = jnp.concatenate([_mm_nt(qx[:, head(h)], kb_ref[:, head(h)]) for h in range(XA_HEADS)], axis=0)
        pr = _softmax_rows(s * (XA_HEAD_DIM ** -0.5))
        emit_ffn()
        o = jnp.concatenate([_mm(pr[h * tt:(h + 1) * tt, :], vb_ref[:, head(h)]) for h in range(XA_HEADS)], axis=1)
        x3_ref[slot, sub] = x2 + _mm(o, wco_ref[...])
        for _ in ffn:
            pass
        y_ref[rows, :] = _rms(x3_prev + 0.5 * ffn_out[0], fn_ref[...])

    @pl.when((t_idx == steps_per_seq - 1) & (step < n_steps))
    def _():
        rgh_ref[...] = h_ref[...]
        row = lax.broadcasted_iota(jnp.int32, (PAIR_K, GLA_DV), 0)
        for p in range(2):
            s_bd = sbd_ref[p]
            s_ref[p * PAIR_K:(p + 1) * PAIR_K, :] = jnp.where(
                row < GLA_DK, s_bd[:, :GLA_DV], s_bd[:, GLA_DV:])


def _sample_mix_kernel(
        x1_ref, p_ref, econv_ref, eh_ref, s0_ref,
        cw_ref, cb_ref, wgate_ref, ba_ref, bx_ref, lam_ref, rgn_ref,
        wa2_ref, ba2_ref, glan_ref, wout_ref, xn_ref, wcq_ref,
        x2_ref, q_ref, rgh_ref, s_ref,
        hs_ref, b_ref, *, seq):
    rows_n = x1_ref.shape[0]
    nb = rows_n // seq

    xb, yb, q, k, v, g, a_low = _split_p(p_ref)

    pos = _row_in_segment(xb.shape, seq)
    econv = econv_ref[...]
    sh = []
    for j in range(1, CONV_WIDTH):
        sh.append(jnp.where(pos >= j, _shift_rows(xb, j), _shift_rows(econv, rows_n - seq + j)))
    xc = _conv(xb, sh[0], sh[1], sh[2], cw_ref, cb_ref)
    r, i = _rg_gates(xc, wgate_ref, ba_ref, bx_ref)
    a, u = _rg_decay_input(xc, r, i, lam_ref)
    u = u + a * eh_ref[...]
    _, hs = _segment_affine_scan(a, u, seq)
    rgh_ref[...] = _last_rows(hs_ref, hs, seq)
    rg_out = _rms(hs * _gelu_tanh(yb), rgn_ref[...]).astype(BF16)

    log_a = _gla_log_decay(a_low, wa2_ref, ba2_ref)
    b, qt, kt = _gla_prepare(q, k, log_a, seq)
    intra = _gla_intra(qt, kt, v, seq)
    b_last = _last_rows(b_ref, b, seq)
    dec_t = jnp.exp(b_last).T
    kv = _gla_chunk_updates(k, v, b, b_last, seq)
    row = lax.broadcasted_iota(jnp.int32, (PAIR_K, GLA_DV), 0)
    top = row < GLA_DK
    states = {}
    for c in range(nb):
        for p in range(2):
            kl = slice(p * PAIR_K, (p + 1) * PAIR_K)
            s_pair = s0_ref[c, kl, :]
            s_bd = jnp.concatenate([jnp.where(top, s_pair, 0.0), jnp.where(top, 0.0, s_pair)], axis=1)
            states[c, p] = s_bd.astype(BF16)
            s_new = s_bd * dec_t[kl, c:c + 1] + kv[c, p]
            s_ref[c, kl, :] = jnp.where(top, s_new[:, :GLA_DV], s_new[:, GLA_DV:])
    gla_out = _gla_out(_gla_combine(intra, qt, states, seq), g, glan_ref).astype(BF16)

    x2 = x1_ref[...] + _mix_project(rg_out, gla_out, wout_ref)
    x2_ref[...] = x2
    q_ref[...] = _mm(_rms(x2, xn_ref[...]), wcq_ref[...])


def _sample_attn_tail_kernel(q_ref, k_ref, v_ref, x2_ref, wco_ref, n2_ref, wg_ref, wu_ref, wd_ref, fn_ref,
                             y_ref, o_ref, *, seq):
    tiles = XA_HEAD_DIM // LANES
    group = tiles * XA_HEADS
    r = N_MEM * group
    nb = k_ref.shape[0] // r
    hs = XA_HEADS * seq
    lane = lax.broadcasted_iota(jnp.int32, (nb * hs, r), 1)
    head = (lax.broadcasted_iota(jnp.int32, (nb * hs, r), 0) // seq) & (XA_HEADS - 1)
    own = (lane & (group - 1)) == head
    s = []
    for j in range(nb):
        q = q_ref[j * seq:(j + 1) * seq, :]
        q_parts = jnp.concatenate(
            [q[:, c * LANES:(c + 1) * LANES] for c in range(XA_HEADS * tiles)], axis=0)
        part = _mm_nt(q_parts, k_ref[j * r:(j + 1) * r, :])
        for h in range(XA_HEADS):
            acc = part[h * tiles * seq:(h * tiles + 1) * seq, :]
            for c in range(1, tiles):
                blk = part[(h * tiles + c) * seq:(h * tiles + c + 1) * seq, :]
                acc = acc + pltpu.roll(blk, r - c * XA_HEADS, axis=1)
            s.append(acc)
    p = _softmax_rows(jnp.where(own, jnp.concatenate(s, axis=0) * (XA_HEAD_DIM ** -0.5), MASKED_SCORE))
    for j in range(nb):
        pj = p[j * hs:(j + 1) * hs, :]
        p_all = jnp.concatenate(
            [pj] + [pltpu.roll(pj, c * XA_HEADS, axis=1) for c in range(1, tiles)], axis=0)
        o = _mm(p_all, v_ref[j * r:(j + 1) * r, :].astype(BF16))
        for c in range(tiles):
            for h in range(XA_HEADS):
                col = h * XA_HEAD_DIM + c * LANES
                o_ref[j * seq:(j + 1) * seq, col:col + LANES] = (
                    o[(c * XA_HEADS + h) * seq:(c * XA_HEADS + h + 1) * seq, :])
    x3 = x2_ref[...] + _mm(o_ref[...], wco_ref[...])
    x4 = x3 + 0.5 * _ffn(x3, n2_ref, wg_ref, wu_ref, wd_ref)
    y_ref[...] = _rms(x4, fn_ref[...])


def _resident(arr):
    nd = arr.ndim
    return pl.BlockSpec(arr.shape, lambda *_: (0,) * nd, pipeline_mode=pl.Buffered(1))


def _params(sem):
    return pltpu.CompilerParams(dimension_semantics=sem, vmem_limit_bytes=VMEM_LIMIT)


def _slab_spec(w, steps):
    per, revisit = w.shape[0] // steps, 1
    while per % BF16_SUBLANES:
        per, revisit = per * 2, revisit * 2
    return pl.BlockSpec((per, w.shape[1]), lambda i: (jnp.minimum(i, steps - 1) // revisit, 0))


def _ffn_in(xa, xb, weights, cast=()):
    block = FFN_IN_TILES * ROW_TILE
    steps_a, steps_b = xa.shape[0] // block, xb.shape[0] // block
    rows = xa.shape[0] + xb.shape[0]
    spec_a = pl.BlockSpec((block, D_MODEL), lambda i: (jnp.minimum(i, steps_a - 1), 0))
    spec_b = pl.BlockSpec((block, D_MODEL), lambda i: (jnp.maximum(i - steps_a, 0), 0))
    row_spec = lambda w: pl.BlockSpec((block, w), lambda i: (i, 0))
    slabs = [_slab_spec(w, steps_a) for w in cast]
    return pl.pallas_call(
        functools.partial(_ffn_in_kernel, steps_a=steps_a),
        grid=(steps_a + steps_b,),
        in_specs=[spec_a, spec_b] + [_resident(w) for w in weights] + slabs,
        out_specs=[row_spec(D_MODEL), row_spec(P_WIDTH)] + slabs,
        out_shape=[jax.ShapeDtypeStruct((rows, D_MODEL), F32),
                   jax.ShapeDtypeStruct((rows, P_WIDTH), F32)]
                  + [jax.ShapeDtypeStruct(w.shape, BF16) for w in cast],
        compiler_params=_params(("arbitrary",)),
        name="ffn_in",
    )(xa, xb, *weights, *cast)


def _mem_kv(mem, weights, cast=()):
    rows = mem.shape[0]
    steps = rows // ROW_TILE
    row_spec = pl.BlockSpec((ROW_TILE, D_MODEL), lambda i: (i, 0))
    head_spec = pl.BlockSpec((ROW_TILE, XA_HEADS, XA_HEAD_DIM), lambda i: (i, 0, 0))
    slabs = [_slab_spec(w, steps) for w in cast]
    return pl.pallas_call(
        _mem_kv_kernel,
        grid=(steps,),
        in_specs=[row_spec] + [_resident(w) for w in weights] + slabs,
        out_specs=[head_spec] * 2 + [row_spec] * 2 + slabs,
        out_shape=[jax.ShapeDtypeStruct((rows, XA_HEADS, XA_HEAD_DIM), F32)] * 2
                  + [jax.ShapeDtypeStruct((rows, D_MODEL), BF16)] * 2
                  + [jax.ShapeDtypeStruct(w.shape, BF16) for w in cast],
        compiler_params=_params(("arbitrary",)),
        name="mem_kv",
    )(mem, *weights, *cast)


def _prompt_tail(x1, p, kb, vb, weights, seq):
    batch = kb.shape[0]
    tt = ROW_TILE
    block = PROMPT_TILES * tt
    steps_per_seq = seq // block
    n_steps = batch * steps_per_seq
    cur = lambda s: jnp.minimum(s, n_steps - 1)
    prev = lambda s: jnp.maximum(s - 1, 0)
    tok = lambda w: pl.BlockSpec((block, w), lambda s: (cur(s), 0))
    per_seq = lambda r, w: pl.BlockSpec((None, r, w), lambda s: (cur(s) // steps_per_seq, 0, 0))
    out_tok = pl.BlockSpec((block, D_MODEL), lambda s: (prev(s), 0))
    return pl.pallas_call(
        functools.partial(_prompt_tail_kernel, steps_per_seq=steps_per_seq, n_steps=n_steps),
        grid=(n_steps + 1,),
        in_specs=[tok(D_MODEL), tok(P_WIDTH), per_seq(N_MEM, D_MODEL), per_seq(N_MEM, D_MODEL)]
                 + [_resident(w) for w in weights],
        out_specs=[out_tok, per_seq(SUBLANES, RG_WIDTH), per_seq(GLA_KEY_WIDTH, GLA_DV)],
        out_shape=[jax.ShapeDtypeStruct((batch * seq, D_MODEL), F32),
                   jax.ShapeDtypeStruct((batch, SUBLANES, RG_WIDTH), F32),
                   jax.ShapeDtypeStruct((batch, GLA_KEY_WIDTH, GLA_DV), F32)],
        scratch_shapes=[pltpu.VMEM((SUBLANES + tt, RG_WIDTH), F32),
                        pltpu.VMEM((SUBLANES, RG_WIDTH), F32),
                        pltpu.VMEM((2, PAIR_K, PAIR_V), F32),
                        pltpu.VMEM((GLA_KEY_WIDTH // LANES, tt, LANES), F32),
                        pltpu.VMEM((2, PROMPT_TILES, tt, D_MODEL), F32)],
        compiler_params=_params(("arbitrary",)),
        name="prompt_tail",
    )(x1, p, kb, vb, *weights)


def _sample_mix(x1, p, econv, eh, s0, weights, seq):
    rows = econv.shape[0]
    first = (x1.shape[0] - rows) // ROW_TILE
    nb = ROW_TILE // seq
    row_spec = lambda w: pl.BlockSpec((ROW_TILE, w), lambda i: (i, 0))
    tail_spec = lambda w: pl.BlockSpec((ROW_TILE, w), lambda i: (first + i, 0))
    state_spec = pl.BlockSpec((nb, GLA_KEY_WIDTH, GLA_DV), lambda i: (i, 0, 0))
    return pl.pallas_call(
        functools.partial(_sample_mix_kernel, seq=seq),
        grid=(rows // ROW_TILE,),
        in_specs=[tail_spec(D_MODEL), tail_spec(P_WIDTH), row_spec(RG_WIDTH), row_spec(RG_WIDTH),
                  state_spec] + [_resident(w) for w in weights],
        out_specs=[row_spec(D_MODEL), row_spec(D_MODEL),
                   pl.BlockSpec((nb, RG_WIDTH), lambda i: (i, 0)), state_spec],
        out_shape=[jax.ShapeDtypeStruct((rows, D_MODEL), F32),
                   jax.ShapeDtypeStruct((rows, D_MODEL), F32),
                   jax.ShapeDtypeStruct((rows // seq, RG_WIDTH), F32),
                   jax.ShapeDtypeStruct((rows // seq, GLA_KEY_WIDTH, GLA_DV), F32)],
        scratch_shapes=[pltpu.VMEM((RG_WIDTH // LANES, ROW_TILE, LANES), F32),
                        pltpu.VMEM((GLA_KEY_WIDTH // LANES, ROW_TILE, LANES), F32)],
        compiler_params=_params(("parallel",)),
        name="sample_mix",
    )(x1, p, econv, eh, s0, *weights)


def _head_interleaved_rows(x):
    b, m, h, dh = x.shape
    tiles = dh // LANES
    return (x.reshape(b, m, h, tiles, LANES).transpose(0, 1, 3, 2, 4)
            .reshape(b * m * tiles * h, LANES))


def _sample_attn_tail(q, k, v, x2, weights, seq):
    rows = q.shape[0]
    nb = SAMPLE_ATTN_BATCH
    row_spec = pl.BlockSpec((nb * seq, D_MODEL), lambda i: (i, 0))
    kv_rows = k.shape[0] // (rows // seq)
    kv_spec = pl.BlockSpec((nb * kv_rows, LANES), lambda i: (i, 0))
    return pl.pallas_call(
        functools.partial(_sample_attn_tail_kernel, seq=seq),
        grid=(rows // (nb * seq),),
        in_specs=[row_spec, kv_spec, kv_spec, row_spec] + [_resident(w) for w in weights],
        out_specs=row_spec,
        out_shape=jax.ShapeDtypeStruct((rows, D_MODEL), F32),
        scratch_shapes=[pltpu.VMEM((nb * seq, D_MODEL), F32)],
        compiler_params=_params(("parallel",)),
        name="sample_attn_tail",
    )(q, k, v, x2, *weights)


def _block_diag_heads(w):
    h, n, _ = w.shape
    eye = jnp.eye(h, dtype=w.dtype)
    return (eye[:, None, :, None] * w[:, :, None, :]).reshape(h * n, h * n)


def kernel(x_prompt, x_sample, cache_mem_k, cache_mem_v, state_conv, state_rglru, state_gla, mem_prompt, ffn1_norm, ffn1_w_gate, ffn1_w_up, ffn1_w_down, mix_norm, w_in, conv_w, conv_b, rg_w_a, rg_b_a, rg_w_x, rg_b_x, rg_lambda, rg_out_norm, gla_w_a2, gla_b_a2, gla_out_norm, w_out, xattn_norm, mem_norm, w_cq, w_ck, w_cv, w_co, ffn2_norm, ffn2_w_gate, ffn2_w_up, ffn2_w_down, final_norm):
    bp, tp, _ = x_prompt.shape
    bs, ts, _ = x_sample.shape
    assert ts == SUBLANES, "each sample sequence must fill exactly one 8-row sublane group"
    assert tp % (PROMPT_TILES * ROW_TILE) == 0 and (bs * ts) % (FFN_IN_TILES * ROW_TILE) == 0
    assert bs % SAMPLE_ATTN_BATCH == 0 and (bp * N_MEM) % ROW_TILE == 0 and tp % (bs * ts) == 0
    assert cache_mem_k.shape[1:] == (N_MEM, XA_HEADS, XA_HEAD_DIM) and mem_prompt.shape[1] == N_MEM
    row = lambda g: g.reshape(1, -1)
    bf = lambda w: w.astype(BF16)

    w_in_p = bf(jnp.pad(w_in, ((0, 0), (0, P_WIDTH - D_IN))))
    hp = RG_HEADS // 2
    w_gate_rg = bf(jnp.stack([
        jnp.concatenate([_block_diag_heads(rg_w_a[c * hp:(c + 1) * hp]),
                         _block_diag_heads(rg_w_x[c * hp:(c + 1) * hp])], axis=1)
        for c in range(2)]))
    w_a2_p = bf(jnp.pad(gla_w_a2, ((0, A_PAD - GLA_GATE_RANK), (0, 0))))

    mem_k, mem_v, mem_kb, mem_vb, w1_gate, w1_up, w1_down = _mem_kv(
        mem_prompt.reshape(bp * N_MEM, D_MODEL), (row(mem_norm), bf(w_ck), bf(w_cv)),
        cast=(ffn1_w_gate, ffn1_w_up, ffn1_w_down))
    ffn_in_w = (row(ffn1_norm), w1_gate, w1_up, w1_down, row(mix_norm), w_in_p)
    x1, p, w2_gate, w2_up, w2_down, w_out_b, w_cq_b, w_co_b = _ffn_in(
        x_prompt.reshape(bp * tp, D_MODEL), x_sample.reshape(bs * ts, D_MODEL), ffn_in_w,
        cast=(ffn2_w_gate, ffn2_w_up, ffn2_w_down, w_out, w_cq, w_co))
    mixer_w = (conv_w, row(conv_b), w_gate_rg, row(rg_b_a), row(rg_b_x), row(rg_lambda),
               row(rg_out_norm), w_a2_p, row(gla_b_a2), row(jnp.tile(gla_out_norm, GLA_HEADS)),
               w_out_b)
    ffn2_w = (row(ffn2_norm), w2_gate, w2_up, w2_down, row(final_norm))
    y_p, rgh_p, s_p = _prompt_tail(
        x1, p, mem_kb.reshape(bp, N_MEM, D_MODEL), mem_vb.reshape(bp, N_MEM, D_MODEL),
        mixer_w + (row(xattn_norm), w_cq_b, w_co_b) + ffn2_w, tp)
    chunks = p.reshape(-1, bs * ts, P_WIDTH)
    per_seq = tp // (bs * ts)
    conv_p = chunks[per_seq - 1:bp * per_seq:per_seq, bs * ts - (CONV_WIDTH - 1):, OFF_RG_X:OFF_RG_Y]
    p_s = chunks[bp * per_seq]

    econv = jnp.pad(state_conv, ((0, 0), (ts - (CONV_WIDTH - 1), 0), (0, 0))).reshape(bs * ts, RG_WIDTH)
    eh = jnp.pad(state_rglru[:, None, :], ((0, 0), (0, ts - 1), (0, 0))).reshape(bs * ts, RG_WIDTH)
    x2_s, q_s, rgh_s, s_s = _sample_mix(
        x1, p, econv, eh, state_gla.reshape(bs, GLA_KEY_WIDTH, GLA_DV),
        mixer_w + (row(xattn_norm), w_cq_b), ts)
    y_s = _sample_attn_tail(q_s, _head_interleaved_rows(cache_mem_k), _head_interleaved_rows(cache_mem_v),
                            x2_s, (w_co_b,) + ffn2_w, ts)
    conv_s = p_s.reshape(bs, ts, P_WIDTH)[:, ts - (CONV_WIDTH - 1):, OFF_RG_X:OFF_RG_Y]

    return (y_p.reshape(bp, tp, D_MODEL), y_s.reshape(bs, ts, D_MODEL),
            mem_k.reshape(bp, N_MEM, XA_HEADS, XA_HEAD_DIM),
            mem_v.reshape(bp, N_MEM, XA_HEADS, XA_HEAD_DIM),
            conv_p, rgh_p[:, 0, :], s_p.reshape(bp, GLA_HEADS, GLA_DK, GLA_DV),
            conv_s, rgh_s, s_s.reshape(bs, GLA_HEADS, GLA_DK, GLA_DV))
```

```python
import functools

import jax
import jax.numpy as jnp
import numpy as np
from jax import lax
from jax.experimental import pallas as pl
from jax.experimental.pallas import tpu as pltpu

F32 = jnp.float32
BF16 = jnp.bfloat16

D_MODEL = 1024
D_FF = 2816
RG_WIDTH = 512
RG_HEADS = 8
RG_HEAD_DIM = 64
CONV_WIDTH = 4
RG_C = 8.0
GLA_WIDTH = 512
GLA_HEADS = 4
GLA_DV = 128
GLA_DK = 64
GLA_KEY_WIDTH = 256
GLA_GATE_RANK = 16
GLA_GATE_NORMALIZER = 16.0
GLA_CHUNK = 32
N_MEM = 256
XA_HEADS = 4
XA_HEAD_DIM = 256
EPS = 1e-6
MASKED_SCORE = -np.inf

OFF_RG_X = 0
OFF_RG_Y = 512
OFF_Q = 1024
OFF_K = 1280
OFF_V = 1536
OFF_G = 2048
OFF_A = 2560
D_IN = 2576

LANES = 128
SUBLANES = 8
BF16_SUBLANES = 16
A_PAD = LANES
P_WIDTH = OFF_A + A_PAD
PAIR_K = 2 * GLA_DK
PAIR_V = 2 * GLA_DV
VMEM_LIMIT = 58 * 1024 * 1024

ROW_TILE = 256
FFN_IN_TILES = 2
PROMPT_TILES = 2
FF_CHUNK = 256
FF_DOWN_GROUP = 4
FFN2_PLACEMENT = (1, 1, 1, 1, 1, 1, 1, 1, 1, 1, 1)
SAMPLE_ATTN_BATCH = 8


def _rms(x, g):
    return x * lax.rsqrt(jnp.mean(x * x, axis=-1, keepdims=True) + EPS) * g


def _mm(a, w):
    return jnp.dot(a.astype(BF16), w, preferred_element_type=F32)


def _mm_nt(a, b):
    return lax.dot_general(a.astype(BF16), b.astype(BF16), (((1,), (1,)), ((), ())),
                           preferred_element_type=F32)


def _mm_tn(a, b):
    return lax.dot_general(a.astype(BF16), b.astype(BF16), (((0,), (0,)), ((), ())),
                           preferred_element_type=F32)


def _silu(x):
    return x * jax.nn.sigmoid(x)


def _gelu_tanh(x):
    c = np.float32(np.sqrt(2.0 / np.pi))
    return x * (0.5 * (1.0 + jnp.tanh(c * (x + 0.044715 * (x * x * x)))))


def _softplus(x):
    return jnp.maximum(x, 0.0) + jnp.log1p(jnp.exp(-jnp.abs(x)))


def _ffn(x, norm_ref, wg_ref, wu_ref, wd_ref):
    h = _rms(x, norm_ref[...]).astype(BF16)
    g = jnp.dot(h, wg_ref[...], preferred_element_type=F32)
    u = jnp.dot(h, wu_ref[...], preferred_element_type=F32)
    return _mm(_silu(g) * u, wd_ref[...])


def _last_rows(buf_ref, x, seg):
    n = x.shape[0] // seg
    outs = []
    for j in range(x.shape[1] // LANES):
        buf_ref[j] = x[:, j * LANES:(j + 1) * LANES]
        outs.append(buf_ref[j, pl.ds(seg - 1, n, stride=seg), :])
    return jnp.concatenate(outs, axis=1)


def _ffn_pieces(x, norm_ref, wg_ref, wu_ref, wd_ref, out):
    h = _rms(x, norm_ref[...]).astype(BF16)
    acc = None
    acts = []
    n = D_FF // FF_CHUNK
    for c in range(n):
        cols = slice(c * FF_CHUNK, (c + 1) * FF_CHUNK)
        g = jnp.dot(h, wg_ref[:, cols], preferred_element_type=F32)
        u = jnp.dot(h, wu_ref[:, cols], preferred_element_type=F32)
        acts.append((_silu(g) * u).astype(BF16))
        if len(acts) == FF_DOWN_GROUP or c == n - 1:
            lo = (c + 1 - len(acts)) * FF_CHUNK
            part = jnp.dot(jnp.concatenate(acts, axis=1), wd_ref[lo:(c + 1) * FF_CHUNK, :],
                           preferred_element_type=F32)
            acc = part if acc is None else acc + part
            acts = []
        yield
    out.append(acc)


def _shift_rows(x, s):
    return pltpu.roll(x, s, axis=0)


def _row_in_segment(shape, seg):
    return lax.broadcasted_iota(jnp.int32, shape, 0) & (seg - 1)


def _segment_cumsum(x, seg):
    pos = _row_in_segment(x.shape, seg)
    s = 1
    while s < seg:
        x = jnp.where(pos >= s, x + _shift_rows(x, s), x)
        s *= 2
    return x


def _segment_affine_scan(a, u, seg):
    pos = _row_in_segment(a.shape, seg)
    s = 1
    while s < seg:
        m = pos >= s
        u = jnp.where(m, a * _shift_rows(u, s) + u, u)
        a = jnp.where(m, a * _shift_rows(a, s), a)
        s *= 2
    return a, u


def _conv(xb, sh1, sh2, sh3, cw_ref, cb_ref):
    y = cb_ref[...] + sh3 * cw_ref[0:1, :]
    y = y + sh2 * cw_ref[1:2, :]
    y = y + sh1 * cw_ref[2:3, :]
    return y + xb * cw_ref[3:4, :]


def _rg_gates(xc, wgate_ref, ba_ref, bx_ref):
    half = RG_WIDTH // 2
    r, i = [], []
    for c in range(2):
        z = _mm(xc[:, c * half:(c + 1) * half], wgate_ref[c])
        r.append(z[:, :half])
        i.append(z[:, half:])
    r = jax.nn.sigmoid(jnp.concatenate(r, axis=1) + ba_ref[...])
    i = jax.nn.sigmoid(jnp.concatenate(i, axis=1) + bx_ref[...])
    return r, i


def _rg_decay_input(xc, r, i, lam_ref):
    log_a = (-RG_C * _softplus(-lam_ref[...])) * r
    a = jnp.exp(log_a)
    mult = jnp.sqrt(-jnp.tanh(log_a) * (a * a + 1.0))
    return a, mult * (i * xc)


def _gla_log_decay(a_low, wa2_ref, ba2_ref):
    z = _mm(a_low, wa2_ref[...]) + ba2_ref[...]
    return (-_softplus(-z)) / GLA_GATE_NORMALIZER


def _gla_intra(qt, kt, v, chunk):
    t = qt.shape[0]
    shift = int(np.log2(chunk))
    ri = lax.broadcasted_iota(jnp.int32, (t, t), 0)
    ci = lax.broadcasted_iota(jnp.int32, (t, t), 1)
    causal = (ri >= ci) & ((ri >> shift) == (ci >> shift))
    lane = lax.broadcasted_iota(jnp.int32, (t, PAIR_K), 1)
    outs = []
    for h in range(GLA_HEADS):
        p = h // 2
        qp = qt[:, p * PAIR_K:(p + 1) * PAIR_K]
        kp = kt[:, p * PAIR_K:(p + 1) * PAIR_K]
        mine = (lane >= GLA_DK) if (h % 2) else (lane < GLA_DK)
        s = _mm_nt(jnp.where(mine, qp, 0.0), kp)
        attn = jnp.where(causal, s, 0.0)
        outs.append(_mm(attn, v[:, h * GLA_DV:(h + 1) * GLA_DV].astype(BF16)))
    return outs


def _pair_blockdiag_mask():
    r = lax.broadcasted_iota(jnp.int32, (PAIR_K, PAIR_V), 0)
    c = lax.broadcasted_iota(jnp.int32, (PAIR_K, PAIR_V), 1)
    return (r < GLA_DK) == (c < GLA_DV)


def _gla_chunk_updates(k, v, b, b_last, chunk):
    bd_mask = _pair_blockdiag_mask()
    kv = {}
    for c in range(k.shape[0] // chunk):
        rows = slice(c * chunk, (c + 1) * chunk)
        for p in range(2):
            kl = slice(p * PAIR_K, (p + 1) * PAIR_K)
            kd = k[rows, kl] * jnp.exp(b_last[c:c + 1, kl] - b[rows, kl])
            kv[c, p] = jnp.where(bd_mask, _mm_tn(kd, v[rows, p * PAIR_V:(p + 1) * PAIR_V]), 0.0)
    return kv


def _gla_combine(intra, qt, states, chunk):
    n_chunks = qt.shape[0] // chunk
    heads = []
    for p in range(2):
        kl = slice(p * PAIR_K, (p + 1) * PAIR_K)
        inter = jnp.concatenate(
            [_mm(qt[c * chunk:(c + 1) * chunk, kl], states[c, p]) for c in range(n_chunks)],
            axis=0)
        heads.append(intra[2 * p] + inter[:, :GLA_DV])
        heads.append(intra[2 * p + 1] + inter[:, GLA_DV:])
    return heads


def _gla_out(o_heads, g, norm_ref):
    outs = []
    for h in range(GLA_HEADS):
        outs.append(_rms(o_heads[h], norm_ref[:, h * GLA_DV:(h + 1) * GLA_DV]))
    return jnp.concatenate(outs, axis=1) * _silu(g)


def _softmax_rows(s):
    m = jnp.max(s, axis=-1, keepdims=True)
    e = jnp.exp(s - m)
    return e * (1.0 / jnp.sum(e, axis=-1, keepdims=True))


def _split_p(p_ref, rows=slice(None)):
    xb = p_ref[rows, OFF_RG_X:OFF_RG_Y]
    yb = p_ref[rows, OFF_RG_Y:OFF_Q]
    q = p_ref[rows, OFF_Q:OFF_K]
    k = p_ref[rows, OFF_K:OFF_V]
    v = p_ref[rows, OFF_V:OFF_G]
    g = p_ref[rows, OFF_G:OFF_A]
    a_low = p_ref[rows, OFF_A:P_WIDTH]
    return xb, yb, q, k, v, g, a_low


def _gla_prepare(q, k, log_a, chunk):
    b = _segment_cumsum(log_a, chunk)
    qt = (q * (GLA_DK ** -0.5)) * jnp.exp(b)
    kt = k * jnp.exp(-b)
    return b, qt, kt


def _mix_project(rg_out, gla_out, wout_ref):
    return (_mm(rg_out, wout_ref[0:RG_WIDTH, :]) + _mm(gla_out, wout_ref[RG_WIDTH:, :]))


def _ffn_in_kernel(xa_ref, xb_ref, n1_ref, wg_ref, wu_ref, wd_ref, n2_ref, win_ref, *refs, steps_a):
    from_a = pl.program_id(0) < steps_a
    n_cast = (len(refs) - 2) // 2
    cast_src, (x1_ref, p_ref), cast_dst = refs[:n_cast], refs[n_cast:n_cast + 2], refs[n_cast + 2:]
    for sub in range(xa_ref.shape[0] // ROW_TILE):
        rows = slice(sub * ROW_TILE, (sub + 1) * ROW_TILE)
        x = jnp.where(from_a, xa_ref[rows, :], xb_ref[rows, :])
        x1 = x + 0.5 * _ffn(x, n1_ref, wg_ref, wu_ref, wd_ref)
        x1_ref[rows, :] = x1
        p_ref[rows, :] = _mm(_rms(x1, n2_ref[...]), win_ref[...])
    for src, dst in zip(cast_src, cast_dst):
        dst[...] = src[...].astype(BF16)


def _mem_kv_kernel(m_ref, n_ref, wk_ref, wv_ref, *refs):
    n_cast = (len(refs) - 4) // 2
    cast_src, (k_ref, v_ref, kb_ref, vb_ref), cast_dst = refs[:n_cast], refs[n_cast:n_cast + 4], refs[n_cast + 4:]
    h = _rms(m_ref[...], n_ref[...]).astype(BF16)
    k = jnp.dot(h, wk_ref[...], preferred_element_type=F32)
    v = jnp.dot(h, wv_ref[...], preferred_element_type=F32)
    for h in range(XA_HEADS):
        sl = slice(h * XA_HEAD_DIM, (h + 1) * XA_HEAD_DIM)
        k_ref[:, h, :] = k[:, sl]
        v_ref[:, h, :] = v[:, sl]
    kb_ref[...] = k.astype(BF16)
    vb_ref[...] = v.astype(BF16)
    for src, dst in zip(cast_src, cast_dst):
        dst[...] = src[...].astype(BF16)


def _prompt_tail_kernel(
        x1_ref, p_ref, kb_ref, vb_ref,
        cw_ref, cb_ref, wgate_ref, ba_ref, bx_ref, lam_ref, rgn_ref,
        wa2_ref, ba2_ref, glan_ref, wout_ref,
        xn_ref, wcq_ref, wco_ref, n2_ref, wg_ref, wu_ref, wd_ref, fn_ref,
        y_ref, rgh_ref, s_ref,
        xpad_ref, h_ref, sbd_ref, b_ref, x3_ref, *, steps_per_seq, n_steps):
    step = pl.program_id(0)
    t_idx = jnp.minimum(step, n_steps - 1) % steps_per_seq
    slot = step % 2
    tt = ROW_TILE
    subs = x1_ref.shape[0] // tt
    n_chunks = tt // GLA_CHUNK

    @pl.when(step == 0)
    def _():
        x3_ref[1] = jnp.zeros((subs, tt, D_MODEL), F32)

    @pl.when(t_idx == 0)
    def _():
        xpad_ref[0:SUBLANES, :] = jnp.zeros((SUBLANES, RG_WIDTH), F32)
        h_ref[...] = jnp.zeros_like(h_ref)
        sbd_ref[...] = jnp.zeros_like(sbd_ref)

    for sub in range(subs):
        rows = slice(sub * tt, (sub + 1) * tt)
        x3_prev = x3_ref[1 - slot, sub]
        ffn_out = []
        ffn = _ffn_pieces(x3_prev, n2_ref, wg_ref, wu_ref, wd_ref, ffn_out)
        placement = iter(FFN2_PLACEMENT)

        def emit_ffn():
            for _ in range(next(placement)):
                next(ffn)

        emit_ffn()

        xb, yb, q, k, v, g, a_low = _split_p(p_ref, rows)

        xpad_ref[SUBLANES:SUBLANES + tt, :] = xb
        sh1 = xpad_ref[SUBLANES - 1:SUBLANES - 1 + tt, :]
        sh2 = xpad_ref[SUBLANES - 2:SUBLANES - 2 + tt, :]
        sh3 = xpad_ref[SUBLANES - 3:SUBLANES - 3 + tt, :]
        xc = _conv(xb, sh1, sh2, sh3, cw_ref, cb_ref)
        xpad_ref[0:SUBLANES, :] = xb[tt - SUBLANES:tt, :]
        emit_ffn()
        r, i = _rg_gates(xc, wgate_ref, ba_ref, bx_ref)
        a, u = _rg_decay_input(xc, r, i, lam_ref)
        emit_ffn()
        a_grp, h_grp = _segment_affine_scan(a, u, SUBLANES)
        emit_ffn()
        carry = h_ref[0:1, :]
        groups = []
        for gi in range(tt // SUBLANES):
            grp = slice(gi * SUBLANES, (gi + 1) * SUBLANES)
            groups.append(h_grp[grp, :] + a_grp[grp, :] * carry)
            carry = groups[-1][SUBLANES - 1:SUBLANES, :]
        hs = jnp.concatenate(groups, axis=0)
        h_ref[...] = jnp.broadcast_to(carry, h_ref.shape)
        rg_out = _rms(hs * _gelu_tanh(yb), rgn_ref[...]).astype(BF16)
        emit_ffn()

        log_a = _gla_log_decay(a_low, wa2_ref, ba2_ref)
        b, qt, kt = _gla_prepare(q, k, log_a, GLA_CHUNK)
        emit_ffn()
        b_last = _last_rows(b_ref, b, GLA_CHUNK)
        dec_t = jnp.exp(b_last).T
        kv = _gla_chunk_updates(k, v, b, b_last, GLA_CHUNK)
        states = {}
        for p in range(2):
            s_bd = sbd_ref[p]
            for c in range(n_chunks):
                states[c, p] = s_bd.astype(BF16)
                s_bd = s_bd * dec_t[p * PAIR_K:(p + 1) * PAIR_K, c:c + 1] + kv[c, p]
            sbd_ref[p] = s_bd
        emit_ffn()
        intra = _gla_intra(qt, kt, v, GLA_CHUNK)
        emit_ffn()
        gla_out = _gla_out(_gla_combine(intra, qt, states, GLA_CHUNK), g, glan_ref).astype(BF16)
        emit_ffn()

        x2 = x1_ref[rows, :] + _mix_project(rg_out, gla_out, wout_ref)

        qx = _mm(_rms(x2, xn_ref[...]), wcq_ref[...])
        emit_ffn()
        head = lambda h: slice(h * XA_HEAD_DIM, (h + 1) * XA_HEAD_DIM)
        s = jnp.concatenate([_mm_nt(qx[:, head(h)], kb_ref[:, head(h)]) for h in range(XA_HEADS)], axis=0)
        pr = _softmax_rows(s * (XA_HEAD_DIM ** -0.5))
        emit_ffn()
        o = jnp.concatenate([_mm(pr[h * tt:(h + 1) * tt, :], vb_ref[:, head(h)]) for h in range(XA_HEADS)], axis=1)
        x3_ref[slot, sub] = x2 + _mm(o, wco_ref[...])
        for _ in ffn:
            pass
        y_ref[rows, :] = _rms(x3_prev + 0.5 * ffn_out[0], fn_ref[...])

    @pl.when((t_idx == steps_per_seq - 1) & (step < n_steps))
    def _():
        rgh_ref[...] = h_ref[...]
        row = lax.broadcasted_iota(jnp.int32, (PAIR_K, GLA_DV), 0)
        for p in range(2):
            s_bd = sbd_ref[p]
            s_ref[p * PAIR_K:(p + 1) * PAIR_K, :] = jnp.where(
                row < GLA_DK, s_bd[:, :GLA_DV], s_bd[:, GLA_DV:])


def _sample_mix_kernel(
        x1_ref, p_ref, econv_ref, eh_ref, s0_ref,
        cw_ref, cb_ref, wgate_ref, ba_ref, bx_ref, lam_ref, rgn_ref,
        wa2_ref, ba2_ref, glan_ref, wout_ref, xn_ref, wcq_ref,
        x2_ref, q_ref, rgh_ref, s_ref,
        hs_ref, b_ref, *, seq):
    rows_n = x1_ref.shape[0]
    nb = rows_n // seq

    xb, yb, q, k, v, g, a_low = _split_p(p_ref)

    pos = _row_in_segment(xb.shape, seq)
    econv = econv_ref[...]
    sh = []
    for j in range(1, CONV_WIDTH):
        sh.append(jnp.where(pos >= j, _shift_rows(xb, j), _shift_rows(econv, rows_n - seq + j)))
    xc = _conv(xb, sh[0], sh[1], sh[2], cw_ref, cb_ref)
    r, i = _rg_gates(xc, wgate_ref, ba_ref, bx_ref)
    a, u = _rg_decay_input(xc, r, i, lam_ref)
    u = u + a * eh_ref[...]
    _, hs = _segment_affine_scan(a, u, seq)
    rgh_ref[...] = _last_rows(hs_ref, hs, seq)
    rg_out = _rms(hs * _gelu_tanh(yb), rgn_ref[...]).astype(BF16)

    log_a = _gla_log_decay(a_low, wa2_ref, ba2_ref)
    b, qt, kt = _gla_prepare(q, k, log_a, seq)
    intra = _gla_intra(qt, kt, v, seq)
    b_last = _last_rows(b_ref, b, seq)
    dec_t = jnp.exp(b_last).T
    kv = _gla_chunk_updates(k, v, b, b_last, seq)
    row = lax.broadcasted_iota(jnp.int32, (PAIR_K, GLA_DV), 0)
    top = row < GLA_DK
    states = {}
    for c in range(nb):
        for p in range(2):
            kl = slice(p * PAIR_K, (p + 1) * PAIR_K)
            s_pair = s0_ref[c, kl, :]
            s_bd = jnp.concatenate([jnp.where(top, s_pair, 0.0), jnp.where(top, 0.0, s_pair)], axis=1)
            states[c, p] = s_bd.astype(BF16)
            s_new = s_bd * dec_t[kl, c:c + 1] + kv[c, p]
            s_ref[c, kl, :] = jnp.where(top, s_new[:, :GLA_DV], s_new[:, GLA_DV:])
    gla_out = _gla_out(_gla_combine(intra, qt, states, seq), g, glan_ref).astype(BF16)

    x2 = x1_ref[...] + _mix_project(rg_out, gla_out, wout_ref)
    x2_ref[...] = x2
    q_ref[...] = _mm(_rms(x2, xn_ref[...]), wcq_ref[...])


def _sample_attn_tail_kernel(q_ref, k_ref, v_ref, x2_ref, wco_ref, n2_ref, wg_ref, wu_ref, wd_ref, fn_ref,
                             y_ref, o_ref, *, seq):
    tiles = XA_HEAD_DIM // LANES
    group = tiles * XA_HEADS
    r = N_MEM * group
    nb = k_ref.shape[0] // r
    hs = XA_HEADS * seq
    lane = lax.broadcasted_iota(jnp.int32, (nb * hs, r), 1)
    head = (lax.broadcasted_iota(jnp.int32, (nb * hs, r), 0) // seq) & (XA_HEADS - 1)
    own = (lane & (group - 1)) == head
    s = []
    for j in range(nb):
        q = q_ref[j * seq:(j + 1) * seq, :]
        q_parts = jnp.concatenate(
            [q[:, c * LANES:(c + 1) * LANES] for c in range(XA_HEADS * tiles)], axis=0)
        part = _mm_nt(q_parts, k_ref[j * r:(j + 1) * r, :])
        for h in range(XA_HEADS):
            acc = part[h * tiles * seq:(h * tiles + 1) * seq, :]
            for c in range(1, tiles):
                blk = part[(h * tiles + c) * seq:(h * tiles + c + 1) * seq, :]
                acc = acc + pltpu.roll(blk, r - c * XA_HEADS, axis=1)
            s.append(acc)
    p = _softmax_rows(jnp.where(own, jnp.concatenate(s, axis=0) * (XA_HEAD_DIM ** -0.5), MASKED_SCORE))
    for j in range(nb):
        pj = p[j * hs:(j + 1) * hs, :]
        p_all = jnp.concatenate(
            [pj] + [pltpu.roll(pj, c * XA_HEADS, axis=1) for c in range(1, tiles)], axis=0)
        o = _mm(p_all, v_ref[j * r:(j + 1) * r, :].astype(BF16))
        for c in range(tiles):
            for h in range(XA_HEADS):
                col = h * XA_HEAD_DIM + c * LANES
                o_ref[j * seq:(j + 1) * seq, col:col + LANES] = (
                    o[(c * XA_HEADS + h) * seq:(c * XA_HEADS + h + 1) * seq, :])
    x3 = x2_ref[...] + _mm(o_ref[...], wco_ref[...])
    x4 = x3 + 0.5 * _ffn(x3, n2_ref, wg_ref, wu_ref, wd_ref)
    y_ref[...] = _rms(x4, fn_ref[...])


def _resident(arr):
    nd = arr.ndim
    return pl.BlockSpec(arr.shape, lambda *_: (0,) * nd, pipeline_mode=pl.Buffered(1))


def _params(sem):
    return pltpu.CompilerParams(dimension_semantics=sem, vmem_limit_bytes=VMEM_LIMIT)


def _slab_spec(w, steps):
    per, revisit = w.shape[0] // steps, 1
    while per % BF16_SUBLANES:
        per, revisit = per * 2, revisit * 2
    return pl.BlockSpec((per, w.shape[1]), lambda i: (jnp.minimum(i, steps - 1) // revisit, 0))


def _ffn_in(xa, xb, weights, cast=()):
    block = FFN_IN_TILES * ROW_TILE
    steps_a, steps_b = xa.shape[0] // block, xb.shape[0] // block
    rows = xa.shape[0] + xb.shape[0]
    spec_a = pl.BlockSpec((block, D_MODEL), lambda i: (jnp.minimum(i, steps_a - 1), 0))
    spec_b = pl.BlockSpec((block, D_MODEL), lambda i: (jnp.maximum(i - steps_a, 0), 0))
    row_spec = lambda w: pl.BlockSpec((block, w), lambda i: (i, 0))
    slabs = [_slab_spec(w, steps_a) for w in cast]
    return pl.pallas_call(
        functools.partial(_ffn_in_kernel, steps_a=steps_a),
        grid=(steps_a + steps_b,),
        in_specs=[spec_a, spec_b] + [_resident(w) for w in weights] + slabs,
        out_specs=[row_spec(D_MODEL), row_spec(P_WIDTH)] + slabs,
        out_shape=[jax.ShapeDtypeStruct((rows, D_MODEL), F32),
                   jax.ShapeDtypeStruct((rows, P_WIDTH), F32)]
                  + [jax.ShapeDtypeStruct(w.shape, BF16) for w in cast],
        compiler_params=_params(("arbitrary",)),
        name="ffn_in",
    )(xa, xb, *weights, *cast)


def _mem_kv(mem, weights, cast=()):
    rows = mem.shape[0]
    steps = rows // ROW_TILE
    row_spec = pl.BlockSpec((ROW_TILE, D_MODEL), lambda i: (i, 0))
    head_spec = pl.BlockSpec((ROW_TILE, XA_HEADS, XA_HEAD_DIM), lambda i: (i, 0, 0))
    slabs = [_slab_spec(w, steps) for w in cast]
    return pl.pallas_call(
        _mem_kv_kernel,
        grid=(steps,),
        in_specs=[row_spec] + [_resident(w) for w in weights] + slabs,
        out_specs=[head_spec] * 2 + [row_spec] * 2 + slabs,
        out_shape=[jax.ShapeDtypeStruct((rows, XA_HEADS, XA_HEAD_DIM), F32)] * 2
                  + [jax.ShapeDtypeStruct((rows, D_MODEL), BF16)] * 2
                  + [jax.ShapeDtypeStruct(w.shape, BF16) for w in cast],
        compiler_params=_params(("arbitrary",)),
        name="mem_kv",
    )(mem, *weights, *cast)


def _prompt_tail(x1, p, kb, vb, weights, seq):
    batch = kb.shape[0]
    tt = ROW_TILE
    block = PROMPT_TILES * tt
    steps_per_seq = seq // block
    n_steps = batch * steps_per_seq
    cur = lambda s: jnp.minimum(s, n_steps - 1)
    prev = lambda s: jnp.maximum(s - 1, 0)
    tok = lambda w: pl.BlockSpec((block, w), lambda s: (cur(s), 0))
    per_seq = lambda r, w: pl.BlockSpec((None, r, w), lambda s: (cur(s) // steps_per_seq, 0, 0))
    out_tok = pl.BlockSpec((block, D_MODEL), lambda s: (prev(s), 0))
    return pl.pallas_call(
        functools.partial(_prompt_tail_kernel, steps_per_seq=steps_per_seq, n_steps=n_steps),
        grid=(n_steps + 1,),
        in_specs=[tok(D_MODEL), tok(P_WIDTH), per_seq(N_MEM, D_MODEL), per_seq(N_MEM, D_MODEL)]
                 + [_resident(w) for w in weights],
        out_specs=[out_tok, per_seq(SUBLANES, RG_WIDTH), per_seq(GLA_KEY_WIDTH, GLA_DV)],
        out_shape=[jax.ShapeDtypeStruct((batch * seq, D_MODEL), F32),
                   jax.ShapeDtypeStruct((batch, SUBLANES, RG_WIDTH), F32),
                   jax.ShapeDtypeStruct((batch, GLA_KEY_WIDTH, GLA_DV), F32)],
        scratch_shapes=[pltpu.VMEM((SUBLANES + tt, RG_WIDTH), F32),
                        pltpu.VMEM((SUBLANES, RG_WIDTH), F32),
                        pltpu.VMEM((2, PAIR_K, PAIR_V), F32),
                        pltpu.VMEM((GLA_KEY_WIDTH // LANES, tt, LANES), F32),
                        pltpu.VMEM((2, PROMPT_TILES, tt, D_MODEL), F32)],
        compiler_params=_params(("arbitrary",)),
        name="prompt_tail",
    )(x1, p, kb, vb, *weights)


def _sample_mix(x1, p, econv, eh, s0, weights, seq):
    rows = econv.shape[0]
    first = (x1.shape[0] - rows) // ROW_TILE
    nb = ROW_TILE // seq
    row_spec = lambda w: pl.BlockSpec((ROW_TILE, w), lambda i: (i, 0))
    tail_spec = lambda w: pl.BlockSpec((ROW_TILE, w), lambda i: (first + i, 0))
    state_spec = pl.BlockSpec((nb, GLA_KEY_WIDTH, GLA_DV), lambda i: (i, 0, 0))
    return pl.pallas_call(
        functools.partial(_sample_mix_kernel, seq=seq),
        grid=(rows // ROW_TILE,),
        in_specs=[tail_spec(D_MODEL), tail_spec(P_WIDTH), row_spec(RG_WIDTH), row_spec(RG_WIDTH),
                  state_spec] + [_resident(w) for w in weights],
        out_specs=[row_spec(D_MODEL), row_spec(D_MODEL),
                   pl.BlockSpec((nb, RG_WIDTH), lambda i: (i, 0)), state_spec],
        out_shape=[jax.ShapeDtypeStruct((rows, D_MODEL), F32),
                   jax.ShapeDtypeStruct((rows, D_MODEL), F32),
                   jax.ShapeDtypeStruct((rows // seq, RG_WIDTH), F32),
                   jax.ShapeDtypeStruct((rows // seq, GLA_KEY_WIDTH, GLA_DV), F32)],
        scratch_shapes=[pltpu.VMEM((RG_WIDTH // LANES, ROW_TILE, LANES), F32),
                        pltpu.VMEM((GLA_KEY_WIDTH // LANES, ROW_TILE, LANES), F32)],
        compiler_params=_params(("parallel",)),
        name="sample_mix",
    )(x1, p, econv, eh, s0, *weights)


def _head_interleaved_rows(x):
    b, m, h, dh = x.shape
    tiles = dh // LANES
    return (x.reshape(b, m, h, tiles, LANES).transpose(0, 1, 3, 2, 4)
            .reshape(b * m * tiles * h, LANES))


def _sample_attn_tail(q, k, v, x2, weights, seq):
    rows = q.shape[0]
    nb = SAMPLE_ATTN_BATCH
    row_spec = pl.BlockSpec((nb * seq, D_MODEL), lambda i: (i, 0))
    kv_rows = k.shape[0] // (rows // seq)
    kv_spec = pl.BlockSpec((nb * kv_rows, LANES), lambda i: (i, 0))
    return pl.pallas_call(
        functools.partial(_sample_attn_tail_kernel, seq=seq),
        grid=(rows // (nb * seq),),
        in_specs=[row_spec, kv_spec, kv_spec, row_spec] + [_resident(w) for w in weights],
        out_specs=row_spec,
        out_shape=jax.ShapeDtypeStruct((rows, D_MODEL), F32),
        scratch_shapes=[pltpu.VMEM((nb * seq, D_MODEL), F32)],
        compiler_params=_params(("parallel",)),
        name="sample_attn_tail",
    )(q, k, v, x2, *weights)


def _block_diag_heads(w):
    h, n, _ = w.shape
    eye = jnp.eye(h, dtype=w.dtype)
    return (eye[:, None, :, None] * w[:, :, None, :]).reshape(h * n, h * n)


def kernel(x_prompt, x_sample, cache_mem_k, cache_mem_v, state_conv, state_rglru, state_gla, mem_prompt, ffn1_norm, ffn1_w_gate, ffn1_w_up, ffn1_w_down, mix_norm, w_in, conv_w, conv_b, rg_w_a, rg_b_a, rg_w_x, rg_b_x, rg_lambda, rg_out_norm, gla_w_a2, gla_b_a2, gla_out_norm, w_out, xattn_norm, mem_norm, w_cq, w_ck, w_cv, w_co, ffn2_norm, ffn2_w_gate, ffn2_w_up, ffn2_w_down, final_norm):
    bp, tp, _ = x_prompt.shape
    bs, ts, _ = x_sample.shape
    assert ts == SUBLANES, "each sample sequence must fill exactly one 8-row sublane group"
    assert tp % (PROMPT_TILES * ROW_TILE) == 0 and (bs * ts) % (FFN_IN_TILES * ROW_TILE) == 0
    assert bs % SAMPLE_ATTN_BATCH == 0 and (bp * N_MEM) % ROW_TILE == 0
    assert cache_mem_k.shape[1:] == (N_MEM, XA_HEADS, XA_HEAD_DIM) and mem_prompt.shape[1] == N_MEM
    row = lambda g: g.reshape(1, -1)
    bf = lambda w: w.astype(BF16)

    w_in_p = bf(jnp.pad(w_in, ((0, 0), (0, P_WIDTH - D_IN))))
    hp = RG_HEADS // 2
    w_gate_rg = bf(jnp.stack([
        jnp.concatenate([_block_diag_heads(rg_w_a[c * hp:(c + 1) * hp]),
                         _block_diag_heads(rg_w_x[c * hp:(c + 1) * hp])], axis=1)
        for c in range(2)]))
    w_a2_p = bf(jnp.pad(gla_w_a2, ((0, A_PAD - GLA_GATE_RANK), (0, 0))))

    mem_k, mem_v, mem_kb, mem_vb, w1_gate, w1_up, w1_down = _mem_kv(
        mem_prompt.reshape(bp * N_MEM, D_MODEL), (row(mem_norm), bf(w_ck), bf(w_cv)),
        cast=(ffn1_w_gate, ffn1_w_up, ffn1_w_down))
    ffn_in_w = (row(ffn1_norm), w1_gate, w1_up, w1_down, row(mix_norm), w_in_p)
    x1, p, w2_gate, w2_up, w2_down, w_out_b, w_cq_b, w_co_b = _ffn_in(
        x_prompt.reshape(bp * tp, D_MODEL), x_sample.reshape(bs * ts, D_MODEL), ffn_in_w,
        cast=(ffn2_w_gate, ffn2_w_up, ffn2_w_down, w_out, w_cq, w_co))
    mixer_w = (conv_w, row(conv_b), w_gate_rg, row(rg_b_a), row(rg_b_x), row(rg_lambda),
               row(rg_out_norm), w_a2_p, row(gla_b_a2), row(jnp.tile(gla_out_norm, GLA_HEADS)),
               w_out_b)
    ffn2_w = (row(ffn2_norm), w2_gate, w2_up, w2_down, row(final_norm))
    y_p, rgh_p, s_p = _prompt_tail(
        x1, p, mem_kb.reshape(bp, N_MEM, D_MODEL), mem_vb.reshape(bp, N_MEM, D_MODEL),
        mixer_w + (row(xattn_norm), w_cq_b, w_co_b) + ffn2_w, tp)
    groups = p.reshape(-1, ts, P_WIDTH)
    per_seq = tp // ts
    conv_p = groups[per_seq - 1:bp * per_seq:per_seq, ts - (CONV_WIDTH - 1):, OFF_RG_X:OFF_RG_Y]
    conv_s = groups[bp * per_seq:, ts - (CONV_WIDTH - 1):, OFF_RG_X:OFF_RG_Y]

    econv = jnp.pad(state_conv, ((0, 0), (ts - (CONV_WIDTH - 1), 0), (0, 0))).reshape(bs * ts, RG_WIDTH)
    eh = jnp.pad(state_rglru[:, None, :], ((0, 0), (0, ts - 1), (0, 0))).reshape(bs * ts, RG_WIDTH)
    x2_s, q_s, rgh_s, s_s = _sample_mix(
        x1, p, econv, eh, state_gla.reshape(bs, GLA_KEY_WIDTH, GLA_DV),
        mixer_w + (row(xattn_norm), w_cq_b), ts)
    y_s = _sample_attn_tail(q_s, _head_interleaved_rows(cache_mem_k), _head_interleaved_rows(cache_mem_v),
                            x2_s, (w_co_b,) + ffn2_w, ts)

    return (y_p.reshape(bp, tp, D_MODEL), y_s.reshape(bs, ts, D_MODEL),
            mem_k.reshape(bp, N_MEM, XA_HEADS, XA_HEAD_DIM),
            mem_v.reshape(bp, N_MEM, XA_HEADS, XA_HEAD_DIM),
            conv_p, rgh_p[:, 0, :], s_p.reshape(bp, GLA_HEADS, GLA_DK, GLA_DV),
            conv_s, rgh_s, s_s.reshape(bs, GLA_HEADS, GLA_DK, GLA_DV))
```

```python
import functools

import jax
import jax.numpy as jnp
import numpy as np
from jax import lax
from jax.experimental import pallas as pl
from jax.experimental.pallas import tpu as pltpu

F32 = jnp.float32
BF16 = jnp.bfloat16

D_MODEL = 1024
D_FF = 2816
RG_WIDTH = 512
RG_HEADS = 8
RG_HEAD_DIM = 64
CONV_WIDTH = 4
RG_C = 8.0
GLA_WIDTH = 512
GLA_HEADS = 4
GLA_DV = 128
GLA_DK = 64
GLA_KEY_WIDTH = 256
GLA_GATE_RANK = 16
GLA_GATE_NORMALIZER = 16.0
GLA_CHUNK = 32
N_MEM = 256
XA_HEADS = 4
XA_HEAD_DIM = 256
EPS = 1e-6
MASKED_SCORE = -np.inf

OFF_RG_X = 0
OFF_RG_Y = 512
OFF_Q = 1024
OFF_K = 1280
OFF_V = 1536
OFF_G = 2048
OFF_A = 2560
D_IN = 2576

LANES = 128
SUBLANES = 8
BF16_SUBLANES = 16
A_PAD = LANES
P_WIDTH = OFF_A + A_PAD
PAIR_K = 2 * GLA_DK
PAIR_V = 2 * GLA_DV
VMEM_LIMIT = 58 * 1024 * 1024

ROW_TILE = 256
FFN_IN_TILES = 2
PROMPT_TILES = 2
FF_CHUNK = 256
FF_DOWN_GROUP = 4
FFN2_PLACEMENT = (1, 1, 1, 1, 1, 1, 1, 1, 1, 1, 1)
SAMPLE_ATTN_BATCH = 8


def _rms(x, g):
    return x * lax.rsqrt(jnp.mean(x * x, axis=-1, keepdims=True) + EPS) * g


def _mm(a, w):
    return jnp.dot(a.astype(BF16), w, preferred_element_type=F32)


def _mm_nt(a, b):
    return lax.dot_general(a.astype(BF16), b.astype(BF16), (((1,), (1,)), ((), ())),
                           preferred_element_type=F32)


def _mm_tn(a, b):
    return lax.dot_general(a.astype(BF16), b.astype(BF16), (((0,), (0,)), ((), ())),
                           preferred_element_type=F32)


def _silu(x):
    return x * jax.nn.sigmoid(x)


def _gelu_tanh(x):
    c = np.float32(np.sqrt(2.0 / np.pi))
    return x * (0.5 * (1.0 + jnp.tanh(c * (x + 0.044715 * (x * x * x)))))


def _softplus(x):
    return jnp.maximum(x, 0.0) + jnp.log1p(jnp.exp(-jnp.abs(x)))


def _ffn(x, norm_ref, wg_ref, wu_ref, wd_ref):
    h = _rms(x, norm_ref[...]).astype(BF16)
    g = jnp.dot(h, wg_ref[...], preferred_element_type=F32)
    u = jnp.dot(h, wu_ref[...], preferred_element_type=F32)
    return _mm(_silu(g) * u, wd_ref[...])


def _last_rows(buf_ref, x, seg):
    n = x.shape[0] // seg
    outs = []
    for j in range(x.shape[1] // LANES):
        buf_ref[j] = x[:, j * LANES:(j + 1) * LANES]
        outs.append(buf_ref[j, pl.ds(seg - 1, n, stride=seg), :])
    return jnp.concatenate(outs, axis=1)


def _ffn_pieces(x, norm_ref, wg_ref, wu_ref, wd_ref, out):
    h = _rms(x, norm_ref[...]).astype(BF16)
    acc = None
    acts = []
    n = D_FF // FF_CHUNK
    for c in range(n):
        cols = slice(c * FF_CHUNK, (c + 1) * FF_CHUNK)
        g = jnp.dot(h, wg_ref[:, cols], preferred_element_type=F32)
        u = jnp.dot(h, wu_ref[:, cols], preferred_element_type=F32)
        acts.append((_silu(g) * u).astype(BF16))
        if len(acts) == FF_DOWN_GROUP or c == n - 1:
            lo = (c + 1 - len(acts)) * FF_CHUNK
            part = jnp.dot(jnp.concatenate(acts, axis=1), wd_ref[lo:(c + 1) * FF_CHUNK, :],
                           preferred_element_type=F32)
            acc = part if acc is None else acc + part
            acts = []
        yield
    out.append(acc)


def _shift_rows(x, s):
    return pltpu.roll(x, s, axis=0)


def _row_in_segment(shape, seg):
    return lax.broadcasted_iota(jnp.int32, shape, 0) & (seg - 1)


def _segment_cumsum(x, seg):
    pos = _row_in_segment(x.shape, seg)
    s = 1
    while s < seg:
        x = jnp.where(pos >= s, x + _shift_rows(x, s), x)
        s *= 2
    return x


def _segment_affine_scan(a, u, seg):
    pos = _row_in_segment(a.shape, seg)
    s = 1
    while s < seg:
        m = pos >= s
        u = jnp.where(m, a * _shift_rows(u, s) + u, u)
        a = jnp.where(m, a * _shift_rows(a, s), a)
        s *= 2
    return a, u


def _conv(xb, sh1, sh2, sh3, cw_ref, cb_ref):
    y = cb_ref[...] + sh3 * cw_ref[0:1, :]
    y = y + sh2 * cw_ref[1:2, :]
    y = y + sh1 * cw_ref[2:3, :]
    return y + xb * cw_ref[3:4, :]


def _rg_gates(xc, wgate_ref, ba_ref, bx_ref):
    half = RG_WIDTH // 2
    r, i = [], []
    for c in range(2):
        z = _mm(xc[:, c * half:(c + 1) * half], wgate_ref[c])
        r.append(z[:, :half])
        i.append(z[:, half:])
    r = jax.nn.sigmoid(jnp.concatenate(r, axis=1) + ba_ref[...])
    i = jax.nn.sigmoid(jnp.concatenate(i, axis=1) + bx_ref[...])
    return r, i


def _rg_decay_input(xc, r, i, lam_ref):
    log_a = (-RG_C * _softplus(-lam_ref[...])) * r
    a = jnp.exp(log_a)
    mult = jnp.sqrt(-jnp.tanh(log_a) * (a * a + 1.0))
    return a, mult * (i * xc)


def _gla_log_decay(a_low, wa2_ref, ba2_ref):
    z = _mm(a_low, wa2_ref[...]) + ba2_ref[...]
    return (-_softplus(-z)) / GLA_GATE_NORMALIZER


def _gla_intra(qt, kt, v, chunk):
    t = qt.shape[0]
    shift = int(np.log2(chunk))
    ri = lax.broadcasted_iota(jnp.int32, (t, t), 0)
    ci = lax.broadcasted_iota(jnp.int32, (t, t), 1)
    causal = (ri >= ci) & ((ri >> shift) == (ci >> shift))
    lane = lax.broadcasted_iota(jnp.int32, (t, PAIR_K), 1)
    outs = []
    for h in range(GLA_HEADS):
        p = h // 2
        qp = qt[:, p * PAIR_K:(p + 1) * PAIR_K]
        kp = kt[:, p * PAIR_K:(p + 1) * PAIR_K]
        mine = (lane >= GLA_DK) if (h % 2) else (lane < GLA_DK)
        s = _mm_nt(jnp.where(mine, qp, 0.0), kp)
        attn = jnp.where(causal, s, 0.0)
        outs.append(_mm(attn, v[:, h * GLA_DV:(h + 1) * GLA_DV].astype(BF16)))
    return outs


def _pair_blockdiag_mask():
    r = lax.broadcasted_iota(jnp.int32, (PAIR_K, PAIR_V), 0)
    c = lax.broadcasted_iota(jnp.int32, (PAIR_K, PAIR_V), 1)
    return (r < GLA_DK) == (c < GLA_DV)


def _gla_chunk_updates(k, v, b, b_last, chunk):
    bd_mask = _pair_blockdiag_mask()
    kv = {}
    for c in range(k.shape[0] // chunk):
        rows = slice(c * chunk, (c + 1) * chunk)
        for p in range(2):
            kl = slice(p * PAIR_K, (p + 1) * PAIR_K)
            kd = k[rows, kl] * jnp.exp(b_last[c:c + 1, kl] - b[rows, kl])
            kv[c, p] = jnp.where(bd_mask, _mm_tn(kd, v[rows, p * PAIR_V:(p + 1) * PAIR_V]), 0.0)
    return kv


def _gla_combine(intra, qt, states, chunk):
    n_chunks = qt.shape[0] // chunk
    heads = []
    for p in range(2):
        kl = slice(p * PAIR_K, (p + 1) * PAIR_K)
        inter = jnp.concatenate(
            [_mm(qt[c * chunk:(c + 1) * chunk, kl], states[c, p]) for c in range(n_chunks)],
            axis=0)
        heads.append(intra[2 * p] + inter[:, :GLA_DV])
        heads.append(intra[2 * p + 1] + inter[:, GLA_DV:])
    return heads


def _gla_out(o_heads, g, norm_ref):
    outs = []
    for h in range(GLA_HEADS):
        outs.append(_rms(o_heads[h], norm_ref[:, h * GLA_DV:(h + 1) * GLA_DV]))
    return jnp.concatenate(outs, axis=1) * _silu(g)


def _softmax_rows(s):
    m = jnp.max(s, axis=-1, keepdims=True)
    e = jnp.exp(s - m)
    return e * (1.0 / jnp.sum(e, axis=-1, keepdims=True))


def _split_p(p_ref, rows=slice(None)):
    xb = p_ref[rows, OFF_RG_X:OFF_RG_Y]
    yb = p_ref[rows, OFF_RG_Y:OFF_Q]
    q = p_ref[rows, OFF_Q:OFF_K]
    k = p_ref[rows, OFF_K:OFF_V]
    v = p_ref[rows, OFF_V:OFF_G]
    g = p_ref[rows, OFF_G:OFF_A]
    a_low = p_ref[rows, OFF_A:P_WIDTH]
    return xb, yb, q, k, v, g, a_low


def _gla_prepare(q, k, log_a, chunk):
    b = _segment_cumsum(log_a, chunk)
    qt = (q * (GLA_DK ** -0.5)) * jnp.exp(b)
    kt = k * jnp.exp(-b)
    return b, qt, kt


def _mix_project(rg_out, gla_out, wout_ref):
    return (_mm(rg_out, wout_ref[0:RG_WIDTH, :]) + _mm(gla_out, wout_ref[RG_WIDTH:, :]))


def _ffn_in_kernel(xa_ref, xb_ref, n1_ref, wg_ref, wu_ref, wd_ref, n2_ref, win_ref, *refs, steps_a):
    from_a = pl.program_id(0) < steps_a
    n_cast = (len(refs) - 2) // 2
    cast_src, (x1_ref, p_ref), cast_dst = refs[:n_cast], refs[n_cast:n_cast + 2], refs[n_cast + 2:]
    for sub in range(xa_ref.shape[0] // ROW_TILE):
        rows = slice(sub * ROW_TILE, (sub + 1) * ROW_TILE)
        x = jnp.where(from_a, xa_ref[rows, :], xb_ref[rows, :])
        x1 = x + 0.5 * _ffn(x, n1_ref, wg_ref, wu_ref, wd_ref)
        x1_ref[rows, :] = x1
        p_ref[rows, :] = _mm(_rms(x1, n2_ref[...]), win_ref[...])
    for src, dst in zip(cast_src, cast_dst):
        dst[...] = src[...].astype(BF16)


def _mem_kv_kernel(m_ref, n_ref, wk_ref, wv_ref, *refs):
    n_cast = (len(refs) - 4) // 2
    cast_src, (k_ref, v_ref, kb_ref, vb_ref), cast_dst = refs[:n_cast], refs[n_cast:n_cast + 4], refs[n_cast + 4:]
    h = _rms(m_ref[...], n_ref[...]).astype(BF16)
    k = jnp.dot(h, wk_ref[...], preferred_element_type=F32)
    v = jnp.dot(h, wv_ref[...], preferred_element_type=F32)
    for h in range(XA_HEADS):
        sl = slice(h * XA_HEAD_DIM, (h + 1) * XA_HEAD_DIM)
        k_ref[:, h, :] = k[:, sl]
        v_ref[:, h, :] = v[:, sl]
    kb_ref[...] = k.astype(BF16)
    vb_ref[...] = v.astype(BF16)
    for src, dst in zip(cast_src, cast_dst):
        dst[...] = src[...].astype(BF16)


def _prompt_tail_kernel(
        x1_ref, p_ref, kb_ref, vb_ref,
        cw_ref, cb_ref, wgate_ref, ba_ref, bx_ref, lam_ref, rgn_ref,
        wa2_ref, ba2_ref, glan_ref, wout_ref,
        xn_ref, wcq_ref, wco_ref, n2_ref, wg_ref, wu_ref, wd_ref, fn_ref,
        y_ref, rgh_ref, s_ref,
        xpad_ref, h_ref, sbd_ref, b_ref, x3_ref, *, steps_per_seq, n_steps):
    step = pl.program_id(0)
    t_idx = jnp.minimum(step, n_steps - 1) % steps_per_seq
    slot = step % 2
    tt = ROW_TILE
    subs = x1_ref.shape[0] // tt
    n_chunks = tt // GLA_CHUNK

    @pl.when(step == 0)
    def _():
        x3_ref[1] = jnp.zeros((subs, tt, D_MODEL), F32)

    @pl.when(t_idx == 0)
    def _():
        xpad_ref[0:SUBLANES, :] = jnp.zeros((SUBLANES, RG_WIDTH), F32)
        h_ref[...] = jnp.zeros_like(h_ref)
        sbd_ref[...] = jnp.zeros_like(sbd_ref)

    for sub in range(subs):
        rows = slice(sub * tt, (sub + 1) * tt)
        x3_prev = x3_ref[1 - slot, sub]
        ffn_out = []
        ffn = _ffn_pieces(x3_prev, n2_ref, wg_ref, wu_ref, wd_ref, ffn_out)
        placement = iter(FFN2_PLACEMENT)

        def emit_ffn():
            for _ in range(next(placement)):
                next(ffn)

        emit_ffn()

        xb, yb, q, k, v, g, a_low = _split_p(p_ref, rows)

        half = RG_WIDTH // 2
        gated = []
        for c in range(2):
            cols = slice(c * half, (c + 1) * half)
            xb_c = xb[:, cols]
            xpad_ref[SUBLANES:SUBLANES + tt, cols] = xb_c
            sh = [xpad_ref[SUBLANES - j:SUBLANES - j + tt, cols] for j in range(1, CONV_WIDTH)]
            xc = cb_ref[:, cols] + sh[2] * cw_ref[0:1, cols]
            xc = xc + sh[1] * cw_ref[1:2, cols]
            xc = xc + sh[0] * cw_ref[2:3, cols]
            xc = xc + xb_c * cw_ref[3:4, cols]
            xpad_ref[0:SUBLANES, cols] = xb_c[tt - SUBLANES:tt, :]
            z = _mm(xc, wgate_ref[c])
            r = jax.nn.sigmoid(z[:, :half] + ba_ref[:, cols])
            i = jax.nn.sigmoid(z[:, half:] + bx_ref[:, cols])
            log_a = (-RG_C * _softplus(-lam_ref[:, cols])) * r
            a = jnp.exp(log_a)
            u = jnp.sqrt(-jnp.tanh(log_a) * (a * a + 1.0)) * (i * xc)
            emit_ffn()
            a_grp, h_grp = _segment_affine_scan(a, u, SUBLANES)
            carry = h_ref[0:1, cols]
            groups = []
            for gi in range(tt // SUBLANES):
                grp = slice(gi * SUBLANES, (gi + 1) * SUBLANES)
                groups.append(h_grp[grp, :] + a_grp[grp, :] * carry)
                carry = groups[-1][SUBLANES - 1:SUBLANES, :]
            h_ref[:, cols] = jnp.broadcast_to(carry, (SUBLANES, half))
            gated.append(jnp.concatenate(groups, axis=0) * _gelu_tanh(yb[:, cols]))
            emit_ffn()
        rg_out = _rms(jnp.concatenate(gated, axis=1), rgn_ref[...]).astype(BF16)

        log_a = _gla_log_decay(a_low, wa2_ref, ba2_ref)
        b, qt, kt = _gla_prepare(q, k, log_a, GLA_CHUNK)
        emit_ffn()
        b_last = _last_rows(b_ref, b, GLA_CHUNK)
        dec_t = jnp.exp(b_last).T
        kv = _gla_chunk_updates(k, v, b, b_last, GLA_CHUNK)
        states = {}
        for p in range(2):
            s_bd = sbd_ref[p]
            for c in range(n_chunks):
                states[c, p] = s_bd.astype(BF16)
                s_bd = s_bd * dec_t[p * PAIR_K:(p + 1) * PAIR_K, c:c + 1] + kv[c, p]
            sbd_ref[p] = s_bd
        emit_ffn()
        intra = _gla_intra(qt, kt, v, GLA_CHUNK)
        emit_ffn()
        gla_out = _gla_out(_gla_combine(intra, qt, states, GLA_CHUNK), g, glan_ref).astype(BF16)
        emit_ffn()

        x2 = x1_ref[rows, :] + _mix_project(rg_out, gla_out, wout_ref)

        qx = _mm(_rms(x2, xn_ref[...]), wcq_ref[...])
        emit_ffn()
        head = lambda h: slice(h * XA_HEAD_DIM, (h + 1) * XA_HEAD_DIM)
        s = jnp.concatenate([_mm_nt(qx[:, head(h)], kb_ref[:, head(h)]) for h in range(XA_HEADS)], axis=0)
        pr = _softmax_rows(s * (XA_HEAD_DIM ** -0.5))
        emit_ffn()
        o = jnp.concatenate([_mm(pr[h * tt:(h + 1) * tt, :], vb_ref[:, head(h)]) for h in range(XA_HEADS)], axis=1)
        x3_ref[slot, sub] = x2 + _mm(o, wco_ref[...])
        for _ in ffn:
            pass
        y_ref[rows, :] = _rms(x3_prev + 0.5 * ffn_out[0], fn_ref[...])

    @pl.when((t_idx == steps_per_seq - 1) & (step < n_steps))
    def _():
        rgh_ref[...] = h_ref[...]
        row = lax.broadcasted_iota(jnp.int32, (PAIR_K, GLA_DV), 0)
        for p in range(2):
            s_bd = sbd_ref[p]
            s_ref[p * PAIR_K:(p + 1) * PAIR_K, :] = jnp.where(
                row < GLA_DK, s_bd[:, :GLA_DV], s_bd[:, GLA_DV:])


def _sample_mix_kernel(
        x1_ref, p_ref, econv_ref, eh_ref, s0_ref,
        cw_ref, cb_ref, wgate_ref, ba_ref, bx_ref, lam_ref, rgn_ref,
        wa2_ref, ba2_ref, glan_ref, wout_ref, xn_ref, wcq_ref,
        x2_ref, q_ref, rgh_ref, s_ref,
        hs_ref, b_ref, *, seq):
    rows_n = x1_ref.shape[0]
    nb = rows_n // seq

    xb, yb, q, k, v, g, a_low = _split_p(p_ref)

    pos = _row_in_segment(xb.shape, seq)
    econv = econv_ref[...]
    sh = []
    for j in range(1, CONV_WIDTH):
        sh.append(jnp.where(pos >= j, _shift_rows(xb, j), _shift_rows(econv, rows_n - seq + j)))
    xc = _conv(xb, sh[0], sh[1], sh[2], cw_ref, cb_ref)
    r, i = _rg_gates(xc, wgate_ref, ba_ref, bx_ref)
    a, u = _rg_decay_input(xc, r, i, lam_ref)
    u = u + a * eh_ref[...]
    _, hs = _segment_affine_scan(a, u, seq)
    rgh_ref[...] = _last_rows(hs_ref, hs, seq)
    rg_out = _rms(hs * _gelu_tanh(yb), rgn_ref[...]).astype(BF16)

    log_a = _gla_log_decay(a_low, wa2_ref, ba2_ref)
    b, qt, kt = _gla_prepare(q, k, log_a, seq)
    intra = _gla_intra(qt, kt, v, seq)
    b_last = _last_rows(b_ref, b, seq)
    dec_t = jnp.exp(b_last).T
    kv = _gla_chunk_updates(k, v, b, b_last, seq)
    row = lax.broadcasted_iota(jnp.int32, (PAIR_K, GLA_DV), 0)
    top = row < GLA_DK
    states = {}
    for c in range(nb):
        for p in range(2):
            kl = slice(p * PAIR_K, (p + 1) * PAIR_K)
            s_pair = s0_ref[c, kl, :]
            s_bd = jnp.concatenate([jnp.where(top, s_pair, 0.0), jnp.where(top, 0.0, s_pair)], axis=1)
            states[c, p] = s_bd.astype(BF16)
            s_new = s_bd * dec_t[kl, c:c + 1] + kv[c, p]
            s_ref[c, kl, :] = jnp.where(top, s_new[:, :GLA_DV], s_new[:, GLA_DV:])
    gla_out = _gla_out(_gla_combine(intra, qt, states, seq), g, glan_ref).astype(BF16)

    x2 = x1_ref[...] + _mix_project(rg_out, gla_out, wout_ref)
    x2_ref[...] = x2
    q_ref[...] = _mm(_rms(x2, xn_ref[...]), wcq_ref[...])


def _sample_attn_tail_kernel(q_ref, k_ref, v_ref, x2_ref, wco_ref, n2_ref, wg_ref, wu_ref, wd_ref, fn_ref,
                             y_ref, o_ref, *, seq):
    tiles = XA_HEAD_DIM // LANES
    group = tiles * XA_HEADS
    r = N_MEM * group
    nb = k_ref.shape[0] // r
    hs = XA_HEADS * seq
    lane = lax.broadcasted_iota(jnp.int32, (nb * hs, r), 1)
    head = (lax.broadcasted_iota(jnp.int32, (nb * hs, r), 0) // seq) & (XA_HEADS - 1)
    own = (lane & (group - 1)) == head
    s = []
    for j in range(nb):
        q = q_ref[j * seq:(j + 1) * seq, :]
        q_parts = jnp.concatenate(
            [q[:, c * LANES:(c + 1) * LANES] for c in range(XA_HEADS * tiles)], axis=0)
        part = _mm_nt(q_parts, k_ref[j * r:(j + 1) * r, :])
        for h in range(XA_HEADS):
            acc = part[h * tiles * seq:(h * tiles + 1) * seq, :]
            for c in range(1, tiles):
                blk = part[(h * tiles + c) * seq:(h * tiles + c + 1) * seq, :]
                acc = acc + pltpu.roll(blk, r - c * XA_HEADS, axis=1)
            s.append(acc)
    p = _softmax_rows(jnp.where(own, jnp.concatenate(s, axis=0) * (XA_HEAD_DIM ** -0.5), MASKED_SCORE))
    for j in range(nb):
        pj = p[j * hs:(j + 1) * hs, :]
        p_all = jnp.concatenate(
            [pj] + [pltpu.roll(pj, c * XA_HEADS, axis=1) for c in range(1, tiles)], axis=0)
        o = _mm(p_all, v_ref[j * r:(j + 1) * r, :].astype(BF16))
        for c in range(tiles):
            for h in range(XA_HEADS):
                col = h * XA_HEAD_DIM + c * LANES
                o_ref[j * seq:(j + 1) * seq, col:col + LANES] = (
                    o[(c * XA_HEADS + h) * seq:(c * XA_HEADS + h + 1) * seq, :])
    x3 = x2_ref[...] + _mm(o_ref[...], wco_ref[...])
    x4 = x3 + 0.5 * _ffn(x3, n2_ref, wg_ref, wu_ref, wd_ref)
    y_ref[...] = _rms(x4, fn_ref[...])


def _resident(arr):
    nd = arr.ndim
    return pl.BlockSpec(arr.shape, lambda *_: (0,) * nd, pipeline_mode=pl.Buffered(1))


def _params(sem):
    return pltpu.CompilerParams(dimension_semantics=sem, vmem_limit_bytes=VMEM_LIMIT)


def _slab_spec(w, steps):
    per, revisit = w.shape[0] // steps, 1
    while per % BF16_SUBLANES:
        per, revisit = per * 2, revisit * 2
    return pl.BlockSpec((per, w.shape[1]), lambda i: (jnp.minimum(i, steps - 1) // revisit, 0))


def _ffn_in(xa, xb, weights, cast=()):
    block = FFN_IN_TILES * ROW_TILE
    steps_a, steps_b = xa.shape[0] // block, xb.shape[0] // block
    rows = xa.shape[0] + xb.shape[0]
    spec_a = pl.BlockSpec((block, D_MODEL), lambda i: (jnp.minimum(i, steps_a - 1), 0))
    spec_b = pl.BlockSpec((block, D_MODEL), lambda i: (jnp.maximum(i - steps_a, 0), 0))
    row_spec = lambda w: pl.BlockSpec((block, w), lambda i: (i, 0))
    slabs = [_slab_spec(w, steps_a) for w in cast]
    return pl.pallas_call(
        functools.partial(_ffn_in_kernel, steps_a=steps_a),
        grid=(steps_a + steps_b,),
        in_specs=[spec_a, spec_b] + [_resident(w) for w in weights] + slabs,
        out_specs=[row_spec(D_MODEL), row_spec(P_WIDTH)] + slabs,
        out_shape=[jax.ShapeDtypeStruct((rows, D_MODEL), F32),
                   jax.ShapeDtypeStruct((rows, P_WIDTH), F32)]
                  + [jax.ShapeDtypeStruct(w.shape, BF16) for w in cast],
        compiler_params=_params(("arbitrary",)),
        name="ffn_in",
    )(xa, xb, *weights, *cast)


def _mem_kv(mem, weights, cast=()):
    rows = mem.shape[0]
    steps = rows // ROW_TILE
    row_spec = pl.BlockSpec((ROW_TILE, D_MODEL), lambda i: (i, 0))
    head_spec = pl.BlockSpec((ROW_TILE, XA_HEADS, XA_HEAD_DIM), lambda i: (i, 0, 0))
    slabs = [_slab_spec(w, steps) for w in cast]
    return pl.pallas_call(
        _mem_kv_kernel,
        grid=(steps,),
        in_specs=[row_spec] + [_resident(w) for w in weights] + slabs,
        out_specs=[head_spec] * 2 + [row_spec] * 2 + slabs,
        out_shape=[jax.ShapeDtypeStruct((rows, XA_HEADS, XA_HEAD_DIM), F32)] * 2
                  + [jax.ShapeDtypeStruct((rows, D_MODEL), BF16)] * 2
                  + [jax.ShapeDtypeStruct(w.shape, BF16) for w in cast],
        compiler_params=_params(("arbitrary",)),
        name="mem_kv",
    )(mem, *weights, *cast)


def _prompt_tail(x1, p, kb, vb, weights, seq):
    batch = kb.shape[0]
    tt = ROW_TILE
    block = PROMPT_TILES * tt
    steps_per_seq = seq // block
    n_steps = batch * steps_per_seq
    cur = lambda s: jnp.minimum(s, n_steps - 1)
    prev = lambda s: jnp.maximum(s - 1, 0)
    tok = lambda w: pl.BlockSpec((block, w), lambda s: (cur(s), 0))
    per_seq = lambda r, w: pl.BlockSpec((None, r, w), lambda s: (cur(s) // steps_per_seq, 0, 0))
    out_tok = pl.BlockSpec((block, D_MODEL), lambda s: (prev(s), 0))
    return pl.pallas_call(
        functools.partial(_prompt_tail_kernel, steps_per_seq=steps_per_seq, n_steps=n_steps),
        grid=(n_steps + 1,),
        in_specs=[tok(D_MODEL), tok(P_WIDTH), per_seq(N_MEM, D_MODEL), per_seq(N_MEM, D_MODEL)]
                 + [_resident(w) for w in weights],
        out_specs=[out_tok, per_seq(SUBLANES, RG_WIDTH), per_seq(GLA_KEY_WIDTH, GLA_DV)],
        out_shape=[jax.ShapeDtypeStruct((batch * seq, D_MODEL), F32),
                   jax.ShapeDtypeStruct((batch, SUBLANES, RG_WIDTH), F32),
                   jax.ShapeDtypeStruct((batch, GLA_KEY_WIDTH, GLA_DV), F32)],
        scratch_shapes=[pltpu.VMEM((SUBLANES + tt, RG_WIDTH), F32),
                        pltpu.VMEM((SUBLANES, RG_WIDTH), F32),
                        pltpu.VMEM((2, PAIR_K, PAIR_V), F32),
                        pltpu.VMEM((GLA_KEY_WIDTH // LANES, tt, LANES), F32),
                        pltpu.VMEM((2, PROMPT_TILES, tt, D_MODEL), F32)],
        compiler_params=_params(("arbitrary",)),
        name="prompt_tail",
    )(x1, p, kb, vb, *weights)


def _sample_mix(x1, p, econv, eh, s0, weights, seq):
    rows = econv.shape[0]
    first = (x1.shape[0] - rows) // ROW_TILE
    nb = ROW_TILE // seq
    row_spec = lambda w: pl.BlockSpec((ROW_TILE, w), lambda i: (i, 0))
    tail_spec = lambda w: pl.BlockSpec((ROW_TILE, w), lambda i: (first + i, 0))
    state_spec = pl.BlockSpec((nb, GLA_KEY_WIDTH, GLA_DV), lambda i: (i, 0, 0))
    return pl.pallas_call(
        functools.partial(_sample_mix_kernel, seq=seq),
        grid=(rows // ROW_TILE,),
        in_specs=[tail_spec(D_MODEL), tail_spec(P_WIDTH), row_spec(RG_WIDTH), row_spec(RG_WIDTH),
                  state_spec] + [_resident(w) for w in weights],
        out_specs=[row_spec(D_MODEL), row_spec(D_MODEL),
                   pl.BlockSpec((nb, RG_WIDTH), lambda i: (i, 0)), state_spec],
        out_shape=[jax.ShapeDtypeStruct((rows, D_MODEL), F32),
                   jax.ShapeDtypeStruct((rows, D_MODEL), F32),
                   jax.ShapeDtypeStruct((rows // seq, RG_WIDTH), F32),
                   jax.ShapeDtypeStruct((rows // seq, GLA_KEY_WIDTH, GLA_DV), F32)],
        scratch_shapes=[pltpu.VMEM((RG_WIDTH // LANES, ROW_TILE, LANES), F32),
                        pltpu.VMEM((GLA_KEY_WIDTH // LANES, ROW_TILE, LANES), F32)],
        compiler_params=_params(("parallel",)),
        name="sample_mix",
    )(x1, p, econv, eh, s0, *weights)


def _head_interleaved_rows(x):
    b, m, h, dh = x.shape
    tiles = dh // LANES
    return (x.reshape(b, m, h, tiles, LANES).transpose(0, 1, 3, 2, 4)
            .reshape(b * m * tiles * h, LANES))


def _sample_attn_tail(q, k, v, x2, weights, seq):
    rows = q.shape[0]
    nb = SAMPLE_ATTN_BATCH
    row_spec = pl.BlockSpec((nb * seq, D_MODEL), lambda i: (i, 0))
    kv_rows = k.shape[0] // (rows // seq)
    kv_spec = pl.BlockSpec((nb * kv_rows, LANES), lambda i: (i, 0))
    return pl.pallas_call(
        functools.partial(_sample_attn_tail_kernel, seq=seq),
        grid=(rows // (nb * seq),),
        in_specs=[row_spec, kv_spec, kv_spec, row_spec] + [_resident(w) for w in weights],
        out_specs=row_spec,
        out_shape=jax.ShapeDtypeStruct((rows, D_MODEL), F32),
        scratch_shapes=[pltpu.VMEM((nb * seq, D_MODEL), F32)],
        compiler_params=_params(("parallel",)),
        name="sample_attn_tail",
    )(q, k, v, x2, *weights)


def _block_diag_heads(w):
    h, n, _ = w.shape
    eye = jnp.eye(h, dtype=w.dtype)
    return (eye[:, None, :, None] * w[:, :, None, :]).reshape(h * n, h * n)


def kernel(x_prompt, x_sample, cache_mem_k, cache_mem_v, state_conv, state_rglru, state_gla, mem_prompt, ffn1_norm, ffn1_w_gate, ffn1_w_up, ffn1_w_down, mix_norm, w_in, conv_w, conv_b, rg_w_a, rg_b_a, rg_w_x, rg_b_x, rg_lambda, rg_out_norm, gla_w_a2, gla_b_a2, gla_out_norm, w_out, xattn_norm, mem_norm, w_cq, w_ck, w_cv, w_co, ffn2_norm, ffn2_w_gate, ffn2_w_up, ffn2_w_down, final_norm):
    bp, tp, _ = x_prompt.shape
    bs, ts, _ = x_sample.shape
    assert ts == SUBLANES, "each sample sequence must fill exactly one 8-row sublane group"
    assert tp % (PROMPT_TILES * ROW_TILE) == 0 and (bs * ts) % (FFN_IN_TILES * ROW_TILE) == 0
    assert bs % SAMPLE_ATTN_BATCH == 0 and (bp * N_MEM) % ROW_TILE == 0
    assert cache_mem_k.shape[1:] == (N_MEM, XA_HEADS, XA_HEAD_DIM) and mem_prompt.shape[1] == N_MEM
    row = lambda g: g.reshape(1, -1)
    bf = lambda w: w.astype(BF16)

    w_in_p = bf(jnp.pad(w_in, ((0, 0), (0, P_WIDTH - D_IN))))
    hp = RG_HEADS // 2
    w_gate_rg = bf(jnp.stack([
        jnp.concatenate([_block_diag_heads(rg_w_a[c * hp:(c + 1) * hp]),
                         _block_diag_heads(rg_w_x[c * hp:(c + 1) * hp])], axis=1)
        for c in range(2)]))
    w_a2_p = bf(jnp.pad(gla_w_a2, ((0, A_PAD - GLA_GATE_RANK), (0, 0))))

    mem_k, mem_v, mem_kb, mem_vb, w1_gate, w1_up, w1_down = _mem_kv(
        mem_prompt.reshape(bp * N_MEM, D_MODEL), (row(mem_norm), bf(w_ck), bf(w_cv)),
        cast=(ffn1_w_gate, ffn1_w_up, ffn1_w_down))
    ffn_in_w = (row(ffn1_norm), w1_gate, w1_up, w1_down, row(mix_norm), w_in_p)
    x1, p, w2_gate, w2_up, w2_down, w_out_b, w_cq_b, w_co_b = _ffn_in(
        x_prompt.reshape(bp * tp, D_MODEL), x_sample.reshape(bs * ts, D_MODEL), ffn_in_w,
        cast=(ffn2_w_gate, ffn2_w_up, ffn2_w_down, w_out, w_cq, w_co))
    mixer_w = (conv_w, row(conv_b), w_gate_rg, row(rg_b_a), row(rg_b_x), row(rg_lambda),
               row(rg_out_norm), w_a2_p, row(gla_b_a2), row(jnp.tile(gla_out_norm, GLA_HEADS)),
               w_out_b)
    ffn2_w = (row(ffn2_norm), w2_gate, w2_up, w2_down, row(final_norm))
    y_p, rgh_p, s_p = _prompt_tail(
        x1, p, mem_kb.reshape(bp, N_MEM, D_MODEL), mem_vb.reshape(bp, N_MEM, D_MODEL),
        mixer_w + (row(xattn_norm), w_cq_b, w_co_b) + ffn2_w, tp)
    groups = p.reshape(-1, ts, P_WIDTH)
    per_seq = tp // ts
    conv_p = groups[per_seq - 1:bp * per_seq:per_seq, ts - (CONV_WIDTH - 1):, OFF_RG_X:OFF_RG_Y]
    conv_s = groups[bp * per_seq:, ts - (CONV_WIDTH - 1):, OFF_RG_X:OFF_RG_Y]

    econv = jnp.pad(state_conv, ((0, 0), (ts - (CONV_WIDTH - 1), 0), (0, 0))).reshape(bs * ts, RG_WIDTH)
    eh = jnp.pad(state_rglru[:, None, :], ((0, 0), (0, ts - 1), (0, 0))).reshape(bs * ts, RG_WIDTH)
    x2_s, q_s, rgh_s, s_s = _sample_mix(
        x1, p, econv, eh, state_gla.reshape(bs, GLA_KEY_WIDTH, GLA_DV),
        mixer_w + (row(xattn_norm), w_cq_b), ts)
    y_s = _sample_attn_tail(q_s, _head_interleaved_rows(cache_mem_k), _head_interleaved_rows(cache_mem_v),
                            x2_s, (w_co_b,) + ffn2_w, ts)

    return (y_p.reshape(bp, tp, D_MODEL), y_s.reshape(bs, ts, D_MODEL),
            mem_k.reshape(bp, N_MEM, XA_HEADS, XA_HEAD_DIM),
            mem_v.reshape(bp, N_MEM, XA_HEADS, XA_HEAD_DIM),
            conv_p, rgh_p[:, 0, :], s_p.reshape(bp, GLA_HEADS, GLA_DK, GLA_DV),
            conv_s, rgh_s, s_s.reshape(bs, GLA_HEADS, GLA_DK, GLA_DV))
```

```python
import functools

import jax
import jax.numpy as jnp
import numpy as np
from jax import lax
from jax.experimental import pallas as pl
from jax.experimental.pallas import tpu as pltpu

F32 = jnp.float32
BF16 = jnp.bfloat16

D_MODEL = 1024
D_FF = 2816
RG_WIDTH = 512
RG_HEADS = 8
RG_HEAD_DIM = 64
CONV_WIDTH = 4
RG_C = 8.0
GLA_WIDTH = 512
GLA_HEADS = 4
GLA_DV = 128
GLA_DK = 64
GLA_KEY_WIDTH = 256
GLA_GATE_RANK = 16
GLA_GATE_NORMALIZER = 16.0
GLA_CHUNK = 32
N_MEM = 256
XA_HEADS = 4
XA_HEAD_DIM = 256
EPS = 1e-6
MASKED_SCORE = -np.inf

OFF_RG_X = 0
OFF_RG_Y = 512
OFF_Q = 1024
OFF_K = 1280
OFF_V = 1536
OFF_G = 2048
OFF_A = 2560
D_IN = 2576

LANES = 128
SUBLANES = 8
BF16_SUBLANES = 16
A_PAD = LANES
P_WIDTH = OFF_A + A_PAD
PAIR_K = 2 * GLA_DK
PAIR_V = 2 * GLA_DV
VMEM_LIMIT = 58 * 1024 * 1024

ROW_TILE = 256
FFN_IN_TILES = 2
PROMPT_TILES = 2
FF_CHUNK = 256
FF_DOWN_GROUP = 4
FFN2_PLACEMENT = (1, 1, 1, 1, 1, 1, 1, 1, 1, 1, 1)
SAMPLE_ATTN_BATCH = 8


def _rms(x, g):
    return x * lax.rsqrt(jnp.mean(x * x, axis=-1, keepdims=True) + EPS) * g


def _mm(a, w):
    return jnp.dot(a.astype(BF16), w, preferred_element_type=F32)


def _mm_nt(a, b):
    return lax.dot_general(a.astype(BF16), b.astype(BF16), (((1,), (1,)), ((), ())),
                           preferred_element_type=F32)


def _mm_tn(a, b):
    return lax.dot_general(a.astype(BF16), b.astype(BF16), (((0,), (0,)), ((), ())),
                           preferred_element_type=F32)


def _silu(x):
    return x * jax.nn.sigmoid(x)


def _gelu_tanh(x):
    c = np.float32(np.sqrt(2.0 / np.pi))
    return x * (0.5 * (1.0 + jnp.tanh(c * (x + 0.044715 * (x * x * x)))))


def _softplus(x):
    return jnp.maximum(x, 0.0) + jnp.log1p(jnp.exp(-jnp.abs(x)))


def _ffn(x, norm_ref, wg_ref, wu_ref, wd_ref):
    h = _rms(x, norm_ref[...]).astype(BF16)
    split = (D_FF // FF_CHUNK // 2) * FF_CHUNK
    out = None
    for cols in (slice(0, split), slice(split, D_FF)):
        g = jnp.dot(h, wg_ref[:, cols], preferred_element_type=F32)
        u = jnp.dot(h, wu_ref[:, cols], preferred_element_type=F32)
        part = _mm(_silu(g) * u, wd_ref[cols, :])
        out = part if out is None else out + part
    return out


def _last_rows(buf_ref, x, seg):
    n = x.shape[0] // seg
    outs = []
    for j in range(x.shape[1] // LANES):
        buf_ref[j] = x[:, j * LANES:(j + 1) * LANES]
        outs.append(buf_ref[j, pl.ds(seg - 1, n, stride=seg), :])
    return jnp.concatenate(outs, axis=1)


def _ffn_pieces(x, norm_ref, wg_ref, wu_ref, wd_ref, out):
    h = _rms(x, norm_ref[...]).astype(BF16)
    acc = None
    acts = []
    n = D_FF // FF_CHUNK
    for c in range(n):
        cols = slice(c * FF_CHUNK, (c + 1) * FF_CHUNK)
        g = jnp.dot(h, wg_ref[:, cols], preferred_element_type=F32)
        u = jnp.dot(h, wu_ref[:, cols], preferred_element_type=F32)
        acts.append((_silu(g) * u).astype(BF16))
        if len(acts) == FF_DOWN_GROUP or c == n - 1:
            lo = (c + 1 - len(acts)) * FF_CHUNK
            part = jnp.dot(jnp.concatenate(acts, axis=1), wd_ref[lo:(c + 1) * FF_CHUNK, :],
                           preferred_element_type=F32)
            acc = part if acc is None else acc + part
            acts = []
        yield
    out.append(acc)


def _shift_rows(x, s):
    return pltpu.roll(x, s, axis=0)


def _row_in_segment(shape, seg):
    return lax.broadcasted_iota(jnp.int32, shape, 0) & (seg - 1)


def _segment_cumsum(x, seg):
    pos = _row_in_segment(x.shape, seg)
    s = 1
    while s < seg:
        x = jnp.where(pos >= s, x + _shift_rows(x, s), x)
        s *= 2
    return x


def _segment_affine_scan(a, u, seg):
    pos = _row_in_segment(a.shape, seg)
    s = 1
    while s < seg:
        m = pos >= s
        u = jnp.where(m, a * _shift_rows(u, s) + u, u)
        a = jnp.where(m, a * _shift_rows(a, s), a)
        s *= 2
    return a, u


def _conv(xb, sh1, sh2, sh3, cw_ref, cb_ref):
    y = cb_ref[...] + sh3 * cw_ref[0:1, :]
    y = y + sh2 * cw_ref[1:2, :]
    y = y + sh1 * cw_ref[2:3, :]
    return y + xb * cw_ref[3:4, :]


def _rg_gates(xc, wgate_ref, ba_ref, bx_ref):
    half = RG_WIDTH // 2
    r, i = [], []
    for c in range(2):
        z = _mm(xc[:, c * half:(c + 1) * half], wgate_ref[c])
        r.append(z[:, :half])
        i.append(z[:, half:])
    r = jax.nn.sigmoid(jnp.concatenate(r, axis=1) + ba_ref[...])
    i = jax.nn.sigmoid(jnp.concatenate(i, axis=1) + bx_ref[...])
    return r, i


def _rg_decay_input(xc, r, i, lam_ref):
    log_a = (-RG_C * _softplus(-lam_ref[...])) * r
    a = jnp.exp(log_a)
    mult = jnp.sqrt(-jnp.tanh(log_a) * (a * a + 1.0))
    return a, mult * (i * xc)


def _gla_log_decay(a_low, wa2_ref, ba2_ref):
    z = _mm(a_low, wa2_ref[...]) + ba2_ref[...]
    return (-_softplus(-z)) / GLA_GATE_NORMALIZER


def _gla_intra(qt, kt, v, chunk):
    t = qt.shape[0]
    shift = int(np.log2(chunk))
    ri = lax.broadcasted_iota(jnp.int32, (t, t), 0)
    ci = lax.broadcasted_iota(jnp.int32, (t, t), 1)
    causal = (ri >= ci) & ((ri >> shift) == (ci >> shift))
    lane = lax.broadcasted_iota(jnp.int32, (t, PAIR_K), 1)
    outs = []
    for h in range(GLA_HEADS):
        p = h // 2
        qp = qt[:, p * PAIR_K:(p + 1) * PAIR_K]
        kp = kt[:, p * PAIR_K:(p + 1) * PAIR_K]
        mine = (lane >= GLA_DK) if (h % 2) else (lane < GLA_DK)
        s = _mm_nt(jnp.where(mine, qp, 0.0), kp)
        attn = jnp.where(causal, s, 0.0)
        outs.append(_mm(attn, v[:, h * GLA_DV:(h + 1) * GLA_DV].astype(BF16)))
    return outs


def _pair_blockdiag_mask():
    r = lax.broadcasted_iota(jnp.int32, (PAIR_K, PAIR_V), 0)
    c = lax.broadcasted_iota(jnp.int32, (PAIR_K, PAIR_V), 1)
    return (r < GLA_DK) == (c < GLA_DV)


def _gla_chunk_updates(k, v, b, b_last, chunk):
    bd_mask = _pair_blockdiag_mask()
    kv = {}
    for c in range(k.shape[0] // chunk):
        rows = slice(c * chunk, (c + 1) * chunk)
        for p in range(2):
            kl = slice(p * PAIR_K, (p + 1) * PAIR_K)
            kd = k[rows, kl] * jnp.exp(b_last[c:c + 1, kl] - b[rows, kl])
            kv[c, p] = jnp.where(bd_mask, _mm_tn(kd, v[rows, p * PAIR_V:(p + 1) * PAIR_V]), 0.0)
    return kv


def _gla_combine(intra, qt, states, chunk):
    n_chunks = qt.shape[0] // chunk
    heads = []
    for p in range(2):
        kl = slice(p * PAIR_K, (p + 1) * PAIR_K)
        inter = jnp.concatenate(
            [_mm(qt[c * chunk:(c + 1) * chunk, kl], states[c, p]) for c in range(n_chunks)],
            axis=0)
        heads.append(intra[2 * p] + inter[:, :GLA_DV])
        heads.append(intra[2 * p + 1] + inter[:, GLA_DV:])
    return heads


def _gla_out(o_heads, g, norm_ref):
    outs = []
    for h in range(GLA_HEADS):
        outs.append(_rms(o_heads[h], norm_ref[:, h * GLA_DV:(h + 1) * GLA_DV]))
    return jnp.concatenate(outs, axis=1) * _silu(g)


def _softmax_rows(s):
    m = jnp.max(s, axis=-1, keepdims=True)
    e = jnp.exp(s - m)
    return e * (1.0 / jnp.sum(e, axis=-1, keepdims=True))


def _split_p(p_ref, rows=slice(None)):
    xb = p_ref[rows, OFF_RG_X:OFF_RG_Y]
    yb = p_ref[rows, OFF_RG_Y:OFF_Q]
    q = p_ref[rows, OFF_Q:OFF_K]
    k = p_ref[rows, OFF_K:OFF_V]
    v = p_ref[rows, OFF_V:OFF_G]
    g = p_ref[rows, OFF_G:OFF_A]
    a_low = p_ref[rows, OFF_A:P_WIDTH]
    return xb, yb, q, k, v, g, a_low


def _gla_prepare(q, k, log_a, chunk):
    b = _segment_cumsum(log_a, chunk)
    qt = (q * (GLA_DK ** -0.5)) * jnp.exp(b)
    kt = k * jnp.exp(-b)
    return b, qt, kt


def _mix_project(rg_out, gla_out, wout_ref):
    return (_mm(rg_out, wout_ref[0:RG_WIDTH, :]) + _mm(gla_out, wout_ref[RG_WIDTH:, :]))


def _ffn_in_kernel(xa_ref, xb_ref, n1_ref, wg_ref, wu_ref, wd_ref, n2_ref, win_ref, *refs, steps_a):
    from_a = pl.program_id(0) < steps_a
    n_cast = (len(refs) - 2) // 2
    cast_src, (x1_ref, p_ref), cast_dst = refs[:n_cast], refs[n_cast:n_cast + 2], refs[n_cast + 2:]
    for sub in range(xa_ref.shape[0] // ROW_TILE):
        rows = slice(sub * ROW_TILE, (sub + 1) * ROW_TILE)
        x = jnp.where(from_a, xa_ref[rows, :], xb_ref[rows, :])
        x1 = x + 0.5 * _ffn(x, n1_ref, wg_ref, wu_ref, wd_ref)
        x1_ref[rows, :] = x1
        p_ref[rows, :] = _mm(_rms(x1, n2_ref[...]), win_ref[...])
    for src, dst in zip(cast_src, cast_dst):
        dst[...] = src[...].astype(BF16)


def _mem_kv_kernel(m_ref, n_ref, wk_ref, wv_ref, *refs):
    n_cast = (len(refs) - 4) // 2
    cast_src, (k_ref, v_ref, kb_ref, vb_ref), cast_dst = refs[:n_cast], refs[n_cast:n_cast + 4], refs[n_cast + 4:]
    h = _rms(m_ref[...], n_ref[...]).astype(BF16)
    k = jnp.dot(h, wk_ref[...], preferred_element_type=F32)
    v = jnp.dot(h, wv_ref[...], preferred_element_type=F32)
    for h in range(XA_HEADS):
        sl = slice(h * XA_HEAD_DIM, (h + 1) * XA_HEAD_DIM)
        k_ref[:, h, :] = k[:, sl]
        v_ref[:, h, :] = v[:, sl]
    kb_ref[...] = k.astype(BF16)
    vb_ref[...] = v.astype(BF16)
    for src, dst in zip(cast_src, cast_dst):
        dst[...] = src[...].astype(BF16)


def _prompt_tail_kernel(
        x1_ref, p_ref, kb_ref, vb_ref,
        cw_ref, cb_ref, wgate_ref, ba_ref, bx_ref, lam_ref, rgn_ref,
        wa2_ref, ba2_ref, glan_ref, wout_ref,
        xn_ref, wcq_ref, wco_ref, n2_ref, wg_ref, wu_ref, wd_ref, fn_ref,
        y_ref, rgh_ref, s_ref,
        xpad_ref, h_ref, sbd_ref, b_ref, x3_ref, *, steps_per_seq, n_steps):
    step = pl.program_id(0)
    t_idx = jnp.minimum(step, n_steps - 1) % steps_per_seq
    slot = step % 2
    tt = ROW_TILE
    subs = x1_ref.shape[0] // tt
    n_chunks = tt // GLA_CHUNK

    @pl.when(step == 0)
    def _():
        x3_ref[1] = jnp.zeros((subs, tt, D_MODEL), F32)

    @pl.when(t_idx == 0)
    def _():
        xpad_ref[0:SUBLANES, :] = jnp.zeros((SUBLANES, RG_WIDTH), F32)
        h_ref[...] = jnp.zeros_like(h_ref)
        sbd_ref[...] = jnp.zeros_like(sbd_ref)

    for sub in range(subs):
        rows = slice(sub * tt, (sub + 1) * tt)
        x3_prev = x3_ref[1 - slot, sub]
        ffn_out = []
        ffn = _ffn_pieces(x3_prev, n2_ref, wg_ref, wu_ref, wd_ref, ffn_out)
        placement = iter(FFN2_PLACEMENT)

        def emit_ffn():
            for _ in range(next(placement)):
                next(ffn)

        emit_ffn()

        xb, yb, q, k, v, g, a_low = _split_p(p_ref, rows)

        half = RG_WIDTH // 2
        gated = []
        for c in range(2):
            cols = slice(c * half, (c + 1) * half)
            xb_c = xb[:, cols]
            xpad_ref[SUBLANES:SUBLANES + tt, cols] = xb_c
            sh = [xpad_ref[SUBLANES - j:SUBLANES - j + tt, cols] for j in range(1, CONV_WIDTH)]
            xc = cb_ref[:, cols] + sh[2] * cw_ref[0:1, cols]
            xc = xc + sh[1] * cw_ref[1:2, cols]
            xc = xc + sh[0] * cw_ref[2:3, cols]
            xc = xc + xb_c * cw_ref[3:4, cols]
            xpad_ref[0:SUBLANES, cols] = xb_c[tt - SUBLANES:tt, :]
            z = _mm(xc, wgate_ref[c])
            r = jax.nn.sigmoid(z[:, :half] + ba_ref[:, cols])
            i = jax.nn.sigmoid(z[:, half:] + bx_ref[:, cols])
            log_a = (-RG_C * _softplus(-lam_ref[:, cols])) * r
            a = jnp.exp(log_a)
            u = jnp.sqrt(-jnp.tanh(log_a) * (a * a + 1.0)) * (i * xc)
            emit_ffn()
            a_grp, h_grp = _segment_affine_scan(a, u, SUBLANES)
            carry = h_ref[0:1, cols]
            groups = []
            for gi in range(tt // SUBLANES):
                grp = slice(gi * SUBLANES, (gi + 1) * SUBLANES)
                groups.append(h_grp[grp, :] + a_grp[grp, :] * carry)
                carry = groups[-1][SUBLANES - 1:SUBLANES, :]
            h_ref[:, cols] = jnp.broadcast_to(carry, (SUBLANES, half))
            gated.append(jnp.concatenate(groups, axis=0) * _gelu_tanh(yb[:, cols]))
            emit_ffn()
        rg_out = _rms(jnp.concatenate(gated, axis=1), rgn_ref[...]).astype(BF16)

        log_a = _gla_log_decay(a_low, wa2_ref, ba2_ref)
        b, qt, kt = _gla_prepare(q, k, log_a, GLA_CHUNK)
        emit_ffn()
        b_last = _last_rows(b_ref, b, GLA_CHUNK)
        dec_t = jnp.exp(b_last).T
        kv = _gla_chunk_updates(k, v, b, b_last, GLA_CHUNK)
        states = {}
        for p in range(2):
            s_bd = sbd_ref[p]
            for c in range(n_chunks):
                states[c, p] = s_bd.astype(BF16)
                s_bd = s_bd * dec_t[p * PAIR_K:(p + 1) * PAIR_K, c:c + 1] + kv[c, p]
            sbd_ref[p] = s_bd
        emit_ffn()
        intra = _gla_intra(qt, kt, v, GLA_CHUNK)
        emit_ffn()
        gla_out = _gla_out(_gla_combine(intra, qt, states, GLA_CHUNK), g, glan_ref).astype(BF16)
        emit_ffn()

        x2 = x1_ref[rows, :] + _mix_project(rg_out, gla_out, wout_ref)

        qx = _mm(_rms(x2, xn_ref[...]), wcq_ref[...])
        emit_ffn()
        head = lambda h: slice(h * XA_HEAD_DIM, (h + 1) * XA_HEAD_DIM)
        s = jnp.concatenate([_mm_nt(qx[:, head(h)], kb_ref[:, head(h)]) for h in range(XA_HEADS)], axis=0)
        pr = _softmax_rows(s * (XA_HEAD_DIM ** -0.5))
        emit_ffn()
        o = jnp.concatenate([_mm(pr[h * tt:(h + 1) * tt, :], vb_ref[:, head(h)]) for h in range(XA_HEADS)], axis=1)
        x3_ref[slot, sub] = x2 + _mm(o, wco_ref[...])
        for _ in ffn:
            pass
        y_ref[rows, :] = _rms(x3_prev + 0.5 * ffn_out[0], fn_ref[...])

    @pl.when((t_idx == steps_per_seq - 1) & (step < n_steps))
    def _():
        rgh_ref[...] = h_ref[...]
        row = lax.broadcasted_iota(jnp.int32, (PAIR_K, GLA_DV), 0)
        for p in range(2):
            s_bd = sbd_ref[p]
            s_ref[p * PAIR_K:(p + 1) * PAIR_K, :] = jnp.where(
                row < GLA_DK, s_bd[:, :GLA_DV], s_bd[:, GLA_DV:])


def _sample_mix_kernel(
        x1_ref, p_ref, econv_ref, eh_ref, s0_ref,
        cw_ref, cb_ref, wgate_ref, ba_ref, bx_ref, lam_ref, rgn_ref,
        wa2_ref, ba2_ref, glan_ref, wout_ref, xn_ref, wcq_ref,
        x2_ref, q_ref, rgh_ref, s_ref,
        hs_ref, b_ref, *, seq):
    rows_n = x1_ref.shape[0]
    nb = rows_n // seq

    xb, yb, q, k, v, g, a_low = _split_p(p_ref)

    pos = _row_in_segment(xb.shape, seq)
    econv = econv_ref[...]
    sh = []
    for j in range(1, CONV_WIDTH):
        sh.append(jnp.where(pos >= j, _shift_rows(xb, j), _shift_rows(econv, rows_n - seq + j)))
    xc = _conv(xb, sh[0], sh[1], sh[2], cw_ref, cb_ref)
    r, i = _rg_gates(xc, wgate_ref, ba_ref, bx_ref)
    a, u = _rg_decay_input(xc, r, i, lam_ref)
    u = u + a * eh_ref[...]
    _, hs = _segment_affine_scan(a, u, seq)
    rgh_ref[...] = _last_rows(hs_ref, hs, seq)
    rg_out = _rms(hs * _gelu_tanh(yb), rgn_ref[...]).astype(BF16)

    log_a = _gla_log_decay(a_low, wa2_ref, ba2_ref)
    b, qt, kt = _gla_prepare(q, k, log_a, seq)
    intra = _gla_intra(qt, kt, v, seq)
    b_last = _last_rows(b_ref, b, seq)
    dec_t = jnp.exp(b_last).T
    kv = _gla_chunk_updates(k, v, b, b_last, seq)
    row = lax.broadcasted_iota(jnp.int32, (PAIR_K, GLA_DV), 0)
    top = row < GLA_DK
    states = {}
    for c in range(nb):
        for p in range(2):
            kl = slice(p * PAIR_K, (p + 1) * PAIR_K)
            s_pair = s0_ref[c, kl, :]
            s_bd = jnp.concatenate([jnp.where(top, s_pair, 0.0), jnp.where(top, 0.0, s_pair)], axis=1)
            states[c, p] = s_bd.astype(BF16)
            s_new = s_bd * dec_t[kl, c:c + 1] + kv[c, p]
            s_ref[c, kl, :] = jnp.where(top, s_new[:, :GLA_DV], s_new[:, GLA_DV:])
    gla_out = _gla_out(_gla_combine(intra, qt, states, seq), g, glan_ref).astype(BF16)

    x2 = x1_ref[...] + _mix_project(rg_out, gla_out, wout_ref)
    x2_ref[...] = x2
    q_ref[...] = _mm(_rms(x2, xn_ref[...]), wcq_ref[...])


def _sample_attn_tail_kernel(q_ref, k_ref, v_ref, x2_ref, wco_ref, n2_ref, wg_ref, wu_ref, wd_ref, fn_ref,
                             y_ref, o_ref, *, seq):
    tiles = XA_HEAD_DIM // LANES
    group = tiles * XA_HEADS
    r = N_MEM * group
    nb = k_ref.shape[0] // r
    hs = XA_HEADS * seq
    lane = lax.broadcasted_iota(jnp.int32, (nb * hs, r), 1)
    head = (lax.broadcasted_iota(jnp.int32, (nb * hs, r), 0) // seq) & (XA_HEADS - 1)
    own = (lane & (group - 1)) == head
    s = []
    for j in range(nb):
        q = q_ref[j * seq:(j + 1) * seq, :]
        q_parts = jnp.concatenate(
            [q[:, c * LANES:(c + 1) * LANES] for c in range(XA_HEADS * tiles)], axis=0)
        part = _mm_nt(q_parts, k_ref[j * r:(j + 1) * r, :])
        for h in range(XA_HEADS):
            acc = part[h * tiles * seq:(h * tiles + 1) * seq, :]
            for c in range(1, tiles):
                blk = part[(h * tiles + c) * seq:(h * tiles + c + 1) * seq, :]
                acc = acc + pltpu.roll(blk, r - c * XA_HEADS, axis=1)
            s.append(acc)
    p = _softmax_rows(jnp.where(own, jnp.concatenate(s, axis=0) * (XA_HEAD_DIM ** -0.5), MASKED_SCORE))
    for j in range(nb):
        pj = p[j * hs:(j + 1) * hs, :]
        p_all = jnp.concatenate(
            [pj] + [pltpu.roll(pj, c * XA_HEADS, axis=1) for c in range(1, tiles)], axis=0)
        o = _mm(p_all, v_ref[j * r:(j + 1) * r, :].astype(BF16))
        for c in range(tiles):
            for h in range(XA_HEADS):
                col = h * XA_HEAD_DIM + c * LANES
                o_ref[j * seq:(j + 1) * seq, col:col + LANES] = (
                    o[(c * XA_HEADS + h) * seq:(c * XA_HEADS + h + 1) * seq, :])
    x3 = x2_ref[...] + _mm(o_ref[...], wco_ref[...])
    x4 = x3 + 0.5 * _ffn(x3, n2_ref, wg_ref, wu_ref, wd_ref)
    y_ref[...] = _rms(x4, fn_ref[...])


def _resident(arr):
    nd = arr.ndim
    return pl.BlockSpec(arr.shape, lambda *_: (0,) * nd, pipeline_mode=pl.Buffered(1))


def _params(sem):
    return pltpu.CompilerParams(dimension_semantics=sem, vmem_limit_bytes=VMEM_LIMIT)


def _slab_spec(w, steps):
    per, revisit = w.shape[0] // steps, 1
    while per % BF16_SUBLANES:
        per, revisit = per * 2, revisit * 2
    return pl.BlockSpec((per, w.shape[1]), lambda i: (jnp.minimum(i, steps - 1) // revisit, 0))


def _ffn_in(xa, xb, weights, cast=()):
    block = FFN_IN_TILES * ROW_TILE
    steps_a, steps_b = xa.shape[0] // block, xb.shape[0] // block
    rows = xa.shape[0] + xb.shape[0]
    spec_a = pl.BlockSpec((block, D_MODEL), lambda i: (jnp.minimum(i, steps_a - 1), 0))
    spec_b = pl.BlockSpec((block, D_MODEL), lambda i: (jnp.maximum(i - steps_a, 0), 0))
    row_spec = lambda w: pl.BlockSpec((block, w), lambda i: (i, 0))
    slabs = [_slab_spec(w, steps_a) for w in cast]
    return pl.pallas_call(
        functools.partial(_ffn_in_kernel, steps_a=steps_a),
        grid=(steps_a + steps_b,),
        in_specs=[spec_a, spec_b] + [_resident(w) for w in weights] + slabs,
        out_specs=[row_spec(D_MODEL), row_spec(P_WIDTH)] + slabs,
        out_shape=[jax.ShapeDtypeStruct((rows, D_MODEL), F32),
                   jax.ShapeDtypeStruct((rows, P_WIDTH), F32)]
                  + [jax.ShapeDtypeStruct(w.shape, BF16) for w in cast],
        compiler_params=_params(("arbitrary",)),
        name="ffn_in",
    )(xa, xb, *weights, *cast)


def _mem_kv(mem, weights, cast=()):
    rows = mem.shape[0]
    steps = rows // ROW_TILE
    row_spec = pl.BlockSpec((ROW_TILE, D_MODEL), lambda i: (i, 0))
    head_spec = pl.BlockSpec((ROW_TILE, XA_HEADS, XA_HEAD_DIM), lambda i: (i, 0, 0))
    slabs = [_slab_spec(w, steps) for w in cast]
    return pl.pallas_call(
        _mem_kv_kernel,
        grid=(steps,),
        in_specs=[row_spec] + [_resident(w) for w in weights] + slabs,
        out_specs=[head_spec] * 2 + [row_spec] * 2 + slabs,
        out_shape=[jax.ShapeDtypeStruct((rows, XA_HEADS, XA_HEAD_DIM), F32)] * 2
                  + [jax.ShapeDtypeStruct((rows, D_MODEL), BF16)] * 2
                  + [jax.ShapeDtypeStruct(w.shape, BF16) for w in cast],
        compiler_params=_params(("arbitrary",)),
        name="mem_kv",
    )(mem, *weights, *cast)


def _prompt_tail(x1, p, kb, vb, weights, seq):
    batch = kb.shape[0]
    tt = ROW_TILE
    block = PROMPT_TILES * tt
    steps_per_seq = seq // block
    n_steps = batch * steps_per_seq
    cur = lambda s: jnp.minimum(s, n_steps - 1)
    prev = lambda s: jnp.maximum(s - 1, 0)
    tok = lambda w: pl.BlockSpec((block, w), lambda s: (cur(s), 0))
    per_seq = lambda r, w: pl.BlockSpec((None, r, w), lambda s: (cur(s) // steps_per_seq, 0, 0))
    out_tok = pl.BlockSpec((block, D_MODEL), lambda s: (prev(s), 0))
    return pl.pallas_call(
        functools.partial(_prompt_tail_kernel, steps_per_seq=steps_per_seq, n_steps=n_steps),
        grid=(n_steps + 1,),
        in_specs=[tok(D_MODEL), tok(P_WIDTH), per_seq(N_MEM, D_MODEL), per_seq(N_MEM, D_MODEL)]
                 + [_resident(w) for w in weights],
        out_specs=[out_tok, per_seq(SUBLANES, RG_WIDTH), per_seq(GLA_KEY_WIDTH, GLA_DV)],
        out_shape=[jax.ShapeDtypeStruct((batch * seq, D_MODEL), F32),
                   jax.ShapeDtypeStruct((batch, SUBLANES, RG_WIDTH), F32),
                   jax.ShapeDtypeStruct((batch, GLA_KEY_WIDTH, GLA_DV), F32)],
        scratch_shapes=[pltpu.VMEM((SUBLANES + tt, RG_WIDTH), F32),
                        pltpu.VMEM((SUBLANES, RG_WIDTH), F32),
                        pltpu.VMEM((2, PAIR_K, PAIR_V), F32),
                        pltpu.VMEM((GLA_KEY_WIDTH // LANES, tt, LANES), F32),
                        pltpu.VMEM((2, PROMPT_TILES, tt, D_MODEL), F32)],
        compiler_params=_params(("arbitrary",)),
        name="prompt_tail",
    )(x1, p, kb, vb, *weights)


def _sample_mix(x1, p, econv, eh, s0, weights, seq):
    rows = econv.shape[0]
    first = (x1.shape[0] - rows) // ROW_TILE
    nb = ROW_TILE // seq
    row_spec = lambda w: pl.BlockSpec((ROW_TILE, w), lambda i: (i, 0))
    tail_spec = lambda w: pl.BlockSpec((ROW_TILE, w), lambda i: (first + i, 0))
    state_spec = pl.BlockSpec((nb, GLA_KEY_WIDTH, GLA_DV), lambda i: (i, 0, 0))
    return pl.pallas_call(
        functools.partial(_sample_mix_kernel, seq=seq),
        grid=(rows // ROW_TILE,),
        in_specs=[tail_spec(D_MODEL), tail_spec(P_WIDTH), row_spec(RG_WIDTH), row_spec(RG_WIDTH),
                  state_spec] + [_resident(w) for w in weights],
        out_specs=[row_spec(D_MODEL), row_spec(D_MODEL),
                   pl.BlockSpec((nb, RG_WIDTH), lambda i: (i, 0)), state_spec],
        out_shape=[jax.ShapeDtypeStruct((rows, D_MODEL), F32),
                   jax.ShapeDtypeStruct((rows, D_MODEL), F32),
                   jax.ShapeDtypeStruct((rows // seq, RG_WIDTH), F32),
                   jax.ShapeDtypeStruct((rows // seq, GLA_KEY_WIDTH, GLA_DV), F32)],
        scratch_shapes=[pltpu.VMEM((RG_WIDTH // LANES, ROW_TILE, LANES), F32),
                        pltpu.VMEM((GLA_KEY_WIDTH // LANES, ROW_TILE, LANES), F32)],
        compiler_params=_params(("parallel",)),
        name="sample_mix",
    )(x1, p, econv, eh, s0, *weights)


def _head_interleaved_rows(x):
    b, m, h, dh = x.shape
    tiles = dh // LANES
    return (x.reshape(b, m, h, tiles, LANES).transpose(0, 1, 3, 2, 4)
            .reshape(b * m * tiles * h, LANES))


def _sample_attn_tail(q, k, v, x2, weights, seq):
    rows = q.shape[0]
    nb = SAMPLE_ATTN_BATCH
    row_spec = pl.BlockSpec((nb * seq, D_MODEL), lambda i: (i, 0))
    kv_rows = k.shape[0] // (rows // seq)
    kv_spec = pl.BlockSpec((nb * kv_rows, LANES), lambda i: (i, 0))
    return pl.pallas_call(
        functools.partial(_sample_attn_tail_kernel, seq=seq),
        grid=(rows // (nb * seq),),
        in_specs=[row_spec, kv_spec, kv_spec, row_spec] + [_resident(w) for w in weights],
        out_specs=row_spec,
        out_shape=jax.ShapeDtypeStruct((rows, D_MODEL), F32),
        scratch_shapes=[pltpu.VMEM((nb * seq, D_MODEL), F32)],
        compiler_params=_params(("parallel",)),
        name="sample_attn_tail",
    )(q, k, v, x2, *weights)


def _block_diag_heads(w):
    h, n, _ = w.shape
    eye = jnp.eye(h, dtype=w.dtype)
    return (eye[:, None, :, None] * w[:, :, None, :]).reshape(h * n, h * n)


def kernel(x_prompt, x_sample, cache_mem_k, cache_mem_v, state_conv, state_rglru, state_gla, mem_prompt, ffn1_norm, ffn1_w_gate, ffn1_w_up, ffn1_w_down, mix_norm, w_in, conv_w, conv_b, rg_w_a, rg_b_a, rg_w_x, rg_b_x, rg_lambda, rg_out_norm, gla_w_a2, gla_b_a2, gla_out_norm, w_out, xattn_norm, mem_norm, w_cq, w_ck, w_cv, w_co, ffn2_norm, ffn2_w_gate, ffn2_w_up, ffn2_w_down, final_norm):
    bp, tp, _ = x_prompt.shape
    bs, ts, _ = x_sample.shape
    assert ts == SUBLANES, "each sample sequence must fill exactly one 8-row sublane group"
    assert tp % (PROMPT_TILES * ROW_TILE) == 0 and (bs * ts) % (FFN_IN_TILES * ROW_TILE) == 0
    assert bs % SAMPLE_ATTN_BATCH == 0 and (bp * N_MEM) % ROW_TILE == 0
    assert cache_mem_k.shape[1:] == (N_MEM, XA_HEADS, XA_HEAD_DIM) and mem_prompt.shape[1] == N_MEM
    row = lambda g: g.reshape(1, -1)
    bf = lambda w: w.astype(BF16)

    w_in_p = bf(jnp.pad(w_in, ((0, 0), (0, P_WIDTH - D_IN))))
    hp = RG_HEADS // 2
    w_gate_rg = bf(jnp.stack([
        jnp.concatenate([_block_diag_heads(rg_w_a[c * hp:(c + 1) * hp]),
                         _block_diag_heads(rg_w_x[c * hp:(c + 1) * hp])], axis=1)
        for c in range(2)]))
    w_a2_p = bf(jnp.pad(gla_w_a2, ((0, A_PAD - GLA_GATE_RANK), (0, 0))))

    mem_k, mem_v, mem_kb, mem_vb, w1_gate, w1_up, w1_down = _mem_kv(
        mem_prompt.reshape(bp * N_MEM, D_MODEL), (row(mem_norm), bf(w_ck), bf(w_cv)),
        cast=(ffn1_w_gate, ffn1_w_up, ffn1_w_down))
    ffn_in_w = (row(ffn1_norm), w1_gate, w1_up, w1_down, row(mix_norm), w_in_p)
    x1, p, w2_gate, w2_up, w2_down, w_out_b, w_cq_b, w_co_b = _ffn_in(
        x_prompt.reshape(bp * tp, D_MODEL), x_sample.reshape(bs * ts, D_MODEL), ffn_in_w,
        cast=(ffn2_w_gate, ffn2_w_up, ffn2_w_down, w_out, w_cq, w_co))
    mixer_w = (conv_w, row(conv_b), w_gate_rg, row(rg_b_a), row(rg_b_x), row(rg_lambda),
               row(rg_out_norm), w_a2_p, row(gla_b_a2), row(jnp.tile(gla_out_norm, GLA_HEADS)),
               w_out_b)
    ffn2_w = (row(ffn2_norm), w2_gate, w2_up, w2_down, row(final_norm))
    y_p, rgh_p, s_p = _prompt_tail(
        x1, p, mem_kb.reshape(bp, N_MEM, D_MODEL), mem_vb.reshape(bp, N_MEM, D_MODEL),
        mixer_w + (row(xattn_norm), w_cq_b, w_co_b) + ffn2_w, tp)
    groups = p.reshape(-1, ts, P_WIDTH)
    per_seq = tp // ts
    conv_p = groups[per_seq - 1:bp * per_seq:per_seq, ts - (CONV_WIDTH - 1):, OFF_RG_X:OFF_RG_Y]
    conv_s = groups[bp * per_seq:, ts - (CONV_WIDTH - 1):, OFF_RG_X:OFF_RG_Y]

    econv = jnp.pad(state_conv, ((0, 0), (ts - (CONV_WIDTH - 1), 0), (0, 0))).reshape(bs * ts, RG_WIDTH)
    eh = jnp.pad(state_rglru[:, None, :], ((0, 0), (0, ts - 1), (0, 0))).reshape(bs * ts, RG_WIDTH)
    x2_s, q_s, rgh_s, s_s = _sample_mix(
        x1, p, econv, eh, state_gla.reshape(bs, GLA_KEY_WIDTH, GLA_DV),
        mixer_w + (row(xattn_norm), w_cq_b), ts)
    y_s = _sample_attn_tail(q_s, _head_interleaved_rows(cache_mem_k), _head_interleaved_rows(cache_mem_v),
                            x2_s, (w_co_b,) + ffn2_w, ts)

    return (y_p.reshape(bp, tp, D_MODEL), y_s.reshape(bs, ts, D_MODEL),
            mem_k.reshape(bp, N_MEM, XA_HEADS, XA_HEAD_DIM),
            mem_v.reshape(bp, N_MEM, XA_HEADS, XA_HEAD_DIM),
            conv_p, rgh_p[:, 0, :], s_p.reshape(bp, GLA_HEADS, GLA_DK, GLA_DV),
            conv_s, rgh_s, s_s.reshape(bs, GLA_HEADS, GLA_DK, GLA_DV))
```

```python
import functools

import jax
import jax.numpy as jnp
import numpy as np
from jax import lax
from jax.experimental import pallas as pl
from jax.experimental.pallas import tpu as pltpu

F32 = jnp.float32
BF16 = jnp.bfloat16

D_MODEL = 1024
D_FF = 2816
RG_WIDTH = 512
RG_HEADS = 8
RG_HEAD_DIM = 64
CONV_WIDTH = 4
RG_C = 8.0
GLA_WIDTH = 512
GLA_HEADS = 4
GLA_DV = 128
GLA_DK = 64
GLA_KEY_WIDTH = 256
GLA_GATE_RANK = 16
GLA_GATE_NORMALIZER = 16.0
GLA_CHUNK = 32
N_MEM = 256
XA_HEADS = 4
XA_HEAD_DIM = 256
EPS = 1e-6
MASKED_SCORE = -np.inf

OFF_RG_X = 0
OFF_RG_Y = 512
OFF_Q = 1024
OFF_K = 1280
OFF_V = 1536
OFF_G = 2048
OFF_A = 2560
D_IN = 2576

LANES = 128
SUBLANES = 8
BF16_SUBLANES = 16
A_PAD = LANES
P_WIDTH = OFF_A + A_PAD
PAIR_K = 2 * GLA_DK
PAIR_V = 2 * GLA_DV
VMEM_LIMIT = 58 * 1024 * 1024

ROW_TILE = 256
FFN_IN_TILES = 2
PROMPT_TILES = 2
FF_CHUNK = 256
FF_DOWN_GROUP = 4
FFN2_PLACEMENT = (1, 1, 1, 1, 1, 1, 1, 1, 1, 1, 1)
SAMPLE_ATTN_BATCH = 8


def _rms(x, g):
    return x * lax.rsqrt(jnp.mean(x * x, axis=-1, keepdims=True) + EPS) * g


def _mm(a, w):
    return jnp.dot(a.astype(BF16), w, preferred_element_type=F32)


def _mm_nt(a, b):
    return lax.dot_general(a.astype(BF16), b.astype(BF16), (((1,), (1,)), ((), ())),
                           preferred_element_type=F32)


def _mm_tn(a, b):
    return lax.dot_general(a.astype(BF16), b.astype(BF16), (((0,), (0,)), ((), ())),
                           preferred_element_type=F32)


def _silu(x):
    return x * jax.nn.sigmoid(x)


def _gelu_tanh(x):
    c = np.float32(np.sqrt(2.0 / np.pi))
    return x * (0.5 * (1.0 + jnp.tanh(c * (x + 0.044715 * (x * x * x)))))


def _softplus(x):
    return jnp.maximum(x, 0.0) + jnp.log1p(jnp.exp(-jnp.abs(x)))


def _ffn(x, norm_ref, wg_ref, wu_ref, wd_ref):
    h = _rms(x, norm_ref[...]).astype(BF16)
    split = (D_FF // FF_CHUNK // 2) * FF_CHUNK
    out = None
    for cols in (slice(0, split), slice(split, D_FF)):
        g = jnp.dot(h, wg_ref[:, cols], preferred_element_type=F32)
        u = jnp.dot(h, wu_ref[:, cols], preferred_element_type=F32)
        part = _mm(_silu(g) * u, wd_ref[cols, :])
        out = part if out is None else out + part
    return out


def _last_rows(buf_ref, x, seg):
    n = x.shape[0] // seg
    outs = []
    for j in range(x.shape[1] // LANES):
        buf_ref[j] = x[:, j * LANES:(j + 1) * LANES]
        outs.append(buf_ref[j, pl.ds(seg - 1, n, stride=seg), :])
    return jnp.concatenate(outs, axis=1)


def _ffn_pieces(x, norm_ref, wg_ref, wu_ref, wd_ref, out):
    h = _rms(x, norm_ref[...]).astype(BF16)
    acc = None
    acts = []
    n = D_FF // FF_CHUNK
    for c in range(n):
        cols = slice(c * FF_CHUNK, (c + 1) * FF_CHUNK)
        g = jnp.dot(h, wg_ref[:, cols], preferred_element_type=F32)
        u = jnp.dot(h, wu_ref[:, cols], preferred_element_type=F32)
        acts.append((_silu(g) * u).astype(BF16))
        if len(acts) == FF_DOWN_GROUP or c == n - 1:
            lo = (c + 1 - len(acts)) * FF_CHUNK
            part = jnp.dot(jnp.concatenate(acts, axis=1), wd_ref[lo:(c + 1) * FF_CHUNK, :],
                           preferred_element_type=F32)
            acc = part if acc is None else acc + part
            acts = []
        yield
    out.append(acc)


def _shift_rows(x, s):
    return pltpu.roll(x, s, axis=0)


def _row_in_segment(shape, seg):
    return lax.broadcasted_iota(jnp.int32, shape, 0) & (seg - 1)


def _segment_cumsum(x, seg):
    pos = _row_in_segment(x.shape, seg)
    s = 1
    while s < seg:
        x = jnp.where(pos >= s, x + _shift_rows(x, s), x)
        s *= 2
    return x


def _segment_affine_scan(a, u, seg):
    pos = _row_in_segment(a.shape, seg)
    s = 1
    while s < seg:
        m = pos >= s
        u = jnp.where(m, a * _shift_rows(u, s) + u, u)
        a = jnp.where(m, a * _shift_rows(a, s), a)
        s *= 2
    return a, u


def _rg_half(xb_c, sh, c, cw_ref, cb_ref, wgate_ref, ba_ref, bx_ref, lam_ref):
    half = RG_WIDTH // 2
    cols = slice(c * half, (c + 1) * half)
    xc = cb_ref[:, cols] + sh[2] * cw_ref[0:1, cols]
    xc = xc + sh[1] * cw_ref[1:2, cols]
    xc = xc + sh[0] * cw_ref[2:3, cols]
    xc = xc + xb_c * cw_ref[3:4, cols]
    z = _mm(xc, wgate_ref[c])
    r = jax.nn.sigmoid(z[:, :half] + ba_ref[:, cols])
    i = jax.nn.sigmoid(z[:, half:] + bx_ref[:, cols])
    log_a = (-RG_C * _softplus(-lam_ref[:, cols])) * r
    a = jnp.exp(log_a)
    return a, jnp.sqrt(-jnp.tanh(log_a) * (a * a + 1.0)) * (i * xc)


def _gla_log_decay(a_low, wa2_ref, ba2_ref):
    z = _mm(a_low, wa2_ref[...]) + ba2_ref[...]
    return (-_softplus(-z)) / GLA_GATE_NORMALIZER


def _gla_intra(qt, kt, v, chunk):
    t = qt.shape[0]
    shift = int(np.log2(chunk))
    ri = lax.broadcasted_iota(jnp.int32, (t, t), 0)
    ci = lax.broadcasted_iota(jnp.int32, (t, t), 1)
    causal = (ri >= ci) & ((ri >> shift) == (ci >> shift))
    lane = lax.broadcasted_iota(jnp.int32, (t, PAIR_K), 1)
    outs = []
    for h in range(GLA_HEADS):
        p = h // 2
        qp = qt[:, p * PAIR_K:(p + 1) * PAIR_K]
        kp = kt[:, p * PAIR_K:(p + 1) * PAIR_K]
        mine = (lane >= GLA_DK) if (h % 2) else (lane < GLA_DK)
        s = _mm_nt(jnp.where(mine, qp, 0.0), kp)
        attn = jnp.where(causal, s, 0.0)
        outs.append(_mm(attn, v[:, h * GLA_DV:(h + 1) * GLA_DV].astype(BF16)))
    return outs


def _pair_blockdiag_mask():
    r = lax.broadcasted_iota(jnp.int32, (PAIR_K, PAIR_V), 0)
    c = lax.broadcasted_iota(jnp.int32, (PAIR_K, PAIR_V), 1)
    return (r < GLA_DK) == (c < GLA_DV)


def _gla_chunk_updates(k, v, b, b_last, chunk):
    bd_mask = _pair_blockdiag_mask()
    kv = {}
    for c in range(k.shape[0] // chunk):
        rows = slice(c * chunk, (c + 1) * chunk)
        for p in range(2):
            kl = slice(p * PAIR_K, (p + 1) * PAIR_K)
            kd = k[rows, kl] * jnp.exp(b_last[c:c + 1, kl] - b[rows, kl])
            kv[c, p] = jnp.where(bd_mask, _mm_tn(kd, v[rows, p * PAIR_V:(p + 1) * PAIR_V]), 0.0)
    return kv


def _gla_combine(intra, qt, states, chunk):
    n_chunks = qt.shape[0] // chunk
    heads = []
    for p in range(2):
        kl = slice(p * PAIR_K, (p + 1) * PAIR_K)
        inter = jnp.concatenate(
            [_mm(qt[c * chunk:(c + 1) * chunk, kl], states[c, p]) for c in range(n_chunks)],
            axis=0)
        heads.append(intra[2 * p] + inter[:, :GLA_DV])
        heads.append(intra[2 * p + 1] + inter[:, GLA_DV:])
    return heads


def _gla_out(o_heads, g, norm_ref):
    outs = []
    for h in range(GLA_HEADS):
        outs.append(_rms(o_heads[h], norm_ref[:, h * GLA_DV:(h + 1) * GLA_DV]))
    return jnp.concatenate(outs, axis=1) * _silu(g)


def _softmax_rows(s):
    m = jnp.max(s, axis=-1, keepdims=True)
    e = jnp.exp(s - m)
    return e * (1.0 / jnp.sum(e, axis=-1, keepdims=True))


def _split_p(p_ref, rows=slice(None)):
    xb = p_ref[rows, OFF_RG_X:OFF_RG_Y]
    yb = p_ref[rows, OFF_RG_Y:OFF_Q]
    q = p_ref[rows, OFF_Q:OFF_K]
    k = p_ref[rows, OFF_K:OFF_V]
    v = p_ref[rows, OFF_V:OFF_G]
    g = p_ref[rows, OFF_G:OFF_A]
    a_low = p_ref[rows, OFF_A:P_WIDTH]
    return xb, yb, q, k, v, g, a_low


def _gla_prepare(q, k, log_a, chunk):
    b = _segment_cumsum(log_a, chunk)
    qt = (q * (GLA_DK ** -0.5)) * jnp.exp(b)
    kt = k * jnp.exp(-b)
    return b, qt, kt


def _mix_project(rg_out, gla_out, wout_ref):
    return (_mm(rg_out, wout_ref[0:RG_WIDTH, :]) + _mm(gla_out, wout_ref[RG_WIDTH:, :]))


def _ffn_in_kernel(xa_ref, xb_ref, n1_ref, wg_ref, wu_ref, wd_ref, n2_ref, win_ref, *refs, steps_a):
    from_a = pl.program_id(0) < steps_a
    n_cast = (len(refs) - 2) // 2
    cast_src, (x1_ref, p_ref), cast_dst = refs[:n_cast], refs[n_cast:n_cast + 2], refs[n_cast + 2:]
    for sub in range(xa_ref.shape[0] // ROW_TILE):
        rows = slice(sub * ROW_TILE, (sub + 1) * ROW_TILE)
        x = jnp.where(from_a, xa_ref[rows, :], xb_ref[rows, :])
        x1 = x + 0.5 * _ffn(x, n1_ref, wg_ref, wu_ref, wd_ref)
        x1_ref[rows, :] = x1
        p_ref[rows, :] = _mm(_rms(x1, n2_ref[...]), win_ref[...])
    for src, dst in zip(cast_src, cast_dst):
        dst[...] = src[...].astype(BF16)


def _mem_kv_kernel(m_ref, n_ref, wk_ref, wv_ref, *refs):
    n_cast = (len(refs) - 4) // 2
    cast_src, (k_ref, v_ref, kb_ref, vb_ref), cast_dst = refs[:n_cast], refs[n_cast:n_cast + 4], refs[n_cast + 4:]
    h = _rms(m_ref[...], n_ref[...]).astype(BF16)
    k = jnp.dot(h, wk_ref[...], preferred_element_type=F32)
    v = jnp.dot(h, wv_ref[...], preferred_element_type=F32)
    for h in range(XA_HEADS):
        sl = slice(h * XA_HEAD_DIM, (h + 1) * XA_HEAD_DIM)
        k_ref[:, h, :] = k[:, sl]
        v_ref[:, h, :] = v[:, sl]
    kb_ref[...] = k.astype(BF16)
    vb_ref[...] = v.astype(BF16)
    for src, dst in zip(cast_src, cast_dst):
        dst[...] = src[...].astype(BF16)


def _prompt_tail_kernel(
        x1_ref, p_ref, kb_ref, vb_ref,
        cw_ref, cb_ref, wgate_ref, ba_ref, bx_ref, lam_ref, rgn_ref,
        wa2_ref, ba2_ref, glan_ref, wout_ref,
        xn_ref, wcq_ref, wco_ref, n2_ref, wg_ref, wu_ref, wd_ref, fn_ref,
        y_ref, rgh_ref, s_ref,
        xpad_ref, h_ref, sbd_ref, b_ref, x3_ref, *, steps_per_seq, n_steps):
    step = pl.program_id(0)
    t_idx = jnp.minimum(step, n_steps - 1) % steps_per_seq
    slot = step % 2
    tt = ROW_TILE
    subs = x1_ref.shape[0] // tt
    n_chunks = tt // GLA_CHUNK

    @pl.when(step == 0)
    def _():
        x3_ref[1] = jnp.zeros((subs, tt, D_MODEL), F32)

    @pl.when(t_idx == 0)
    def _():
        xpad_ref[0:SUBLANES, :] = jnp.zeros((SUBLANES, RG_WIDTH), F32)
        h_ref[...] = jnp.zeros_like(h_ref)
        sbd_ref[...] = jnp.zeros_like(sbd_ref)

    for sub in range(subs):
        rows = slice(sub * tt, (sub + 1) * tt)
        x3_prev = x3_ref[1 - slot, sub]
        ffn_out = []
        ffn = _ffn_pieces(x3_prev, n2_ref, wg_ref, wu_ref, wd_ref, ffn_out)
        placement = iter(FFN2_PLACEMENT)

        def emit_ffn():
            for _ in range(next(placement)):
                next(ffn)

        emit_ffn()

        xb, yb, q, k, v, g, a_low = _split_p(p_ref, rows)

        half = RG_WIDTH // 2
        gated = []
        for c in range(2):
            cols = slice(c * half, (c + 1) * half)
            xb_c = xb[:, cols]
            xpad_ref[SUBLANES:SUBLANES + tt, cols] = xb_c
            sh = [xpad_ref[SUBLANES - j:SUBLANES - j + tt, cols] for j in range(1, CONV_WIDTH)]
            xpad_ref[0:SUBLANES, cols] = xb_c[tt - SUBLANES:tt, :]
            a, u = _rg_half(xb_c, sh, c, cw_ref, cb_ref, wgate_ref, ba_ref, bx_ref, lam_ref)
            emit_ffn()
            a_grp, h_grp = _segment_affine_scan(a, u, SUBLANES)
            carry = h_ref[0:1, cols]
            groups = []
            for gi in range(tt // SUBLANES):
                grp = slice(gi * SUBLANES, (gi + 1) * SUBLANES)
                groups.append(h_grp[grp, :] + a_grp[grp, :] * carry)
                carry = groups[-1][SUBLANES - 1:SUBLANES, :]
            h_ref[:, cols] = jnp.broadcast_to(carry, (SUBLANES, half))
            gated.append(jnp.concatenate(groups, axis=0) * _gelu_tanh(yb[:, cols]))
            emit_ffn()
        rg_out = _rms(jnp.concatenate(gated, axis=1), rgn_ref[...]).astype(BF16)

        log_a = _gla_log_decay(a_low, wa2_ref, ba2_ref)
        b, qt, kt = _gla_prepare(q, k, log_a, GLA_CHUNK)
        emit_ffn()
        b_last = _last_rows(b_ref, b, GLA_CHUNK)
        dec_t = jnp.exp(b_last).T
        kv = _gla_chunk_updates(k, v, b, b_last, GLA_CHUNK)
        states = {}
        for p in range(2):
            s_bd = sbd_ref[p]
            for c in range(n_chunks):
                states[c, p] = s_bd.astype(BF16)
                s_bd = s_bd * dec_t[p * PAIR_K:(p + 1) * PAIR_K, c:c + 1] + kv[c, p]
            sbd_ref[p] = s_bd
        emit_ffn()
        intra = _gla_intra(qt, kt, v, GLA_CHUNK)
        emit_ffn()
        gla_out = _gla_out(_gla_combine(intra, qt, states, GLA_CHUNK), g, glan_ref).astype(BF16)
        emit_ffn()

        x2 = x1_ref[rows, :] + _mix_project(rg_out, gla_out, wout_ref)

        qx = _mm(_rms(x2, xn_ref[...]), wcq_ref[...])
        emit_ffn()
        head = lambda h: slice(h * XA_HEAD_DIM, (h + 1) * XA_HEAD_DIM)
        s = jnp.concatenate([_mm_nt(qx[:, head(h)], kb_ref[:, head(h)]) for h in range(XA_HEADS)], axis=0)
        pr = _softmax_rows(s * (XA_HEAD_DIM ** -0.5))
        emit_ffn()
        o = jnp.concatenate([_mm(pr[h * tt:(h + 1) * tt, :], vb_ref[:, head(h)]) for h in range(XA_HEADS)], axis=1)
        x3_ref[slot, sub] = x2 + _mm(o, wco_ref[...])
        for _ in ffn:
            pass
        y_ref[rows, :] = _rms(x3_prev + 0.5 * ffn_out[0], fn_ref[...])

    @pl.when((t_idx == steps_per_seq - 1) & (step < n_steps))
    def _():
        rgh_ref[...] = h_ref[...]
        row = lax.broadcasted_iota(jnp.int32, (PAIR_K, GLA_DV), 0)
        for p in range(2):
            s_bd = sbd_ref[p]
            s_ref[p * PAIR_K:(p + 1) * PAIR_K, :] = jnp.where(
                row < GLA_DK, s_bd[:, :GLA_DV], s_bd[:, GLA_DV:])


def _sample_mix_kernel(
        x1_ref, p_ref, econv_ref, eh_ref, s0_ref,
        cw_ref, cb_ref, wgate_ref, ba_ref, bx_ref, lam_ref, rgn_ref,
        wa2_ref, ba2_ref, glan_ref, wout_ref, xn_ref, wcq_ref,
        x2_ref, q_ref, rgh_ref, s_ref,
        hs_ref, b_ref, *, seq):
    rows_n = x1_ref.shape[0]
    nb = rows_n // seq

    xb, yb, q, k, v, g, a_low = _split_p(p_ref)

    half = RG_WIDTH // 2
    pos = _row_in_segment((rows_n, half), seq)
    hs_halves = []
    for c in range(2):
        cols = slice(c * half, (c + 1) * half)
        xb_c, econv_c = xb[:, cols], econv_ref[:, cols]
        sh = [jnp.where(pos >= j, _shift_rows(xb_c, j), _shift_rows(econv_c, rows_n - seq + j))
              for j in range(1, CONV_WIDTH)]
        a, u = _rg_half(xb_c, sh, c, cw_ref, cb_ref, wgate_ref, ba_ref, bx_ref, lam_ref)
        hs_halves.append(_segment_affine_scan(a, u + a * eh_ref[:, cols], seq)[1])
    hs = jnp.concatenate(hs_halves, axis=1)
    rgh_ref[...] = _last_rows(hs_ref, hs, seq)
    rg_out = _rms(hs * _gelu_tanh(yb), rgn_ref[...]).astype(BF16)

    log_a = _gla_log_decay(a_low, wa2_ref, ba2_ref)
    b, qt, kt = _gla_prepare(q, k, log_a, seq)
    intra = _gla_intra(qt, kt, v, seq)
    b_last = _last_rows(b_ref, b, seq)
    dec_t = jnp.exp(b_last).T
    kv = _gla_chunk_updates(k, v, b, b_last, seq)
    row = lax.broadcasted_iota(jnp.int32, (PAIR_K, GLA_DV), 0)
    top = row < GLA_DK
    states = {}
    for c in range(nb):
        for p in range(2):
            kl = slice(p * PAIR_K, (p + 1) * PAIR_K)
            s_pair = s0_ref[c, kl, :]
            s_bd = jnp.concatenate([jnp.where(top, s_pair, 0.0), jnp.where(top, 0.0, s_pair)], axis=1)
            states[c, p] = s_bd.astype(BF16)
            s_new = s_bd * dec_t[kl, c:c + 1] + kv[c, p]
            s_ref[c, kl, :] = jnp.where(top, s_new[:, :GLA_DV], s_new[:, GLA_DV:])
    gla_out = _gla_out(_gla_combine(intra, qt, states, seq), g, glan_ref).astype(BF16)

    x2 = x1_ref[...] + _mix_project(rg_out, gla_out, wout_ref)
    x2_ref[...] = x2
    q_ref[...] = _mm(_rms(x2, xn_ref[...]), wcq_ref[...])


def _sample_attn_tail_kernel(q_ref, k_ref, v_ref, x2_ref, wco_ref, n2_ref, wg_ref, wu_ref, wd_ref, fn_ref,
                             y_ref, o_ref, *, seq):
    tiles = XA_HEAD_DIM // LANES
    group = tiles * XA_HEADS
    r = N_MEM * group
    nb = k_ref.shape[0] // r
    hs = XA_HEADS * seq
    lane = lax.broadcasted_iota(jnp.int32, (nb * hs, r), 1)
    head = (lax.broadcasted_iota(jnp.int32, (nb * hs, r), 0) // seq) & (XA_HEADS - 1)
    own = (lane & (group - 1)) == head
    s = []
    for j in range(nb):
        q = q_ref[j * seq:(j + 1) * seq, :]
        q_parts = jnp.concatenate(
            [q[:, c * LANES:(c + 1) * LANES] for c in range(XA_HEADS * tiles)], axis=0)
        part = _mm_nt(q_parts, k_ref[j * r:(j + 1) * r, :])
        for h in range(XA_HEADS):
            acc = part[h * tiles * seq:(h * tiles + 1) * seq, :]
            for c in range(1, tiles):
                blk = part[(h * tiles + c) * seq:(h * tiles + c + 1) * seq, :]
                acc = acc + pltpu.roll(blk, r - c * XA_HEADS, axis=1)
            s.append(acc)
    p = _softmax_rows(jnp.where(own, jnp.concatenate(s, axis=0) * (XA_HEAD_DIM ** -0.5), MASKED_SCORE))
    for j in range(nb):
        pj = p[j * hs:(j + 1) * hs, :]
        p_all = jnp.concatenate(
            [pj] + [pltpu.roll(pj, c * XA_HEADS, axis=1) for c in range(1, tiles)], axis=0)
        o = _mm(p_all, v_ref[j * r:(j + 1) * r, :].astype(BF16))
        for c in range(tiles):
            for h in range(XA_HEADS):
                col = h * XA_HEAD_DIM + c * LANES
                o_ref[j * seq:(j + 1) * seq, col:col + LANES] = (
                    o[(c * XA_HEADS + h) * seq:(c * XA_HEADS + h + 1) * seq, :])
    x3 = x2_ref[...] + _mm(o_ref[...], wco_ref[...])
    x4 = x3 + 0.5 * _ffn(x3, n2_ref, wg_ref, wu_ref, wd_ref)
    y_ref[...] = _rms(x4, fn_ref[...])


def _resident(arr):
    nd = arr.ndim
    return pl.BlockSpec(arr.shape, lambda *_: (0,) * nd, pipeline_mode=pl.Buffered(1))


def _params(sem):
    return pltpu.CompilerParams(dimension_semantics=sem, vmem_limit_bytes=VMEM_LIMIT)


def _slab_spec(w, steps):
    per, revisit = w.shape[0] // steps, 1
    while per % BF16_SUBLANES:
        per, revisit = per * 2, revisit * 2
    return pl.BlockSpec((per, w.shape[1]), lambda i: (jnp.minimum(i, steps - 1) // revisit, 0))


def _ffn_in(xa, xb, weights, cast=()):
    block = FFN_IN_TILES * ROW_TILE
    steps_a, steps_b = xa.shape[0] // block, xb.shape[0] // block
    rows = xa.shape[0] + xb.shape[0]
    spec_a = pl.BlockSpec((block, D_MODEL), lambda i: (jnp.minimum(i, steps_a - 1), 0))
    spec_b = pl.BlockSpec((block, D_MODEL), lambda i: (jnp.maximum(i - steps_a, 0), 0))
    row_spec = lambda w: pl.BlockSpec((block, w), lambda i: (i, 0))
    slabs = [_slab_spec(w, steps_a) for w in cast]
    return pl.pallas_call(
        functools.partial(_ffn_in_kernel, steps_a=steps_a),
        grid=(steps_a + steps_b,),
        in_specs=[spec_a, spec_b] + [_resident(w) for w in weights] + slabs,
        out_specs=[row_spec(D_MODEL), row_spec(P_WIDTH)] + slabs,
        out_shape=[jax.ShapeDtypeStruct((rows, D_MODEL), F32),
                   jax.ShapeDtypeStruct((rows, P_WIDTH), F32)]
                  + [jax.ShapeDtypeStruct(w.shape, BF16) for w in cast],
        compiler_params=_params(("arbitrary",)),
        name="ffn_in",
    )(xa, xb, *weights, *cast)


def _mem_kv(mem, weights, cast=()):
    rows = mem.shape[0]
    steps = rows // ROW_TILE
    row_spec = pl.BlockSpec((ROW_TILE, D_MODEL), lambda i: (i, 0))
    head_spec = pl.BlockSpec((ROW_TILE, XA_HEADS, XA_HEAD_DIM), lambda i: (i, 0, 0))
    slabs = [_slab_spec(w, steps) for w in cast]
    return pl.pallas_call(
        _mem_kv_kernel,
        grid=(steps,),
        in_specs=[row_spec] + [_resident(w) for w in weights] + slabs,
        out_specs=[head_spec] * 2 + [row_spec] * 2 + slabs,
        out_shape=[jax.ShapeDtypeStruct((rows, XA_HEADS, XA_HEAD_DIM), F32)] * 2
                  + [jax.ShapeDtypeStruct((rows, D_MODEL), BF16)] * 2
                  + [jax.ShapeDtypeStruct(w.shape, BF16) for w in cast],
        compiler_params=_params(("arbitrary",)),
        name="mem_kv",
    )(mem, *weights, *cast)


def _prompt_tail(x1, p, kb, vb, weights, seq):
    batch = kb.shape[0]
    tt = ROW_TILE
    block = PROMPT_TILES * tt
    steps_per_seq = seq // block
    n_steps = batch * steps_per_seq
    cur = lambda s: jnp.minimum(s, n_steps - 1)
    prev = lambda s: jnp.maximum(s - 1, 0)
    tok = lambda w: pl.BlockSpec((block, w), lambda s: (cur(s), 0))
    per_seq = lambda r, w: pl.BlockSpec((None, r, w), lambda s: (cur(s) // steps_per_seq, 0, 0))
    out_tok = pl.BlockSpec((block, D_MODEL), lambda s: (prev(s), 0))
    return pl.pallas_call(
        functools.partial(_prompt_tail_kernel, steps_per_seq=steps_per_seq, n_steps=n_steps),
        grid=(n_steps + 1,),
        in_specs=[tok(D_MODEL), tok(P_WIDTH), per_seq(N_MEM, D_MODEL), per_seq(N_MEM, D_MODEL)]
                 + [_resident(w) for w in weights],
        out_specs=[out_tok, per_seq(SUBLANES, RG_WIDTH), per_seq(GLA_KEY_WIDTH, GLA_DV)],
        out_shape=[jax.ShapeDtypeStruct((batch * seq, D_MODEL), F32),
                   jax.ShapeDtypeStruct((batch, SUBLANES, RG_WIDTH), F32),
                   jax.ShapeDtypeStruct((batch, GLA_KEY_WIDTH, GLA_DV), F32)],
        scratch_shapes=[pltpu.VMEM((SUBLANES + tt, RG_WIDTH), F32),
                        pltpu.VMEM((SUBLANES, RG_WIDTH), F32),
                        pltpu.VMEM((2, PAIR_K, PAIR_V), F32),
                        pltpu.VMEM((GLA_KEY_WIDTH // LANES, tt, LANES), F32),
                        pltpu.VMEM((2, PROMPT_TILES, tt, D_MODEL), F32)],
        compiler_params=_params(("arbitrary",)),
        name="prompt_tail",
    )(x1, p, kb, vb, *weights)


def _sample_mix(x1, p, econv, eh, s0, weights, seq):
    rows = econv.shape[0]
    first = (x1.shape[0] - rows) // ROW_TILE
    nb = ROW_TILE // seq
    row_spec = lambda w: pl.BlockSpec((ROW_TILE, w), lambda i: (i, 0))
    tail_spec = lambda w: pl.BlockSpec((ROW_TILE, w), lambda i: (first + i, 0))
    state_spec = pl.BlockSpec((nb, GLA_KEY_WIDTH, GLA_DV), lambda i: (i, 0, 0))
    return pl.pallas_call(
        functools.partial(_sample_mix_kernel, seq=seq),
        grid=(rows // ROW_TILE,),
        in_specs=[tail_spec(D_MODEL), tail_spec(P_WIDTH), row_spec(RG_WIDTH), row_spec(RG_WIDTH),
                  state_spec] + [_resident(w) for w in weights],
        out_specs=[row_spec(D_MODEL), row_spec(D_MODEL),
                   pl.BlockSpec((nb, RG_WIDTH), lambda i: (i, 0)), state_spec],
        out_shape=[jax.ShapeDtypeStruct((rows, D_MODEL), F32),
                   jax.ShapeDtypeStruct((rows, D_MODEL), F32),
                   jax.ShapeDtypeStruct((rows // seq, RG_WIDTH), F32),
                   jax.ShapeDtypeStruct((rows // seq, GLA_KEY_WIDTH, GLA_DV), F32)],
        scratch_shapes=[pltpu.VMEM((RG_WIDTH // LANES, ROW_TILE, LANES), F32),
                        pltpu.VMEM((GLA_KEY_WIDTH // LANES, ROW_TILE, LANES), F32)],
        compiler_params=_params(("parallel",)),
        name="sample_mix",
    )(x1, p, econv, eh, s0, *weights)


def _head_interleaved_rows(x):
    b, m, h, dh = x.shape
    tiles = dh // LANES
    return (x.reshape(b, m, h, tiles, LANES).transpose(0, 1, 3, 2, 4)
            .reshape(b * m * tiles * h, LANES))


def _sample_attn_tail(q, k, v, x2, weights, seq):
    rows = q.shape[0]
    nb = SAMPLE_ATTN_BATCH
    row_spec = pl.BlockSpec((nb * seq, D_MODEL), lambda i: (i, 0))
    kv_rows = k.shape[0] // (rows // seq)
    kv_spec = pl.BlockSpec((nb * kv_rows, LANES), lambda i: (i, 0))
    return pl.pallas_call(
        functools.partial(_sample_attn_tail_kernel, seq=seq),
        grid=(rows // (nb * seq),),
        in_specs=[row_spec, kv_spec, kv_spec, row_spec] + [_resident(w) for w in weights],
        out_specs=row_spec,
        out_shape=jax.ShapeDtypeStruct((rows, D_MODEL), F32),
        scratch_shapes=[pltpu.VMEM((nb * seq, D_MODEL), F32)],
        compiler_params=_params(("parallel",)),
        name="sample_attn_tail",
    )(q, k, v, x2, *weights)


def _block_diag_heads(w):
    h, n, _ = w.shape
    eye = jnp.eye(h, dtype=w.dtype)
    return (eye[:, None, :, None] * w[:, :, None, :]).reshape(h * n, h * n)


def kernel(x_prompt, x_sample, cache_mem_k, cache_mem_v, state_conv, state_rglru, state_gla, mem_prompt, ffn1_norm, ffn1_w_gate, ffn1_w_up, ffn1_w_down, mix_norm, w_in, conv_w, conv_b, rg_w_a, rg_b_a, rg_w_x, rg_b_x, rg_lambda, rg_out_norm, gla_w_a2, gla_b_a2, gla_out_norm, w_out, xattn_norm, mem_norm, w_cq, w_ck, w_cv, w_co, ffn2_norm, ffn2_w_gate, ffn2_w_up, ffn2_w_down, final_norm):
    bp, tp, _ = x_prompt.shape
    bs, ts, _ = x_sample.shape
    assert ts == SUBLANES, "each sample sequence must fill exactly one 8-row sublane group"
    assert tp % (PROMPT_TILES * ROW_TILE) == 0 and (bs * ts) % (FFN_IN_TILES * ROW_TILE) == 0
    assert bs % SAMPLE_ATTN_BATCH == 0 and (bp * N_MEM) % ROW_TILE == 0
    assert cache_mem_k.shape[1:] == (N_MEM, XA_HEADS, XA_HEAD_DIM) and mem_prompt.shape[1] == N_MEM
    row = lambda g: g.reshape(1, -1)
    bf = lambda w: w.astype(BF16)

    w_in_p = bf(jnp.pad(w_in, ((0, 0), (0, P_WIDTH - D_IN))))
    hp = RG_HEADS // 2
    w_gate_rg = bf(jnp.stack([
        jnp.concatenate([_block_diag_heads(rg_w_a[c * hp:(c + 1) * hp]),
                         _block_diag_heads(rg_w_x[c * hp:(c + 1) * hp])], axis=1)
        for c in range(2)]))
    w_a2_p = bf(jnp.pad(gla_w_a2, ((0, A_PAD - GLA_GATE_RANK), (0, 0))))

    mem_k, mem_v, mem_kb, mem_vb, w1_gate, w1_up, w1_down = _mem_kv(
        mem_prompt.reshape(bp * N_MEM, D_MODEL), (row(mem_norm), bf(w_ck), bf(w_cv)),
        cast=(ffn1_w_gate, ffn1_w_up, ffn1_w_down))
    ffn_in_w = (row(ffn1_norm), w1_gate, w1_up, w1_down, row(mix_norm), w_in_p)
    x1, p, w2_gate, w2_up, w2_down, w_out_b, w_cq_b, w_co_b = _ffn_in(
        x_prompt.reshape(bp * tp, D_MODEL), x_sample.reshape(bs * ts, D_MODEL), ffn_in_w,
        cast=(ffn2_w_gate, ffn2_w_up, ffn2_w_down, w_out, w_cq, w_co))
    mixer_w = (conv_w, row(conv_b), w_gate_rg, row(rg_b_a), row(rg_b_x), row(rg_lambda),
               row(rg_out_norm), w_a2_p, row(gla_b_a2), row(jnp.tile(gla_out_norm, GLA_HEADS)),
               w_out_b)
    ffn2_w = (row(ffn2_norm), w2_gate, w2_up, w2_down, row(final_norm))
    y_p, rgh_p, s_p = _prompt_tail(
        x1, p, mem_kb.reshape(bp, N_MEM, D_MODEL), mem_vb.reshape(bp, N_MEM, D_MODEL),
        mixer_w + (row(xattn_norm), w_cq_b, w_co_b) + ffn2_w, tp)
    groups = p.reshape(-1, ts, P_WIDTH)
    per_seq = tp // ts
    conv_p = groups[per_seq - 1:bp * per_seq:per_seq, ts - (CONV_WIDTH - 1):, OFF_RG_X:OFF_RG_Y]
    conv_s = groups[bp * per_seq:, ts - (CONV_WIDTH - 1):, OFF_RG_X:OFF_RG_Y]

    econv = jnp.pad(state_conv, ((0, 0), (ts - (CONV_WIDTH - 1), 0), (0, 0))).reshape(bs * ts, RG_WIDTH)
    eh = jnp.pad(state_rglru[:, None, :], ((0, 0), (0, ts - 1), (0, 0))).reshape(bs * ts, RG_WIDTH)
    x2_s, q_s, rgh_s, s_s = _sample_mix(
        x1, p, econv, eh, state_gla.reshape(bs, GLA_KEY_WIDTH, GLA_DV),
        mixer_w + (row(xattn_norm), w_cq_b), ts)
    y_s = _sample_attn_tail(q_s, _head_interleaved_rows(cache_mem_k), _head_interleaved_rows(cache_mem_v),
                            x2_s, (w_co_b,) + ffn2_w, ts)

    return (y_p.reshape(bp, tp, D_MODEL), y_s.reshape(bs, ts, D_MODEL),
            mem_k.reshape(bp, N_MEM, XA_HEADS, XA_HEAD_DIM),
            mem_v.reshape(bp, N_MEM, XA_HEADS, XA_HEAD_DIM),
            conv_p, rgh_p[:, 0, :], s_p.reshape(bp, GLA_HEADS, GLA_DK, GLA_DV),
            conv_s, rgh_s, s_s.reshape(bs, GLA_HEADS, GLA_DK, GLA_DV))
```
